```python
import jax
import jax.numpy as jnp
from jax import lax
import numpy as np

D_MODEL = 1024
BATCH = 16
SEQ = 4096
DEPTH = 2

F32 = jnp.float32
N_EVEN = (DEPTH + 1) // 2
N_ODD = DEPTH // 2
EPS = 1e-6
ROPE_THETA = 500000.0

HG_DIM = 128
HG_WIDTH = D_MODEL // 2
HG_HEADS = HG_WIDTH // HG_DIM
HG_CHUNK = 64

NSA_HD = 64
NSA_WIDTH = D_MODEL // 2
NSA_HEADS = NSA_WIDTH // NSA_HD
NSA_KV = 2
CMP_LEN = 32
CMP_STRIDE = 16
CMP_HIDDEN = 256
SEL_LEN = 64
SEL_TOPN = 8
NSA_WINDOW = 512
NSA_QBLOCK = 64
FORCE_BONUS = 1e4

SWA_HD = 64
SWA_HEADS = D_MODEL // SWA_HD
SWA_KV = 2
SWA_WINDOW = 128
SWA_QBLOCK = 128

N_GROUPS = 4
EXP_PER_GROUP = 8
N_EXPERTS = N_GROUPS * EXP_PER_GROUP
TOP_K = 2
EXPERT_FF = 512
MOE_BLOCK = 256

EV_PROJ = 4 * HG_WIDTH + NSA_WIDTH + 6 * NSA_KV * NSA_HD + 3 * NSA_HEADS
OD_PROJ = SWA_HEADS * SWA_HD + 2 * SWA_KV * SWA_HD

kernel_name = 'hybrid_hgrn2_nsa_swa_hmoe'


def rmsnorm(x, gain):
    xf = x.astype(F32)
    y = xf * lax.rsqrt(jnp.mean(xf * xf, axis=-1, keepdims=True) + EPS)
    return (y * gain.astype(F32)).astype(x.dtype)


def rope_partial(x, pos):
    hd = x.shape[-1]
    rot = hd // 4
    half = rot // 2
    inv = jnp.power(ROPE_THETA, -jnp.arange(half, dtype=F32) * 2.0 / rot)
    ang = pos[:, None] * inv[None, :]
    cos = jnp.cos(ang)[:, None, :]
    sin = jnp.sin(ang)[:, None, :]
    xf = x.astype(F32)
    x1 = xf[..., :half]
    x2 = xf[..., half:rot]
    out = jnp.concatenate([x1 * cos - x2 * sin, x2 * cos + x1 * sin, xf[..., rot:]], axis=-1)
    return out.astype(x.dtype)


def masked_softmax(s, mask):
    s = jnp.where(mask, s.astype(F32), -jnp.inf)
    m = jnp.max(s, axis=-1, keepdims=True)
    m = jnp.where(jnp.isfinite(m), m, 0.0)
    p = jnp.exp(s - m)
    return p / jnp.maximum(jnp.sum(p, axis=-1, keepdims=True), 1e-30)


def hgrn2_mix(q, f_logit, inp, gate, lb, out_gain):
    B, S, W = q.shape
    H, Dh, C = HG_HEADS, HG_DIM, HG_CHUNK
    n_chunks = S // C
    z = f_logit.astype(F32)
    log_f = jnp.log(lb + (1.0 - lb) * jax.nn.sigmoid(z))
    k = (1.0 - lb) * jax.nn.sigmoid(-z)

    def chunks(t):
        return t.astype(F32).reshape(B, n_chunks, C, H, Dh).transpose(1, 0, 3, 2, 4)

    causal = jnp.tril(jnp.ones((C, C), dtype=bool))

    def step(state, xs):
        qc, kc, vc, lfc = xs
        b = jnp.cumsum(lfc, axis=2)
        diff = jnp.where(causal[:, :, None], b[:, :, :, None, :] - b[:, :, None, :, :], -jnp.inf)
        scores = jnp.einsum('bhtd,bhsd,bhtsd->bhts', qc, kc, jnp.exp(diff))
        b_last = b[:, :, -1, :]
        o = (jnp.einsum('bhts,bhsv->bhtv', scores, vc)
             + jnp.einsum('bhtd,bhdv->bhtv', qc * jnp.exp(b), state))
        state = (state * jnp.exp(b_last)[..., None]
                 + jnp.einsum('bhsd,bhsv->bhdv', kc * jnp.exp(b_last[:, :, None, :] - b), vc))
        return state, o

    s0 = jnp.zeros((B, H, Dh, Dh), F32)
    _, o = lax.scan(step, s0, (chunks(q), chunks(k), chunks(inp), chunks(log_f)))
    o = o.transpose(1, 0, 3, 2, 4).reshape(B, S, H, Dh)
    g = gate.astype(F32).reshape(B, S, H, Dh)
    o = rmsnorm(o, out_gain) * jax.nn.silu(g)
    return o.reshape(B, S, W).astype(q.dtype)


def compress_blocks(t, pe, w1, w2):
    B, S, G, d = t.shape
    r = CMP_LEN // CMP_STRIDE
    n_cmp = S // CMP_STRIDE - r + 1
    c = t.reshape(B, S // CMP_STRIDE, CMP_STRIDE, G, d)
    blocks = jnp.concatenate([c[:, j:j + n_cmp] for j in range(r)], axis=2)
    blocks = blocks + pe[:, None, :]
    flat = blocks.transpose(0, 1, 3, 2, 4).reshape(B, n_cmp, G, CMP_LEN * d)
    return jax.nn.gelu(flat @ w1) @ w2


def nsa_mix(q, kc, vc, ks, vs, kw, vw, gate_logits, q_gain, k_gain, cmp_pe, cmp_w1, cmp_w2):
    B, S, H, d = q.shape
    G = kc.shape[2]
    hpg = H // G
    QB = NSA_QBLOCK
    nqb = S // QB
    scale = d ** -0.5
    pos = jnp.arange(S, dtype=F32)
    q = rope_partial(rmsnorm(q, q_gain), pos)
    kc_blk = compress_blocks(kc, cmp_pe[0], cmp_w1[0], cmp_w2[0])
    vc_blk = compress_blocks(vc, cmp_pe[1], cmp_w1[1], cmp_w2[1]).astype(F32)
    n_cmp = kc_blk.shape[1]
    cmp_start = jnp.arange(n_cmp) * CMP_STRIDE
    cmp_end = cmp_start + CMP_LEN - 1
    kc_blk = rope_partial(rmsnorm(kc_blk, k_gain[0]), cmp_end.astype(F32))
    ks = rope_partial(rmsnorm(ks, k_gain[1]), pos)
    kw = rope_partial(rmsnorm(kw, k_gain[2]), pos)
    n_sel = S // SEL_LEN
    n_top = min(SEL_TOPN, n_sel)
    sel_start = jnp.arange(n_sel) * SEL_LEN
    cover = ((cmp_start[:, None] < sel_start[None, :] + SEL_LEN)
             & (cmp_start[:, None] + CMP_LEN > sel_start[None, :])).astype(F32)
    ks_t = ks.transpose(0, 2, 1, 3)
    vs_t = vs.transpose(0, 2, 1, 3)
    bi = jnp.arange(B)[:, None, None]
    gi = jnp.arange(G)[None, :, None]
    kw_pad = jnp.pad(kw, ((0, 0), (NSA_WINDOW, 0), (0, 0), (0, 0)))
    vw_pad = jnp.pad(vw, ((0, 0), (NSA_WINDOW, 0), (0, 0), (0, 0)))
    n_win = NSA_WINDOW + QB
    n_selk = n_top * SEL_LEN
    q_blocks = q.reshape(B, nqb, QB, G, hpg, d).transpose(1, 0, 2, 3, 4, 5)
    g_blocks = jax.nn.sigmoid(gate_logits.astype(F32)).reshape(B, nqb, QB, G, hpg, 3).transpose(1, 0, 2, 3, 4, 5)

    def block(args):
        qg, gb, i = args
        t = i * QB + jnp.arange(QB)
        s_c = jnp.einsum('bqghd,bngd->bqghn', qg, kc_blk).astype(F32) * scale
        p_c = masked_softmax(s_c, (cmp_end[None, :] <= t[:, None])[None, :, None, None, :])
        o_c = jnp.einsum('bqghn,bngd->bqghd', p_c, vc_blk)
        imp = jnp.einsum('bqghn,nj->bqgj', p_c, cover)
        cur = t // SEL_LEN
        jj = jnp.arange(n_sel)[None, :]
        allowed = sel_start[None, :] <= t[:, None]
        forced = (jj == 0) | (jj == cur[:, None]) | (jj == cur[:, None] - 1)
        score = jnp.where(allowed[None, :, None, :], imp + FORCE_BONUS * forced[None, :, None, :], -jnp.inf)
        top_s, top_j = lax.top_k(score, n_top)
        tok = top_j[..., None] * SEL_LEN + jnp.arange(SEL_LEN)
        valid = jnp.isfinite(top_s)[..., None] & (tok <= t[None, :, None, None, None])
        idx = tok.transpose(0, 2, 1, 3, 4).reshape(B, G, QB * n_selk)
        k_sel = ks_t[bi, gi, idx].reshape(B, G, QB, n_selk, d)
        v_sel = vs_t[bi, gi, idx].reshape(B, G, QB, n_selk, d).astype(F32)
        s_s = jnp.einsum('bqghd,bgqkd->bqghk', qg, k_sel).astype(F32) * scale
        p_s = masked_softmax(s_s, valid.reshape(B, QB, G, 1, n_selk))
        o_s = jnp.einsum('bqghk,bgqkd->bqghd', p_s, v_sel)
        kp = i * QB - NSA_WINDOW + jnp.arange(n_win)
        kb = lax.dynamic_slice_in_dim(kw_pad, i * QB, n_win, axis=1)
        vb = lax.dynamic_slice_in_dim(vw_pad, i * QB, n_win, axis=1).astype(F32)
        wmask = (kp[None, :] <= t[:, None]) & (kp[None, :] > t[:, None] - NSA_WINDOW) & (kp[None, :] >= 0)
        s_w = jnp.einsum('bqghd,bkgd->bqghk', qg, kb).astype(F32) * scale
        p_w = masked_softmax(s_w, wmask[None, :, None, None, :])
        o_w = jnp.einsum('bqghk,bkgd->bqghd', p_w, vb)
        o = gb[..., 0:1] * o_c + gb[..., 1:2] * o_s + gb[..., 2:3] * o_w
        return o.reshape(B, QB, H * d).astype(q.dtype)

    out = lax.map(block, (q_blocks, g_blocks, jnp.arange(nqb)))
    return out.transpose(1, 0, 2, 3).reshape(B, S, H * d)


def swa_sink_attention(q, k, v, sinks):
    B, S, H, d = q.shape
    G = k.shape[2]
    hpg = H // G
    QB, W = SWA_QBLOCK, SWA_WINDOW
    nqb = S // QB
    n_keys = W + QB
    scale = d ** -0.5
    kpad = jnp.pad(k, ((0, 0), (W, 0), (0, 0), (0, 0)))
    vpad = jnp.pad(v, ((0, 0), (W, 0), (0, 0), (0, 0)))
    q_blocks = q.reshape(B, nqb, QB, G, hpg, d).transpose(1, 0, 2, 3, 4, 5)
    sink = sinks.astype(F32).reshape(G, hpg)

    def block(args):
        qg, i = args
        t = i * QB + jnp.arange(QB)
        kp = i * QB - W + jnp.arange(n_keys)
        kb = lax.dynamic_slice_in_dim(kpad, i * QB, n_keys, axis=1)
        vb = lax.dynamic_slice_in_dim(vpad, i * QB, n_keys, axis=1).astype(F32)
        mask = (kp[None, :] <= t[:, None]) & (kp[None, :] > t[:, None] - W) & (kp[None, :] >= 0)
        s = jnp.einsum('bqghd,bkgd->bqghk', qg, kb).astype(F32) * scale
        s = jnp.where(mask[None, :, None, None, :], s, -jnp.inf)
        m = jnp.maximum(jnp.max(s, axis=-1), sink[None, None])
        p = jnp.exp(s - m[..., None])
        den = jnp.sum(p, axis=-1) + jnp.exp(sink[None, None] - m)
        o = jnp.einsum('bqghk,bkgd->bqghd', p, vb) / den[..., None]
        return o.reshape(B, QB, H * d).astype(q.dtype)

    out = lax.map(block, (q_blocks, jnp.arange(nqb)))
    return out.transpose(1, 0, 2, 3).reshape(B, S, H * d)


def even_mixer(h, w_in, lb, hg_gain, nsa_q_gain, nsa_k_gain, cmp_pe, cmp_w1, cmp_w2, w_out):
    B, S, _ = h.shape
    proj = h @ w_in
    sizes = [HG_WIDTH] * 4 + [NSA_WIDTH] + [NSA_KV * NSA_HD] * 6 + [3 * NSA_HEADS]
    cuts = [int(c) for c in np.cumsum(sizes)[:-1]]
    a_q, a_f, a_i, a_g, b_q, b_kc, b_vc, b_ks, b_vs, b_kw, b_vw, b_gate = jnp.split(proj, cuts, axis=-1)
    a_out = hgrn2_mix(a_q, a_f, a_i, a_g, lb, hg_gain)
    kv = lambda t: t.reshape(B, S, NSA_KV, NSA_HD)
    b_out = nsa_mix(b_q.reshape(B, S, NSA_HEADS, NSA_HD), kv(b_kc), kv(b_vc), kv(b_ks), kv(b_vs),
                    kv(b_kw), kv(b_vw), b_gate.reshape(B, S, NSA_HEADS, 3),
                    nsa_q_gain, nsa_k_gain, cmp_pe, cmp_w1, cmp_w2)
    return jnp.concatenate([a_out, b_out], axis=-1) @ w_out


def odd_mixer(h, w_in, q_gain, k_gain, sinks, w_out):
    B, S, _ = h.shape
    proj = h @ w_in
    qd = SWA_HEADS * SWA_HD
    kvd = SWA_KV * SWA_HD
    q, k, v = jnp.split(proj, [qd, qd + kvd], axis=-1)
    pos = jnp.arange(S, dtype=F32)
    q = rope_partial(rmsnorm(q.reshape(B, S, SWA_HEADS, SWA_HD), q_gain), pos)
    k = rope_partial(rmsnorm(k.reshape(B, S, SWA_KV, SWA_HD), k_gain), pos)
    v = v.reshape(B, S, SWA_KV, SWA_HD)
    return swa_sink_attention(q, k, v, sinks) @ w_out


def hier_moe(h, w_grp, b_grp, w_exp, b_exp, w_gate, w_up, w_down):
    B, S, D = h.shape
    T = B * S
    xt = h.reshape(T, D)
    tok_ids = jnp.arange(T)
    grp_logits = (xt @ w_grp).astype(F32) + b_grp.astype(F32)
    grp_prob = jax.nn.softmax(grp_logits, axis=-1)
    g_idx = jnp.argmax(grp_logits, axis=-1)
    g_w = grp_prob[tok_ids, g_idx]
    exp_logits = ((xt @ w_exp).astype(F32) + b_exp.astype(F32)).reshape(T, N_GROUPS, EXP_PER_GROUP)
    within = exp_logits[tok_ids, g_idx]
    top_l, top_j = lax.top_k(within, TOP_K)
    wts = jax.nn.softmax(top_l, axis=-1) * g_w[:, None]
    e_idx = g_idx[:, None] * EXP_PER_GROUP + top_j
    A = T * TOP_K
    flat_e = e_idx.reshape(A)
    flat_w = wts.reshape(A)
    flat_tok = jnp.arange(A) // TOP_K
    order = jnp.argsort(flat_e)
    se, stok, sw = flat_e[order], flat_tok[order], flat_w[order]
    counts = jnp.bincount(flat_e, length=N_EXPERTS)
    start = jnp.cumsum(counts) - counts
    pcounts = (counts + MOE_BLOCK - 1) // MOE_BLOCK * MOE_BLOCK
    pend = jnp.cumsum(pcounts)
    pstart = pend - pcounts
    dest = pstart[se] + jnp.arange(A) - start[se]
    n_rows = -(-A // MOE_BLOCK) * MOE_BLOCK + N_EXPERTS * MOE_BLOCK
    n_blocks = n_rows // MOE_BLOCK
    row_tok = jnp.full((n_rows,), T, dtype=jnp.int32).at[dest].set(stok.astype(jnp.int32))
    x_pad = jnp.concatenate([xt, jnp.zeros((1, D), xt.dtype)], axis=0)
    xb = x_pad[row_tok].reshape(n_blocks, MOE_BLOCK, D)
    blk_expert = jnp.minimum(jnp.searchsorted(pend, jnp.arange(n_blocks) * MOE_BLOCK, side='right'), N_EXPERTS - 1)

    def expert_block(args):
        xblk, e = args
        hid = jax.nn.silu(xblk @ w_gate[e]) * (xblk @ w_up[e])
        return hid @ w_down[e]

    yb = lax.map(expert_block, (xb, blk_expert)).reshape(n_rows, D)
    y = jnp.zeros((T, D), yb.dtype).at[stok].add(yb[dest] * sw[:, None].astype(yb.dtype))
    return y.reshape(B, S, D).astype(h.dtype)


def setup_inputs(seed: int = 0) -> dict:
    key = jax.random.key(seed)
    k = jax.random.split(key, 26)

    def normal(kk, shape, scale):
        return jax.random.normal(kk, shape, F32) * scale

    def gain(kk, shape):
        return 1.0 + 0.02 * jax.random.normal(kk, shape, F32)

    return {
        'x': normal(k[0], (BATCH, SEQ, D_MODEL), 1.0),
        'ev_norm': gain(k[1], (N_EVEN, D_MODEL)),
        'ev_w_in': normal(k[2], (N_EVEN, D_MODEL, EV_PROJ), D_MODEL ** -0.5),
        'hg_lb_logits': normal(k[3], (N_EVEN + 1, HG_WIDTH), 1.0),
        'hg_out_gain': gain(k[4], (N_EVEN, HG_DIM)),
        'nsa_q_gain': gain(k[5], (N_EVEN, NSA_HD)),
        'nsa_k_gain': gain(k[6], (N_EVEN, 3, NSA_HD)),
        'nsa_cmp_pe': normal(k[7], (N_EVEN, 2, CMP_LEN, NSA_HD), 0.1),
        'nsa_cmp_w1': normal(k[8], (N_EVEN, 2, CMP_LEN * NSA_HD, CMP_HIDDEN), (CMP_LEN * NSA_HD) ** -0.5),
        'nsa_cmp_w2': normal(k[9], (N_EVEN, 2, CMP_HIDDEN, NSA_HD), CMP_HIDDEN ** -0.5),
        'ev_w_out': normal(k[10], (N_EVEN, HG_WIDTH + NSA_WIDTH, D_MODEL), (HG_WIDTH + NSA_WIDTH) ** -0.5),
        'od_norm': gain(k[11], (N_ODD, D_MODEL)),
        'od_w_in': normal(k[12], (N_ODD, D_MODEL, OD_PROJ), D_MODEL ** -0.5),
        'swa_q_gain': gain(k[13], (N_ODD, SWA_HD)),
        'swa_k_gain': gain(k[14], (N_ODD, SWA_HD)),
        'swa_sinks': normal(k[15], (N_ODD, SWA_HEADS), 0.5),
        'od_w_out': normal(k[16], (N_ODD, SWA_HEADS * SWA_HD, D_MODEL), (SWA_HEADS * SWA_HD) ** -0.5),
        'moe_norm': gain(k[17], (DEPTH, D_MODEL)),
        'moe_w_grp': normal(k[18], (DEPTH, D_MODEL, N_GROUPS), D_MODEL ** -0.5),
        'moe_b_grp': normal(k[19], (DEPTH, N_GROUPS), 0.01),
        'moe_w_exp': normal(k[20], (DEPTH, D_MODEL, N_EXPERTS), D_MODEL ** -0.5),
        'moe_b_exp': normal(k[21], (DEPTH, N_EXPERTS), 0.01),
        'moe_w_gate': normal(k[22], (DEPTH, N_EXPERTS, D_MODEL, EXPERT_FF), D_MODEL ** -0.5),
        'moe_w_up': normal(k[23], (DEPTH, N_EXPERTS, D_MODEL, EXPERT_FF), D_MODEL ** -0.5),
        'moe_w_down': normal(k[24], (DEPTH, N_EXPERTS, EXPERT_FF, D_MODEL), EXPERT_FF ** -0.5),
    }


def reference(x, ev_norm, ev_w_in, hg_lb_logits, hg_out_gain, nsa_q_gain, nsa_k_gain, nsa_cmp_pe,
              nsa_cmp_w1, nsa_cmp_w2, ev_w_out, od_norm, od_w_in, swa_q_gain, swa_k_gain, swa_sinks,
              od_w_out, moe_norm, moe_w_grp, moe_b_grp, moe_w_exp, moe_b_exp, moe_w_gate, moe_w_up,
              moe_w_down):
    lower_bounds = jnp.cumsum(jax.nn.softmax(hg_lb_logits.astype(F32), axis=0), axis=0)
    for layer in range(DEPTH):
        j = layer // 2
        if layer % 2 == 0:
            x = x + even_mixer(rmsnorm(x, ev_norm[j]), ev_w_in[j], lower_bounds[j], hg_out_gain[j],
                               nsa_q_gain[j], nsa_k_gain[j], nsa_cmp_pe[j], nsa_cmp_w1[j],
                               nsa_cmp_w2[j], ev_w_out[j])
        else:
            x = x + odd_mixer(rmsnorm(x, od_norm[j]), od_w_in[j], swa_q_gain[j], swa_k_gain[j],
                              swa_sinks[j], od_w_out[j])
        x = x + hier_moe(rmsnorm(x, moe_norm[layer]), moe_w_grp[layer], moe_b_grp[layer],
                         moe_w_exp[layer], moe_b_exp[layer], moe_w_gate[layer], moe_w_up[layer],
                         moe_w_down[layer])
    return x
```

```python
import functools

import numpy as np
import jax
import jax.numpy as jnp
from jax import lax
from jax.experimental import pallas as pl
from jax.experimental.pallas import tpu as pltpu

F32 = jnp.float32
BF16 = jnp.bfloat16
I32 = jnp.int32
HI = lax.Precision.HIGHEST

D_MODEL = 1024
EPS = 1e-6
ROPE_THETA = 500000.0
HD = 64
ROT = HD // 4
HALF = ROT // 2

HG_DIM = 128
HG_WIDTH = 512
HG_HEADS = 4
HG_CHUNK = 64

NSA_HEADS = 8
NSA_KV = 2
NSA_HPG = NSA_HEADS // NSA_KV
CMP_LEN = 32
CMP_STRIDE = 16
CMP_HIDDEN = 256
SEL_LEN = 64
SEL_TOPN = 8
NSA_WINDOW = 512
FORCE_BONUS = 1e4

SWA_HEADS = 16
SWA_KV = 2
SWA_HPG = SWA_HEADS // SWA_KV
SWA_WINDOW = 128

N_GROUPS = 4
EXP_PER_GROUP = 8
N_EXPERTS = 32
EXPERT_FF = 512
MOE_BLOCK = 256

LANES = 128
NEG = -1e30
VMEM_LIMIT = 56 * 1024 * 1024

C_HG = 0
C_NQ = 2048
C_KV = 2560
C_GATE = 3328
N0 = 3456
N1 = 1280


def _cparams(sem):
    return pltpu.CompilerParams(dimension_semantics=sem, vmem_limit_bytes=VMEM_LIMIT)


def _dot(a, b):
    return jnp.dot(a, b, preferred_element_type=F32)


def _dot_nt(a, b):
    return lax.dot_general(a, b, (((1,), (1,)), ((), ())), preferred_element_type=F32)


def _rms_matmul_kernel(x_ref, g_ref, w_ref, o_ref, xn_ref):
    @pl.when(pl.program_id(1) == 0)
    def _():
        x = x_ref[...]
        ms = jnp.mean(x * x, axis=-1, keepdims=True)
        xn_ref[...] = (x * lax.rsqrt(ms + EPS) * g_ref[...]).astype(BF16)

    o_ref[...] = _dot(xn_ref[...], w_ref[...])


def rms_matmul(x, gain, w, tm=512, tn=None):
    T, D = x.shape
    N = w.shape[1]
    tn = N if tn is None else tn
    return pl.pallas_call(
        _rms_matmul_kernel,
        grid=(T // tm, N // tn),
        in_specs=[pl.BlockSpec((tm, D), lambda i, j: (i, 0)),
                  pl.BlockSpec((1, D), lambda i, j: (0, 0)),
                  pl.BlockSpec((D, tn), lambda i, j: (0, j))],
        out_specs=pl.BlockSpec((tm, tn), lambda i, j: (i, j)),
        out_shape=jax.ShapeDtypeStruct((T, N), F32),
        scratch_shapes=[pltpu.VMEM((tm, D), BF16)],
        compiler_params=_cparams(("parallel", "arbitrary")),
        name="rms_matmul",
    )(x, gain.reshape(1, D), w)


def _proj_res_kernel(a_ref, b_ref, wa_ref, wb_ref, r_ref, o_ref):
    acc = _dot(a_ref[...], wa_ref[...]) + _dot(b_ref[...], wb_ref[...])
    o_ref[...] = r_ref[...] + acc


def proj_res(a, a_blk, b, b_blk, w, res, tm=512):
    T, D = res.shape
    K = w.shape[0]
    kh = K // 2
    return pl.pallas_call(
        _proj_res_kernel,
        grid=(T // tm,),
        in_specs=[pl.BlockSpec((tm, kh), lambda i: (i, a_blk)),
                  pl.BlockSpec((tm, kh), lambda i: (i, b_blk)),
                  pl.BlockSpec((kh, D), lambda i: (0, 0)),
                  pl.BlockSpec((kh, D), lambda i: (1, 0)),
                  pl.BlockSpec((tm, D), lambda i: (i, 0))],
        out_specs=pl.BlockSpec((tm, D), lambda i: (i, 0)),
        out_shape=jax.ShapeDtypeStruct((T, D), F32),
        compiler_params=_cparams(("parallel",)),
        name="proj_res",
    )(a, b, w, w, res)


def _rope_tables(pos, width):
    inv = jnp.power(ROPE_THETA, -jnp.arange(HALF, dtype=F32) * 2.0 / ROT)
    ang = pos.astype(F32)[:, None] * inv[None, :]
    cos, sin = jnp.cos(ang), jnp.sin(ang)
    n = pos.shape[0]
    c = jnp.concatenate([cos, cos, jnp.ones((n, HD - ROT), F32)], axis=1)
    s = jnp.concatenate([-sin, sin, jnp.zeros((n, HD - ROT), F32)], axis=1)
    reps = width // HD
    return jnp.tile(c, (1, reps)), jnp.tile(s, (1, reps))


def _block_ones(width):
    idx = np.arange(width) // HD
    return jnp.asarray((idx[:, None] == idx[None, :]).astype(np.float32))


def _head_norm_rope(x, gain, cos, sin, bones, scale):
    ms = jnp.dot(x * x, bones, precision=HI, preferred_element_type=F32) * (1.0 / HD)
    y = x * lax.rsqrt(ms + EPS) * gain
    lane = lax.broadcasted_iota(I32, y.shape, 1) % HD
    partner = jnp.where(lane < HALF, pltpu.roll(y, LANES - HALF, axis=1), pltpu.roll(y, HALF, axis=1))
    out = y * cos + partner * sin
    return out * scale if scale != 1.0 else out


def _prep_kernel(n_heads, norm_flags, scales, *refs):
    nt = len(n_heads)
    x_refs = refs[:nt]
    gain_ref, cos_ref, sin_ref, bones_ref = refs[nt:nt + 4]
    o_refs = refs[nt + 4:]
    cos = cos_ref[...]
    sin = sin_ref[...]
    bones = bones_ref[...]
    for t in range(nt):
        for c in range(n_heads[t] // 2):
            x = x_refs[t][0, :, c * LANES:(c + 1) * LANES]
            if norm_flags[t]:
                g = gain_ref[t:t + 1, :]
                x = _head_norm_rope(x, g, cos, sin, bones, scales[t])
            o_refs[t][0, 2 * c] = x[:, :HD].astype(BF16)
            o_refs[t][0, 2 * c + 1] = pltpu.roll(x, HD, axis=1)[:, :HD].astype(BF16)


def prep_heads(proj, col_blocks, n_heads, gains, norm_flags, scales, S, ts=512):
    B = proj.shape[0]
    nt = len(n_heads)
    cos, sin = _rope_tables(jnp.arange(S), LANES)
    gain_rows = []
    for t in range(nt):
        g = gains[t] if gains[t] is not None else jnp.ones((HD,), F32)
        gain_rows.append(jnp.tile(g.astype(F32), 2))
    gain_arr = jnp.stack(gain_rows)
    in_specs = []
    for t in range(nt):
        w = n_heads[t] * HD
        in_specs.append(pl.BlockSpec((1, ts, w), functools.partial(lambda b, s, cb: (b, s, cb), cb=col_blocks[t])))
    in_specs += [pl.BlockSpec((nt, LANES), lambda b, s: (0, 0)),
                 pl.BlockSpec((ts, LANES), lambda b, s: (s, 0)),
                 pl.BlockSpec((ts, LANES), lambda b, s: (s, 0)),
                 pl.BlockSpec((LANES, LANES), lambda b, s: (0, 0))]
    out_specs = [pl.BlockSpec((1, n_heads[t], ts, HD), lambda b, s: (b, 0, s, 0)) for t in range(nt)]
    out_shape = [jax.ShapeDtypeStruct((B, n_heads[t], S, HD), BF16) for t in range(nt)]
    return pl.pallas_call(
        functools.partial(_prep_kernel, tuple(n_heads), tuple(norm_flags), tuple(scales)),
        grid=(B, S // ts),
        in_specs=in_specs,
        out_specs=out_specs,
        out_shape=out_shape,
        compiler_params=_cparams(("parallel", "parallel")),
        name="prep_heads",
    )(*([proj] * nt), gain_arr, cos, sin, _block_ones(LANES))


_N_LEVELS = 6


def _hgrn_constants():
    C = HG_CHUNK
    r = np.arange(C)
    mats = [(r[None, :] <= r[:, None]).astype(np.float32),
            (r[None, :] > r[:, None]).astype(np.float32)]
    masks = []
    for lv in range(_N_LEVELS):
        n = 1 << lv
        blk = r // (2 * n)
        right = (r // n) % 2 == 1
        bnd = blk * 2 * n + n - 1
        m = np.zeros((C, C), np.float32)
        for t in range(C):
            if right[t]:
                m[t, bnd[t] + 1:t + 1] = 1.0
            else:
                m[t, t + 1:bnd[t] + 1] = 1.0
        mats.append(m)
        masks.append((blk[:, None] == blk[None, :]).astype(np.float32))
    masks.append(np.eye(C, dtype=np.float32))
    return jnp.asarray(np.concatenate(mats, axis=0)), jnp.asarray(np.stack(masks))


def _hgrn_kernel(q_ref, f_ref, i_ref, g_ref, lb_ref, gain_ref, mall_ref, masks_ref, o_ref, st_ref):
    C = HG_CHUNK
    ts = q_ref.shape[1]

    @pl.when(pl.program_id(1) == 0)
    def _():
        st_ref[...] = jnp.zeros_like(st_ref)

    mall = mall_ref[...]
    row = lax.broadcasted_iota(I32, (C, HG_DIM), 0)
    gain = gain_ref[...]

    def chunk(c, carry):
        r0 = pl.multiple_of(c * C, C)
        for h in range(HG_HEADS):
            ls = slice(h * HG_DIM, (h + 1) * HG_DIM)
            q = q_ref[0, pl.ds(r0, C), ls]
            z = f_ref[0, pl.ds(r0, C), ls]
            v = i_ref[0, pl.ds(r0, C), ls]
            g = g_ref[0, pl.ds(r0, C), ls]
            lb = lb_ref[:, ls]
            lf = jnp.log(lb + (1.0 - lb) * jax.nn.sigmoid(z))
            k = (1.0 - lb) * jax.nn.sigmoid(-z)
            ex = jnp.exp(jnp.dot(mall, lf, precision=HI, preferred_element_type=F32))
            qs = q * ex[0:C]
            ks = k * ex[C:2 * C]
            st = st_ref[h]
            o = _dot_nt(qs.astype(BF16), st.astype(BF16))
            a = jnp.where(masks_ref[_N_LEVELS] > 0.5, _dot_nt(q.astype(BF16), k.astype(BF16)), 0.0)
            for lv in range(_N_LEVELS):
                e = ex[(2 + lv) * C:(3 + lv) * C]
                right = ((row >> lv) & 1) == 1
                qe = jnp.where(right, q * e, 0.0).astype(BF16)
                ke = jnp.where(right, 0.0, k * e).astype(BF16)
                a = a + jnp.where(masks_ref[lv] > 0.5, _dot_nt(qe, ke), 0.0)
            vb = v.astype(BF16)
            o = o + _dot(a.astype(BF16), vb)
            decay = ex[C - 1:C]
            st_ref[h] = st * decay + _dot(v.T.astype(BF16), ks.astype(BF16))
            ms = jnp.mean(o * o, axis=-1, keepdims=True)
            y = o * lax.rsqrt(ms + EPS) * gain * (g * jax.nn.sigmoid(g))
            o_ref[0, pl.ds(r0, C), ls] = y.astype(BF16)
        return carry

    lax.fori_loop(0, ts // C, chunk, 0)


def hgrn2(proj, lb, out_gain, ts=512):
    B, S, _ = proj.shape
    mall, masks = _hgrn_constants()
    W = HG_WIDTH
    return pl.pallas_call(
        _hgrn_kernel,
        grid=(B, S // ts),
        in_specs=[pl.BlockSpec((1, ts, W), lambda b, s: (b, s, 0)),
                  pl.BlockSpec((1, ts, W), lambda b, s: (b, s, 1)),
                  pl.BlockSpec((1, ts, W), lambda b, s: (b, s, 2)),
                  pl.BlockSpec((1, ts, W), lambda b, s: (b, s, 3)),
                  pl.BlockSpec((1, W), lambda b, s: (0, 0)),
                  pl.BlockSpec((1, HG_DIM), lambda b, s: (0, 0)),
                  pl.BlockSpec(mall.shape, lambda b, s: (0, 0)),
                  pl.BlockSpec(masks.shape, lambda b, s: (0, 0, 0))],
        out_specs=pl.BlockSpec((1, ts, W), lambda b, s: (b, s, 0)),
        out_shape=jax.ShapeDtypeStruct((B, S, W), BF16),
        scratch_shapes=[pltpu.VMEM((HG_HEADS, HG_DIM, HG_DIM), F32)],
        compiler_params=_cparams(("parallel", "arbitrary")),
        name="hgrn2",
    )(proj, proj, proj, proj, lb.reshape(1, W), out_gain.reshape(1, HG_DIM), mall, masks)


def _rot_matrix():
    p = np.zeros((HD, HD), np.float32)
    for l in range(HALF):
        p[l + HALF, l] = 1.0
        p[l, l + HALF] = 1.0
    return jnp.asarray(p)


def _compress_kernel(xk_ref, xv_ref, w1k_ref, w1v_ref, pek_ref, pev_ref, w2k_ref, w2v_ref,
                     gain_ref, cos_ref, sin_ref, rot_ref, ok_ref, ov_ref):
    nc = xk_ref.shape[0]
    hid = CMP_HIDDEN

    def mlp(x_ref, w1_ref, pe_ref, w2_ref):
        w1 = w1_ref[...]
        half = CMP_STRIDE * HD
        x = x_ref[...].astype(BF16)
        u0 = _dot(x, w1[:half])
        u1 = _dot(x, w1[half:])
        const = _dot(pe_ref[...].astype(BF16), w1)[0:1]
        h = u0 + pltpu.roll(u1, nc - 1, axis=0) + const
        return _dot(jax.nn.gelu(h).astype(BF16), w2_ref[...])

    yk = mlp(xk_ref, w1k_ref, pek_ref, w2k_ref)
    ms = jnp.mean(yk * yk, axis=-1, keepdims=True)
    yk = yk * lax.rsqrt(ms + EPS) * gain_ref[...]
    partner = jnp.dot(yk, rot_ref[...], precision=HI, preferred_element_type=F32)
    ok_ref[0] = (yk * cos_ref[...] + partner * sin_ref[...]).astype(BF16)
    ov_ref[0] = mlp(xv_ref, w1v_ref, pev_ref, w2v_ref).astype(BF16)


def nsa_compress(xk, xv, pe, w1, w2, k_gain, n_seq):
    nc = xk.shape[0] // n_seq
    feat = CMP_STRIDE * HD
    pos_end = jnp.arange(nc) * CMP_STRIDE + CMP_LEN - 1
    cos, sin = _rope_tables(pos_end, HD)
    pe_flat = jnp.broadcast_to(pe.reshape(2, 1, CMP_LEN * HD), (2, 8, CMP_LEN * HD))
    w1b = w1.astype(BF16)
    w2b = w2.astype(BF16)
    full = lambda shape: pl.BlockSpec(shape, lambda i: (0,) * len(shape))
    return pl.pallas_call(
        _compress_kernel,
        grid=(n_seq,),
        in_specs=[pl.BlockSpec((nc, feat), lambda i: (i, 0)),
                  pl.BlockSpec((nc, feat), lambda i: (i, 0)),
                  full((CMP_LEN * HD, CMP_HIDDEN)), full((CMP_LEN * HD, CMP_HIDDEN)),
                  full((8, CMP_LEN * HD)), full((8, CMP_LEN * HD)),
                  full((CMP_HIDDEN, HD)), full((CMP_HIDDEN, HD)),
                  full((1, HD)), full((nc, HD)), full((nc, HD)), full((HD, HD))],
        out_specs=[pl.BlockSpec((1, nc, HD), lambda i: (i, 0, 0)),
                   pl.BlockSpec((1, nc, HD), lambda i: (i, 0, 0))],
        out_shape=[jax.ShapeDtypeStruct((n_seq, nc, HD), BF16)] * 2,
        compiler_params=_cparams(("parallel",)),
        name="nsa_compress",
    )(xk, xv, w1b[0], w1b[1], pe_flat[0], pe_flat[1], w2b[0], w2b[1],
      k_gain.reshape(1, HD).astype(F32), cos, sin, _rot_matrix())


NSA_TQ = 128
NSA_TK = 512


def _softmax_rows(s, mask):
    m = jnp.max(jnp.where(mask, s, NEG), axis=-1, keepdims=True)
    p = jnp.where(mask, jnp.exp(s - m), 0.0)
    return p / jnp.maximum(jnp.sum(p, axis=-1, keepdims=True), 1e-30)


def _head_placement(n_heads):
    p = np.zeros((n_heads, HD, n_heads * HD), np.float32)
    for h in range(n_heads):
        p[h, np.arange(HD), h * HD + np.arange(HD)] = 1.0
    return jnp.asarray(p, dtype=BF16)


def _nsa_kernel(q_ref, kc_ref, vc_ref, ks_ref, vs_ref, kw_ref, vw_ref, gate_ref, cover_ref, gexp_ref, place_ref,
                o_ref, m_ref, l_ref, acc_ref):
    TQ, TK, H = NSA_TQ, NSA_TK, NSA_HPG
    R = H * TQ
    i = pl.program_id(2)
    t0 = i * TQ
    q = q_ref[0].reshape(R, HD)
    tq = t0 + lax.broadcasted_iota(I32, (TQ, 1), 0)

    kc = kc_ref[0]
    ncmp = kc.shape[0]
    s = _dot_nt(q, kc).reshape(H, TQ, ncmp)
    cmp_end = lax.broadcasted_iota(I32, (1, ncmp), 1) * CMP_STRIDE + (CMP_LEN - 1)
    p_c = _softmax_rows(s, (cmp_end <= tq)[None])
    o_c = _dot(p_c.reshape(R, ncmp).astype(BF16), vc_ref[0])

    imp = jnp.dot(jnp.sum(p_c, axis=0), cover_ref[...], precision=HI, preferred_element_type=F32)
    nsel = imp.shape[1]
    jj = lax.broadcasted_iota(I32, (TQ, nsel), 1)
    jf = jj.astype(F32)
    cur = tq >> 6
    allowed = jj * SEL_LEN <= tq
    forced = (jj == 0) | (jj == cur) | (jj == cur - 1)
    score = jnp.where(allowed, imp + jnp.where(forced, FORCE_BONUS, 0.0), NEG)
    sel = jnp.zeros((TQ, nsel), F32)
    for _ in range(min(SEL_TOPN, nsel)):
        mx = jnp.max(score, axis=-1, keepdims=True)
        first = jnp.min(jnp.where(score == mx, jf, float(nsel)), axis=-1, keepdims=True)
        hit = jf == first
        sel = jnp.where(hit & (mx > 0.5 * NEG), 1.0, sel)
        score = jnp.where(hit, NEG, score)
    sel_b = sel.astype(BF16)

    m_ref[...] = jnp.full(m_ref.shape, NEG, F32)
    l_ref[...] = jnp.zeros(l_ref.shape, F32)
    acc_ref[...] = jnp.zeros(acc_ref.shape, F32)
    blk_of_col = lax.broadcasted_iota(I32, (nsel, TK), 1) >> 6
    blk_row = lax.broadcasted_iota(I32, (nsel, TK), 0)
    col = lax.broadcasted_iota(I32, (1, TK), 1)

    def sweep(kt, carry):
        k0 = pl.multiple_of(kt * TK, TK)
        expand = jnp.where(blk_row == blk_of_col + kt * (TK // SEL_LEN), 1.0, 0.0).astype(BF16)
        chosen = _dot(sel_b, expand) > 0.5
        mask = (chosen & (k0 + col <= tq))[None]
        sc = _dot_nt(q, ks_ref[0, 0, pl.ds(k0, TK), :]).reshape(H, TQ, TK)
        m_old = m_ref[...]
        m_new = jnp.maximum(m_old, jnp.max(jnp.where(mask, sc, NEG), axis=-1, keepdims=True))
        p = jnp.where(mask, jnp.exp(sc - m_new), 0.0)
        alpha = jnp.exp(m_old - m_new)
        l_ref[...] = alpha * l_ref[...] + jnp.sum(p, axis=-1, keepdims=True)
        pv = _dot(p.reshape(R, TK).astype(BF16), vs_ref[0, 0, pl.ds(k0, TK), :])
        acc_ref[...] = alpha.reshape(R, 1) * acc_ref[...] + pv
        m_ref[...] = m_new
        return carry

    lax.fori_loop(0, t0 // TK + 1, sweep, 0)
    o_s = acc_ref[...] / jnp.maximum(l_ref[...].reshape(R, 1), 1e-30)

    WK = NSA_WINDOW + TQ
    ws = pl.multiple_of(jnp.maximum(t0 - NSA_WINDOW, 0), TQ)
    kp = ws + lax.broadcasted_iota(I32, (1, WK), 1)
    wmask = ((kp <= tq) & (kp > tq - NSA_WINDOW))[None]
    sw = _dot_nt(q, kw_ref[0, 0, pl.ds(ws, WK), :]).reshape(H, TQ, WK)
    p_w = _softmax_rows(sw, wmask)
    o_w = _dot(p_w.reshape(R, WK).astype(BF16), vw_ref[0, 0, pl.ds(ws, WK), :])

    ge = jnp.dot(jax.nn.sigmoid(gate_ref[0]), gexp_ref[0], precision=HI, preferred_element_type=F32)
    out = jnp.zeros((TQ, H * HD), F32)
    for h in range(H):
        rows = slice(h * TQ, (h + 1) * TQ)
        g = [ge[:, (3 * h + br) * LANES:(3 * h + br) * LANES + HD] for br in range(3)]
        o_h = g[0] * o_c[rows] + g[1] * o_s[rows] + g[2] * o_w[rows]
        out = out + _dot(o_h.astype(BF16), place_ref[h])
    o_ref[0] = out.astype(BF16)


def nsa_attention(qn, kcb, vcb, ksn, vsb, kwn, vwb, proj):
    B, _, S, _ = qn.shape
    G, H, TQ = NSA_KV, NSA_HPG, NSA_TQ
    assert S % NSA_TK == 0 and S >= NSA_WINDOW + TQ
    ncmp = kcb.shape[1]
    nsel = S // SEL_LEN
    n = np.arange(ncmp)[:, None] * CMP_STRIDE
    j = np.arange(nsel)[None, :] * SEL_LEN
    cover = jnp.asarray(((n < j + SEL_LEN) & (n + CMP_LEN > j)).astype(np.float32))
    gexp = np.zeros((G, LANES, 3 * H * LANES), np.float32)
    for g in range(G):
        for c in range(3 * H):
            gexp[g, 3 * H * g + c, c * LANES:(c + 1) * LANES] = 1.0
    R = H * TQ
    kv_spec = pl.BlockSpec((1, 1, S, HD), lambda b, g, i: (b, g, 0, 0))
    return pl.pallas_call(
        _nsa_kernel,
        grid=(B, G, S // TQ),
        in_specs=[pl.BlockSpec((1, H, TQ, HD), lambda b, g, i: (b, g, i, 0)),
                  pl.BlockSpec((1, ncmp, HD), lambda b, g, i: (b * G + g, 0, 0)),
                  pl.BlockSpec((1, ncmp, HD), lambda b, g, i: (b * G + g, 0, 0)),
                  kv_spec, kv_spec, kv_spec, kv_spec,
                  pl.BlockSpec((1, TQ, LANES), lambda b, g, i: (b, i, C_GATE // LANES)),
                  pl.BlockSpec((ncmp, nsel), lambda b, g, i: (0, 0)),
                  pl.BlockSpec((1, LANES, 3 * H * LANES), lambda b, g, i: (g, 0, 0)),
                  pl.BlockSpec((H, HD, H * HD), lambda b, g, i: (0, 0, 0))],
        out_specs=pl.BlockSpec((1, TQ, H * HD), lambda b, g, i: (b, i, g)),
        out_shape=jax.ShapeDtypeStruct((B, S, NSA_HEADS * HD), BF16),
        scratch_shapes=[pltpu.VMEM((H, TQ, 1), F32), pltpu.VMEM((H, TQ, 1), F32), pltpu.VMEM((R, HD), F32)],
        compiler_params=_cparams(("parallel", "parallel", "arbitrary")),
        name="nsa_attention",
    )(qn, kcb, vcb, ksn, vsb, kwn, vwb, proj, cover, jnp.asarray(gexp), _head_placement(H))


SWA_TQ = 128


def _swa_kernel(sink_ref, q_ref, k_ref, v_ref, place_ref, o_ref):
    TQ, H = SWA_TQ, SWA_HPG
    R = H * TQ
    g = pl.program_id(1)
    t0 = pl.program_id(2) * TQ
    q = q_ref[0].reshape(R, HD)
    tq = t0 + lax.broadcasted_iota(I32, (TQ, 1), 0)
    WK = SWA_WINDOW + TQ
    ws = pl.multiple_of(jnp.maximum(t0 - SWA_WINDOW, 0), TQ)
    kp = ws + lax.broadcasted_iota(I32, (1, WK), 1)
    mask = ((kp <= tq) & (kp > tq - SWA_WINDOW))[None]
    s = _dot_nt(q, k_ref[0, 0, pl.ds(ws, WK), :]).reshape(H, TQ, WK)
    head = lax.broadcasted_iota(I32, (H, TQ, 1), 0)
    sink = jnp.zeros((H, TQ, 1), F32)
    for h in range(H):
        sink = jnp.where(head == h, sink_ref[g * H + h], sink)
    m = jnp.maximum(jnp.max(jnp.where(mask, s, NEG), axis=-1, keepdims=True), sink)
    p = jnp.where(mask, jnp.exp(s - m), 0.0)
    den = jnp.sum(p, axis=-1, keepdims=True) + jnp.exp(sink - m)
    o = _dot(p.reshape(R, WK).astype(BF16), v_ref[0, 0, pl.ds(ws, WK), :]) / den.reshape(R, 1)
    out = jnp.zeros((TQ, H * HD), F32)
    for h in range(H):
        out = out + _dot(o[h * TQ:(h + 1) * TQ].astype(BF16), place_ref[h])
    o_ref[0] = out.astype(BF16)


def swa_attention(qn, kn, vb, sinks):
    B, _, S, _ = qn.shape
    G, H, TQ = SWA_KV, SWA_HPG, SWA_TQ
    assert S >= SWA_WINDOW + TQ
    kv_spec = pl.BlockSpec((1, 1, S, HD), lambda b, g, i, sk: (b, g, 0, 0))
    return pl.pallas_call(
        _swa_kernel,
        grid_spec=pltpu.PrefetchScalarGridSpec(
            num_scalar_prefetch=1,
            grid=(B, G, S // TQ),
            in_specs=[pl.BlockSpec((1, H, TQ, HD), lambda b, g, i, sk: (b, g, i, 0)), kv_spec, kv_spec,
                      pl.BlockSpec((H, HD, H * HD), lambda b, g, i, sk: (0, 0, 0))],
            out_specs=pl.BlockSpec((1, TQ, H * HD), lambda b, g, i, sk: (b, i, g))),
        out_shape=jax.ShapeDtypeStruct((B, S, SWA_HEADS * HD), BF16),
        compiler_params=_cparams(("parallel", "parallel", "arbitrary")),
        name="swa_attention",
    )(sinks.astype(F32), qn, kn, vb, _head_placement(H))


R_GRP = 0
R_EXP = 32


def _router_kernel(x_ref, g_ref, w_ref, b_ref, hn_ref, route_ref):
    x = x_ref[...]
    ms = jnp.mean(x * x, axis=-1, keepdims=True)
    h = x * lax.rsqrt(ms + EPS) * g_ref[...]
    hn_ref[...] = h
    logits = jnp.dot(h, w_ref[...], precision=HI, preferred_element_type=F32) + b_ref[...]
    lane = lax.broadcasted_iota(I32, logits.shape, 1)
    is_grp = lane < N_GROUPS
    gl = jnp.where(is_grp, logits, NEG)
    gmax = jnp.max(gl, axis=-1, keepdims=True)
    lanef = lane.astype(F32)
    gidx = jnp.min(jnp.where(gl == gmax, lanef, float(LANES)), axis=-1, keepdims=True)
    gw = 1.0 / jnp.sum(jnp.where(is_grp, jnp.exp(logits - gmax), 0.0), axis=-1, keepdims=True)
    in_grp = ((lane >> 3) - R_EXP // EXP_PER_GROUP).astype(F32) == gidx
    w0 = jnp.where(in_grp, logits, NEG)
    m1 = jnp.max(w0, axis=-1, keepdims=True)
    i1 = jnp.min(jnp.where(w0 == m1, lanef, float(LANES)), axis=-1, keepdims=True)
    w1 = jnp.where(lanef == i1, NEG, w0)
    m2 = jnp.max(w1, axis=-1, keepdims=True)
    i2 = jnp.min(jnp.where(w1 == m2, lanef, float(LANES)), axis=-1, keepdims=True)
    e2 = jnp.exp(m2 - m1)
    p1 = gw / (1.0 + e2)
    p2 = gw * e2 / (1.0 + e2)
    route = jnp.where(lane == 0, i1 - R_EXP,
                      jnp.where(lane == 1, i2 - R_EXP,
                                jnp.where(lane == 2, p1, jnp.where(lane == 3, p2, 0.0))))
    route_ref[...] = route


def moe_router(x, gain, w_grp, b_grp, w_exp, b_exp, tm=256):
    T, D = x.shape
    w = jnp.zeros((D, LANES), F32).at[:, R_GRP:R_GRP + N_GROUPS].set(w_grp).at[:, R_EXP:R_EXP + N_EXPERTS].set(w_exp)
    b = jnp.zeros((1, LANES), F32).at[0, R_GRP:R_GRP + N_GROUPS].set(b_grp).at[0, R_EXP:R_EXP + N_EXPERTS].set(b_exp)
    return pl.pallas_call(
        _router_kernel,
        grid=(T // tm,),
        in_specs=[pl.BlockSpec((tm, D), lambda i: (i, 0)),
                  pl.BlockSpec((1, D), lambda i: (0, 0)),
                  pl.BlockSpec((D, LANES), lambda i: (0, 0)),
                  pl.BlockSpec((1, LANES), lambda i: (0, 0))],
        out_specs=[pl.BlockSpec((tm, D), lambda i: (i, 0)),
                   pl.BlockSpec((tm, LANES), lambda i: (i, 0))],
        out_shape=[jax.ShapeDtypeStruct((T, D), F32), jax.ShapeDtypeStruct((T, LANES), F32)],
        compiler_params=_cparams(("parallel",)),
        name="moe_router",
    )(x, gain.reshape(1, D), w, b)


def _row_copy(src, src_row, dst, dst_row, sem):
    return pltpu.make_async_copy(src.at[pl.ds(src_row, 1)], dst.at[pl.ds(dst_row, 1)], sem)


def _expert_kernel(be_ref, rt_ref, ro_ref, hn_ref, wg_ref, wu_ref, wd_ref, out_ref, xbuf, ybuf, sem_in, sem_out):
    del be_ref

    def gather(r, c):
        _row_copy(hn_ref, rt_ref[0, 0, r], xbuf, r, sem_in).start()
        return c

    lax.fori_loop(0, MOE_BLOCK, gather, 0)

    def gather_wait(r, c):
        _row_copy(hn_ref, 0, xbuf, r, sem_in).wait()
        return c

    lax.fori_loop(0, MOE_BLOCK, gather_wait, 0)
    x = xbuf[...].astype(BF16)
    hid = _dot(x, wg_ref[0])
    hid = hid * jax.nn.sigmoid(hid) * _dot(x, wu_ref[0])
    ybuf[...] = _dot(hid.astype(BF16), wd_ref[0])

    def scatter(r, c):
        _row_copy(ybuf, r, out_ref, ro_ref[0, 0, r], sem_out).start()
        return c

    lax.fori_loop(0, MOE_BLOCK, scatter, 0)

    def scatter_wait(r, c):
        _row_copy(ybuf, r, out_ref, 0, sem_out).wait()
        return c

    lax.fori_loop(0, MOE_BLOCK, scatter_wait, 0)


def moe_experts(hn, blk_expert, row_tok, row_out, w_gate, w_up, w_down, n_rows):
    T, D = hn.shape
    n_blocks = n_rows // MOE_BLOCK
    wspec = lambda shape: pl.BlockSpec((1,) + shape, lambda i, be: (be[i], 0, 0))
    ispec = pl.BlockSpec((1, 1, MOE_BLOCK), lambda i, be: (i, 0, 0), memory_space=pltpu.SMEM)
    row_tok = row_tok.reshape(n_blocks, 1, MOE_BLOCK)
    row_out = row_out.reshape(n_blocks, 1, MOE_BLOCK)
    return pl.pallas_call(
        _expert_kernel,
        grid_spec=pltpu.PrefetchScalarGridSpec(
            num_scalar_prefetch=1,
            grid=(n_blocks,),
            in_specs=[ispec, ispec, pl.BlockSpec(memory_space=pl.ANY),
                      wspec((D, EXPERT_FF)), wspec((D, EXPERT_FF)), wspec((EXPERT_FF, D))],
            out_specs=pl.BlockSpec(memory_space=pl.ANY),
            scratch_shapes=[pltpu.VMEM((MOE_BLOCK, D), F32), pltpu.VMEM((MOE_BLOCK, D), F32),
                            pltpu.SemaphoreType.DMA, pltpu.SemaphoreType.DMA]),
        out_shape=jax.ShapeDtypeStruct((n_rows, D), F32),
        compiler_params=_cparams(("arbitrary",)),
        name="moe_experts",
    )(blk_expert, row_tok, row_out, hn, w_gate, w_up, w_down)


def _combine_kernel(x_ref, y_ref, route_ref, o_ref):
    D = x_ref.shape[1]
    r = route_ref[...]
    o_ref[...] = x_ref[...] + r[:, 2:3] * y_ref[:, :D] + r[:, 3:4] * y_ref[:, D:]


def moe_combine(x, y2, route, tm=512):
    T, D = x.shape
    return pl.pallas_call(
        _combine_kernel,
        grid=(T // tm,),
        in_specs=[pl.BlockSpec((tm, D), lambda i: (i, 0)),
                  pl.BlockSpec((tm, 2 * D), lambda i: (i, 0)),
                  pl.BlockSpec((tm, LANES), lambda i: (i, 0))],
        out_specs=pl.BlockSpec((tm, D), lambda i: (i, 0)),
        out_shape=jax.ShapeDtypeStruct((T, D), F32),
        compiler_params=_cparams(("parallel",)),
        name="moe_combine",
    )(x, y2, route)


def hier_moe_block(x, gain, w_grp, b_grp, w_exp, b_exp, w_gate, w_up, w_down):
    T, D = x.shape
    hn, route = moe_router(x, gain, w_grp, b_grp, w_exp, b_exp)
    A = 2 * T
    flat_e = route[:, 0:2].astype(I32).reshape(A)
    onehot = (flat_e[:, None] == jnp.arange(N_EXPERTS, dtype=I32)[None, :]).astype(I32)
    csum = jnp.cumsum(onehot, axis=0)
    rank = jnp.sum(csum * onehot, axis=1) - 1
    counts = csum[-1]
    pcounts = (counts + MOE_BLOCK - 1) // MOE_BLOCK * MOE_BLOCK
    pend = jnp.cumsum(pcounts)
    pstart = pend - pcounts
    dest = pstart[flat_e] + rank
    n_rows = -(-A // MOE_BLOCK) * MOE_BLOCK + N_EXPERTS * MOE_BLOCK
    n_blocks = n_rows // MOE_BLOCK
    assign = jnp.arange(A, dtype=I32)
    is_pad = jnp.ones((n_rows,), I32).at[dest].set(0)
    pad_slot = A + jnp.cumsum(is_pad) - 1
    row_tok = jnp.zeros((n_rows,), I32).at[dest].set(assign // 2)
    row_out = jnp.where(is_pad == 1, pad_slot, jnp.zeros((n_rows,), I32).at[dest].set(assign)).astype(I32)
    blk_expert = jnp.minimum(
        jnp.searchsorted(pend, jnp.arange(n_blocks, dtype=I32) * MOE_BLOCK, side='right'), N_EXPERTS - 1).astype(I32)
    y = moe_experts(hn, blk_expert, row_tok, row_out,
                    w_gate.astype(BF16), w_up.astype(BF16), w_down.astype(BF16), n_rows)
    return moe_combine(x, y.reshape(n_rows // 2, 2 * D), route)


def _even_w_in(w_in):
    pad = jnp.zeros((w_in.shape[0], N0 - w_in.shape[1]), w_in.dtype)
    return jnp.concatenate([w_in, pad], axis=1).astype(BF16)


def even_layer(x, B, S, norm, w_in, lb, hg_gain, q_gain, k_gain, cmp_pe, cmp_w1, cmp_w2, w_out):
    T = B * S
    proj = rms_matmul(x, norm, _even_w_in(w_in), tn=N0 // 3).reshape(B, S, N0)
    a_out = hgrn2(proj, lb, hg_gain)
    kvb = C_KV // LANES
    qn, ksn, kwn, vsb, vwb = prep_heads(
        proj, [C_NQ // (NSA_HEADS * HD), kvb + 2, kvb + 4, kvb + 3, kvb + 5],
        [NSA_HEADS, NSA_KV, NSA_KV, NSA_KV, NSA_KV],
        [q_gain, k_gain[1], k_gain[2], None, None],
        [True, True, True, False, False], [HD ** -0.5, 1.0, 1.0, 1.0, 1.0], S)
    nc = S // CMP_STRIDE

    def chunk_flat(c0):
        t = proj[:, :, c0:c0 + NSA_KV * HD].reshape(B, nc, CMP_STRIDE, NSA_KV, HD)
        return t.transpose(0, 3, 1, 2, 4).reshape(B * NSA_KV * nc, CMP_STRIDE * HD)

    kcb, vcb = nsa_compress(chunk_flat(C_KV), chunk_flat(C_KV + LANES), cmp_pe, cmp_w1, cmp_w2,
                            k_gain[0], B * NSA_KV)
    b_out = nsa_attention(qn, kcb, vcb, ksn, vsb, kwn, vwb, proj)
    return proj_res(a_out.reshape(T, HG_WIDTH), 0, b_out.reshape(T, NSA_HEADS * HD), 0, w_out.astype(BF16), x)


def odd_layer(x, B, S, norm, w_in, q_gain, k_gain, sinks, w_out):
    T = B * S
    proj = rms_matmul(x, norm, w_in.astype(BF16), tn=N1 // 2).reshape(B, S, N1)
    qd = SWA_HEADS * HD
    qn, kn, vb = prep_heads(
        proj, [0, qd // LANES, qd // LANES + 1], [SWA_HEADS, SWA_KV, SWA_KV],
        [q_gain, k_gain, None], [True, True, False], [HD ** -0.5, 1.0, 1.0], S)
    att = swa_attention(qn, kn, vb, sinks).reshape(T, qd)
    return proj_res(att, 0, att, 1, w_out.astype(BF16), x)


def kernel(x, ev_norm, ev_w_in, hg_lb_logits, hg_out_gain, nsa_q_gain, nsa_k_gain, nsa_cmp_pe, nsa_cmp_w1,
           nsa_cmp_w2, ev_w_out, od_norm, od_w_in, swa_q_gain, swa_k_gain, swa_sinks, od_w_out, moe_norm,
           moe_w_grp, moe_b_grp, moe_w_exp, moe_b_exp, moe_w_gate, moe_w_up, moe_w_down):
    B, S, D = x.shape
    lower_bounds = jnp.cumsum(jax.nn.softmax(hg_lb_logits.astype(F32), axis=0), axis=0)
    h = x.reshape(B * S, D)
    h = even_layer(h, B, S, ev_norm[0], ev_w_in[0], lower_bounds[0], hg_out_gain[0], nsa_q_gain[0],
                   nsa_k_gain[0], nsa_cmp_pe[0], nsa_cmp_w1[0], nsa_cmp_w2[0], ev_w_out[0])
    h = hier_moe_block(h, moe_norm[0], moe_w_grp[0], moe_b_grp[0], moe_w_exp[0], moe_b_exp[0],
                       moe_w_gate[0], moe_w_up[0], moe_w_down[0])
    h = odd_layer(h, B, S, od_norm[0], od_w_in[0], swa_q_gain[0], swa_k_gain[0], swa_sinks[0], od_w_out[0])
    h = hier_moe_block(h, moe_norm[1], moe_w_grp[1], moe_b_grp[1], moe_w_exp[1], moe_b_exp[1],
                       moe_w_gate[1], moe_w_up[1], moe_w_down[1])
    return h.reshape(B, S, D)
```

```python
import functools

import numpy as np
import jax
import jax.numpy as jnp
from jax import lax
from jax.experimental import pallas as pl
from jax.experimental.pallas import tpu as pltpu

F32 = jnp.float32
BF16 = jnp.bfloat16
I32 = jnp.int32
HI = lax.Precision.HIGHEST

D_MODEL = 1024
EPS = 1e-6
ROPE_THETA = 500000.0
HD = 64
ROT = HD // 4
HALF = ROT // 2

HG_DIM = 128
HG_WIDTH = 512
HG_HEADS = 4
HG_CHUNK = 64

NSA_HEADS = 8
NSA_KV = 2
NSA_HPG = NSA_HEADS // NSA_KV
CMP_LEN = 32
CMP_STRIDE = 16
CMP_HIDDEN = 256
SEL_LEN = 64
SEL_TOPN = 8
NSA_WINDOW = 512
FORCE_BONUS = 1e4

SWA_HEADS = 16
SWA_KV = 2
SWA_HPG = SWA_HEADS // SWA_KV
SWA_WINDOW = 128

N_GROUPS = 4
EXP_PER_GROUP = 8
N_EXPERTS = 32
EXPERT_FF = 512
MOE_BLOCK = 256

LANES = 128
NEG = -1e30
VMEM_LIMIT = 56 * 1024 * 1024

C_HG = 0
C_NQ = 2048
C_KV = 2560
C_GATE = 3328
N0 = 3584
N1 = 1280


def _cparams(sem):
    return pltpu.CompilerParams(dimension_semantics=sem, vmem_limit_bytes=VMEM_LIMIT)


def _dot(a, b):
    return jnp.dot(a, b, preferred_element_type=F32)


def _dot_nt(a, b):
    return lax.dot_general(a, b, (((1,), (1,)), ((), ())), preferred_element_type=F32)


def _rms_matmul_kernel(x_ref, g_ref, w_ref, o_ref, xn_ref):
    @pl.when(pl.program_id(1) == 0)
    def _():
        x = x_ref[...]
        ms = jnp.mean(x * x, axis=-1, keepdims=True)
        xn_ref[...] = (x * lax.rsqrt(ms + EPS) * g_ref[...]).astype(BF16)

    o_ref[...] = _dot(xn_ref[...], w_ref[...])


def rms_matmul(x, gain, w, tm=512, tn=None):
    T, D = x.shape
    N = w.shape[1]
    tn = N if tn is None else tn
    return pl.pallas_call(
        _rms_matmul_kernel,
        grid=(T // tm, N // tn),
        in_specs=[pl.BlockSpec((tm, D), lambda i, j: (i, 0)),
                  pl.BlockSpec((1, D), lambda i, j: (0, 0)),
                  pl.BlockSpec((D, tn), lambda i, j: (0, j))],
        out_specs=pl.BlockSpec((tm, tn), lambda i, j: (i, j)),
        out_shape=jax.ShapeDtypeStruct((T, N), F32),
        scratch_shapes=[pltpu.VMEM((tm, D), BF16)],
        compiler_params=_cparams(("parallel", "arbitrary")),
        name="rms_matmul",
    )(x, gain.reshape(1, D), w)


def _proj_res_kernel(a_ref, b_ref, wa_ref, wb_ref, r_ref, o_ref):
    acc = _dot(a_ref[...], wa_ref[...]) + _dot(b_ref[...], wb_ref[...])
    o_ref[...] = r_ref[...] + acc


def proj_res(a, a_blk, b, b_blk, w, res, tm=512):
    T, D = res.shape
    K = w.shape[0]
    kh = K // 2
    return pl.pallas_call(
        _proj_res_kernel,
        grid=(T // tm,),
        in_specs=[pl.BlockSpec((tm, kh), lambda i: (i, a_blk)),
                  pl.BlockSpec((tm, kh), lambda i: (i, b_blk)),
                  pl.BlockSpec((kh, D), lambda i: (0, 0)),
                  pl.BlockSpec((kh, D), lambda i: (1, 0)),
                  pl.BlockSpec((tm, D), lambda i: (i, 0))],
        out_specs=pl.BlockSpec((tm, D), lambda i: (i, 0)),
        out_shape=jax.ShapeDtypeStruct((T, D), F32),
        compiler_params=_cparams(("parallel",)),
        name="proj_res",
    )(a, b, w, w, res)


def _rope_tables(pos, width):
    inv = jnp.power(ROPE_THETA, -jnp.arange(HALF, dtype=F32) * 2.0 / ROT)
    ang = pos.astype(F32)[:, None] * inv[None, :]
    cos, sin = jnp.cos(ang), jnp.sin(ang)
    n = pos.shape[0]
    c = jnp.concatenate([cos, cos, jnp.ones((n, HD - ROT), F32)], axis=1)
    s = jnp.concatenate([-sin, sin, jnp.zeros((n, HD - ROT), F32)], axis=1)
    reps = width // HD
    return jnp.tile(c, (1, reps)), jnp.tile(s, (1, reps))


def _block_ones(width):
    idx = np.arange(width) // HD
    return jnp.asarray((idx[:, None] == idx[None, :]).astype(np.float32))


def _head_norm_rope(x, gain, cos, sin, bones, scale):
    ms = jnp.dot(x * x, bones, precision=HI, preferred_element_type=F32) * (1.0 / HD)
    y = x * lax.rsqrt(ms + EPS) * gain
    lane = lax.broadcasted_iota(I32, y.shape, 1) % HD
    partner = jnp.where(lane < HALF, pltpu.roll(y, LANES - HALF, axis=1), pltpu.roll(y, HALF, axis=1))
    out = y * cos + partner * sin
    return out * scale if scale != 1.0 else out


def _prep_kernel(n_heads, norm_flags, scales, *refs):
    nt = len(n_heads)
    x_refs = refs[:nt]
    gain_ref, cos_ref, sin_ref, bones_ref = refs[nt:nt + 4]
    o_refs = refs[nt + 4:]
    cos = cos_ref[...]
    sin = sin_ref[...]
    bones = bones_ref[...]
    for t in range(nt):
        for c in range(n_heads[t] // 2):
            x = x_refs[t][0, :, c * LANES:(c + 1) * LANES]
            if norm_flags[t]:
                g = gain_ref[t:t + 1, :]
                x = _head_norm_rope(x, g, cos, sin, bones, scales[t])
            o_refs[t][0, 2 * c] = x[:, :HD].astype(BF16)
            o_refs[t][0, 2 * c + 1] = pltpu.roll(x, HD, axis=1)[:, :HD].astype(BF16)


def prep_heads(proj, col_blocks, n_heads, gains, norm_flags, scales, S, ts=512):
    B = proj.shape[0]
    nt = len(n_heads)
    cos, sin = _rope_tables(jnp.arange(S), LANES)
    gain_rows = []
    for t in range(nt):
        g = gains[t] if gains[t] is not None else jnp.ones((HD,), F32)
        gain_rows.append(jnp.tile(g.astype(F32), 2))
    gain_arr = jnp.stack(gain_rows)
    in_specs = []
    for t in range(nt):
        w = n_heads[t] * HD
        in_specs.append(pl.BlockSpec((1, ts, w), functools.partial(lambda b, s, cb: (b, s, cb), cb=col_blocks[t])))
    in_specs += [pl.BlockSpec((nt, LANES), lambda b, s: (0, 0)),
                 pl.BlockSpec((ts, LANES), lambda b, s: (s, 0)),
                 pl.BlockSpec((ts, LANES), lambda b, s: (s, 0)),
                 pl.BlockSpec((LANES, LANES), lambda b, s: (0, 0))]
    out_specs = [pl.BlockSpec((1, n_heads[t], ts, HD), lambda b, s: (b, 0, s, 0)) for t in range(nt)]
    out_shape = [jax.ShapeDtypeStruct((B, n_heads[t], S, HD), BF16) for t in range(nt)]
    return pl.pallas_call(
        functools.partial(_prep_kernel, tuple(n_heads), tuple(norm_flags), tuple(scales)),
        grid=(B, S // ts),
        in_specs=in_specs,
        out_specs=out_specs,
        out_shape=out_shape,
        compiler_params=_cparams(("parallel", "parallel")),
        name="prep_heads",
    )(*([proj] * nt), gain_arr, cos, sin, _block_ones(LANES))


_N_LEVELS = 6


def _hgrn_constants():
    C = HG_CHUNK
    r = np.arange(C)
    mats = [(r[None, :] <= r[:, None]).astype(np.float32),
            (r[None, :] > r[:, None]).astype(np.float32)]
    masks = []
    for lv in range(_N_LEVELS):
        n = 1 << lv
        blk = r // (2 * n)
        right = (r // n) % 2 == 1
        bnd = blk * 2 * n + n - 1
        m = np.zeros((C, C), np.float32)
        for t in range(C):
            if right[t]:
                m[t, bnd[t] + 1:t + 1] = 1.0
            else:
                m[t, t + 1:bnd[t] + 1] = 1.0
        mats.append(m)
        masks.append((blk[:, None] == blk[None, :]).astype(np.float32))
    masks.append(np.eye(C, dtype=np.float32))
    return jnp.asarray(np.concatenate(mats, axis=0)), jnp.asarray(np.stack(masks))


def _hgrn_kernel(q_ref, f_ref, i_ref, g_ref, lb_ref, gain_ref, mall_ref, masks_ref, o_ref, st_ref):
    C = HG_CHUNK
    ts = q_ref.shape[1]

    @pl.when(pl.program_id(1) == 0)
    def _():
        st_ref[...] = jnp.zeros_like(st_ref)

    mall = mall_ref[...]
    row = lax.broadcasted_iota(I32, (C, HG_DIM), 0)
    gain = gain_ref[...]

    def chunk(c, carry):
        r0 = pl.multiple_of(c * C, C)
        for h in range(HG_HEADS):
            ls = slice(h * HG_DIM, (h + 1) * HG_DIM)
            q = q_ref[0, pl.ds(r0, C), ls]
            z = f_ref[0, pl.ds(r0, C), ls]
            v = i_ref[0, pl.ds(r0, C), ls]
            g = g_ref[0, pl.ds(r0, C), ls]
            lb = lb_ref[:, ls]
            lf = jnp.log(lb + (1.0 - lb) * jax.nn.sigmoid(z))
            k = (1.0 - lb) * jax.nn.sigmoid(-z)
            ex = jnp.exp(jnp.dot(mall, lf, precision=HI, preferred_element_type=F32))
            qs = q * ex[0:C]
            ks = k * ex[C:2 * C]
            st = st_ref[h]
            o = _dot_nt(qs.astype(BF16), st.astype(BF16))
            a = jnp.where(masks_ref[_N_LEVELS] > 0.5, _dot_nt(q.astype(BF16), k.astype(BF16)), 0.0)
            for lv in range(_N_LEVELS):
                e = ex[(2 + lv) * C:(3 + lv) * C]
                right = ((row >> lv) & 1) == 1
                qe = jnp.where(right, q * e, 0.0).astype(BF16)
                ke = jnp.where(right, 0.0, k * e).astype(BF16)
                a = a + jnp.where(masks_ref[lv] > 0.5, _dot_nt(qe, ke), 0.0)
            vb = v.astype(BF16)
            o = o + _dot(a.astype(BF16), vb)
            decay = ex[C - 1:C]
            st_ref[h] = st * decay + _dot(v.T.astype(BF16), ks.astype(BF16))
            ms = jnp.mean(o * o, axis=-1, keepdims=True)
            y = o * lax.rsqrt(ms + EPS) * gain * (g * jax.nn.sigmoid(g))
            o_ref[0, pl.ds(r0, C), ls] = y.astype(BF16)
        return carry

    lax.fori_loop(0, ts // C, chunk, 0)


def hgrn2(proj, lb, out_gain, ts=512):
    B, S, _ = proj.shape
    mall, masks = _hgrn_constants()
    W = HG_WIDTH
    return pl.pallas_call(
        _hgrn_kernel,
        grid=(B, S // ts),
        in_specs=[pl.BlockSpec((1, ts, W), lambda b, s: (b, s, 0)),
                  pl.BlockSpec((1, ts, W), lambda b, s: (b, s, 1)),
                  pl.BlockSpec((1, ts, W), lambda b, s: (b, s, 2)),
                  pl.BlockSpec((1, ts, W), lambda b, s: (b, s, 3)),
                  pl.BlockSpec((1, W), lambda b, s: (0, 0)),
                  pl.BlockSpec((1, HG_DIM), lambda b, s: (0, 0)),
                  pl.BlockSpec(mall.shape, lambda b, s: (0, 0)),
                  pl.BlockSpec(masks.shape, lambda b, s: (0, 0, 0))],
        out_specs=pl.BlockSpec((1, ts, W), lambda b, s: (b, s, 0)),
        out_shape=jax.ShapeDtypeStruct((B, S, W), BF16),
        scratch_shapes=[pltpu.VMEM((HG_HEADS, HG_DIM, HG_DIM), F32)],
        compiler_params=_cparams(("parallel", "arbitrary")),
        name="hgrn2",
    )(proj, proj, proj, proj, lb.reshape(1, W), out_gain.reshape(1, HG_DIM), mall, masks)


def _compress_kernel(xk_ref, xv_ref, w1k_ref, w1v_ref, pek_ref, pev_ref, w2k_ref, w2v_ref,
                     gain_ref, cos_ref, sin_ref, bones_ref, ok_ref, ov_ref):
    nc = xk_ref.shape[1] // CMP_STRIDE

    def mlp(x_ref, w1_ref, pe_ref, w2_ref):
        width = NSA_KV * CMP_HIDDEN
        u0 = jnp.zeros((nc, width), F32)
        u1 = jnp.zeros((nc, width), F32)
        const = jnp.zeros((8, width), F32)
        for p in range(CMP_STRIDE):
            xp = x_ref[0, pl.ds(p, nc, stride=CMP_STRIDE), :].astype(BF16)
            u0 = u0 + _dot(xp, w1_ref[p])
            u1 = u1 + _dot(xp, w1_ref[CMP_STRIDE + p])
            const = const + _dot(pe_ref[p].astype(BF16), w1_ref[p])
            const = const + _dot(pe_ref[CMP_STRIDE + p].astype(BF16), w1_ref[CMP_STRIDE + p])
        h = u0 + pltpu.roll(u1, nc - 1, axis=0) + const[0:1]
        return _dot(jax.nn.gelu(h).astype(BF16), w2_ref[...])

    yk = _head_norm_rope(mlp(xk_ref, w1k_ref, pek_ref, w2k_ref), gain_ref[...], cos_ref[...], sin_ref[...],
                         bones_ref[...], 1.0)
    yv = mlp(xv_ref, w1v_ref, pev_ref, w2v_ref)
    for y, o_ref in ((yk, ok_ref), (yv, ov_ref)):
        o_ref[0] = y[:, :HD].astype(BF16)
        o_ref[1] = pltpu.roll(y, HD, axis=1)[:, :HD].astype(BF16)


def _group_diag(w):
    z = jnp.zeros_like(w)
    return jnp.concatenate([jnp.concatenate([w, z], axis=-1), jnp.concatenate([z, w], axis=-1)], axis=-2)


def nsa_compress(proj, kc_blk, vc_blk, pe, w1, w2, k_gain):
    B, S, _ = proj.shape
    nc = S // CMP_STRIDE
    pos_end = jnp.arange(nc) * CMP_STRIDE + CMP_LEN - 1
    cos, sin = _rope_tables(pos_end, LANES)
    w1d = _group_diag(w1.reshape(2, CMP_LEN, HD, CMP_HIDDEN)).astype(BF16)
    w2d = _group_diag(w2).astype(BF16)
    pe2 = jnp.broadcast_to(jnp.tile(pe, (1, 1, NSA_KV))[:, :, None, :], (2, CMP_LEN, 8, LANES))
    full = lambda shape: pl.BlockSpec(shape, lambda b: (0,) * len(shape))
    w1_shape = (CMP_LEN, LANES, NSA_KV * CMP_HIDDEN)
    return pl.pallas_call(
        _compress_kernel,
        grid=(B,),
        in_specs=[pl.BlockSpec((1, S, LANES), lambda b: (b, 0, kc_blk)),
                  pl.BlockSpec((1, S, LANES), lambda b: (b, 0, vc_blk)),
                  full(w1_shape), full(w1_shape),
                  full((CMP_LEN, 8, LANES)), full((CMP_LEN, 8, LANES)),
                  full((NSA_KV * CMP_HIDDEN, LANES)), full((NSA_KV * CMP_HIDDEN, LANES)),
                  full((1, LANES)), full((nc, LANES)), full((nc, LANES)), full((LANES, LANES))],
        out_specs=[pl.BlockSpec((NSA_KV, nc, HD), lambda b: (b, 0, 0)),
                   pl.BlockSpec((NSA_KV, nc, HD), lambda b: (b, 0, 0))],
        out_shape=[jax.ShapeDtypeStruct((B * NSA_KV, nc, HD), BF16)] * 2,
        compiler_params=_cparams(("parallel",)),
        name="nsa_compress",
    )(proj, proj, w1d[0], w1d[1], pe2[0], pe2[1], w2d[0], w2d[1],
      jnp.tile(k_gain.astype(F32), 2).reshape(1, LANES), cos, sin, _block_ones(LANES))


NSA_TQ = 128
NSA_TK = 512


def _softmax_rows(s, mask):
    m = jnp.max(jnp.where(mask, s, NEG), axis=-1, keepdims=True)
    p = jnp.where(mask, jnp.exp(s - m), 0.0)
    return p / jnp.maximum(jnp.sum(p, axis=-1, keepdims=True), 1e-30)


def _head_placement(n_heads):
    p = np.zeros((n_heads, HD, n_heads * HD), np.float32)
    for h in range(n_heads):
        p[h, np.arange(HD), h * HD + np.arange(HD)] = 1.0
    return jnp.asarray(p, dtype=BF16)


def _nsa_kernel(q_ref, kc_ref, vc_ref, ks_ref, vs_ref, kw_ref, vw_ref, gate_ref, cover_ref, place_ref,
                o_ref, m_ref, l_ref, acc_ref):
    TQ, TK, H = NSA_TQ, NSA_TK, NSA_HPG
    R = H * TQ
    i = pl.program_id(2)
    t0 = i * TQ
    q = q_ref[0].reshape(R, HD)
    tq = t0 + lax.broadcasted_iota(I32, (TQ, 1), 0)

    kc = kc_ref[0]
    ncmp = kc.shape[0]
    s = _dot_nt(q, kc).reshape(H, TQ, ncmp)
    cmp_end = lax.broadcasted_iota(I32, (1, ncmp), 1) * CMP_STRIDE + (CMP_LEN - 1)
    p_c = _softmax_rows(s, (cmp_end <= tq)[None])
    o_c = _dot(p_c.reshape(R, ncmp).astype(BF16), vc_ref[0])

    imp = jnp.dot(jnp.sum(p_c, axis=0), cover_ref[...], precision=HI, preferred_element_type=F32)
    nsel = imp.shape[1]
    jj = lax.broadcasted_iota(I32, (TQ, nsel), 1)
    jf = jj.astype(F32)
    cur = tq >> 6
    allowed = jj * SEL_LEN <= tq
    forced = (jj == 0) | (jj == cur) | (jj == cur - 1)
    score = jnp.where(allowed, imp + jnp.where(forced, FORCE_BONUS, 0.0), NEG)
    sel = jnp.zeros((TQ, nsel), F32)
    for _ in range(min(SEL_TOPN, nsel)):
        mx = jnp.max(score, axis=-1, keepdims=True)
        first = jnp.min(jnp.where(score == mx, jf, float(nsel)), axis=-1, keepdims=True)
        hit = jf == first
        sel = jnp.where(hit & (mx > 0.5 * NEG), 1.0, sel)
        score = jnp.where(hit, NEG, score)
    sel_b = sel.astype(BF16)

    m_ref[...] = jnp.full(m_ref.shape, NEG, F32)
    l_ref[...] = jnp.zeros(l_ref.shape, F32)
    acc_ref[...] = jnp.zeros(acc_ref.shape, F32)
    blk_of_col = lax.broadcasted_iota(I32, (nsel, TK), 1) >> 6
    blk_row = lax.broadcasted_iota(I32, (nsel, TK), 0)
    col = lax.broadcasted_iota(I32, (1, TK), 1)

    def sweep(kt, carry):
        k0 = pl.multiple_of(kt * TK, TK)
        expand = jnp.where(blk_row == blk_of_col + kt * (TK // SEL_LEN), 1.0, 0.0).astype(BF16)
        chosen = _dot(sel_b, expand) > 0.5
        bias = jnp.where(chosen & (k0 + col <= tq), 0.0, NEG)[None]
        sc = _dot_nt(q, ks_ref[0, 0, pl.ds(k0, TK), :]).reshape(H, TQ, TK) + bias
        m_old = m_ref[...]
        m_new = jnp.maximum(m_old, jnp.max(sc, axis=-1, keepdims=True))
        p = jnp.exp(sc - m_new)
        alpha = jnp.exp(m_old - m_new)
        l_ref[...] = alpha * l_ref[...] + jnp.sum(p, axis=-1, keepdims=True)
        pv = _dot(p.reshape(R, TK).astype(BF16), vs_ref[0, 0, pl.ds(k0, TK), :])
        acc_ref[...] = alpha.reshape(R, 1) * acc_ref[...] + pv
        m_ref[...] = m_new
        return carry

    lax.fori_loop(0, t0 // TK + 1, sweep, 0)
    o_s = acc_ref[...] / jnp.maximum(l_ref[...].reshape(R, 1), 1e-30)

    WK = NSA_WINDOW + TQ
    ws = pl.multiple_of(jnp.maximum(t0 - NSA_WINDOW, 0), TQ)
    kp = ws + lax.broadcasted_iota(I32, (1, WK), 1)
    wbias = jnp.where((kp <= tq) & (kp > tq - NSA_WINDOW), 0.0, NEG)[None]
    sw = _dot_nt(q, kw_ref[0, 0, pl.ds(ws, WK), :]).reshape(H, TQ, WK) + wbias
    p_w = jnp.exp(sw - jnp.max(sw, axis=-1, keepdims=True))
    l_w = jnp.sum(p_w, axis=-1, keepdims=True).reshape(R, 1)
    o_w = _dot(p_w.reshape(R, WK).astype(BF16), vw_ref[0, 0, pl.ds(ws, WK), :]) / l_w

    sg = jax.nn.sigmoid(gate_ref[0])
    out = jnp.zeros((TQ, H * HD), F32)
    for h in range(H):
        rows = slice(h * TQ, (h + 1) * TQ)
        o_h = (sg[:, 3 * h:3 * h + 1] * o_c[rows] + sg[:, 3 * h + 1:3 * h + 2] * o_s[rows]
               + sg[:, 3 * h + 2:3 * h + 3] * o_w[rows])
        out = out + _dot(o_h.astype(BF16), place_ref[h])
    o_ref[0] = out.astype(BF16)


def nsa_attention(qn, kcb, vcb, ksn, vsb, kwn, vwb, proj):
    B, _, S, _ = qn.shape
    G, H, TQ = NSA_KV, NSA_HPG, NSA_TQ
    assert S % NSA_TK == 0 and S >= NSA_WINDOW + TQ
    ncmp = kcb.shape[1]
    nsel = S // SEL_LEN
    n = np.arange(ncmp)[:, None] * CMP_STRIDE
    j = np.arange(nsel)[None, :] * SEL_LEN
    cover = jnp.asarray(((n < j + SEL_LEN) & (n + CMP_LEN > j)).astype(np.float32))
    R = H * TQ
    kv_spec = pl.BlockSpec((1, 1, S, HD), lambda b, g, i: (b, g, 0, 0))
    return pl.pallas_call(
        _nsa_kernel,
        grid=(B, G, S // TQ),
        in_specs=[pl.BlockSpec((1, H, TQ, HD), lambda b, g, i: (b, g, i, 0)),
                  pl.BlockSpec((1, ncmp, HD), lambda b, g, i: (b * G + g, 0, 0)),
                  pl.BlockSpec((1, ncmp, HD), lambda b, g, i: (b * G + g, 0, 0)),
                  kv_spec, kv_spec, kv_spec, kv_spec,
                  pl.BlockSpec((1, TQ, LANES), lambda b, g, i: (b, i, C_GATE // LANES + g)),
                  pl.BlockSpec((ncmp, nsel), lambda b, g, i: (0, 0)),
                  pl.BlockSpec((H, HD, H * HD), lambda b, g, i: (0, 0, 0))],
        out_specs=pl.BlockSpec((1, TQ, H * HD), lambda b, g, i: (b, i, g)),
        out_shape=jax.ShapeDtypeStruct((B, S, NSA_HEADS * HD), BF16),
        scratch_shapes=[pltpu.VMEM((H, TQ, 1), F32), pltpu.VMEM((H, TQ, 1), F32), pltpu.VMEM((R, HD), F32)],
        compiler_params=_cparams(("parallel", "parallel", "arbitrary")),
        name="nsa_attention",
    )(qn, kcb, vcb, ksn, vsb, kwn, vwb, proj, cover, _head_placement(H))


SWA_TQ = 128


def _swa_kernel(sink_ref, q_ref, k_ref, v_ref, place_ref, o_ref):
    TQ, H = SWA_TQ, SWA_HPG
    R = H * TQ
    g = pl.program_id(1)
    t0 = pl.program_id(2) * TQ
    q = q_ref[0].reshape(R, HD)
    tq = t0 + lax.broadcasted_iota(I32, (TQ, 1), 0)
    WK = SWA_WINDOW + TQ
    ws = pl.multiple_of(jnp.maximum(t0 - SWA_WINDOW, 0), TQ)
    kp = ws + lax.broadcasted_iota(I32, (1, WK), 1)
    bias = jnp.where((kp <= tq) & (kp > tq - SWA_WINDOW), 0.0, NEG)[None]
    s = _dot_nt(q, k_ref[0, 0, pl.ds(ws, WK), :]).reshape(H, TQ, WK) + bias
    head = lax.broadcasted_iota(I32, (H, TQ, 1), 0)
    sink = jnp.zeros((H, TQ, 1), F32)
    for h in range(H):
        sink = jnp.where(head == h, sink_ref[g * H + h], sink)
    m = jnp.maximum(jnp.max(s, axis=-1, keepdims=True), sink)
    p = jnp.exp(s - m)
    den = jnp.sum(p, axis=-1, keepdims=True) + jnp.exp(sink - m)
    o = _dot(p.reshape(R, WK).astype(BF16), v_ref[0, 0, pl.ds(ws, WK), :]) / den.reshape(R, 1)
    out = jnp.zeros((TQ, H * HD), F32)
    for h in range(H):
        out = out + _dot(o[h * TQ:(h + 1) * TQ].astype(BF16), place_ref[h])
    o_ref[0] = out.astype(BF16)


def swa_attention(qn, kn, vb, sinks):
    B, _, S, _ = qn.shape
    G, H, TQ = SWA_KV, SWA_HPG, SWA_TQ
    assert S >= SWA_WINDOW + TQ
    kv_spec = pl.BlockSpec((1, 1, S, HD), lambda b, g, i, sk: (b, g, 0, 0))
    return pl.pallas_call(
        _swa_kernel,
        grid_spec=pltpu.PrefetchScalarGridSpec(
            num_scalar_prefetch=1,
            grid=(B, G, S // TQ),
            in_specs=[pl.BlockSpec((1, H, TQ, HD), lambda b, g, i, sk: (b, g, i, 0)), kv_spec, kv_spec,
                      pl.BlockSpec((H, HD, H * HD), lambda b, g, i, sk: (0, 0, 0))],
            out_specs=pl.BlockSpec((1, TQ, H * HD), lambda b, g, i, sk: (b, i, g))),
        out_shape=jax.ShapeDtypeStruct((B, S, SWA_HEADS * HD), BF16),
        compiler_params=_cparams(("parallel", "parallel", "arbitrary")),
        name="swa_attention",
    )(sinks.astype(F32), qn, kn, vb, _head_placement(H))


R_GRP = 0
R_EXP = 32


def _router_kernel(x_ref, g_ref, w_ref, b_ref, tri_ref, hn_ref, route_ref, cnt_ref, base_ref):
    i = pl.program_id(0)
    last = pl.num_programs(0) - 1

    @pl.when(i == 0)
    def _():
        base_ref[...] = jnp.zeros_like(base_ref)

    @pl.when(i < last)
    def _():
        _route_tile(x_ref, g_ref, w_ref, b_ref, tri_ref, hn_ref, route_ref, base_ref)

    @pl.when(i == last)
    def _():
        hn_ref[...] = jnp.zeros_like(hn_ref)
        route_ref[...] = jnp.zeros_like(route_ref)

    cnt_ref[...] = base_ref[...]


def _route_tile(x_ref, g_ref, w_ref, b_ref, tri_ref, hn_ref, route_ref, base_ref):
    x = x_ref[...]
    ms = jnp.mean(x * x, axis=-1, keepdims=True)
    h = x * lax.rsqrt(ms + EPS) * g_ref[...]
    hn_ref[...] = h
    logits = jnp.dot(h, w_ref[...], precision=HI, preferred_element_type=F32) + b_ref[...]
    lane = lax.broadcasted_iota(I32, logits.shape, 1)
    is_grp = lane < N_GROUPS
    gl = jnp.where(is_grp, logits, NEG)
    gmax = jnp.max(gl, axis=-1, keepdims=True)
    lanef = lane.astype(F32)
    gidx = jnp.min(jnp.where(gl == gmax, lanef, float(LANES)), axis=-1, keepdims=True)
    gw = 1.0 / jnp.sum(jnp.where(is_grp, jnp.exp(logits - gmax), 0.0), axis=-1, keepdims=True)
    in_grp = ((lane >> 3) - R_EXP // EXP_PER_GROUP).astype(F32) == gidx
    w0 = jnp.where(in_grp, logits, NEG)
    m1 = jnp.max(w0, axis=-1, keepdims=True)
    i1 = jnp.min(jnp.where(w0 == m1, lanef, float(LANES)), axis=-1, keepdims=True)
    w1 = jnp.where(lanef == i1, NEG, w0)
    m2 = jnp.max(w1, axis=-1, keepdims=True)
    i2 = jnp.min(jnp.where(w1 == m2, lanef, float(LANES)), axis=-1, keepdims=True)
    e2 = jnp.exp(m2 - m1)
    p1 = gw / (1.0 + e2)
    p2 = gw * e2 / (1.0 + e2)
    oh1 = jnp.where(lanef == i1, 1.0, 0.0)
    oh2 = jnp.where(lanef == i2, 1.0, 0.0)
    oh = oh1 + oh2
    before = _dot(tri_ref[...], oh.astype(BF16)) + base_ref[0:1, :]
    r1 = jnp.sum(before * oh1, axis=-1, keepdims=True)
    r2 = jnp.sum(before * oh2, axis=-1, keepdims=True)
    base_ref[...] = base_ref[...] + jnp.sum(oh, axis=0, keepdims=True)
    route = jnp.zeros_like(logits)
    for ln, val in enumerate((i1 - R_EXP, i2 - R_EXP, p1, p2, r1, r2)):
        route = jnp.where(lane == ln, val, route)
    route_ref[...] = route


def moe_router(x, gain, w_grp, b_grp, w_exp, b_exp, tm=256):
    T, D = x.shape
    nt = T // tm
    w = jnp.zeros((D, LANES), F32).at[:, R_GRP:R_GRP + N_GROUPS].set(w_grp).at[:, R_EXP:R_EXP + N_EXPERTS].set(w_exp)
    b = jnp.zeros((1, LANES), F32).at[0, R_GRP:R_GRP + N_GROUPS].set(b_grp).at[0, R_EXP:R_EXP + N_EXPERTS].set(b_exp)
    tri = jnp.asarray(np.tril(np.ones((tm, tm), np.float32), -1), dtype=BF16)
    return pl.pallas_call(
        _router_kernel,
        grid=(nt + 1,),
        in_specs=[pl.BlockSpec((tm, D), lambda i: (jnp.minimum(i, nt - 1), 0)),
                  pl.BlockSpec((1, D), lambda i: (0, 0)),
                  pl.BlockSpec((D, LANES), lambda i: (0, 0)),
                  pl.BlockSpec((1, LANES), lambda i: (0, 0)),
                  pl.BlockSpec((tm, tm), lambda i: (0, 0))],
        out_specs=[pl.BlockSpec((tm, D), lambda i: (i, 0)),
                   pl.BlockSpec((tm, LANES), lambda i: (i, 0)),
                   pl.BlockSpec((8, LANES), lambda i: (0, 0))],
        out_shape=[jax.ShapeDtypeStruct((T + tm, D), F32), jax.ShapeDtypeStruct((T + tm, LANES), F32),
                   jax.ShapeDtypeStruct((8, LANES), F32)],
        scratch_shapes=[pltpu.VMEM((8, LANES), F32)],
        compiler_params=_cparams(("arbitrary",)),
        name="moe_router",
    )(x, gain.reshape(1, D), w, b, tri)


def _row_copy(src, src_row, dst, dst_row, sem):
    return pltpu.make_async_copy(src.at[pl.ds(src_row, 1)], dst.at[pl.ds(dst_row, 1)], sem)


DISPATCH_TILE = 512


def _wait_rows(src, dst, sem, n):
    def body(r, c):
        _row_copy(src, 0, dst, 0, sem).wait()
        return c

    lax.fori_loop(0, n, body, 0)


def _dispatch_kernel(n_tok, dest_ref, hn_ref, xb_ref, sems):
    i = pl.program_id(0)
    slot = i % 2
    base = i * DISPATCH_TILE

    def issue(r, c):
        src = jnp.minimum((base + r) >> 1, n_tok)
        _row_copy(hn_ref, src, xb_ref, dest_ref[0, 0, r], sems.at[slot]).start()
        return c

    lax.fori_loop(0, DISPATCH_TILE, issue, 0, unroll=8)

    @pl.when(i > 0)
    def _():
        _wait_rows(hn_ref, xb_ref, sems.at[1 - slot], DISPATCH_TILE)

    @pl.when(i == pl.num_programs(0) - 1)
    def _():
        _wait_rows(hn_ref, xb_ref, sems.at[slot], DISPATCH_TILE)


def moe_dispatch(hn, dest_all, n_tok):
    n_rows = dest_all.shape[0]
    D = hn.shape[1]
    nt = n_rows // DISPATCH_TILE
    return pl.pallas_call(
        functools.partial(_dispatch_kernel, n_tok),
        grid=(nt,),
        in_specs=[pl.BlockSpec((1, 1, DISPATCH_TILE), lambda i: (i, 0, 0), memory_space=pltpu.SMEM),
                  pl.BlockSpec(memory_space=pl.ANY)],
        out_specs=pl.BlockSpec(memory_space=pl.ANY),
        out_shape=jax.ShapeDtypeStruct((n_rows, D), F32),
        scratch_shapes=[pltpu.SemaphoreType.DMA((2,))],
        compiler_params=_cparams(("arbitrary",)),
        name="moe_dispatch",
    )(dest_all.reshape(nt, 1, DISPATCH_TILE), hn)


def _expert_kernel(be_ref, x_ref, wg_ref, wu_ref, wd_ref, y_ref):
    del be_ref
    x = x_ref[...].astype(BF16)
    hid = _dot(x, wg_ref[0])
    hid = hid * jax.nn.sigmoid(hid) * _dot(x, wu_ref[0])
    y_ref[...] = _dot(hid.astype(BF16), wd_ref[0])


def moe_experts(xb, blk_expert, w_gate, w_up, w_down):
    n_rows, D = xb.shape
    wspec = lambda shape: pl.BlockSpec((1,) + shape, lambda i, be: (be[i], 0, 0))
    return pl.pallas_call(
        _expert_kernel,
        grid_spec=pltpu.PrefetchScalarGridSpec(
            num_scalar_prefetch=1,
            grid=(n_rows // MOE_BLOCK,),
            in_specs=[pl.BlockSpec((MOE_BLOCK, D), lambda i, be: (i, 0)),
                      wspec((D, EXPERT_FF)), wspec((D, EXPERT_FF)), wspec((EXPERT_FF, D))],
            out_specs=pl.BlockSpec((MOE_BLOCK, D), lambda i, be: (i, 0))),
        out_shape=jax.ShapeDtypeStruct((n_rows, D), F32),
        compiler_params=_cparams(("arbitrary",)),
        name="moe_experts",
    )(blk_expert, xb, w_gate, w_up, w_down)


def _combine_kernel(dest_ref, dest_next_ref, x_ref, route_ref, yb_ref, o_ref, ybuf, sems):
    tm = x_ref.shape[0]
    i = pl.program_id(0)
    slot = i % 2

    def gather(d_ref, s):
        def body(t, c):
            _row_copy(yb_ref, d_ref[0, 0, 2 * t], ybuf.at[s], t, sems.at[s]).start()
            _row_copy(yb_ref, d_ref[0, 0, 2 * t + 1], ybuf.at[s], tm + t, sems.at[s]).start()
            return c

        lax.fori_loop(0, tm, body, 0, unroll=4)

    @pl.when(i == 0)
    def _():
        gather(dest_ref, 0)

    @pl.when(i + 1 < pl.num_programs(0))
    def _():
        gather(dest_next_ref, 1 - slot)

    _wait_rows(yb_ref, ybuf.at[slot], sems.at[slot], 2 * tm)
    r = route_ref[...]
    o_ref[...] = x_ref[...] + r[:, 2:3] * ybuf[slot, 0:tm, :] + r[:, 3:4] * ybuf[slot, tm:2 * tm, :]


def moe_combine(x, yb, dest, route):
    T, D = x.shape
    tm = DISPATCH_TILE // 2
    nt = T // tm
    dest3 = dest.reshape(nt, 1, DISPATCH_TILE)
    return pl.pallas_call(
        _combine_kernel,
        grid=(nt,),
        in_specs=[pl.BlockSpec((1, 1, DISPATCH_TILE), lambda i: (i, 0, 0), memory_space=pltpu.SMEM),
                  pl.BlockSpec((1, 1, DISPATCH_TILE), lambda i: (jnp.minimum(i + 1, nt - 1), 0, 0),
                               memory_space=pltpu.SMEM),
                  pl.BlockSpec((tm, D), lambda i: (i, 0)),
                  pl.BlockSpec((tm, LANES), lambda i: (i, 0)),
                  pl.BlockSpec(memory_space=pl.ANY)],
        out_specs=pl.BlockSpec((tm, D), lambda i: (i, 0)),
        out_shape=jax.ShapeDtypeStruct((T, D), F32),
        scratch_shapes=[pltpu.VMEM((2, DISPATCH_TILE, D), F32), pltpu.SemaphoreType.DMA((2,))],
        compiler_params=_cparams(("arbitrary",)),
        name="moe_combine",
    )(dest3, dest3, x, route, yb)


def hier_moe_block(x, gain, w_grp, b_grp, w_exp, b_exp, w_gate, w_up, w_down):
    T, D = x.shape
    hn, route, cnt = moe_router(x, gain, w_grp, b_grp, w_exp, b_exp)
    A = 2 * T
    n_rows = -(-A // MOE_BLOCK) * MOE_BLOCK + N_EXPERTS * MOE_BLOCK
    n_blocks = n_rows // MOE_BLOCK
    counts = cnt[0, R_EXP:R_EXP + N_EXPERTS].astype(I32)
    pcounts = (counts + MOE_BLOCK - 1) // MOE_BLOCK * MOE_BLOCK
    pend = jnp.cumsum(pcounts)
    pstart = pend - pcounts
    experts = jnp.arange(N_EXPERTS, dtype=I32)
    flat_e = route[:T, 0:2].astype(I32).reshape(A)
    rank = route[:T, 4:6].astype(I32).reshape(A)
    dest = rank + jnp.sum(jnp.where(flat_e[:, None] == experts[None, :], pstart[None, :], 0), axis=1)
    n_pad = n_rows - A
    pad_cnt = jnp.concatenate([pcounts - counts, (n_rows - pend[-1])[None]])
    pad_start = jnp.concatenate([pstart + counts, pend[-1:]])
    pad_end = jnp.cumsum(pad_cnt)
    k = jnp.arange(n_pad, dtype=I32)
    seg = (k[:, None] >= pad_end[None, :]).astype(I32).sum(axis=1)
    seg_hot = seg[:, None] == jnp.arange(N_EXPERTS + 1, dtype=I32)[None, :]
    pad_dest = k + jnp.sum(jnp.where(seg_hot, (pad_start - (pad_end - pad_cnt))[None, :], 0), axis=1)
    dest_all = jnp.concatenate([dest, pad_dest]).astype(I32)
    blk_expert = jnp.minimum(
        jnp.searchsorted(pend, jnp.arange(n_blocks, dtype=I32) * MOE_BLOCK, side='right'), N_EXPERTS - 1).astype(I32)
    xb = moe_dispatch(hn, dest_all, T)
    yb = moe_experts(xb, blk_expert, w_gate.astype(BF16), w_up.astype(BF16), w_down.astype(BF16))
    return moe_combine(x, yb, dest, route)


def _even_w_in(w_in):
    ng = 3 * NSA_HPG
    pad = jnp.zeros((w_in.shape[0], LANES - ng), w_in.dtype)
    cols = [w_in[:, :C_GATE]]
    for g in range(NSA_KV):
        cols += [w_in[:, C_GATE + g * ng:C_GATE + (g + 1) * ng], pad]
    return jnp.concatenate(cols, axis=1).astype(BF16)


def even_layer(x, B, S, norm, w_in, lb, hg_gain, q_gain, k_gain, cmp_pe, cmp_w1, cmp_w2, w_out):
    T = B * S
    proj = rms_matmul(x, norm, _even_w_in(w_in), tn=N0 // 4).reshape(B, S, N0)
    a_out = hgrn2(proj, lb, hg_gain)
    kvb = C_KV // LANES
    qn, ksn, kwn, vsb, vwb = prep_heads(
        proj, [C_NQ // (NSA_HEADS * HD), kvb + 2, kvb + 4, kvb + 3, kvb + 5],
        [NSA_HEADS, NSA_KV, NSA_KV, NSA_KV, NSA_KV],
        [q_gain, k_gain[1], k_gain[2], None, None],
        [True, True, True, False, False], [HD ** -0.5, 1.0, 1.0, 1.0, 1.0], S)
    kcb, vcb = nsa_compress(proj, kvb, kvb + 1, cmp_pe, cmp_w1, cmp_w2, k_gain[0])
    b_out = nsa_attention(qn, kcb, vcb, ksn, vsb, kwn, vwb, proj)
    return proj_res(a_out.reshape(T, HG_WIDTH), 0, b_out.reshape(T, NSA_HEADS * HD), 0, w_out.astype(BF16), x)


def odd_layer(x, B, S, norm, w_in, q_gain, k_gain, sinks, w_out):
    T = B * S
    proj = rms_matmul(x, norm, w_in.astype(BF16), tn=N1 // 2).reshape(B, S, N1)
    qd = SWA_HEADS * HD
    qn, kn, vb = prep_heads(
        proj, [0, qd // LANES, qd // LANES + 1], [SWA_HEADS, SWA_KV, SWA_KV],
        [q_gain, k_gain, None], [True, True, False], [HD ** -0.5, 1.0, 1.0], S)
    att = swa_attention(qn, kn, vb, sinks).reshape(T, qd)
    return proj_res(att, 0, att, 1, w_out.astype(BF16), x)


def kernel(x, ev_norm, ev_w_in, hg_lb_logits, hg_out_gain, nsa_q_gain, nsa_k_gain, nsa_cmp_pe, nsa_cmp_w1,
           nsa_cmp_w2, ev_w_out, od_norm, od_w_in, swa_q_gain, swa_k_gain, swa_sinks, od_w_out, moe_norm,
           moe_w_grp, moe_b_grp, moe_w_exp, moe_b_exp, moe_w_gate, moe_w_up, moe_w_down):
    B, S, D = x.shape
    lower_bounds = jnp.cumsum(jax.nn.softmax(hg_lb_logits.astype(F32), axis=0), axis=0)
    h = x.reshape(B * S, D)
    h = even_layer(h, B, S, ev_norm[0], ev_w_in[0], lower_bounds[0], hg_out_gain[0], nsa_q_gain[0],
                   nsa_k_gain[0], nsa_cmp_pe[0], nsa_cmp_w1[0], nsa_cmp_w2[0], ev_w_out[0])
    h = hier_moe_block(h, moe_norm[0], moe_w_grp[0], moe_b_grp[0], moe_w_exp[0], moe_b_exp[0],
                       moe_w_gate[0], moe_w_up[0], moe_w_down[0])
    h = odd_layer(h, B, S, od_norm[0], od_w_in[0], swa_q_gain[0], swa_k_gain[0], swa_sinks[0], od_w_out[0])
    h = hier_moe_block(h, moe_norm[1], moe_w_grp[1], moe_b_grp[1], moe_w_exp[1], moe_b_exp[1],
                       moe_w_gate[1], moe_w_up[1], moe_w_down[1])
    return h.reshape(B, S, D)
```

```python
import functools

import numpy as np
import jax
import jax.numpy as jnp
from jax import lax
from jax.experimental import pallas as pl
from jax.experimental.pallas import tpu as pltpu

F32 = jnp.float32
BF16 = jnp.bfloat16
I32 = jnp.int32
HI = lax.Precision.HIGHEST

D_MODEL = 1024
EPS = 1e-6
ROPE_THETA = 500000.0
HD = 64
ROT = HD // 4
HALF = ROT // 2

HG_DIM = 128
HG_WIDTH = 512
HG_HEADS = 4
HG_CHUNK = 64

NSA_HEADS = 8
NSA_KV = 2
NSA_HPG = NSA_HEADS // NSA_KV
CMP_LEN = 32
CMP_STRIDE = 16
CMP_HIDDEN = 256
SEL_LEN = 64
SEL_TOPN = 8
NSA_WINDOW = 512
FORCE_BONUS = 1e4

SWA_HEADS = 16
SWA_KV = 2
SWA_HPG = SWA_HEADS // SWA_KV
SWA_WINDOW = 128

N_GROUPS = 4
EXP_PER_GROUP = 8
N_EXPERTS = 32
EXPERT_FF = 512
MOE_BLOCK = 256

LANES = 128
NEG = -1e30
VMEM_LIMIT = 56 * 1024 * 1024

C_HG = 0
C_NQ = 2048
C_KV = 2560
C_GATE = 3328
N0 = 3584
N1 = 1280


def _cparams(sem):
    return pltpu.CompilerParams(dimension_semantics=sem, vmem_limit_bytes=VMEM_LIMIT)


def _dot(a, b):
    return jnp.dot(a, b, preferred_element_type=F32)


def _dot_nt(a, b):
    return lax.dot_general(a, b, (((1,), (1,)), ((), ())), preferred_element_type=F32)


def _rms_matmul_kernel(x_ref, g_ref, w_ref, o_ref, xn_ref):
    @pl.when(pl.program_id(1) == 0)
    def _():
        x = x_ref[...]
        ms = jnp.mean(x * x, axis=-1, keepdims=True)
        xn_ref[...] = (x * lax.rsqrt(ms + EPS) * g_ref[...]).astype(BF16)

    o_ref[...] = _dot(xn_ref[...], w_ref[...])


def rms_matmul(x, gain, w, tm=512, tn=None):
    T, D = x.shape
    N = w.shape[1]
    tn = N if tn is None else tn
    return pl.pallas_call(
        _rms_matmul_kernel,
        grid=(T // tm, N // tn),
        in_specs=[pl.BlockSpec((tm, D), lambda i, j: (i, 0)),
                  pl.BlockSpec((1, D), lambda i, j: (0, 0)),
                  pl.BlockSpec((D, tn), lambda i, j: (0, j))],
        out_specs=pl.BlockSpec((tm, tn), lambda i, j: (i, j)),
        out_shape=jax.ShapeDtypeStruct((T, N), F32),
        scratch_shapes=[pltpu.VMEM((tm, D), BF16)],
        compiler_params=_cparams(("parallel", "arbitrary")),
        name="rms_matmul",
    )(x, gain.reshape(1, D), w)


def _proj_res_kernel(a_ref, b_ref, wa_ref, wb_ref, r_ref, o_ref):
    acc = _dot(a_ref[...], wa_ref[...]) + _dot(b_ref[...], wb_ref[...])
    o_ref[...] = r_ref[...] + acc


def proj_res(a, a_blk, b, b_blk, w, res, tm=512):
    T, D = res.shape
    K = w.shape[0]
    kh = K // 2
    return pl.pallas_call(
        _proj_res_kernel,
        grid=(T // tm,),
        in_specs=[pl.BlockSpec((tm, kh), lambda i: (i, a_blk)),
                  pl.BlockSpec((tm, kh), lambda i: (i, b_blk)),
                  pl.BlockSpec((kh, D), lambda i: (0, 0)),
                  pl.BlockSpec((kh, D), lambda i: (1, 0)),
                  pl.BlockSpec((tm, D), lambda i: (i, 0))],
        out_specs=pl.BlockSpec((tm, D), lambda i: (i, 0)),
        out_shape=jax.ShapeDtypeStruct((T, D), F32),
        compiler_params=_cparams(("parallel",)),
        name="proj_res",
    )(a, b, w, w, res)


def _rope_tables(pos, width):
    inv = jnp.power(ROPE_THETA, -jnp.arange(HALF, dtype=F32) * 2.0 / ROT)
    ang = pos.astype(F32)[:, None] * inv[None, :]
    cos, sin = jnp.cos(ang), jnp.sin(ang)
    n = pos.shape[0]
    c = jnp.concatenate([cos, cos, jnp.ones((n, HD - ROT), F32)], axis=1)
    s = jnp.concatenate([-sin, sin, jnp.zeros((n, HD - ROT), F32)], axis=1)
    reps = width // HD
    return jnp.tile(c, (1, reps)), jnp.tile(s, (1, reps))


def _block_ones(width):
    idx = np.arange(width) // HD
    return jnp.asarray((idx[:, None] == idx[None, :]).astype(np.float32))


def _head_norm_rope(x, gain, cos, sin, bones, scale):
    ms = jnp.dot(x * x, bones, precision=HI, preferred_element_type=F32) * (1.0 / HD)
    y = x * lax.rsqrt(ms + EPS) * gain
    lane = lax.broadcasted_iota(I32, y.shape, 1) % HD
    partner = jnp.where(lane < HALF, pltpu.roll(y, LANES - HALF, axis=1), pltpu.roll(y, HALF, axis=1))
    out = y * cos + partner * sin
    return out * scale if scale != 1.0 else out


def _prep_kernel(n_heads, norm_flags, scales, *refs):
    nt = len(n_heads)
    x_refs = refs[:nt]
    gain_ref, cos_ref, sin_ref, bones_ref = refs[nt:nt + 4]
    o_refs = refs[nt + 4:]
    cos = cos_ref[...]
    sin = sin_ref[...]
    bones = bones_ref[...]
    lane = lax.broadcasted_iota(I32, cos.shape, 1)
    for t in range(nt):
        for c in range(n_heads[t] // 2):
            x = x_refs[t][0, :, c * LANES:(c + 1) * LANES]
            if norm_flags[t]:
                g = gain_ref[t:t + 1, :]
                x = _head_norm_rope(x, g, cos, sin, bones, scales[t])
                o_refs[t][0, 2 * c] = x[:, :HD].astype(BF16)
                o_refs[t][0, 2 * c + 1] = pltpu.roll(x, HD, axis=1)[:, :HD].astype(BF16)
            else:
                for j, xh in enumerate((x, pltpu.roll(x, HD, axis=1))):
                    o_refs[t][0, 2 * c + j] = jnp.where(lane < HD, xh, jnp.where(lane == HD, 1.0, 0.0)).astype(BF16)


def prep_heads(proj, col_blocks, n_heads, gains, norm_flags, scales, S, ts=512):
    B = proj.shape[0]
    nt = len(n_heads)
    cos, sin = _rope_tables(jnp.arange(S), LANES)
    gain_rows = []
    for t in range(nt):
        g = gains[t] if gains[t] is not None else jnp.ones((HD,), F32)
        gain_rows.append(jnp.tile(g.astype(F32), 2))
    gain_arr = jnp.stack(gain_rows)
    in_specs = []
    for t in range(nt):
        w = n_heads[t] * HD
        in_specs.append(pl.BlockSpec((1, ts, w), functools.partial(lambda b, s, cb: (b, s, cb), cb=col_blocks[t])))
    in_specs += [pl.BlockSpec((nt, LANES), lambda b, s: (0, 0)),
                 pl.BlockSpec((ts, LANES), lambda b, s: (s, 0)),
                 pl.BlockSpec((ts, LANES), lambda b, s: (s, 0)),
                 pl.BlockSpec((LANES, LANES), lambda b, s: (0, 0))]
    widths = [HD if norm_flags[t] else LANES for t in range(nt)]
    out_specs = [pl.BlockSpec((1, n_heads[t], ts, widths[t]), lambda b, s: (b, 0, s, 0)) for t in range(nt)]
    out_shape = [jax.ShapeDtypeStruct((B, n_heads[t], S, widths[t]), BF16) for t in range(nt)]
    return pl.pallas_call(
        functools.partial(_prep_kernel, tuple(n_heads), tuple(norm_flags), tuple(scales)),
        grid=(B, S // ts),
        in_specs=in_specs,
        out_specs=out_specs,
        out_shape=out_shape,
        compiler_params=_cparams(("parallel", "parallel")),
        name="prep_heads",
    )(*([proj] * nt), gain_arr, cos, sin, _block_ones(LANES))


_N_LEVELS = 6


def _hgrn_constants():
    C = HG_CHUNK
    r = np.arange(C)
    mats = [(r[None, :] <= r[:, None]).astype(np.float32),
            (r[None, :] > r[:, None]).astype(np.float32)]
    masks = []
    for lv in range(_N_LEVELS):
        n = 1 << lv
        blk = r // (2 * n)
        right = (r // n) % 2 == 1
        bnd = blk * 2 * n + n - 1
        m = np.zeros((C, C), np.float32)
        for t in range(C):
            if right[t]:
                m[t, bnd[t] + 1:t + 1] = 1.0
            else:
                m[t, t + 1:bnd[t] + 1] = 1.0
        mats.append(m)
        masks.append((blk[:, None] == blk[None, :]).astype(np.float32))
    masks.append(np.eye(C, dtype=np.float32))
    return jnp.asarray(np.concatenate(mats, axis=0), dtype=BF16), jnp.asarray(np.stack(masks))


def _hgrn_kernel(q_ref, f_ref, i_ref, g_ref, lb_ref, gain_ref, mall_ref, masks_ref, o_ref, st_ref):
    C = HG_CHUNK
    ts = q_ref.shape[1]

    @pl.when(pl.program_id(1) == 0)
    def _():
        st_ref[...] = jnp.zeros_like(st_ref)

    mall = mall_ref[...]
    row = lax.broadcasted_iota(I32, (C, HG_DIM), 0)
    gain = gain_ref[...]

    def chunk(c, carry):
        r0 = pl.multiple_of(c * C, C)
        for h in range(HG_HEADS):
            ls = slice(h * HG_DIM, (h + 1) * HG_DIM)
            q = q_ref[0, pl.ds(r0, C), ls]
            z = f_ref[0, pl.ds(r0, C), ls]
            v = i_ref[0, pl.ds(r0, C), ls]
            g = g_ref[0, pl.ds(r0, C), ls]
            lb = lb_ref[:, ls]
            lf = jnp.log(lb + (1.0 - lb) * jax.nn.sigmoid(z))
            k = (1.0 - lb) * jax.nn.sigmoid(-z)
            lf_hi = lf.astype(BF16)
            lf_lo = (lf - lf_hi.astype(F32)).astype(BF16)
            seg = _dot(mall, jnp.concatenate([lf_hi, lf_lo], axis=1))
            ex = jnp.exp(seg[:, :HG_DIM] + seg[:, HG_DIM:])
            qs = q * ex[0:C]
            ks = k * ex[C:2 * C]
            st = st_ref[h]
            o = _dot_nt(qs.astype(BF16), st.astype(BF16))
            a = jnp.where(masks_ref[_N_LEVELS] > 0.5, _dot_nt(q.astype(BF16), k.astype(BF16)), 0.0)
            for lv in range(_N_LEVELS):
                e = ex[(2 + lv) * C:(3 + lv) * C]
                right = ((row >> lv) & 1) == 1
                qe = jnp.where(right, q * e, 0.0).astype(BF16)
                ke = jnp.where(right, 0.0, k * e).astype(BF16)
                a = a + jnp.where(masks_ref[lv] > 0.5, _dot_nt(qe, ke), 0.0)
            vb = v.astype(BF16)
            o = o + _dot(a.astype(BF16), vb)
            decay = ex[C - 1:C]
            st_ref[h] = st * decay + _dot(v.T.astype(BF16), ks.astype(BF16))
            ms = jnp.mean(o * o, axis=-1, keepdims=True)
            y = o * lax.rsqrt(ms + EPS) * gain * (g * jax.nn.sigmoid(g))
            o_ref[0, pl.ds(r0, C), ls] = y.astype(BF16)
        return carry

    lax.fori_loop(0, ts // C, chunk, 0)


def hgrn2(proj, lb, out_gain, ts=512):
    B, S, _ = proj.shape
    mall, masks = _hgrn_constants()
    W = HG_WIDTH
    return pl.pallas_call(
        _hgrn_kernel,
        grid=(B, S // ts),
        in_specs=[pl.BlockSpec((1, ts, W), lambda b, s: (b, s, 0)),
                  pl.BlockSpec((1, ts, W), lambda b, s: (b, s, 1)),
                  pl.BlockSpec((1, ts, W), lambda b, s: (b, s, 2)),
                  pl.BlockSpec((1, ts, W), lambda b, s: (b, s, 3)),
                  pl.BlockSpec((1, W), lambda b, s: (0, 0)),
                  pl.BlockSpec((1, HG_DIM), lambda b, s: (0, 0)),
                  pl.BlockSpec(mall.shape, lambda b, s: (0, 0)),
                  pl.BlockSpec(masks.shape, lambda b, s: (0, 0, 0))],
        out_specs=pl.BlockSpec((1, ts, W), lambda b, s: (b, s, 0)),
        out_shape=jax.ShapeDtypeStruct((B, S, W), BF16),
        scratch_shapes=[pltpu.VMEM((HG_HEADS, HG_DIM, HG_DIM), F32)],
        compiler_params=_cparams(("parallel", "arbitrary")),
        name="hgrn2",
    )(proj, proj, proj, proj, lb.reshape(1, W), out_gain.reshape(1, HG_DIM), mall, masks)


def _compress_kernel(xk_ref, xv_ref, w1k_ref, w1v_ref, pek_ref, pev_ref, w2k_ref, w2v_ref,
                     gain_ref, cos_ref, sin_ref, bones_ref, ok_ref, ov_ref):
    nc = xk_ref.shape[1] // CMP_STRIDE

    def mlp(x_ref, w1_ref, pe_ref, w2_ref):
        width = NSA_KV * CMP_HIDDEN
        u0 = jnp.zeros((nc, width), F32)
        u1 = jnp.zeros((nc, width), F32)
        const = jnp.zeros((8, width), F32)
        for p in range(CMP_STRIDE):
            xp = x_ref[0, pl.ds(p, nc, stride=CMP_STRIDE), :].astype(BF16)
            u0 = u0 + _dot(xp, w1_ref[p])
            u1 = u1 + _dot(xp, w1_ref[CMP_STRIDE + p])
            const = const + _dot(pe_ref[p].astype(BF16), w1_ref[p])
            const = const + _dot(pe_ref[CMP_STRIDE + p].astype(BF16), w1_ref[CMP_STRIDE + p])
        h = u0 + pltpu.roll(u1, nc - 1, axis=0) + const[0:1]
        return _dot(jax.nn.gelu(h).astype(BF16), w2_ref[...])

    yk = _head_norm_rope(mlp(xk_ref, w1k_ref, pek_ref, w2k_ref), gain_ref[...], cos_ref[...], sin_ref[...],
                         bones_ref[...], 1.0)
    yv = mlp(xv_ref, w1v_ref, pev_ref, w2v_ref)
    for y, o_ref in ((yk, ok_ref), (yv, ov_ref)):
        o_ref[0] = y[:, :HD].astype(BF16)
        o_ref[1] = pltpu.roll(y, HD, axis=1)[:, :HD].astype(BF16)


def _group_diag(w):
    z = jnp.zeros_like(w)
    return jnp.concatenate([jnp.concatenate([w, z], axis=-1), jnp.concatenate([z, w], axis=-1)], axis=-2)


def nsa_compress(proj, kc_blk, vc_blk, pe, w1, w2, k_gain):
    B, S, _ = proj.shape
    nc = S // CMP_STRIDE
    pos_end = jnp.arange(nc) * CMP_STRIDE + CMP_LEN - 1
    cos, sin = _rope_tables(pos_end, LANES)
    w1d = _group_diag(w1.reshape(2, CMP_LEN, HD, CMP_HIDDEN)).astype(BF16)
    w2d = _group_diag(w2).astype(BF16)
    pe2 = jnp.broadcast_to(jnp.tile(pe, (1, 1, NSA_KV))[:, :, None, :], (2, CMP_LEN, 8, LANES))
    full = lambda shape: pl.BlockSpec(shape, lambda b: (0,) * len(shape))
    w1_shape = (CMP_LEN, LANES, NSA_KV * CMP_HIDDEN)
    return pl.pallas_call(
        _compress_kernel,
        grid=(B,),
        in_specs=[pl.BlockSpec((1, S, LANES), lambda b: (b, 0, kc_blk)),
                  pl.BlockSpec((1, S, LANES), lambda b: (b, 0, vc_blk)),
                  full(w1_shape), full(w1_shape),
                  full((CMP_LEN, 8, LANES)), full((CMP_LEN, 8, LANES)),
                  full((NSA_KV * CMP_HIDDEN, LANES)), full((NSA_KV * CMP_HIDDEN, LANES)),
                  full((1, LANES)), full((nc, LANES)), full((nc, LANES)), full((LANES, LANES))],
        out_specs=[pl.BlockSpec((NSA_KV, nc, HD), lambda b: (b, 0, 0)),
                   pl.BlockSpec((NSA_KV, nc, HD), lambda b: (b, 0, 0))],
        out_shape=[jax.ShapeDtypeStruct((B * NSA_KV, nc, HD), BF16)] * 2,
        compiler_params=_cparams(("parallel",)),
        name="nsa_compress",
    )(proj, proj, w1d[0], w1d[1], pe2[0], pe2[1], w2d[0], w2d[1],
      jnp.tile(k_gain.astype(F32), 2).reshape(1, LANES), cos, sin, _block_ones(LANES))


NSA_TQ = 128
NSA_TK = 512
ROW_CHUNK = 64


def _softmax_rows(s, mask):
    m = jnp.max(jnp.where(mask, s, NEG), axis=-1, keepdims=True)
    p = jnp.where(mask, jnp.exp(s - m), 0.0)
    return p / jnp.maximum(jnp.sum(p, axis=-1, keepdims=True), 1e-30)


def _head_placement(n_heads):
    p = np.zeros((n_heads, HD, n_heads * HD), np.float32)
    for h in range(n_heads):
        p[h, np.arange(HD), h * HD + np.arange(HD)] = 1.0
    return jnp.asarray(p, dtype=BF16)


def _nsa_kernel(q_ref, kc_ref, vc_ref, ks_ref, vs_ref, kw_ref, vw_ref, gate_ref, cover_ref, place_ref,
                o_ref, m_ref, acc_ref):
    TQ, TK, H = NSA_TQ, NSA_TK, NSA_HPG
    R = H * TQ
    i = pl.program_id(2)
    t0 = i * TQ
    q = q_ref[0].reshape(R, HD)
    tq = t0 + lax.broadcasted_iota(I32, (TQ, 1), 0)
    CH = ROW_CHUNK
    chunks = [(slice(c * CH, (c + 1) * CH), slice(c * CH % TQ, c * CH % TQ + CH)) for c in range(R // CH)]

    kc = kc_ref[0]
    ncmp = kc.shape[0]
    s_all = _dot_nt(q, kc)
    cmask = (lax.broadcasted_iota(I32, (1, ncmp), 1) * CMP_STRIDE + (CMP_LEN - 1)) <= tq
    p_sum = [jnp.zeros((CH, ncmp), F32) for _ in range(TQ // CH)]
    p_parts = []
    for rows, toks in chunks:
        p = _softmax_rows(s_all[rows], cmask[toks])
        p_sum[toks.start // CH] = p_sum[toks.start // CH] + p
        p_parts.append(p.astype(BF16))
    o_c = _dot(jnp.concatenate(p_parts, axis=0), vc_ref[0])

    p_tok = jnp.concatenate(p_sum, axis=0)
    p_hi = p_tok.astype(BF16)
    p_lo = (p_tok - p_hi.astype(F32)).astype(BF16)
    imp = _dot_nt(cover_ref[...], p_hi) + _dot_nt(cover_ref[...], p_lo)
    nsel = imp.shape[0]
    jj = lax.broadcasted_iota(I32, (nsel, TQ), 0)
    jf = jj.astype(F32)
    tl = t0 + lax.broadcasted_iota(I32, (1, TQ), 1)
    cur = tl >> 6
    allowed = jj * SEL_LEN <= tl
    forced = (jj == 0) | (jj == cur) | (jj == cur - 1)
    score = jnp.where(allowed, imp + jnp.where(forced, FORCE_BONUS, 0.0), NEG)
    sel = jnp.zeros((nsel, TQ), F32)
    for _ in range(min(SEL_TOPN, nsel)):
        mx = jnp.max(score, axis=0, keepdims=True)
        first = jnp.min(jnp.where(score == mx, jf, float(nsel)), axis=0, keepdims=True)
        hit = jf == first
        sel = jnp.where(hit & (mx > 0.5 * NEG), 1.0, sel)
        score = jnp.where(hit, NEG, score)
    sel_b = sel.T.astype(BF16)

    m_ref[...] = jnp.full(m_ref.shape, NEG, F32)
    acc_ref[...] = jnp.zeros(acc_ref.shape, F32)
    blk_of_col = lax.broadcasted_iota(I32, (nsel, TK), 1) >> 6
    blk_row = lax.broadcasted_iota(I32, (nsel, TK), 0)
    col = lax.broadcasted_iota(I32, (1, TK), 1)

    def sweep(kt, carry):
        k0 = pl.multiple_of(kt * TK, TK)
        expand = jnp.where(blk_row == blk_of_col + kt * (TK // SEL_LEN), 1.0, 0.0).astype(BF16)
        chosen = _dot(sel_b, expand) > 0.5
        bias = jnp.where(chosen & (k0 + col <= tq), 0.0, NEG)
        s_all = _dot_nt(q, ks_ref[0, 0, pl.ds(k0, TK), :])
        p_parts = []
        for rows, toks in chunks:
            sc = s_all[rows] + bias[toks]
            m_old = m_ref[rows]
            m_new = jnp.maximum(m_old, jnp.max(sc, axis=-1, keepdims=True))
            p_parts.append(jnp.exp(sc - m_new).astype(BF16))
            acc_ref[rows] = jnp.exp(m_old - m_new) * acc_ref[rows]
            m_ref[rows] = m_new
        acc_ref[...] = acc_ref[...] + _dot(jnp.concatenate(p_parts, axis=0), vs_ref[0, 0, pl.ds(k0, TK), :])
        return carry

    lax.fori_loop(0, t0 // TK + 1, sweep, 0)
    acc = acc_ref[...]
    o_s = acc[:, :HD] / jnp.maximum(acc[:, HD:HD + 1], 1e-30)

    WK = NSA_WINDOW + TQ
    ws = pl.multiple_of(jnp.maximum(t0 - NSA_WINDOW, 0), TQ)
    kp = ws + lax.broadcasted_iota(I32, (1, WK), 1)
    wbias = jnp.where((kp <= tq) & (kp > tq - NSA_WINDOW), 0.0, NEG)
    s_all = _dot_nt(q, kw_ref[0, 0, pl.ds(ws, WK), :])
    p_parts = []
    for rows, toks in chunks:
        sc = s_all[rows] + wbias[toks]
        p_parts.append(jnp.exp(sc - jnp.max(sc, axis=-1, keepdims=True)).astype(BF16))
    acc_w = _dot(jnp.concatenate(p_parts, axis=0), vw_ref[0, 0, pl.ds(ws, WK), :])
    o_w = acc_w[:, :HD] / acc_w[:, HD:HD + 1]

    sg = jax.nn.sigmoid(gate_ref[0])
    out = jnp.zeros((TQ, H * HD), F32)
    for h in range(H):
        rows = slice(h * TQ, (h + 1) * TQ)
        o_h = (sg[:, 3 * h:3 * h + 1] * o_c[rows] + sg[:, 3 * h + 1:3 * h + 2] * o_s[rows]
               + sg[:, 3 * h + 2:3 * h + 3] * o_w[rows])
        out = out + _dot(o_h.astype(BF16), place_ref[h])
    o_ref[0] = out.astype(BF16)


def nsa_attention(qn, kcb, vcb, ksn, vsb, kwn, vwb, proj):
    B, _, S, _ = qn.shape
    G, H, TQ = NSA_KV, NSA_HPG, NSA_TQ
    assert S % NSA_TK == 0 and S >= NSA_WINDOW + TQ
    ncmp = kcb.shape[1]
    nsel = S // SEL_LEN
    n = np.arange(ncmp)[:, None] * CMP_STRIDE
    j = np.arange(nsel)[None, :] * SEL_LEN
    cover = jnp.asarray(((n < j + SEL_LEN) & (n + CMP_LEN > j)).astype(np.float32).T, dtype=BF16)
    R = H * TQ
    k_spec = pl.BlockSpec((1, 1, S, HD), lambda b, g, i: (b, g, 0, 0))
    v_spec = pl.BlockSpec((1, 1, S, LANES), lambda b, g, i: (b, g, 0, 0))
    return pl.pallas_call(
        _nsa_kernel,
        grid=(B, G, S // TQ),
        in_specs=[pl.BlockSpec((1, H, TQ, HD), lambda b, g, i: (b, g, i, 0)),
                  pl.BlockSpec((1, ncmp, HD), lambda b, g, i: (b * G + g, 0, 0)),
                  pl.BlockSpec((1, ncmp, HD), lambda b, g, i: (b * G + g, 0, 0)),
                  k_spec, v_spec, k_spec, v_spec,
                  pl.BlockSpec((1, TQ, LANES), lambda b, g, i: (b, i, C_GATE // LANES + g)),
                  pl.BlockSpec((nsel, ncmp), lambda b, g, i: (0, 0)),
                  pl.BlockSpec((H, HD, H * HD), lambda b, g, i: (0, 0, 0))],
        out_specs=pl.BlockSpec((1, TQ, H * HD), lambda b, g, i: (b, i, g)),
        out_shape=jax.ShapeDtypeStruct((B, S, NSA_HEADS * HD), BF16),
        scratch_shapes=[pltpu.VMEM((R, 1), F32), pltpu.VMEM((R, LANES), F32)],
        compiler_params=_cparams(("parallel", "parallel", "arbitrary")),
        name="nsa_attention",
    )(qn, kcb, vcb, ksn, vsb, kwn, vwb, proj, cover, _head_placement(H))


SWA_TQ = 128


def _swa_kernel(sink_ref, q_ref, k_ref, v_ref, place_ref, o_ref):
    TQ, H = SWA_TQ, SWA_HPG
    R = H * TQ
    g = pl.program_id(1)
    t0 = pl.program_id(2) * TQ
    q = q_ref[0].reshape(R, HD)
    tq = t0 + lax.broadcasted_iota(I32, (TQ, 1), 0)
    WK = SWA_WINDOW + TQ
    ws = pl.multiple_of(jnp.maximum(t0 - SWA_WINDOW, 0), TQ)
    kp = ws + lax.broadcasted_iota(I32, (1, WK), 1)
    bias = jnp.where((kp <= tq) & (kp > tq - SWA_WINDOW), 0.0, NEG)
    s_all = _dot_nt(q, k_ref[0, 0, pl.ds(ws, WK), :])
    CH = ROW_CHUNK
    p_parts, sink_terms = [], []
    for c in range(R // CH):
        sink = sink_ref[g * H + c * CH // TQ]
        sc = s_all[c * CH:(c + 1) * CH] + bias[c * CH % TQ:c * CH % TQ + CH]
        m = jnp.maximum(jnp.max(sc, axis=-1, keepdims=True), sink)
        p_parts.append(jnp.exp(sc - m).astype(BF16))
        sink_terms.append(jnp.exp(sink - m))
    acc = _dot(jnp.concatenate(p_parts, axis=0), v_ref[0, 0, pl.ds(ws, WK), :])
    o = acc[:, :HD] / (acc[:, HD:HD + 1] + jnp.concatenate(sink_terms, axis=0))
    out = jnp.zeros((TQ, H * HD), F32)
    for h in range(H):
        out = out + _dot(o[h * TQ:(h + 1) * TQ].astype(BF16), place_ref[h])
    o_ref[0] = out.astype(BF16)


def swa_attention(qn, kn, vb, sinks):
    B, _, S, _ = qn.shape
    G, H, TQ = SWA_KV, SWA_HPG, SWA_TQ
    assert S >= SWA_WINDOW + TQ
    k_spec = pl.BlockSpec((1, 1, S, HD), lambda b, g, i, sk: (b, g, 0, 0))
    v_spec = pl.BlockSpec((1, 1, S, LANES), lambda b, g, i, sk: (b, g, 0, 0))
    return pl.pallas_call(
        _swa_kernel,
        grid_spec=pltpu.PrefetchScalarGridSpec(
            num_scalar_prefetch=1,
            grid=(B, G, S // TQ),
            in_specs=[pl.BlockSpec((1, H, TQ, HD), lambda b, g, i, sk: (b, g, i, 0)), k_spec, v_spec,
                      pl.BlockSpec((H, HD, H * HD), lambda b, g, i, sk: (0, 0, 0))],
            out_specs=pl.BlockSpec((1, TQ, H * HD), lambda b, g, i, sk: (b, i, g))),
        out_shape=jax.ShapeDtypeStruct((B, S, SWA_HEADS * HD), BF16),
        compiler_params=_cparams(("parallel", "parallel", "arbitrary")),
        name="swa_attention",
    )(sinks.astype(F32), qn, kn, vb, _head_placement(H))


R_GRP = 0
R_EXP = 32


SUBLANES = 8
ROW_TILE = (SUBLANES, LANES)


def _store_row_tiles(ref, val):
    for s in range(SUBLANES):
        ref[:, s, :] = val[:, s * LANES:(s + 1) * LANES]


def _load_row_tiles(ref, rows=slice(None)):
    return [ref[rows, s, :] for s in range(SUBLANES)]


def _router_kernel(x_ref, g_ref, w_ref, b_ref, tri_ref, hn_ref, route_ref, cnt_ref, base_ref):
    i = pl.program_id(0)
    last = pl.num_programs(0) - 1

    @pl.when(i == 0)
    def _():
        base_ref[...] = jnp.zeros_like(base_ref)

    @pl.when(i < last)
    def _():
        _route_tile(x_ref, g_ref, w_ref, b_ref, tri_ref, hn_ref, route_ref, base_ref)

    @pl.when(i == last)
    def _():
        hn_ref[...] = jnp.zeros_like(hn_ref)
        route_ref[...] = jnp.zeros_like(route_ref)

    cnt_ref[...] = base_ref[...]


def _route_tile(x_ref, g_ref, w_ref, b_ref, tri_ref, hn_ref, route_ref, base_ref):
    x = x_ref[...]
    ms = jnp.mean(x * x, axis=-1, keepdims=True)
    h = x * lax.rsqrt(ms + EPS) * g_ref[...]
    _store_row_tiles(hn_ref, h)
    logits = jnp.dot(h, w_ref[...], precision=HI, preferred_element_type=F32) + b_ref[...]
    lane = lax.broadcasted_iota(I32, logits.shape, 1)
    is_grp = lane < N_GROUPS
    gl = jnp.where(is_grp, logits, NEG)
    gmax = jnp.max(gl, axis=-1, keepdims=True)
    lanef = lane.astype(F32)
    gidx = jnp.min(jnp.where(gl == gmax, lanef, float(LANES)), axis=-1, keepdims=True)
    gw = 1.0 / jnp.sum(jnp.where(is_grp, jnp.exp(logits - gmax), 0.0), axis=-1, keepdims=True)
    in_grp = ((lane >> 3) - R_EXP // EXP_PER_GROUP).astype(F32) == gidx
    w0 = jnp.where(in_grp, logits, NEG)
    m1 = jnp.max(w0, axis=-1, keepdims=True)
    i1 = jnp.min(jnp.where(w0 == m1, lanef, float(LANES)), axis=-1, keepdims=True)
    w1 = jnp.where(lanef == i1, NEG, w0)
    m2 = jnp.max(w1, axis=-1, keepdims=True)
    i2 = jnp.min(jnp.where(w1 == m2, lanef, float(LANES)), axis=-1, keepdims=True)
    e2 = jnp.exp(m2 - m1)
    p1 = gw / (1.0 + e2)
    p2 = gw * e2 / (1.0 + e2)
    oh1 = jnp.where(lanef == i1, 1.0, 0.0)
    oh2 = jnp.where(lanef == i2, 1.0, 0.0)
    oh = oh1 + oh2
    before = _dot(tri_ref[...], oh.astype(BF16)) + base_ref[0:1, :]
    r1 = jnp.sum(before * oh1, axis=-1, keepdims=True)
    r2 = jnp.sum(before * oh2, axis=-1, keepdims=True)
    base_ref[...] = base_ref[...] + jnp.sum(oh, axis=0, keepdims=True)
    route = jnp.zeros_like(logits)
    for ln, val in enumerate((i1 - R_EXP, i2 - R_EXP, p1, p2, r1, r2)):
        route = jnp.where(lane == ln, val, route)
    route_ref[...] = route


def moe_router(x, gain, w_grp, b_grp, w_exp, b_exp, tm=256):
    T, D = x.shape
    nt = T // tm
    w = jnp.zeros((D, LANES), F32).at[:, R_GRP:R_GRP + N_GROUPS].set(w_grp).at[:, R_EXP:R_EXP + N_EXPERTS].set(w_exp)
    b = jnp.zeros((1, LANES), F32).at[0, R_GRP:R_GRP + N_GROUPS].set(b_grp).at[0, R_EXP:R_EXP + N_EXPERTS].set(b_exp)
    tri = jnp.asarray(np.tril(np.ones((tm, tm), np.float32), -1), dtype=BF16)
    return pl.pallas_call(
        _router_kernel,
        grid=(nt + 1,),
        in_specs=[pl.BlockSpec((tm, D), lambda i: (jnp.minimum(i, nt - 1), 0)),
                  pl.BlockSpec((1, D), lambda i: (0, 0)),
                  pl.BlockSpec((D, LANES), lambda i: (0, 0)),
                  pl.BlockSpec((1, LANES), lambda i: (0, 0)),
                  pl.BlockSpec((tm, tm), lambda i: (0, 0))],
        out_specs=[pl.BlockSpec((tm,) + ROW_TILE, lambda i: (i, 0, 0)),
                   pl.BlockSpec((tm, LANES), lambda i: (i, 0)),
                   pl.BlockSpec((8, LANES), lambda i: (0, 0))],
        out_shape=[jax.ShapeDtypeStruct((T + tm,) + ROW_TILE, F32), jax.ShapeDtypeStruct((T + tm, LANES), F32),
                   jax.ShapeDtypeStruct((8, LANES), F32)],
        scratch_shapes=[pltpu.VMEM((8, LANES), F32)],
        compiler_params=_cparams(("arbitrary",)),
        name="moe_router",
    )(x, gain.reshape(1, D), w, b, tri)


def _row_copy(src, src_row, dst, dst_row, sem):
    return pltpu.make_async_copy(src.at[pl.ds(src_row, 1)], dst.at[pl.ds(dst_row, 1)], sem)


DISPATCH_TILE = 512


def _wait_rows(hbm_ref, sem, n):
    view = hbm_ref.at[pl.ds(0, n)]
    pltpu.make_async_copy(view, view, sem).wait()


def _dispatch_kernel(dest_ref, x_ref, xb_ref, sem):
    def issue(t, c):
        _row_copy(x_ref, t, xb_ref, dest_ref[0, 0, 2 * t], sem).start()
        _row_copy(x_ref, t, xb_ref, dest_ref[0, 0, 2 * t + 1], sem).start()
        return c

    lax.fori_loop(0, x_ref.shape[0], issue, 0, unroll=4)
    _wait_rows(xb_ref, sem, DISPATCH_TILE)


def moe_dispatch(hn, dest_all, n_tok):
    n_rows = dest_all.shape[0]
    tm = DISPATCH_TILE // 2
    nt = n_rows // DISPATCH_TILE
    n_tok_tiles = n_tok // tm
    return pl.pallas_call(
        _dispatch_kernel,
        grid=(nt,),
        in_specs=[pl.BlockSpec((1, 1, DISPATCH_TILE), lambda i: (i, 0, 0), memory_space=pltpu.SMEM),
                  pl.BlockSpec((tm,) + ROW_TILE, lambda i: (jnp.minimum(i, n_tok_tiles), 0, 0))],
        out_specs=pl.BlockSpec(memory_space=pl.ANY),
        out_shape=jax.ShapeDtypeStruct((n_rows,) + ROW_TILE, F32),
        scratch_shapes=[pltpu.SemaphoreType.DMA],
        compiler_params=_cparams(("arbitrary",)),
        name="moe_dispatch",
    )(dest_all.reshape(nt, 1, DISPATCH_TILE), hn)


def _expert_kernel(be_ref, x_ref, wg_ref, wu_ref, wd_ref, y_ref):
    del be_ref
    x = jnp.concatenate(_load_row_tiles(x_ref), axis=1).astype(BF16)
    hid = _dot(x, wg_ref[0])
    hid = hid * jax.nn.sigmoid(hid) * _dot(x, wu_ref[0])
    _store_row_tiles(y_ref, _dot(hid.astype(BF16), wd_ref[0]))


def moe_experts(xb, blk_expert, w_gate, w_up, w_down):
    n_rows = xb.shape[0]
    D = w_gate.shape[1]
    wspec = lambda shape: pl.BlockSpec((1,) + shape, lambda i, be: (be[i], 0, 0))
    return pl.pallas_call(
        _expert_kernel,
        grid_spec=pltpu.PrefetchScalarGridSpec(
            num_scalar_prefetch=1,
            grid=(n_rows // MOE_BLOCK,),
            in_specs=[pl.BlockSpec((MOE_BLOCK,) + ROW_TILE, lambda i, be: (i, 0, 0)),
                      wspec((D, EXPERT_FF)), wspec((D, EXPERT_FF)), wspec((EXPERT_FF, D))],
            out_specs=pl.BlockSpec((MOE_BLOCK,) + ROW_TILE, lambda i, be: (i, 0, 0))),
        out_shape=jax.ShapeDtypeStruct((n_rows,) + ROW_TILE, F32),
        compiler_params=_cparams(("arbitrary",)),
        name="moe_experts",
    )(blk_expert, xb, w_gate, w_up, w_down)


def _combine_kernel(dest_ref, dest_next_ref, x_ref, route_ref, yb_ref, o_ref, ybuf, sems):
    tm = x_ref.shape[0]
    i = pl.program_id(0)
    slot = i % 2

    def gather(d_ref, s):
        def body(t, c):
            _row_copy(yb_ref, d_ref[0, 0, 2 * t], ybuf.at[s], t, sems.at[s]).start()
            _row_copy(yb_ref, d_ref[0, 0, 2 * t + 1], ybuf.at[s], tm + t, sems.at[s]).start()
            return c

        lax.fori_loop(0, tm, body, 0, unroll=4)

    @pl.when(i == 0)
    def _():
        gather(dest_ref, 0)

    @pl.when(i + 1 < pl.num_programs(0))
    def _():
        gather(dest_next_ref, 1 - slot)

    _wait_rows(yb_ref, sems.at[slot], 2 * tm)
    r = route_ref[...]
    w0, w1 = r[:, 2:3], r[:, 3:4]
    for s in range(SUBLANES):
        cols = slice(s * LANES, (s + 1) * LANES)
        o_ref[:, cols] = x_ref[:, cols] + w0 * ybuf[slot, 0:tm, s, :] + w1 * ybuf[slot, tm:2 * tm, s, :]


def moe_combine(x, yb, dest, route):
    T, D = x.shape
    tm = DISPATCH_TILE // 2
    nt = T // tm
    dest3 = dest.reshape(nt, 1, DISPATCH_TILE)
    return pl.pallas_call(
        _combine_kernel,
        grid=(nt,),
        in_specs=[pl.BlockSpec((1, 1, DISPATCH_TILE), lambda i: (i, 0, 0), memory_space=pltpu.SMEM),
                  pl.BlockSpec((1, 1, DISPATCH_TILE), lambda i: (jnp.minimum(i + 1, nt - 1), 0, 0),
                               memory_space=pltpu.SMEM),
                  pl.BlockSpec((tm, D), lambda i: (i, 0)),
                  pl.BlockSpec((tm, LANES), lambda i: (i, 0)),
                  pl.BlockSpec(memory_space=pl.ANY)],
        out_specs=pl.BlockSpec((tm, D), lambda i: (i, 0)),
        out_shape=jax.ShapeDtypeStruct((T, D), F32),
        scratch_shapes=[pltpu.VMEM((2, DISPATCH_TILE) + ROW_TILE, F32), pltpu.SemaphoreType.DMA((2,))],
        compiler_params=_cparams(("arbitrary",)),
        name="moe_combine",
    )(dest3, dest3, x, route, yb)


def hier_moe_block(x, gain, w_grp, b_grp, w_exp, b_exp, w_gate, w_up, w_down):
    T, D = x.shape
    hn, route, cnt = moe_router(x, gain, w_grp, b_grp, w_exp, b_exp)
    A = 2 * T
    n_rows = -(-A // MOE_BLOCK) * MOE_BLOCK + N_EXPERTS * MOE_BLOCK
    n_blocks = n_rows // MOE_BLOCK
    counts = cnt[0, R_EXP:R_EXP + N_EXPERTS].astype(I32)
    pcounts = (counts + MOE_BLOCK - 1) // MOE_BLOCK * MOE_BLOCK
    pend = jnp.cumsum(pcounts)
    pstart = pend - pcounts
    experts = jnp.arange(N_EXPERTS, dtype=I32)
    flat_e = route[:T, 0:2].astype(I32).reshape(A)
    rank = route[:T, 4:6].astype(I32).reshape(A)
    dest = rank + jnp.sum(jnp.where(flat_e[:, None] == experts[None, :], pstart[None, :], 0), axis=1)
    n_pad = n_rows - A
    pad_cnt = jnp.concatenate([pcounts - counts, (n_rows - pend[-1])[None]])
    pad_start = jnp.concatenate([pstart + counts, pend[-1:]])
    pad_end = jnp.cumsum(pad_cnt)
    k = jnp.arange(n_pad, dtype=I32)
    seg = (k[:, None] >= pad_end[None, :]).astype(I32).sum(axis=1)
    seg_hot = seg[:, None] == jnp.arange(N_EXPERTS + 1, dtype=I32)[None, :]
    pad_dest = k + jnp.sum(jnp.where(seg_hot, (pad_start - (pad_end - pad_cnt))[None, :], 0), axis=1)
    dest_all = jnp.concatenate([dest, pad_dest]).astype(I32)
    blk_expert = jnp.minimum(
        jnp.searchsorted(pend, jnp.arange(n_blocks, dtype=I32) * MOE_BLOCK, side='right'), N_EXPERTS - 1).astype(I32)
    xb = moe_dispatch(hn, dest_all, T)
    yb = moe_experts(xb, blk_expert, w_gate.astype(BF16), w_up.astype(BF16), w_down.astype(BF16))
    return moe_combine(x, yb, dest, route)


def _even_w_in(w_in):
    ng = 3 * NSA_HPG
    pad = jnp.zeros((w_in.shape[0], LANES - ng), w_in.dtype)
    cols = [w_in[:, :C_GATE]]
    for g in range(NSA_KV):
        cols += [w_in[:, C_GATE + g * ng:C_GATE + (g + 1) * ng], pad]
    return jnp.concatenate(cols, axis=1).astype(BF16)


def even_layer(x, B, S, norm, w_in, lb, hg_gain, q_gain, k_gain, cmp_pe, cmp_w1, cmp_w2, w_out):
    T = B * S
    proj = rms_matmul(x, norm, _even_w_in(w_in), tn=N0 // 4).reshape(B, S, N0)
    a_out = hgrn2(proj, lb, hg_gain)
    kvb = C_KV // LANES
    qn, ksn, kwn, vsb, vwb = prep_heads(
        proj, [C_NQ // (NSA_HEADS * HD), kvb + 2, kvb + 4, kvb + 3, kvb + 5],
        [NSA_HEADS, NSA_KV, NSA_KV, NSA_KV, NSA_KV],
        [q_gain, k_gain[1], k_gain[2], None, None],
        [True, True, True, False, False], [HD ** -0.5, 1.0, 1.0, 1.0, 1.0], S)
    kcb, vcb = nsa_compress(proj, kvb, kvb + 1, cmp_pe, cmp_w1, cmp_w2, k_gain[0])
    b_out = nsa_attention(qn, kcb, vcb, ksn, vsb, kwn, vwb, proj)
    return proj_res(a_out.reshape(T, HG_WIDTH), 0, b_out.reshape(T, NSA_HEADS * HD), 0, w_out.astype(BF16), x)


def odd_layer(x, B, S, norm, w_in, q_gain, k_gain, sinks, w_out):
    T = B * S
    proj = rms_matmul(x, norm, w_in.astype(BF16), tn=N1 // 2).reshape(B, S, N1)
    qd = SWA_HEADS * HD
    qn, kn, vb = prep_heads(
        proj, [0, qd // LANES, qd // LANES + 1], [SWA_HEADS, SWA_KV, SWA_KV],
        [q_gain, k_gain, None], [True, True, False], [HD ** -0.5, 1.0, 1.0], S)
    att = swa_attention(qn, kn, vb, sinks).reshape(T, qd)
    return proj_res(att, 0, att, 1, w_out.astype(BF16), x)


def kernel(x, ev_norm, ev_w_in, hg_lb_logits, hg_out_gain, nsa_q_gain, nsa_k_gain, nsa_cmp_pe, nsa_cmp_w1,
           nsa_cmp_w2, ev_w_out, od_norm, od_w_in, swa_q_gain, swa_k_gain, swa_sinks, od_w_out, moe_norm,
           moe_w_grp, moe_b_grp, moe_w_exp, moe_b_exp, moe_w_gate, moe_w_up, moe_w_down):
    B, S, D = x.shape
    lower_bounds = jnp.cumsum(jax.nn.softmax(hg_lb_logits.astype(F32), axis=0), axis=0)
    h = x.reshape(B * S, D)
    h = even_layer(h, B, S, ev_norm[0], ev_w_in[0], lower_bounds[0], hg_out_gain[0], nsa_q_gain[0],
                   nsa_k_gain[0], nsa_cmp_pe[0], nsa_cmp_w1[0], nsa_cmp_w2[0], ev_w_out[0])
    h = hier_moe_block(h, moe_norm[0], moe_w_grp[0], moe_b_grp[0], moe_w_exp[0], moe_b_exp[0],
                       moe_w_gate[0], moe_w_up[0], moe_w_down[0])
    h = odd_layer(h, B, S, od_norm[0], od_w_in[0], swa_q_gain[0], swa_k_gain[0], swa_sinks[0], od_w_out[0])
    h = hier_moe_block(h, moe_norm[1], moe_w_grp[1], moe_b_grp[1], moe_w_exp[1], moe_b_exp[1],
                       moe_w_gate[1], moe_w_up[1], moe_w_down[1])
    return h.reshape(B, S, D)
```

```python
import functools

import numpy as np
import jax
import jax.numpy as jnp
from jax import lax
from jax.experimental import pallas as pl
from jax.experimental.pallas import tpu as pltpu

F32 = jnp.float32
BF16 = jnp.bfloat16
I32 = jnp.int32
HI = lax.Precision.HIGHEST

D_MODEL = 1024
EPS = 1e-6
ROPE_THETA = 500000.0
HD = 64
ROT = HD // 4
HALF = ROT // 2

HG_DIM = 128
HG_WIDTH = 512
HG_HEADS = 4
HG_CHUNK = 64

NSA_HEADS = 8
NSA_KV = 2
NSA_HPG = NSA_HEADS // NSA_KV
CMP_LEN = 32
CMP_STRIDE = 16
CMP_HIDDEN = 256
SEL_LEN = 64
SEL_TOPN = 8
NSA_WINDOW = 512
FORCE_BONUS = 1e4

SWA_HEADS = 16
SWA_KV = 2
SWA_HPG = SWA_HEADS // SWA_KV
SWA_WINDOW = 128

N_GROUPS = 4
EXP_PER_GROUP = 8
N_EXPERTS = 32
EXPERT_FF = 512
MOE_BLOCK = 256

LANES = 128
NEG = -1e30
VMEM_LIMIT = 56 * 1024 * 1024

C_HG = 0
C_NQ = 2048
C_KV = 2560
C_GATE = 3328
N0 = 3584
N1 = 1280


def _cparams(sem):
    return pltpu.CompilerParams(dimension_semantics=sem, vmem_limit_bytes=VMEM_LIMIT)


def _dot(a, b):
    return jnp.dot(a, b, preferred_element_type=F32)


def _dot_nt(a, b):
    return lax.dot_general(a, b, (((1,), (1,)), ((), ())), preferred_element_type=F32)


def _rms_matmul_kernel(x_ref, g_ref, w_ref, o_ref, xn_ref):
    @pl.when(pl.program_id(1) == 0)
    def _():
        x = x_ref[...]
        ms = jnp.mean(x * x, axis=-1, keepdims=True)
        xn_ref[...] = (x * lax.rsqrt(ms + EPS) * g_ref[...]).astype(BF16)

    o_ref[...] = _dot(xn_ref[...], w_ref[...])


def rms_matmul(x, gain, w, tm=512, tn=None):
    T, D = x.shape
    N = w.shape[1]
    tn = N if tn is None else tn
    return pl.pallas_call(
        _rms_matmul_kernel,
        grid=(T // tm, N // tn),
        in_specs=[pl.BlockSpec((tm, D), lambda i, j: (i, 0)),
                  pl.BlockSpec((1, D), lambda i, j: (0, 0)),
                  pl.BlockSpec((D, tn), lambda i, j: (0, j))],
        out_specs=pl.BlockSpec((tm, tn), lambda i, j: (i, j)),
        out_shape=jax.ShapeDtypeStruct((T, N), F32),
        scratch_shapes=[pltpu.VMEM((tm, D), BF16)],
        compiler_params=_cparams(("parallel", "arbitrary")),
        name="rms_matmul",
    )(x, gain.reshape(1, D), w)


def _proj_res_kernel(a_ref, b_ref, wa_ref, wb_ref, r_ref, o_ref):
    acc = _dot(a_ref[...], wa_ref[...]) + _dot(b_ref[...], wb_ref[...])
    o_ref[...] = r_ref[...] + acc


def proj_res(a, a_blk, b, b_blk, w, res, tm=512):
    T, D = res.shape
    K = w.shape[0]
    kh = K // 2
    return pl.pallas_call(
        _proj_res_kernel,
        grid=(T // tm,),
        in_specs=[pl.BlockSpec((tm, kh), lambda i: (i, a_blk)),
                  pl.BlockSpec((tm, kh), lambda i: (i, b_blk)),
                  pl.BlockSpec((kh, D), lambda i: (0, 0)),
                  pl.BlockSpec((kh, D), lambda i: (1, 0)),
                  pl.BlockSpec((tm, D), lambda i: (i, 0))],
        out_specs=pl.BlockSpec((tm, D), lambda i: (i, 0)),
        out_shape=jax.ShapeDtypeStruct((T, D), F32),
        compiler_params=_cparams(("parallel",)),
        name="proj_res",
    )(a, b, w, w, res)


def _rope_tables(pos, width):
    inv = jnp.power(ROPE_THETA, -jnp.arange(HALF, dtype=F32) * 2.0 / ROT)
    ang = pos.astype(F32)[:, None] * inv[None, :]
    cos, sin = jnp.cos(ang), jnp.sin(ang)
    n = pos.shape[0]
    c = jnp.concatenate([cos, cos, jnp.ones((n, HD - ROT), F32)], axis=1)
    s = jnp.concatenate([-sin, sin, jnp.zeros((n, HD - ROT), F32)], axis=1)
    reps = width // HD
    return jnp.tile(c, (1, reps)), jnp.tile(s, (1, reps))


def _block_ones(width):
    idx = np.arange(width) // HD
    return jnp.asarray((idx[:, None] == idx[None, :]).astype(np.float32))


def _head_norm_rope(x, gain, cos, sin, bones, scale):
    ms = jnp.dot(x * x, bones, precision=HI, preferred_element_type=F32) * (1.0 / HD)
    y = x * lax.rsqrt(ms + EPS) * gain
    lane = lax.broadcasted_iota(I32, y.shape, 1) % HD
    partner = jnp.where(lane < HALF, pltpu.roll(y, LANES - HALF, axis=1), pltpu.roll(y, HALF, axis=1))
    out = y * cos + partner * sin
    return out * scale if scale != 1.0 else out


def _prep_kernel(n_heads, norm_flags, scales, *refs):
    nt = len(n_heads)
    x_refs = refs[:nt]
    gain_ref, cos_ref, sin_ref, bones_ref = refs[nt:nt + 4]
    o_refs = refs[nt + 4:]
    cos = cos_ref[...]
    sin = sin_ref[...]
    bones = bones_ref[...]
    lane = lax.broadcasted_iota(I32, cos.shape, 1)
    for t in range(nt):
        for c in range(n_heads[t] // 2):
            x = x_refs[t][0, :, c * LANES:(c + 1) * LANES]
            if norm_flags[t]:
                g = gain_ref[t:t + 1, :]
                x = _head_norm_rope(x, g, cos, sin, bones, scales[t])
                o_refs[t][0, 2 * c] = x[:, :HD].astype(BF16)
                o_refs[t][0, 2 * c + 1] = pltpu.roll(x, HD, axis=1)[:, :HD].astype(BF16)
            else:
                for j, xh in enumerate((x, pltpu.roll(x, HD, axis=1))):
                    v1 = jnp.where(lane < HD, xh, jnp.where(lane == HD, 1.0, 0.0))
                    o_refs[t][0, 2 * c + j] = v1.T.astype(BF16)


def prep_heads(proj, col_blocks, n_heads, gains, norm_flags, scales, S, ts=512):
    B = proj.shape[0]
    nt = len(n_heads)
    cos, sin = _rope_tables(jnp.arange(S), LANES)
    gain_rows = []
    for t in range(nt):
        g = gains[t] if gains[t] is not None else jnp.ones((HD,), F32)
        gain_rows.append(jnp.tile(g.astype(F32), 2))
    gain_arr = jnp.stack(gain_rows)
    in_specs = []
    for t in range(nt):
        w = n_heads[t] * HD
        in_specs.append(pl.BlockSpec((1, ts, w), functools.partial(lambda b, s, cb: (b, s, cb), cb=col_blocks[t])))
    in_specs += [pl.BlockSpec((nt, LANES), lambda b, s: (0, 0)),
                 pl.BlockSpec((ts, LANES), lambda b, s: (s, 0)),
                 pl.BlockSpec((ts, LANES), lambda b, s: (s, 0)),
                 pl.BlockSpec((LANES, LANES), lambda b, s: (0, 0))]
    out_specs, out_shape = [], []
    for t in range(nt):
        if norm_flags[t]:
            out_specs.append(pl.BlockSpec((1, n_heads[t], ts, HD), lambda b, s: (b, 0, s, 0)))
            out_shape.append(jax.ShapeDtypeStruct((B, n_heads[t], S, HD), BF16))
        else:
            out_specs.append(pl.BlockSpec((1, n_heads[t], LANES, ts), lambda b, s: (b, 0, 0, s)))
            out_shape.append(jax.ShapeDtypeStruct((B, n_heads[t], LANES, S), BF16))
    return pl.pallas_call(
        functools.partial(_prep_kernel, tuple(n_heads), tuple(norm_flags), tuple(scales)),
        grid=(B, S // ts),
        in_specs=in_specs,
        out_specs=out_specs,
        out_shape=out_shape,
        compiler_params=_cparams(("parallel", "parallel")),
        name="prep_heads",
    )(*([proj] * nt), gain_arr, cos, sin, _block_ones(LANES))


_N_LEVELS = 6


def _hgrn_constants():
    C = HG_CHUNK
    r = np.arange(C)
    mats = [(r[None, :] <= r[:, None]).astype(np.float32),
            (r[None, :] > r[:, None]).astype(np.float32)]
    masks = []
    for lv in range(_N_LEVELS):
        n = 1 << lv
        blk = r // (2 * n)
        right = (r // n) % 2 == 1
        bnd = blk * 2 * n + n - 1
        m = np.zeros((C, C), np.float32)
        for t in range(C):
            if right[t]:
                m[t, bnd[t] + 1:t + 1] = 1.0
            else:
                m[t, t + 1:bnd[t] + 1] = 1.0
        mats.append(m)
        masks.append((blk[:, None] == blk[None, :]).astype(np.float32))
    masks.append(np.eye(C, dtype=np.float32))
    return jnp.asarray(np.concatenate(mats, axis=0), dtype=BF16), jnp.asarray(np.stack(masks))


def _hgrn_kernel(q_ref, f_ref, i_ref, g_ref, lb_ref, gain_ref, mall_ref, masks_ref, o_ref, st_ref):
    C = HG_CHUNK
    ts = q_ref.shape[1]

    @pl.when(pl.program_id(1) == 0)
    def _():
        st_ref[...] = jnp.zeros_like(st_ref)

    mall = mall_ref[...]
    row = lax.broadcasted_iota(I32, (C, HG_DIM), 0)
    gain = gain_ref[...]

    def chunk(c, carry):
        r0 = pl.multiple_of(c * C, C)
        for h in range(HG_HEADS):
            ls = slice(h * HG_DIM, (h + 1) * HG_DIM)
            q = q_ref[0, pl.ds(r0, C), ls]
            z = f_ref[0, pl.ds(r0, C), ls]
            v = i_ref[0, pl.ds(r0, C), ls]
            g = g_ref[0, pl.ds(r0, C), ls]
            lb = lb_ref[:, ls]
            lf = jnp.log(lb + (1.0 - lb) * jax.nn.sigmoid(z))
            k = (1.0 - lb) * jax.nn.sigmoid(-z)
            lf_hi = lf.astype(BF16)
            lf_lo = (lf - lf_hi.astype(F32)).astype(BF16)
            seg = _dot(mall, jnp.concatenate([lf_hi, lf_lo], axis=1))
            ex = jnp.exp(seg[:, :HG_DIM] + seg[:, HG_DIM:])
            qs = q * ex[0:C]
            ks = k * ex[C:2 * C]
            st = st_ref[h]
            o = _dot_nt(qs.astype(BF16), st.astype(BF16))
            a = jnp.where(masks_ref[_N_LEVELS] > 0.5, _dot_nt(q.astype(BF16), k.astype(BF16)), 0.0)
            for lv in range(_N_LEVELS):
                e = ex[(2 + lv) * C:(3 + lv) * C]
                right = ((row >> lv) & 1) == 1
                qe = jnp.where(right, q * e, 0.0).astype(BF16)
                ke = jnp.where(right, 0.0, k * e).astype(BF16)
                a = a + jnp.where(masks_ref[lv] > 0.5, _dot_nt(qe, ke), 0.0)
            vb = v.astype(BF16)
            o = o + _dot(a.astype(BF16), vb)
            decay = ex[C - 1:C]
            st_ref[h] = st * decay + _dot(v.T.astype(BF16), ks.astype(BF16))
            ms = jnp.mean(o * o, axis=-1, keepdims=True)
            y = o * lax.rsqrt(ms + EPS) * gain * (g * jax.nn.sigmoid(g))
            o_ref[0, pl.ds(r0, C), ls] = y.astype(BF16)
        return carry

    lax.fori_loop(0, ts // C, chunk, 0)


def hgrn2(proj, lb, out_gain, ts=512):
    B, S, _ = proj.shape
    mall, masks = _hgrn_constants()
    W = HG_WIDTH
    return pl.pallas_call(
        _hgrn_kernel,
        grid=(B, S // ts),
        in_specs=[pl.BlockSpec((1, ts, W), lambda b, s: (b, s, 0)),
                  pl.BlockSpec((1, ts, W), lambda b, s: (b, s, 1)),
                  pl.BlockSpec((1, ts, W), lambda b, s: (b, s, 2)),
                  pl.BlockSpec((1, ts, W), lambda b, s: (b, s, 3)),
                  pl.BlockSpec((1, W), lambda b, s: (0, 0)),
                  pl.BlockSpec((1, HG_DIM), lambda b, s: (0, 0)),
                  pl.BlockSpec(mall.shape, lambda b, s: (0, 0)),
                  pl.BlockSpec(masks.shape, lambda b, s: (0, 0, 0))],
        out_specs=pl.BlockSpec((1, ts, W), lambda b, s: (b, s, 0)),
        out_shape=jax.ShapeDtypeStruct((B, S, W), BF16),
        scratch_shapes=[pltpu.VMEM((HG_HEADS, HG_DIM, HG_DIM), F32)],
        compiler_params=_cparams(("parallel", "arbitrary")),
        name="hgrn2",
    )(proj, proj, proj, proj, lb.reshape(1, W), out_gain.reshape(1, HG_DIM), mall, masks)


def _compress_kernel(xk_ref, xv_ref, w1k_ref, w1v_ref, pek_ref, pev_ref, w2k_ref, w2v_ref,
                     gain_ref, cos_ref, sin_ref, bones_ref, ok_ref, ov_ref):
    nc = xk_ref.shape[1] // CMP_STRIDE

    def hidden(x_ref, w1_ref, pe_ref):
        width = NSA_KV * CMP_HIDDEN
        u0 = jnp.zeros((nc, width), F32)
        u1 = jnp.zeros((nc, width), F32)
        const = jnp.zeros((8, width), F32)
        for p in range(CMP_STRIDE):
            xp = x_ref[0, pl.ds(p, nc, stride=CMP_STRIDE), :].astype(BF16)
            u0 = u0 + _dot(xp, w1_ref[p])
            u1 = u1 + _dot(xp, w1_ref[CMP_STRIDE + p])
            const = const + _dot(pe_ref[p].astype(BF16), w1_ref[p])
            const = const + _dot(pe_ref[CMP_STRIDE + p].astype(BF16), w1_ref[CMP_STRIDE + p])
        h = u0 + pltpu.roll(u1, nc - 1, axis=0) + const[0:1]
        return jax.nn.gelu(h).astype(BF16)

    yk = _dot(hidden(xk_ref, w1k_ref, pek_ref), w2k_ref[...])
    yk = _head_norm_rope(yk, gain_ref[...], cos_ref[...], sin_ref[...], bones_ref[...], 1.0)
    ok_ref[0] = yk[:, :HD].astype(BF16)
    ok_ref[1] = pltpu.roll(yk, HD, axis=1)[:, :HD].astype(BF16)
    yv_t = _dot_nt(w2v_ref[...], hidden(xv_ref, w1v_ref, pev_ref)).astype(BF16)
    ov_ref[0] = yv_t[:HD]
    ov_ref[1] = yv_t[HD:]


def _group_diag(w):
    z = jnp.zeros_like(w)
    return jnp.concatenate([jnp.concatenate([w, z], axis=-1), jnp.concatenate([z, w], axis=-1)], axis=-2)


def nsa_compress(proj, kc_blk, vc_blk, pe, w1, w2, k_gain):
    B, S, _ = proj.shape
    nc = S // CMP_STRIDE
    pos_end = jnp.arange(nc) * CMP_STRIDE + CMP_LEN - 1
    cos, sin = _rope_tables(pos_end, LANES)
    w1d = _group_diag(w1.reshape(2, CMP_LEN, HD, CMP_HIDDEN)).astype(BF16)
    w2d = _group_diag(w2).astype(BF16)
    pe2 = jnp.broadcast_to(jnp.tile(pe, (1, 1, NSA_KV))[:, :, None, :], (2, CMP_LEN, 8, LANES))
    full = lambda shape: pl.BlockSpec(shape, lambda b: (0,) * len(shape))
    w1_shape = (CMP_LEN, LANES, NSA_KV * CMP_HIDDEN)
    return pl.pallas_call(
        _compress_kernel,
        grid=(B,),
        in_specs=[pl.BlockSpec((1, S, LANES), lambda b: (b, 0, kc_blk)),
                  pl.BlockSpec((1, S, LANES), lambda b: (b, 0, vc_blk)),
                  full(w1_shape), full(w1_shape),
                  full((CMP_LEN, 8, LANES)), full((CMP_LEN, 8, LANES)),
                  full((NSA_KV * CMP_HIDDEN, LANES)), full((LANES, NSA_KV * CMP_HIDDEN)),
                  full((1, LANES)), full((nc, LANES)), full((nc, LANES)), full((LANES, LANES))],
        out_specs=[pl.BlockSpec((NSA_KV, nc, HD), lambda b: (b, 0, 0)),
                   pl.BlockSpec((NSA_KV, HD, nc), lambda b: (b, 0, 0))],
        out_shape=[jax.ShapeDtypeStruct((B * NSA_KV, nc, HD), BF16),
                   jax.ShapeDtypeStruct((B * NSA_KV, HD, nc), BF16)],
        compiler_params=_cparams(("parallel",)),
        name="nsa_compress",
    )(proj, proj, w1d[0], w1d[1], pe2[0], pe2[1], w2d[0], w2d[1].T,
      jnp.tile(k_gain.astype(F32), 2).reshape(1, LANES), cos, sin, _block_ones(LANES))


NSA_TQ = 128
NSA_TK = 512


def _nsa_kernel(q_ref, kc_ref, vc_ref, ks_ref, vs_ref, kw_ref, vw_ref, gate_ref, cover_ref,
                o_ref, acc_ref, sel_ref):
    TQ, TK, H = NSA_TQ, NSA_TK, NSA_HPG
    R = H * TQ
    t0 = pl.program_id(2) * TQ
    q = q_ref[0].reshape(R, HD)
    tl = t0 + lax.broadcasted_iota(I32, (1, TQ), 1)

    def all_heads(x):
        return jnp.concatenate([x] * H, axis=1)

    kc = kc_ref[0]
    ncmp = kc.shape[0]
    cmp_end = lax.broadcasted_iota(I32, (ncmp, 1), 0) * CMP_STRIDE + (CMP_LEN - 1)
    valid = all_heads(jnp.where(cmp_end <= tl, 1.0, 0.0))
    s = jnp.where(valid > 0.5, _dot_nt(kc, q), NEG)
    p = jnp.exp(s - jnp.max(s, axis=0, keepdims=True)) * valid
    p_c = p / jnp.maximum(jnp.sum(p, axis=0, keepdims=True), 1e-30)
    o_c = _dot(vc_ref[0], p_c.astype(BF16))

    p_tok = p_c[:, 0:TQ]
    for h in range(1, H):
        p_tok = p_tok + p_c[:, h * TQ:(h + 1) * TQ]
    p_hi = p_tok.astype(BF16)
    p_lo = (p_tok - p_hi.astype(F32)).astype(BF16)
    imp = _dot(cover_ref[...], p_hi) + _dot(cover_ref[...], p_lo)
    nsel = imp.shape[0]
    jj = lax.broadcasted_iota(I32, (nsel, TQ), 0)
    jf = jj.astype(F32)
    cur = tl >> 6
    allowed = jj * SEL_LEN <= tl
    forced = (jj == 0) | (jj == cur) | (jj == cur - 1)
    score = jnp.where(allowed, imp + jnp.where(forced, FORCE_BONUS, 0.0), NEG)
    sel = jnp.zeros((nsel, TQ), F32)
    for _ in range(min(SEL_TOPN, nsel)):
        mx = jnp.max(score, axis=0, keepdims=True)
        first = jnp.min(jnp.where(score == mx, jf, float(nsel)), axis=0, keepdims=True)
        hit = jf == first
        sel = jnp.where(hit & (mx > 0.5 * NEG), 1.0, sel)
        score = jnp.where(hit, NEG, score)
    sel_ref[...] = sel

    acc_ref[...] = jnp.zeros(acc_ref.shape, F32)
    key_in_tile = lax.broadcasted_iota(I32, (TK, 1), 0)
    blocks_per_tile = TK // SEL_LEN

    def sweep(kt, m_old):
        k0 = pl.multiple_of(kt * TK, TK)
        sel_t = sel_ref[pl.ds(pl.multiple_of(kt * blocks_per_tile, blocks_per_tile), blocks_per_tile), :]
        chosen = jnp.concatenate(
            [jnp.broadcast_to(sel_t[j:j + 1, :], (SEL_LEN, TQ)) for j in range(blocks_per_tile)], axis=0)
        bias = jnp.where((chosen > 0.5) & (k0 + key_in_tile <= tl), 0.0, NEG)
        s = _dot_nt(ks_ref[0, 0, pl.ds(k0, TK), :], q) + all_heads(bias)
        m_new = jnp.maximum(m_old, jnp.max(s, axis=0, keepdims=True))
        p = jnp.exp(s - m_new).astype(BF16)
        acc_ref[...] = jnp.exp(m_old - m_new) * acc_ref[...] + _dot(vs_ref[0, 0, :, pl.ds(k0, TK)], p)
        return m_new

    lax.fori_loop(0, t0 // TK + 1, sweep, jnp.full((1, R), NEG, F32))
    acc = acc_ref[...]
    o_s = acc[:HD] / jnp.maximum(acc[HD:HD + 1], 1e-30)

    WK = NSA_WINDOW + TQ
    ws = pl.multiple_of(jnp.maximum(t0 - NSA_WINDOW, 0), TQ)
    kp = ws + lax.broadcasted_iota(I32, (WK, 1), 0)
    wbias = jnp.where((kp <= tl) & (kp > tl - NSA_WINDOW), 0.0, NEG)
    s = _dot_nt(kw_ref[0, 0, pl.ds(ws, WK), :], q) + all_heads(wbias)
    p = jnp.exp(s - jnp.max(s, axis=0, keepdims=True)).astype(BF16)
    acc_w = _dot(vw_ref[0, 0, :, pl.ds(ws, WK)], p)
    o_w = acc_w[:HD] / acc_w[HD:HD + 1]

    sg = jax.nn.sigmoid(gate_ref[0]).T
    heads_out = []
    for h in range(H):
        cols = slice(h * TQ, (h + 1) * TQ)
        heads_out.append(sg[3 * h:3 * h + 1] * o_c[:, cols] + sg[3 * h + 1:3 * h + 2] * o_s[:, cols]
                         + sg[3 * h + 2:3 * h + 3] * o_w[:, cols])
    o_ref[0] = jnp.concatenate(heads_out, axis=0).T.astype(BF16)


def nsa_attention(qn, kcb, vcb, ksn, vsb, kwn, vwb, proj):
    B, _, S, _ = qn.shape
    G, H, TQ = NSA_KV, NSA_HPG, NSA_TQ
    assert S % NSA_TK == 0 and S >= NSA_WINDOW + TQ
    ncmp = kcb.shape[1]
    nsel = S // SEL_LEN
    n = np.arange(ncmp)[:, None] * CMP_STRIDE
    j = np.arange(nsel)[None, :] * SEL_LEN
    cover = jnp.asarray(((n < j + SEL_LEN) & (n + CMP_LEN > j)).astype(np.float32).T, dtype=BF16)
    R = H * TQ
    k_spec = pl.BlockSpec((1, 1, S, HD), lambda b, g, i: (b, g, 0, 0))
    v_spec = pl.BlockSpec((1, 1, LANES, S), lambda b, g, i: (b, g, 0, 0))
    return pl.pallas_call(
        _nsa_kernel,
        grid=(B, G, S // TQ),
        in_specs=[pl.BlockSpec((1, H, TQ, HD), lambda b, g, i: (b, g, i, 0)),
                  pl.BlockSpec((1, ncmp, HD), lambda b, g, i: (b * G + g, 0, 0)),
                  pl.BlockSpec((1, HD, ncmp), lambda b, g, i: (b * G + g, 0, 0)),
                  k_spec, v_spec, k_spec, v_spec,
                  pl.BlockSpec((1, TQ, LANES), lambda b, g, i: (b, i, C_GATE // LANES + g)),
                  pl.BlockSpec((nsel, ncmp), lambda b, g, i: (0, 0))],
        out_specs=pl.BlockSpec((1, TQ, H * HD), lambda b, g, i: (b, i, g)),
        out_shape=jax.ShapeDtypeStruct((B, S, NSA_HEADS * HD), BF16),
        scratch_shapes=[pltpu.VMEM((LANES, R), F32), pltpu.VMEM((nsel, TQ), F32)],
        compiler_params=_cparams(("parallel", "parallel", "arbitrary")),
        name="nsa_attention",
    )(qn, kcb, vcb, ksn, vsb, kwn, vwb, proj, cover)


SWA_TQ = 128


def _swa_kernel(sink_ref, q_ref, k_ref, v_ref, o_ref):
    TQ, H = SWA_TQ, SWA_HPG
    R = H * TQ
    g = pl.program_id(1)
    t0 = pl.program_id(2) * TQ
    q = q_ref[0].reshape(R, HD)
    tl = t0 + lax.broadcasted_iota(I32, (1, TQ), 1)
    WK = SWA_WINDOW + TQ
    ws = pl.multiple_of(jnp.maximum(t0 - SWA_WINDOW, 0), TQ)
    kp = ws + lax.broadcasted_iota(I32, (WK, 1), 0)
    bias = jnp.where((kp <= tl) & (kp > tl - SWA_WINDOW), 0.0, NEG)
    s = _dot_nt(k_ref[0, 0, pl.ds(ws, WK), :], q) + jnp.concatenate([bias] * H, axis=1)
    head = lax.broadcasted_iota(I32, (1, R), 1) // TQ
    sink = jnp.zeros((1, R), F32)
    for h in range(H):
        sink = jnp.where(head == h, sink_ref[g * H + h], sink)
    m = jnp.maximum(jnp.max(s, axis=0, keepdims=True), sink)
    p = jnp.exp(s - m).astype(BF16)
    acc = _dot(v_ref[0, 0, :, pl.ds(ws, WK)], p)
    o = acc[:HD] / (acc[HD:HD + 1] + jnp.exp(sink - m))
    o_ref[0] = jnp.concatenate([o[:, h * TQ:(h + 1) * TQ] for h in range(H)], axis=0).T.astype(BF16)


def swa_attention(qn, kn, vb, sinks):
    B, _, S, _ = qn.shape
    G, H, TQ = SWA_KV, SWA_HPG, SWA_TQ
    assert S >= SWA_WINDOW + TQ
    k_spec = pl.BlockSpec((1, 1, S, HD), lambda b, g, i, sk: (b, g, 0, 0))
    v_spec = pl.BlockSpec((1, 1, LANES, S), lambda b, g, i, sk: (b, g, 0, 0))
    return pl.pallas_call(
        _swa_kernel,
        grid_spec=pltpu.PrefetchScalarGridSpec(
            num_scalar_prefetch=1,
            grid=(B, G, S // TQ),
            in_specs=[pl.BlockSpec((1, H, TQ, HD), lambda b, g, i, sk: (b, g, i, 0)), k_spec, v_spec],
            out_specs=pl.BlockSpec((1, TQ, H * HD), lambda b, g, i, sk: (b, i, g))),
        out_shape=jax.ShapeDtypeStruct((B, S, SWA_HEADS * HD), BF16),
        compiler_params=_cparams(("parallel", "parallel", "arbitrary")),
        name="swa_attention",
    )(sinks.astype(F32), qn, kn, vb)


R_GRP = 0
R_EXP = 32


SUBLANES = 8
ROW_TILE = (SUBLANES, LANES)


def _store_row_tiles(ref, val):
    for s in range(SUBLANES):
        ref[:, s, :] = val[:, s * LANES:(s + 1) * LANES]


def _load_row_tiles(ref, rows=slice(None)):
    return [ref[rows, s, :] for s in range(SUBLANES)]


def _router_kernel(x_ref, g_ref, w_ref, b_ref, tri_ref, hn_ref, route_ref, cnt_ref, base_ref):
    i = pl.program_id(0)
    last = pl.num_programs(0) - 1

    @pl.when(i == 0)
    def _():
        base_ref[...] = jnp.zeros_like(base_ref)

    @pl.when(i < last)
    def _():
        _route_tile(x_ref, g_ref, w_ref, b_ref, tri_ref, hn_ref, route_ref, base_ref)

    @pl.when(i == last)
    def _():
        hn_ref[...] = jnp.zeros_like(hn_ref)
        route_ref[...] = jnp.zeros_like(route_ref)

    cnt_ref[...] = base_ref[...]


def _route_tile(x_ref, g_ref, w_ref, b_ref, tri_ref, hn_ref, route_ref, base_ref):
    x = x_ref[...]
    ms = jnp.mean(x * x, axis=-1, keepdims=True)
    h = x * lax.rsqrt(ms + EPS) * g_ref[...]
    _store_row_tiles(hn_ref, h)
    logits = jnp.dot(h, w_ref[...], precision=HI, preferred_element_type=F32) + b_ref[...]
    lane = lax.broadcasted_iota(I32, logits.shape, 1)
    is_grp = lane < N_GROUPS
    gl = jnp.where(is_grp, logits, NEG)
    gmax = jnp.max(gl, axis=-1, keepdims=True)
    lanef = lane.astype(F32)
    gidx = jnp.min(jnp.where(gl == gmax, lanef, float(LANES)), axis=-1, keepdims=True)
    gw = 1.0 / jnp.sum(jnp.where(is_grp, jnp.exp(logits - gmax), 0.0), axis=-1, keepdims=True)
    in_grp = ((lane >> 3) - R_EXP // EXP_PER_GROUP).astype(F32) == gidx
    w0 = jnp.where(in_grp, logits, NEG)
    m1 = jnp.max(w0, axis=-1, keepdims=True)
    i1 = jnp.min(jnp.where(w0 == m1, lanef, float(LANES)), axis=-1, keepdims=True)
    w1 = jnp.where(lanef == i1, NEG, w0)
    m2 = jnp.max(w1, axis=-1, keepdims=True)
    i2 = jnp.min(jnp.where(w1 == m2, lanef, float(LANES)), axis=-1, keepdims=True)
    e2 = jnp.exp(m2 - m1)
    p1 = gw / (1.0 + e2)
    p2 = gw * e2 / (1.0 + e2)
    oh1 = jnp.where(lanef == i1, 1.0, 0.0)
    oh2 = jnp.where(lanef == i2, 1.0, 0.0)
    oh = oh1 + oh2
    before = _dot(tri_ref[...], oh.astype(BF16)) + base_ref[0:1, :]
    r1 = jnp.sum(before * oh1, axis=-1, keepdims=True)
    r2 = jnp.sum(before * oh2, axis=-1, keepdims=True)
    base_ref[...] = base_ref[...] + jnp.sum(oh, axis=0, keepdims=True)
    route = jnp.zeros_like(logits)
    for ln, val in enumerate((i1 - R_EXP, i2 - R_EXP, p1, p2, r1, r2)):
        route = jnp.where(lane == ln, val, route)
    route_ref[...] = route


def moe_router(x, gain, w_grp, b_grp, w_exp, b_exp, tm=256):
    T, D = x.shape
    nt = T // tm
    w = jnp.zeros((D, LANES), F32).at[:, R_GRP:R_GRP + N_GROUPS].set(w_grp).at[:, R_EXP:R_EXP + N_EXPERTS].set(w_exp)
    b = jnp.zeros((1, LANES), F32).at[0, R_GRP:R_GRP + N_GROUPS].set(b_grp).at[0, R_EXP:R_EXP + N_EXPERTS].set(b_exp)
    tri = jnp.asarray(np.tril(np.ones((tm, tm), np.float32), -1), dtype=BF16)
    return pl.pallas_call(
        _router_kernel,
        grid=(nt + 1,),
        in_specs=[pl.BlockSpec((tm, D), lambda i: (jnp.minimum(i, nt - 1), 0)),
                  pl.BlockSpec((1, D), lambda i: (0, 0)),
                  pl.BlockSpec((D, LANES), lambda i: (0, 0)),
                  pl.BlockSpec((1, LANES), lambda i: (0, 0)),
                  pl.BlockSpec((tm, tm), lambda i: (0, 0))],
        out_specs=[pl.BlockSpec((tm,) + ROW_TILE, lambda i: (i, 0, 0)),
                   pl.BlockSpec((tm, LANES), lambda i: (i, 0)),
                   pl.BlockSpec((8, LANES), lambda i: (0, 0))],
        out_shape=[jax.ShapeDtypeStruct((T + tm,) + ROW_TILE, F32), jax.ShapeDtypeStruct((T + tm, LANES), F32),
                   jax.ShapeDtypeStruct((8, LANES), F32)],
        scratch_shapes=[pltpu.VMEM((8, LANES), F32)],
        compiler_params=_cparams(("arbitrary",)),
        name="moe_router",
    )(x, gain.reshape(1, D), w, b, tri)


def _row_copy(src, src_row, dst, dst_row, sem):
    return pltpu.make_async_copy(src.at[pl.ds(src_row, 1)], dst.at[pl.ds(dst_row, 1)], sem)


DISPATCH_TILE = 512


def _wait_rows(hbm_ref, sem, n):
    view = hbm_ref.at[pl.ds(0, n)]
    pltpu.make_async_copy(view, view, sem).wait()


def _dispatch_kernel(dest_ref, x_ref, xb_ref, stage, sems):
    i = pl.program_id(0)
    last = pl.num_programs(0) - 1
    slot = i % 2

    @pl.when(i >= 2)
    def _():
        _wait_rows(xb_ref, sems.at[slot], DISPATCH_TILE)

    stage[slot] = x_ref[...]

    def issue(t, c):
        _row_copy(stage.at[slot], t, xb_ref, dest_ref[0, 0, 2 * t], sems.at[slot]).start()
        _row_copy(stage.at[slot], t, xb_ref, dest_ref[0, 0, 2 * t + 1], sems.at[slot]).start()
        return c

    lax.fori_loop(0, x_ref.shape[0], issue, 0, unroll=4)

    @pl.when(i == last)
    def _():
        _wait_rows(xb_ref, sems.at[slot], DISPATCH_TILE)

    @pl.when((i == last) & (i >= 1))
    def _():
        _wait_rows(xb_ref, sems.at[1 - slot], DISPATCH_TILE)


def moe_dispatch(hn, dest_all, n_tok):
    n_rows = dest_all.shape[0]
    tm = DISPATCH_TILE // 2
    nt = n_rows // DISPATCH_TILE
    n_tok_tiles = n_tok // tm
    return pl.pallas_call(
        _dispatch_kernel,
        grid=(nt,),
        in_specs=[pl.BlockSpec((1, 1, DISPATCH_TILE), lambda i: (i, 0, 0), memory_space=pltpu.SMEM),
                  pl.BlockSpec((tm,) + ROW_TILE, lambda i: (jnp.minimum(i, n_tok_tiles), 0, 0))],
        out_specs=pl.BlockSpec(memory_space=pl.ANY),
        out_shape=jax.ShapeDtypeStruct((n_rows,) + ROW_TILE, F32),
        scratch_shapes=[pltpu.VMEM((2, tm) + ROW_TILE, F32), pltpu.SemaphoreType.DMA((2,))],
        compiler_params=_cparams(("arbitrary",)),
        name="moe_dispatch",
    )(dest_all.reshape(nt, 1, DISPATCH_TILE), hn)


def _expert_kernel(be_ref, x_ref, wg_ref, wu_ref, wd_ref, y_ref, wg_b, wu_b, wd_b):
    i = pl.program_id(0)

    @pl.when((i == 0) | (be_ref[i] != be_ref[jnp.maximum(i - 1, 0)]))
    def _():
        wg_b[...] = wg_ref[0].astype(BF16)
        wu_b[...] = wu_ref[0].astype(BF16)
        wd_b[...] = wd_ref[0].astype(BF16)

    x = jnp.concatenate(_load_row_tiles(x_ref), axis=1).astype(BF16)
    hid = _dot(x, wg_b[...])
    hid = hid * jax.nn.sigmoid(hid) * _dot(x, wu_b[...])
    _store_row_tiles(y_ref, _dot(hid.astype(BF16), wd_b[...]))


def moe_experts(xb, blk_expert, w_gate, w_up, w_down):
    n_rows = xb.shape[0]
    D = w_gate.shape[1]
    wspec = lambda shape: pl.BlockSpec((1,) + shape, lambda i, be: (be[i], 0, 0))
    return pl.pallas_call(
        _expert_kernel,
        grid_spec=pltpu.PrefetchScalarGridSpec(
            num_scalar_prefetch=1,
            grid=(n_rows // MOE_BLOCK,),
            in_specs=[pl.BlockSpec((MOE_BLOCK,) + ROW_TILE, lambda i, be: (i, 0, 0)),
                      wspec((D, EXPERT_FF)), wspec((D, EXPERT_FF)), wspec((EXPERT_FF, D))],
            out_specs=pl.BlockSpec((MOE_BLOCK,) + ROW_TILE, lambda i, be: (i, 0, 0)),
            scratch_shapes=[pltpu.VMEM((D, EXPERT_FF), BF16), pltpu.VMEM((D, EXPERT_FF), BF16),
                            pltpu.VMEM((EXPERT_FF, D), BF16)]),
        out_shape=jax.ShapeDtypeStruct((n_rows,) + ROW_TILE, F32),
        compiler_params=_cparams(("arbitrary",)),
        name="moe_experts",
    )(blk_expert, xb, w_gate, w_up, w_down)


def _combine_kernel(dest_ref, dest_next_ref, x_ref, route_ref, yb_ref, o_ref, ybuf, sems):
    tm = x_ref.shape[0]
    i = pl.program_id(0)
    slot = i % 2

    def gather(d_ref, s):
        def body(t, c):
            _row_copy(yb_ref, d_ref[0, 0, 2 * t], ybuf.at[s], t, sems.at[s]).start()
            _row_copy(yb_ref, d_ref[0, 0, 2 * t + 1], ybuf.at[s], tm + t, sems.at[s]).start()
            return c

        lax.fori_loop(0, tm, body, 0, unroll=4)

    @pl.when(i == 0)
    def _():
        gather(dest_ref, 0)

    @pl.when(i + 1 < pl.num_programs(0))
    def _():
        gather(dest_next_ref, 1 - slot)

    _wait_rows(yb_ref, sems.at[slot], 2 * tm)
    r = route_ref[...]
    w0, w1 = r[:, 2:3], r[:, 3:4]
    for s in range(SUBLANES):
        cols = slice(s * LANES, (s + 1) * LANES)
        o_ref[:, cols] = x_ref[:, cols] + w0 * ybuf[slot, 0:tm, s, :] + w1 * ybuf[slot, tm:2 * tm, s, :]


def moe_combine(x, yb, dest, route):
    T, D = x.shape
    tm = DISPATCH_TILE // 2
    nt = T // tm
    dest3 = dest.reshape(nt, 1, DISPATCH_TILE)
    return pl.pallas_call(
        _combine_kernel,
        grid=(nt,),
        in_specs=[pl.BlockSpec((1, 1, DISPATCH_TILE), lambda i: (i, 0, 0), memory_space=pltpu.SMEM),
                  pl.BlockSpec((1, 1, DISPATCH_TILE), lambda i: (jnp.minimum(i + 1, nt - 1), 0, 0),
                               memory_space=pltpu.SMEM),
                  pl.BlockSpec((tm, D), lambda i: (i, 0)),
                  pl.BlockSpec((tm, LANES), lambda i: (i, 0)),
                  pl.BlockSpec(memory_space=pl.ANY)],
        out_specs=pl.BlockSpec((tm, D), lambda i: (i, 0)),
        out_shape=jax.ShapeDtypeStruct((T, D), F32),
        scratch_shapes=[pltpu.VMEM((2, DISPATCH_TILE) + ROW_TILE, F32), pltpu.SemaphoreType.DMA((2,))],
        compiler_params=_cparams(("arbitrary",)),
        name="moe_combine",
    )(dest3, dest3, x, route, yb)


def hier_moe_block(x, gain, w_grp, b_grp, w_exp, b_exp, w_gate, w_up, w_down):
    T, D = x.shape
    hn, route, cnt = moe_router(x, gain, w_grp, b_grp, w_exp, b_exp)
    A = 2 * T
    n_rows = -(-A // MOE_BLOCK) * MOE_BLOCK + N_EXPERTS * MOE_BLOCK
    n_blocks = n_rows // MOE_BLOCK
    counts = cnt[0, R_EXP:R_EXP + N_EXPERTS].astype(I32)
    pcounts = (counts + MOE_BLOCK - 1) // MOE_BLOCK * MOE_BLOCK
    pend = jnp.cumsum(pcounts)
    pstart = pend - pcounts
    experts = jnp.arange(N_EXPERTS, dtype=I32)
    flat_e = route[:T, 0:2].astype(I32).reshape(A)
    rank = route[:T, 4:6].astype(I32).reshape(A)
    dest = rank + jnp.sum(jnp.where(flat_e[:, None] == experts[None, :], pstart[None, :], 0), axis=1)
    n_pad = n_rows - A
    pad_cnt = jnp.concatenate([pcounts - counts, (n_rows - pend[-1])[None]])
    pad_start = jnp.concatenate([pstart + counts, pend[-1:]])
    pad_end = jnp.cumsum(pad_cnt)
    k = jnp.arange(n_pad, dtype=I32)
    seg = (k[:, None] >= pad_end[None, :]).astype(I32).sum(axis=1)
    seg_hot = seg[:, None] == jnp.arange(N_EXPERTS + 1, dtype=I32)[None, :]
    pad_dest = k + jnp.sum(jnp.where(seg_hot, (pad_start - (pad_end - pad_cnt))[None, :], 0), axis=1)
    dest_all = jnp.concatenate([dest, pad_dest]).astype(I32)
    blk_start = jnp.arange(n_blocks, dtype=I32) * MOE_BLOCK
    blk_expert = jnp.minimum((pend[None, :] <= blk_start[:, None]).astype(I32).sum(axis=1), N_EXPERTS - 1)
    xb = moe_dispatch(hn, dest_all, T)
    yb = moe_experts(xb, blk_expert, w_gate, w_up, w_down)
    return moe_combine(x, yb, dest, route)


def _even_w_in(w_in):
    ng = 3 * NSA_HPG
    pad = jnp.zeros((w_in.shape[0], LANES - ng), w_in.dtype)
    cols = [w_in[:, :C_GATE]]
    for g in range(NSA_KV):
        cols += [w_in[:, C_GATE + g * ng:C_GATE + (g + 1) * ng], pad]
    return jnp.concatenate(cols, axis=1).astype(BF16)


def even_layer(x, B, S, norm, w_in, lb, hg_gain, q_gain, k_gain, cmp_pe, cmp_w1, cmp_w2, w_out):
    T = B * S
    proj = rms_matmul(x, norm, _even_w_in(w_in), tn=N0 // 4).reshape(B, S, N0)
    a_out = hgrn2(proj, lb, hg_gain)
    kvb = C_KV // LANES
    qn, ksn, kwn, vsb, vwb = prep_heads(
        proj, [C_NQ // (NSA_HEADS * HD), kvb + 2, kvb + 4, kvb + 3, kvb + 5],
        [NSA_HEADS, NSA_KV, NSA_KV, NSA_KV, NSA_KV],
        [q_gain, k_gain[1], k_gain[2], None, None],
        [True, True, True, False, False], [HD ** -0.5, 1.0, 1.0, 1.0, 1.0], S)
    kcb, vcb = nsa_compress(proj, kvb, kvb + 1, cmp_pe, cmp_w1, cmp_w2, k_gain[0])
    b_out = nsa_attention(qn, kcb, vcb, ksn, vsb, kwn, vwb, proj)
    return proj_res(a_out.reshape(T, HG_WIDTH), 0, b_out.reshape(T, NSA_HEADS * HD), 0, w_out.astype(BF16), x)


def odd_layer(x, B, S, norm, w_in, q_gain, k_gain, sinks, w_out):
    T = B * S
    proj = rms_matmul(x, norm, w_in.astype(BF16), tn=N1 // 2).reshape(B, S, N1)
    qd = SWA_HEADS * HD
    qn, kn, vb = prep_heads(
        proj, [0, qd // LANES, qd // LANES + 1], [SWA_HEADS, SWA_KV, SWA_KV],
        [q_gain, k_gain, None], [True, True, False], [HD ** -0.5, 1.0, 1.0], S)
    att = swa_attention(qn, kn, vb, sinks).reshape(T, qd)
    return proj_res(att, 0, att, 1, w_out.astype(BF16), x)


def kernel(x, ev_norm, ev_w_in, hg_lb_logits, hg_out_gain, nsa_q_gain, nsa_k_gain, nsa_cmp_pe, nsa_cmp_w1,
           nsa_cmp_w2, ev_w_out, od_norm, od_w_in, swa_q_gain, swa_k_gain, swa_sinks, od_w_out, moe_norm,
           moe_w_grp, moe_b_grp, moe_w_exp, moe_b_exp, moe_w_gate, moe_w_up, moe_w_down):
    B, S, D = x.shape
    lower_bounds = jnp.cumsum(jax.nn.softmax(hg_lb_logits.astype(F32), axis=0), axis=0)
    h = x.reshape(B * S, D)
    h = even_layer(h, B, S, ev_norm[0], ev_w_in[0], lower_bounds[0], hg_out_gain[0], nsa_q_gain[0],
                   nsa_k_gain[0], nsa_cmp_pe[0], nsa_cmp_w1[0], nsa_cmp_w2[0], ev_w_out[0])
    h = hier_moe_block(h, moe_norm[0], moe_w_grp[0], moe_b_grp[0], moe_w_exp[0], moe_b_exp[0],
                       moe_w_gate[0], moe_w_up[0], moe_w_down[0])
    h = odd_layer(h, B, S, od_norm[0], od_w_in[0], swa_q_gain[0], swa_k_gain[0], swa_sinks[0], od_w_out[0])
    h = hier_moe_block(h, moe_norm[1], moe_w_grp[1], moe_b_grp[1], moe_w_exp[1], moe_b_exp[1],
                       moe_w_gate[1], moe_w_up[1], moe_w_down[1])
    return h.reshape(B, S, D)
```

```python
import functools

import numpy as np
import jax
import jax.numpy as jnp
from jax import lax
from jax.experimental import pallas as pl
from jax.experimental.pallas import tpu as pltpu

F32 = jnp.float32
BF16 = jnp.bfloat16
I32 = jnp.int32
HI = lax.Precision.HIGHEST

D_MODEL = 1024
EPS = 1e-6
ROPE_THETA = 500000.0
HD = 64
ROT = HD // 4
HALF = ROT // 2

HG_DIM = 128
HG_WIDTH = 512
HG_HEADS = 4
HG_CHUNK = 64

NSA_HEADS = 8
NSA_KV = 2
NSA_HPG = NSA_HEADS // NSA_KV
CMP_LEN = 32
CMP_STRIDE = 16
CMP_HIDDEN = 256
SEL_LEN = 64
SEL_TOPN = 8
NSA_WINDOW = 512
FORCE_BONUS = 1e4

SWA_HEADS = 16
SWA_KV = 2
SWA_HPG = SWA_HEADS // SWA_KV
SWA_WINDOW = 128

N_GROUPS = 4
EXP_PER_GROUP = 8
N_EXPERTS = 32
EXPERT_FF = 512
MOE_BLOCK = 256

LANES = 128
NEG = -1e30
VMEM_LIMIT = 56 * 1024 * 1024

C_HG = 0
C_NQ = 2048
C_KV = 2560
C_GATE = 3328
N0 = 3584
N1 = 1280


def _cparams(sem):
    return pltpu.CompilerParams(dimension_semantics=sem, vmem_limit_bytes=VMEM_LIMIT)


def _dot(a, b):
    return jnp.dot(a, b, preferred_element_type=F32)


def _dot_nt(a, b):
    return lax.dot_general(a, b, (((1,), (1,)), ((), ())), preferred_element_type=F32)


def _rms_matmul_kernel(tn, x_ref, g_ref, w_ref, o_ref):
    x = x_ref[...]
    ms = jnp.mean(x * x, axis=-1, keepdims=True)
    xn = (x * lax.rsqrt(ms + EPS) * g_ref[...]).astype(BF16)
    for j in range(o_ref.shape[1] // tn):
        o_ref[:, j * tn:(j + 1) * tn] = _dot(xn, w_ref[:, j * tn:(j + 1) * tn])


def rms_matmul(x, gain, w, tm=512, tn=None):
    T, D = x.shape
    N = w.shape[1]
    tn = N if tn is None else tn
    return pl.pallas_call(
        functools.partial(_rms_matmul_kernel, tn),
        grid=(T // tm,),
        in_specs=[pl.BlockSpec((tm, D), lambda i: (i, 0)),
                  pl.BlockSpec((1, D), lambda i: (0, 0)),
                  pl.BlockSpec((D, N), lambda i: (0, 0))],
        out_specs=pl.BlockSpec((tm, N), lambda i: (i, 0)),
        out_shape=jax.ShapeDtypeStruct((T, N), F32),
        compiler_params=_cparams(("parallel",)),
        name="rms_matmul",
    )(x, gain.reshape(1, D), w)


def _proj_res_kernel(a_ref, b_ref, wa_ref, wb_ref, r_ref, o_ref):
    acc = _dot(a_ref[...], wa_ref[...]) + _dot(b_ref[...], wb_ref[...])
    o_ref[...] = r_ref[...] + acc


def proj_res(a, a_blk, b, b_blk, w, res, tm=512):
    T, D = res.shape
    K = w.shape[0]
    kh = K // 2
    return pl.pallas_call(
        _proj_res_kernel,
        grid=(T // tm,),
        in_specs=[pl.BlockSpec((tm, kh), lambda i: (i, a_blk)),
                  pl.BlockSpec((tm, kh), lambda i: (i, b_blk)),
                  pl.BlockSpec((kh, D), lambda i: (0, 0)),
                  pl.BlockSpec((kh, D), lambda i: (1, 0)),
                  pl.BlockSpec((tm, D), lambda i: (i, 0))],
        out_specs=pl.BlockSpec((tm, D), lambda i: (i, 0)),
        out_shape=jax.ShapeDtypeStruct((T, D), F32),
        compiler_params=_cparams(("parallel",)),
        name="proj_res",
    )(a, b, w, w, res)


def _rope_tables(pos, width):
    inv = jnp.power(ROPE_THETA, -jnp.arange(HALF, dtype=F32) * 2.0 / ROT)
    ang = pos.astype(F32)[:, None] * inv[None, :]
    cos, sin = jnp.cos(ang), jnp.sin(ang)
    n = pos.shape[0]
    c = jnp.concatenate([cos, cos, jnp.ones((n, HD - ROT), F32)], axis=1)
    s = jnp.concatenate([-sin, sin, jnp.zeros((n, HD - ROT), F32)], axis=1)
    reps = width // HD
    return jnp.tile(c, (1, reps)), jnp.tile(s, (1, reps))


def _block_ones(width):
    idx = np.arange(width) // HD
    return jnp.asarray((idx[:, None] == idx[None, :]).astype(np.float32), dtype=BF16)


def _split_dot(a, b):
    hi = a.astype(BF16)
    lo = (a - hi.astype(F32)).astype(BF16)
    return _dot(hi, b) + _dot(lo, b)


def _head_norm_rope(x, gain, cos, sin, bones, scale):
    ms = _split_dot(x * x, bones) * (1.0 / HD)
    y = x * lax.rsqrt(ms + EPS) * gain
    lane = lax.broadcasted_iota(I32, y.shape, 1) % HD
    partner = jnp.where(lane < HALF, pltpu.roll(y, LANES - HALF, axis=1), pltpu.roll(y, HALF, axis=1))
    out = y * cos + partner * sin
    return out * scale if scale != 1.0 else out


def _prep_kernel(n_heads, norm_flags, scales, *refs):
    nt = len(n_heads)
    x_refs = refs[:nt]
    gain_ref, cos_ref, sin_ref, bones_ref = refs[nt:nt + 4]
    o_refs = refs[nt + 4:]
    cos = cos_ref[...]
    sin = sin_ref[...]
    bones = bones_ref[...]
    lane = lax.broadcasted_iota(I32, cos.shape, 1)
    for t in range(nt):
        for c in range(n_heads[t] // 2):
            x = x_refs[t][0, :, c * LANES:(c + 1) * LANES]
            if norm_flags[t]:
                g = gain_ref[t:t + 1, :]
                x = _head_norm_rope(x, g, cos, sin, bones, scales[t])
                o_refs[t][0, 2 * c] = x[:, :HD].astype(BF16)
                o_refs[t][0, 2 * c + 1] = pltpu.roll(x, HD, axis=1)[:, :HD].astype(BF16)
            else:
                for j, xh in enumerate((x, pltpu.roll(x, HD, axis=1))):
                    v1 = jnp.where(lane < HD, xh, jnp.where(lane == HD, 1.0, 0.0))
                    o_refs[t][0, 2 * c + j] = v1.T.astype(BF16)


def prep_heads(proj, col_blocks, n_heads, gains, norm_flags, scales, S, ts=512):
    B = proj.shape[0]
    nt = len(n_heads)
    cos, sin = _rope_tables(jnp.arange(S), LANES)
    gain_rows = []
    for t in range(nt):
        g = gains[t] if gains[t] is not None else jnp.ones((HD,), F32)
        gain_rows.append(jnp.tile(g.astype(F32), 2))
    gain_arr = jnp.stack(gain_rows)
    in_specs = []
    for t in range(nt):
        w = n_heads[t] * HD
        in_specs.append(pl.BlockSpec((1, ts, w), functools.partial(lambda b, s, cb: (b, s, cb), cb=col_blocks[t])))
    in_specs += [pl.BlockSpec((nt, LANES), lambda b, s: (0, 0)),
                 pl.BlockSpec((ts, LANES), lambda b, s: (s, 0)),
                 pl.BlockSpec((ts, LANES), lambda b, s: (s, 0)),
                 pl.BlockSpec((LANES, LANES), lambda b, s: (0, 0))]
    out_specs, out_shape = [], []
    for t in range(nt):
        if norm_flags[t]:
            out_specs.append(pl.BlockSpec((1, n_heads[t], ts, HD), lambda b, s: (b, 0, s, 0)))
            out_shape.append(jax.ShapeDtypeStruct((B, n_heads[t], S, HD), BF16))
        else:
            out_specs.append(pl.BlockSpec((1, n_heads[t], LANES, ts), lambda b, s: (b, 0, 0, s)))
            out_shape.append(jax.ShapeDtypeStruct((B, n_heads[t], LANES, S), BF16))
    return pl.pallas_call(
        functools.partial(_prep_kernel, tuple(n_heads), tuple(norm_flags), tuple(scales)),
        grid=(B, S // ts),
        in_specs=in_specs,
        out_specs=out_specs,
        out_shape=out_shape,
        compiler_params=_cparams(("parallel", "parallel")),
        name="prep_heads",
    )(*([proj] * nt), gain_arr, cos, sin, _block_ones(LANES))


_N_LEVELS = 6


def _hgrn_constants():
    C = HG_CHUNK
    r = np.arange(C)
    mats = [(r[None, :] <= r[:, None]).astype(np.float32),
            (r[None, :] > r[:, None]).astype(np.float32)]
    masks = []
    for lv in range(_N_LEVELS):
        n = 1 << lv
        blk = r // (2 * n)
        right = (r // n) % 2 == 1
        bnd = blk * 2 * n + n - 1
        m = np.zeros((C, C), np.float32)
        for t in range(C):
            if right[t]:
                m[t, bnd[t] + 1:t + 1] = 1.0
            else:
                m[t, t + 1:bnd[t] + 1] = 1.0
        mats.append(m)
        masks.append((blk[:, None] == blk[None, :]).astype(np.float32))
    masks.append(np.eye(C, dtype=np.float32))
    return jnp.asarray(np.concatenate(mats, axis=0), dtype=BF16), jnp.asarray(np.stack(masks))


def _hgrn_kernel(q_ref, f_ref, i_ref, g_ref, lb_ref, gain_ref, mall_ref, masks_ref, o_ref, st_ref):
    C = HG_CHUNK
    ts = q_ref.shape[1]

    @pl.when(pl.program_id(1) == 0)
    def _():
        st_ref[...] = jnp.zeros_like(st_ref)

    mall = mall_ref[...]
    row = lax.broadcasted_iota(I32, (C, HG_DIM), 0)
    gain = gain_ref[...]

    def chunk(c, carry):
        r0 = pl.multiple_of(c * C, C)
        for h in range(HG_HEADS):
            ls = slice(h * HG_DIM, (h + 1) * HG_DIM)
            q = q_ref[0, pl.ds(r0, C), ls]
            z = f_ref[0, pl.ds(r0, C), ls]
            v = i_ref[0, pl.ds(r0, C), ls]
            g = g_ref[0, pl.ds(r0, C), ls]
            lb = lb_ref[:, ls]
            lf = jnp.log(lb + (1.0 - lb) * jax.nn.sigmoid(z))
            k = (1.0 - lb) * jax.nn.sigmoid(-z)
            lf_hi = lf.astype(BF16)
            lf_lo = (lf - lf_hi.astype(F32)).astype(BF16)
            seg = _dot(mall, jnp.concatenate([lf_hi, lf_lo], axis=1))
            ex = jnp.exp(seg[:, :HG_DIM] + seg[:, HG_DIM:])
            qs = q * ex[0:C]
            ks = k * ex[C:2 * C]
            st = st_ref[h]
            o = _dot_nt(qs.astype(BF16), st.astype(BF16))
            a = jnp.where(masks_ref[_N_LEVELS] > 0.5, _dot_nt(q.astype(BF16), k.astype(BF16)), 0.0)
            for lv in range(_N_LEVELS):
                e = ex[(2 + lv) * C:(3 + lv) * C]
                right = ((row >> lv) & 1) == 1
                qe = jnp.where(right, q * e, 0.0).astype(BF16)
                ke = jnp.where(right, 0.0, k * e).astype(BF16)
                a = a + jnp.where(masks_ref[lv] > 0.5, _dot_nt(qe, ke), 0.0)
            vb = v.astype(BF16)
            o = o + _dot(a.astype(BF16), vb)
            decay = ex[C - 1:C]
            st_ref[h] = st * decay + _dot(v.T.astype(BF16), ks.astype(BF16))
            ms = jnp.mean(o * o, axis=-1, keepdims=True)
            y = o * lax.rsqrt(ms + EPS) * gain * (g * jax.nn.sigmoid(g))
            o_ref[0, pl.ds(r0, C), ls] = y.astype(BF16)
        return carry

    lax.fori_loop(0, ts // C, chunk, 0)


def hgrn2(proj, lb, out_gain, ts=512):
    B, S, _ = proj.shape
    mall, masks = _hgrn_constants()
    W = HG_WIDTH
    return pl.pallas_call(
        _hgrn_kernel,
        grid=(B, S // ts),
        in_specs=[pl.BlockSpec((1, ts, W), lambda b, s: (b, s, 0)),
                  pl.BlockSpec((1, ts, W), lambda b, s: (b, s, 1)),
                  pl.BlockSpec((1, ts, W), lambda b, s: (b, s, 2)),
                  pl.BlockSpec((1, ts, W), lambda b, s: (b, s, 3)),
                  pl.BlockSpec((1, W), lambda b, s: (0, 0)),
                  pl.BlockSpec((1, HG_DIM), lambda b, s: (0, 0)),
                  pl.BlockSpec(mall.shape, lambda b, s: (0, 0)),
                  pl.BlockSpec(masks.shape, lambda b, s: (0, 0, 0))],
        out_specs=pl.BlockSpec((1, ts, W), lambda b, s: (b, s, 0)),
        out_shape=jax.ShapeDtypeStruct((B, S, W), BF16),
        scratch_shapes=[pltpu.VMEM((HG_HEADS, HG_DIM, HG_DIM), F32)],
        compiler_params=_cparams(("parallel", "arbitrary")),
        name="hgrn2",
    )(proj, proj, proj, proj, lb.reshape(1, W), out_gain.reshape(1, HG_DIM), mall, masks)


def _compress_kernel(xk_ref, xv_ref, w1k_ref, w1v_ref, pek_ref, pev_ref, w2k_ref, w2v_ref,
                     gain_ref, cos_ref, sin_ref, bones_ref, ok_ref, ov_ref):
    nc = xk_ref.shape[1] // CMP_STRIDE

    def hidden(x_ref, w1_ref, pe_ref):
        width = NSA_KV * CMP_HIDDEN
        u0 = jnp.zeros((nc, width), F32)
        u1 = jnp.zeros((nc, width), F32)
        const = jnp.zeros((8, width), F32)
        for p in range(CMP_STRIDE):
            xp = x_ref[0, pl.ds(p, nc, stride=CMP_STRIDE), :].astype(BF16)
            u0 = u0 + _dot(xp, w1_ref[p])
            u1 = u1 + _dot(xp, w1_ref[CMP_STRIDE + p])
            const = const + _dot(pe_ref[p].astype(BF16), w1_ref[p])
            const = const + _dot(pe_ref[CMP_STRIDE + p].astype(BF16), w1_ref[CMP_STRIDE + p])
        h = u0 + pltpu.roll(u1, nc - 1, axis=0) + const[0:1]
        return jax.nn.gelu(h).astype(BF16)

    yk = _dot(hidden(xk_ref, w1k_ref, pek_ref), w2k_ref[...])
    yk = _head_norm_rope(yk, gain_ref[...], cos_ref[...], sin_ref[...], bones_ref[...], 1.0)
    ok_ref[0] = yk[:, :HD].astype(BF16)
    ok_ref[1] = pltpu.roll(yk, HD, axis=1)[:, :HD].astype(BF16)
    yv_t = _dot_nt(w2v_ref[...], hidden(xv_ref, w1v_ref, pev_ref)).astype(BF16)
    ov_ref[0] = yv_t[:HD]
    ov_ref[1] = yv_t[HD:]


def _group_diag(w):
    z = jnp.zeros_like(w)
    return jnp.concatenate([jnp.concatenate([w, z], axis=-1), jnp.concatenate([z, w], axis=-1)], axis=-2)


def nsa_compress(proj, kc_blk, vc_blk, pe, w1, w2, k_gain):
    B, S, _ = proj.shape
    nc = S // CMP_STRIDE
    pos_end = jnp.arange(nc) * CMP_STRIDE + CMP_LEN - 1
    cos, sin = _rope_tables(pos_end, LANES)
    w1d = _group_diag(w1.reshape(2, CMP_LEN, HD, CMP_HIDDEN)).astype(BF16)
    w2d = _group_diag(w2).astype(BF16)
    pe2 = jnp.broadcast_to(jnp.tile(pe, (1, 1, NSA_KV))[:, :, None, :], (2, CMP_LEN, 8, LANES))
    full = lambda shape: pl.BlockSpec(shape, lambda b: (0,) * len(shape))
    w1_shape = (CMP_LEN, LANES, NSA_KV * CMP_HIDDEN)
    return pl.pallas_call(
        _compress_kernel,
        grid=(B,),
        in_specs=[pl.BlockSpec((1, S, LANES), lambda b: (b, 0, kc_blk)),
                  pl.BlockSpec((1, S, LANES), lambda b: (b, 0, vc_blk)),
                  full(w1_shape), full(w1_shape),
                  full((CMP_LEN, 8, LANES)), full((CMP_LEN, 8, LANES)),
                  full((NSA_KV * CMP_HIDDEN, LANES)), full((LANES, NSA_KV * CMP_HIDDEN)),
                  full((1, LANES)), full((nc, LANES)), full((nc, LANES)), full((LANES, LANES))],
        out_specs=[pl.BlockSpec((NSA_KV, nc, HD), lambda b: (b, 0, 0)),
                   pl.BlockSpec((NSA_KV, HD, nc), lambda b: (b, 0, 0))],
        out_shape=[jax.ShapeDtypeStruct((B * NSA_KV, nc, HD), BF16),
                   jax.ShapeDtypeStruct((B * NSA_KV, HD, nc), BF16)],
        compiler_params=_cparams(("parallel",)),
        name="nsa_compress",
    )(proj, proj, w1d[0], w1d[1], pe2[0], pe2[1], w2d[0], w2d[1].T,
      jnp.tile(k_gain.astype(F32), 2).reshape(1, LANES), cos, sin, _block_ones(LANES))


NSA_TQ = 256
NSA_TK = 1024


def _nsa_kernel(q_ref, kc_ref, vc_ref, ks_ref, vs_ref, kw_ref, vw_ref, gate_ref, cover_ref,
                o_ref, acc_ref, sel_ref):
    TQ, TK, H = NSA_TQ, NSA_TK, NSA_HPG
    R = H * TQ
    t0 = pl.program_id(2) * TQ
    q = q_ref[0].reshape(R, HD)
    tl = t0 + lax.broadcasted_iota(I32, (1, TQ), 1)

    def all_heads(x):
        return jnp.concatenate([x] * H, axis=1)

    kc = kc_ref[0]
    ncmp = kc.shape[0]
    cmp_end = lax.broadcasted_iota(I32, (ncmp, 1), 0) * CMP_STRIDE + (CMP_LEN - 1)
    valid = all_heads(jnp.where(cmp_end <= tl, 1.0, 0.0))
    s = jnp.where(valid > 0.5, _dot_nt(kc, q), NEG)
    p = jnp.exp(s - jnp.max(s, axis=0, keepdims=True)) * valid
    p_c = p / jnp.maximum(jnp.sum(p, axis=0, keepdims=True), 1e-30)
    o_c = _dot(vc_ref[0], p_c.astype(BF16))

    p_tok = p_c[:, 0:TQ]
    for h in range(1, H):
        p_tok = p_tok + p_c[:, h * TQ:(h + 1) * TQ]
    p_hi = p_tok.astype(BF16)
    p_lo = (p_tok - p_hi.astype(F32)).astype(BF16)
    imp = _dot(cover_ref[...], p_hi) + _dot(cover_ref[...], p_lo)
    nsel = imp.shape[0]
    jj = lax.broadcasted_iota(I32, (nsel, TQ), 0)
    jf = jj.astype(F32)
    cur = tl >> 6
    allowed = jj * SEL_LEN <= tl
    forced = (jj == 0) | (jj == cur) | (jj == cur - 1)
    score = jnp.where(allowed, imp + jnp.where(forced, FORCE_BONUS, 0.0), NEG)
    sel = jnp.zeros((nsel, TQ), F32)
    for _ in range(min(SEL_TOPN, nsel)):
        mx = jnp.max(score, axis=0, keepdims=True)
        first = jnp.min(jnp.where(score == mx, jf, float(nsel)), axis=0, keepdims=True)
        hit = jf == first
        sel = jnp.where(hit & (mx > 0.5 * NEG), 1.0, sel)
        score = jnp.where(hit, NEG, score)
    sel_ref[...] = sel

    acc_ref[...] = jnp.zeros(acc_ref.shape, F32)
    key_in_tile = lax.broadcasted_iota(I32, (TK, 1), 0)
    blocks_per_tile = TK // SEL_LEN

    def sweep(kt, m_old):
        k0 = pl.multiple_of(kt * TK, TK)
        sel_t = sel_ref[pl.ds(pl.multiple_of(kt * blocks_per_tile, blocks_per_tile), blocks_per_tile), :]
        chosen = jnp.concatenate(
            [jnp.broadcast_to(sel_t[j:j + 1, :], (SEL_LEN, TQ)) for j in range(blocks_per_tile)], axis=0)
        bias = jnp.where((chosen > 0.5) & (k0 + key_in_tile <= tl), 0.0, NEG)
        s = _dot_nt(ks_ref[0, 0, pl.ds(k0, TK), :], q) + all_heads(bias)
        m_new = jnp.maximum(m_old, jnp.max(s, axis=0, keepdims=True))
        p = jnp.exp(s - m_new).astype(BF16)
        acc_ref[...] = jnp.exp(m_old - m_new) * acc_ref[...] + _dot(vs_ref[0, 0, :, pl.ds(k0, TK)], p)
        return m_new

    lax.fori_loop(0, t0 // TK + 1, sweep, jnp.full((1, R), NEG, F32))
    acc = acc_ref[...]
    o_s = acc[:HD] / jnp.maximum(acc[HD:HD + 1], 1e-30)

    WK = NSA_WINDOW + TQ
    ws = pl.multiple_of(jnp.maximum(t0 - NSA_WINDOW, 0), TQ)
    kp = ws + lax.broadcasted_iota(I32, (WK, 1), 0)
    wbias = jnp.where((kp <= tl) & (kp > tl - NSA_WINDOW), 0.0, NEG)
    s = _dot_nt(kw_ref[0, 0, pl.ds(ws, WK), :], q) + all_heads(wbias)
    p = jnp.exp(s - jnp.max(s, axis=0, keepdims=True)).astype(BF16)
    acc_w = _dot(vw_ref[0, 0, :, pl.ds(ws, WK)], p)
    o_w = acc_w[:HD] / acc_w[HD:HD + 1]

    sg = jax.nn.sigmoid(gate_ref[0]).T
    heads_out = []
    for h in range(H):
        cols = slice(h * TQ, (h + 1) * TQ)
        heads_out.append(sg[3 * h:3 * h + 1] * o_c[:, cols] + sg[3 * h + 1:3 * h + 2] * o_s[:, cols]
                         + sg[3 * h + 2:3 * h + 3] * o_w[:, cols])
    o_ref[0] = jnp.concatenate(heads_out, axis=0).T.astype(BF16)


def nsa_attention(qn, kcb, vcb, ksn, vsb, kwn, vwb, proj):
    B, _, S, _ = qn.shape
    G, H, TQ = NSA_KV, NSA_HPG, NSA_TQ
    assert S % NSA_TK == 0 and S >= NSA_WINDOW + TQ
    ncmp = kcb.shape[1]
    nsel = S // SEL_LEN
    n = np.arange(ncmp)[:, None] * CMP_STRIDE
    j = np.arange(nsel)[None, :] * SEL_LEN
    cover = jnp.asarray(((n < j + SEL_LEN) & (n + CMP_LEN > j)).astype(np.float32).T, dtype=BF16)
    R = H * TQ
    k_spec = pl.BlockSpec((1, 1, S, HD), lambda b, g, i: (b, g, 0, 0))
    v_spec = pl.BlockSpec((1, 1, LANES, S), lambda b, g, i: (b, g, 0, 0))
    return pl.pallas_call(
        _nsa_kernel,
        grid=(B, G, S // TQ),
        in_specs=[pl.BlockSpec((1, H, TQ, HD), lambda b, g, i: (b, g, i, 0)),
                  pl.BlockSpec((1, ncmp, HD), lambda b, g, i: (b * G + g, 0, 0)),
                  pl.BlockSpec((1, HD, ncmp), lambda b, g, i: (b * G + g, 0, 0)),
                  k_spec, v_spec, k_spec, v_spec,
                  pl.BlockSpec((1, TQ, LANES), lambda b, g, i: (b, i, C_GATE // LANES + g)),
                  pl.BlockSpec((nsel, ncmp), lambda b, g, i: (0, 0))],
        out_specs=pl.BlockSpec((1, TQ, H * HD), lambda b, g, i: (b, i, g)),
        out_shape=jax.ShapeDtypeStruct((B, S, NSA_HEADS * HD), BF16),
        scratch_shapes=[pltpu.VMEM((LANES, R), F32), pltpu.VMEM((nsel, TQ), F32)],
        compiler_params=_cparams(("parallel", "parallel", "arbitrary")),
        name="nsa_attention",
    )(qn, kcb, vcb, ksn, vsb, kwn, vwb, proj, cover)


SWA_TQ = 256


def _swa_kernel(sink_ref, q_ref, k_ref, v_ref, o_ref):
    TQ, H = SWA_TQ, SWA_HPG
    R = H * TQ
    g = pl.program_id(1)
    t0 = pl.program_id(2) * TQ
    q = q_ref[0].reshape(R, HD)
    tl = t0 + lax.broadcasted_iota(I32, (1, TQ), 1)
    WK = SWA_WINDOW + TQ
    ws = pl.multiple_of(jnp.maximum(t0 - SWA_WINDOW, 0), LANES)
    kp = ws + lax.broadcasted_iota(I32, (WK, 1), 0)
    bias = jnp.where((kp <= tl) & (kp > tl - SWA_WINDOW), 0.0, NEG)
    s = _dot_nt(k_ref[0, 0, pl.ds(ws, WK), :], q) + jnp.concatenate([bias] * H, axis=1)
    head = lax.broadcasted_iota(I32, (1, R), 1) // TQ
    sink = jnp.zeros((1, R), F32)
    for h in range(H):
        sink = jnp.where(head == h, sink_ref[g * H + h], sink)
    m = jnp.maximum(jnp.max(s, axis=0, keepdims=True), sink)
    p = jnp.exp(s - m).astype(BF16)
    acc = _dot(v_ref[0, 0, :, pl.ds(ws, WK)], p)
    o = acc[:HD] / (acc[HD:HD + 1] + jnp.exp(sink - m))
    o_ref[0] = jnp.concatenate([o[:, h * TQ:(h + 1) * TQ] for h in range(H)], axis=0).T.astype(BF16)


def swa_attention(qn, kn, vb, sinks):
    B, _, S, _ = qn.shape
    G, H, TQ = SWA_KV, SWA_HPG, SWA_TQ
    assert S >= SWA_WINDOW + TQ
    k_spec = pl.BlockSpec((1, 1, S, HD), lambda b, g, i, sk: (b, g, 0, 0))
    v_spec = pl.BlockSpec((1, 1, LANES, S), lambda b, g, i, sk: (b, g, 0, 0))
    return pl.pallas_call(
        _swa_kernel,
        grid_spec=pltpu.PrefetchScalarGridSpec(
            num_scalar_prefetch=1,
            grid=(B, G, S // TQ),
            in_specs=[pl.BlockSpec((1, H, TQ, HD), lambda b, g, i, sk: (b, g, i, 0)), k_spec, v_spec],
            out_specs=pl.BlockSpec((1, TQ, H * HD), lambda b, g, i, sk: (b, i, g))),
        out_shape=jax.ShapeDtypeStruct((B, S, SWA_HEADS * HD), BF16),
        compiler_params=_cparams(("parallel", "parallel", "arbitrary")),
        name="swa_attention",
    )(sinks.astype(F32), qn, kn, vb)


R_GRP = 0
R_EXP = 32


SUBLANES = 8


def _store_row_tiles(ref, val, base=0):
    for s in range(SUBLANES):
        ref[pl.ds(base * SUBLANES + s, val.shape[0], stride=SUBLANES), :] = val[:, s * LANES:(s + 1) * LANES]


def _load_row_tiles(ref, rows, base=0):
    return [ref[pl.ds(base * SUBLANES + s, rows, stride=SUBLANES), :] for s in range(SUBLANES)]


def _router_kernel(x_ref, g_ref, w_ref, b_ref, tri_ref, hn_ref, route_ref, cnt_ref, base_ref):
    i = pl.program_id(0)
    last = pl.num_programs(0) - 1

    @pl.when(i == 0)
    def _():
        base_ref[...] = jnp.zeros_like(base_ref)

    @pl.when(i < last)
    def _():
        _route_tile(x_ref, g_ref, w_ref, b_ref, tri_ref, hn_ref, route_ref, base_ref)

    @pl.when(i == last)
    def _():
        hn_ref[...] = jnp.zeros_like(hn_ref)
        route_ref[...] = jnp.zeros_like(route_ref)

    cnt_ref[...] = base_ref[...]


def _route_tile(x_ref, g_ref, w_ref, b_ref, tri_ref, hn_ref, route_ref, base_ref):
    x = x_ref[...]
    ms = jnp.mean(x * x, axis=-1, keepdims=True)
    h = x * lax.rsqrt(ms + EPS) * g_ref[...]
    _store_row_tiles(hn_ref, h)
    h_hi = h.astype(BF16)
    h_lo = (h - h_hi.astype(F32)).astype(BF16)
    logits = _dot(h_hi, w_ref[0]) + _dot(h_lo, w_ref[0]) + _dot(h_hi, w_ref[1]) + b_ref[...]
    lane = lax.broadcasted_iota(I32, logits.shape, 1)
    is_grp = lane < N_GROUPS
    gl = jnp.where(is_grp, logits, NEG)
    gmax = jnp.max(gl, axis=-1, keepdims=True)
    lanef = lane.astype(F32)
    gidx = jnp.min(jnp.where(gl == gmax, lanef, float(LANES)), axis=-1, keepdims=True)
    gw = 1.0 / jnp.sum(jnp.where(is_grp, jnp.exp(logits - gmax), 0.0), axis=-1, keepdims=True)
    in_grp = ((lane >> 3) - R_EXP // EXP_PER_GROUP).astype(F32) == gidx
    w0 = jnp.where(in_grp, logits, NEG)
    m1 = jnp.max(w0, axis=-1, keepdims=True)
    i1 = jnp.min(jnp.where(w0 == m1, lanef, float(LANES)), axis=-1, keepdims=True)
    w1 = jnp.where(lanef == i1, NEG, w0)
    m2 = jnp.max(w1, axis=-1, keepdims=True)
    i2 = jnp.min(jnp.where(w1 == m2, lanef, float(LANES)), axis=-1, keepdims=True)
    e2 = jnp.exp(m2 - m1)
    p1 = gw / (1.0 + e2)
    p2 = gw * e2 / (1.0 + e2)
    oh1 = jnp.where(lanef == i1, 1.0, 0.0)
    oh2 = jnp.where(lanef == i2, 1.0, 0.0)
    oh = oh1 + oh2
    before = _dot(tri_ref[...], oh.astype(BF16)) + base_ref[0:1, :]
    r1 = jnp.sum(before * oh1, axis=-1, keepdims=True)
    r2 = jnp.sum(before * oh2, axis=-1, keepdims=True)
    base_ref[...] = base_ref[...] + jnp.sum(oh, axis=0, keepdims=True)
    route = jnp.zeros_like(logits)
    for ln, val in enumerate((i1 - R_EXP, i2 - R_EXP, p1, p2, r1, r2)):
        route = jnp.where(lane == ln, val, route)
    route_ref[...] = route


def moe_router(x, gain, w_grp, b_grp, w_exp, b_exp, tm=256):
    T, D = x.shape
    nt = T // tm
    w = jnp.zeros((D, LANES), F32).at[:, R_GRP:R_GRP + N_GROUPS].set(w_grp).at[:, R_EXP:R_EXP + N_EXPERTS].set(w_exp)
    b = jnp.zeros((1, LANES), F32).at[0, R_GRP:R_GRP + N_GROUPS].set(b_grp).at[0, R_EXP:R_EXP + N_EXPERTS].set(b_exp)
    w_hi = w.astype(BF16)
    w = jnp.stack([w_hi, (w - w_hi.astype(F32)).astype(BF16)])
    tri = jnp.asarray(np.tril(np.ones((tm, tm), np.float32), -1), dtype=BF16)
    return pl.pallas_call(
        _router_kernel,
        grid=(nt + 1,),
        in_specs=[pl.BlockSpec((tm, D), lambda i: (jnp.minimum(i, nt - 1), 0)),
                  pl.BlockSpec((1, D), lambda i: (0, 0)),
                  pl.BlockSpec((2, D, LANES), lambda i: (0, 0, 0)),
                  pl.BlockSpec((1, LANES), lambda i: (0, 0)),
                  pl.BlockSpec((tm, tm), lambda i: (0, 0))],
        out_specs=[pl.BlockSpec((tm * SUBLANES, LANES), lambda i: (i, 0)),
                   pl.BlockSpec((tm, LANES), lambda i: (i, 0)),
                   pl.BlockSpec((8, LANES), lambda i: (0, 0))],
        out_shape=[jax.ShapeDtypeStruct(((T + tm) * SUBLANES, LANES), F32),
                   jax.ShapeDtypeStruct((T + tm, LANES), F32),
                   jax.ShapeDtypeStruct((8, LANES), F32)],
        scratch_shapes=[pltpu.VMEM((8, LANES), F32)],
        compiler_params=_cparams(("arbitrary",)),
        name="moe_router",
    )(x, gain.reshape(1, D), w, b, tri)


def _row_copy(src, src_row, dst, dst_row, sem):
    def tile(ref, row):
        return ref.at[pl.ds(pl.multiple_of(row * SUBLANES, SUBLANES), SUBLANES)]

    return pltpu.make_async_copy(tile(src, src_row), tile(dst, dst_row), sem)


DISPATCH_TILE = 512
ROW_UNROLL = 4


def _wait_rows(hbm_ref, sem, n):
    view = hbm_ref.at[pl.ds(0, n * SUBLANES)]
    pltpu.make_async_copy(view, view, sem).wait()


def _dispatch_kernel(dest_ref, x_ref, xb_ref, stage, sems):
    i = pl.program_id(0)
    last = pl.num_programs(0) - 1
    slot = i % 2

    @pl.when(i >= 2)
    def _():
        _wait_rows(xb_ref, sems.at[slot], DISPATCH_TILE)

    stage[slot] = x_ref[...]

    def issue(t, c):
        _row_copy(stage.at[slot], t, xb_ref, dest_ref[0, 0, 2 * t], sems.at[slot]).start(priority=0)
        _row_copy(stage.at[slot], t, xb_ref, dest_ref[0, 0, 2 * t + 1], sems.at[slot]).start(priority=1)
        return c

    lax.fori_loop(0, DISPATCH_TILE // 2, issue, 0, unroll=ROW_UNROLL)

    @pl.when(i == last)
    def _():
        _wait_rows(xb_ref, sems.at[slot], DISPATCH_TILE)

    @pl.when((i == last) & (i >= 1))
    def _():
        _wait_rows(xb_ref, sems.at[1 - slot], DISPATCH_TILE)


def moe_dispatch(hn, dest_all, n_tok):
    n_rows = dest_all.shape[0]
    tm = DISPATCH_TILE // 2
    nt = n_rows // DISPATCH_TILE
    n_tok_tiles = n_tok // tm
    return pl.pallas_call(
        _dispatch_kernel,
        grid=(nt,),
        in_specs=[pl.BlockSpec((1, 1, DISPATCH_TILE), lambda i: (i, 0, 0), memory_space=pltpu.SMEM),
                  pl.BlockSpec((tm * SUBLANES, LANES), lambda i: (jnp.minimum(i, n_tok_tiles), 0))],
        out_specs=pl.BlockSpec(memory_space=pl.ANY),
        out_shape=jax.ShapeDtypeStruct((n_rows * SUBLANES, LANES), F32),
        scratch_shapes=[pltpu.VMEM((2, tm * SUBLANES, LANES), F32), pltpu.SemaphoreType.DMA((2,))],
        compiler_params=_cparams(("arbitrary",)),
        name="moe_dispatch",
    )(dest_all.reshape(nt, 1, DISPATCH_TILE), hn)


def _expert_kernel(be_ref, x_ref, wg_ref, wu_ref, wd_ref, y_ref, wg_b, wu_b, wd_b):
    i = pl.program_id(0)

    @pl.when((i == 0) | (be_ref[i] != be_ref[jnp.maximum(i - 1, 0)]))
    def _():
        wg_b[...] = wg_ref[0].astype(BF16)
        wu_b[...] = wu_ref[0].astype(BF16)
        wd_b[...] = wd_ref[0].astype(BF16)

    x = jnp.concatenate(_load_row_tiles(x_ref, MOE_BLOCK), axis=1).astype(BF16)
    hid = _dot(x, wg_b[...])
    hid = hid * jax.nn.sigmoid(hid) * _dot(x, wu_b[...])
    _store_row_tiles(y_ref, _dot(hid.astype(BF16), wd_b[...]))


def moe_experts(xb, blk_expert, w_gate, w_up, w_down):
    n_rows = xb.shape[0] // SUBLANES
    D = w_gate.shape[1]
    blk = (MOE_BLOCK * SUBLANES, LANES)
    wspec = lambda shape: pl.BlockSpec((1,) + shape, lambda i, be: (be[i], 0, 0))
    return pl.pallas_call(
        _expert_kernel,
        grid_spec=pltpu.PrefetchScalarGridSpec(
            num_scalar_prefetch=1,
            grid=(n_rows // MOE_BLOCK,),
            in_specs=[pl.BlockSpec(blk, lambda i, be: (i, 0)),
                      wspec((D, EXPERT_FF)), wspec((D, EXPERT_FF)), wspec((EXPERT_FF, D))],
            out_specs=pl.BlockSpec(blk, lambda i, be: (i, 0)),
            scratch_shapes=[pltpu.VMEM((D, EXPERT_FF), BF16), pltpu.VMEM((D, EXPERT_FF), BF16),
                            pltpu.VMEM((EXPERT_FF, D), BF16)]),
        out_shape=jax.ShapeDtypeStruct((n_rows * SUBLANES, LANES), F32),
        compiler_params=_cparams(("arbitrary",)),
        name="moe_experts",
    )(blk_expert, xb, w_gate, w_up, w_down)


def _combine_kernel(dest_ref, dest_next_ref, x_ref, route_ref, yb_ref, o_ref, ybuf, sems):
    tm = x_ref.shape[0]
    i = pl.program_id(0)
    slot = i % 2

    def gather(d_ref, s):
        def body(t, c):
            _row_copy(yb_ref, d_ref[0, 0, 2 * t], ybuf.at[s], t, sems.at[s]).start(priority=0)
            _row_copy(yb_ref, d_ref[0, 0, 2 * t + 1], ybuf.at[s], tm + t, sems.at[s]).start(priority=1)
            return c

        lax.fori_loop(0, tm, body, 0, unroll=ROW_UNROLL)

    @pl.when(i == 0)
    def _():
        gather(dest_ref, 0)

    @pl.when(i + 1 < pl.num_programs(0))
    def _():
        gather(dest_next_ref, 1 - slot)

    _wait_rows(yb_ref, sems.at[slot], 2 * tm)
    r = route_ref[...]
    w0, w1 = r[:, 2:3], r[:, 3:4]
    y0 = _load_row_tiles(ybuf.at[slot], tm)
    y1 = _load_row_tiles(ybuf.at[slot], tm, base=tm)
    for s in range(SUBLANES):
        cols = slice(s * LANES, (s + 1) * LANES)
        o_ref[:, cols] = x_ref[:, cols] + w0 * y0[s] + w1 * y1[s]


def moe_combine(x, yb, dest, route):
    T, D = x.shape
    tm = DISPATCH_TILE // 2
    nt = T // tm
    dest3 = dest.reshape(nt, 1, DISPATCH_TILE)
    return pl.pallas_call(
        _combine_kernel,
        grid=(nt,),
        in_specs=[pl.BlockSpec((1, 1, DISPATCH_TILE), lambda i: (i, 0, 0), memory_space=pltpu.SMEM),
                  pl.BlockSpec((1, 1, DISPATCH_TILE), lambda i: (jnp.minimum(i + 1, nt - 1), 0, 0),
                               memory_space=pltpu.SMEM),
                  pl.BlockSpec((tm, D), lambda i: (i, 0)),
                  pl.BlockSpec((tm, LANES), lambda i: (i, 0)),
                  pl.BlockSpec(memory_space=pl.ANY)],
        out_specs=pl.BlockSpec((tm, D), lambda i: (i, 0)),
        out_shape=jax.ShapeDtypeStruct((T, D), F32),
        scratch_shapes=[pltpu.VMEM((2, DISPATCH_TILE * SUBLANES, LANES), F32), pltpu.SemaphoreType.DMA((2,))],
        compiler_params=_cparams(("arbitrary",)),
        name="moe_combine",
    )(dest3, dest3, x, route, yb)


def hier_moe_block(x, gain, w_grp, b_grp, w_exp, b_exp, w_gate, w_up, w_down):
    T, D = x.shape
    hn, route, cnt = moe_router(x, gain, w_grp, b_grp, w_exp, b_exp)
    A = 2 * T
    n_rows = -(-A // MOE_BLOCK) * MOE_BLOCK + N_EXPERTS * MOE_BLOCK
    n_blocks = n_rows // MOE_BLOCK
    counts = cnt[0, R_EXP:R_EXP + N_EXPERTS].astype(I32)
    pcounts = (counts + MOE_BLOCK - 1) // MOE_BLOCK * MOE_BLOCK
    pend = jnp.cumsum(pcounts)
    pstart = pend - pcounts
    experts = jnp.arange(N_EXPERTS, dtype=I32)
    flat_e = route[:T, 0:2].astype(I32).reshape(A)
    rank = route[:T, 4:6].astype(I32).reshape(A)
    dest = rank + jnp.sum(jnp.where(flat_e[:, None] == experts[None, :], pstart[None, :], 0), axis=1)
    n_pad = n_rows - A
    pad_cnt = jnp.concatenate([pcounts - counts, (n_rows - pend[-1])[None]])
    pad_start = jnp.concatenate([pstart + counts, pend[-1:]])
    pad_end = jnp.cumsum(pad_cnt)
    k = jnp.arange(n_pad, dtype=I32)
    seg = (k[:, None] >= pad_end[None, :]).astype(I32).sum(axis=1)
    seg_hot = seg[:, None] == jnp.arange(N_EXPERTS + 1, dtype=I32)[None, :]
    pad_dest = k + jnp.sum(jnp.where(seg_hot, (pad_start - (pad_end - pad_cnt))[None, :], 0), axis=1)
    dest_all = jnp.concatenate([dest, pad_dest]).astype(I32)
    blk_start = jnp.arange(n_blocks, dtype=I32) * MOE_BLOCK
    blk_expert = jnp.minimum((pend[None, :] <= blk_start[:, None]).astype(I32).sum(axis=1), N_EXPERTS - 1)
    xb = moe_dispatch(hn, dest_all, T)
    yb = moe_experts(xb, blk_expert, w_gate, w_up, w_down)
    return moe_combine(x, yb, dest, route)


def _even_w_in(w_in):
    ng = 3 * NSA_HPG
    pad = jnp.zeros((w_in.shape[0], LANES - ng), w_in.dtype)
    cols = [w_in[:, :C_GATE]]
    for g in range(NSA_KV):
        cols += [w_in[:, C_GATE + g * ng:C_GATE + (g + 1) * ng], pad]
    return jnp.concatenate(cols, axis=1).astype(BF16)


def even_layer(x, B, S, norm, w_in, lb, hg_gain, q_gain, k_gain, cmp_pe, cmp_w1, cmp_w2, w_out):
    T = B * S
    proj = rms_matmul(x, norm, _even_w_in(w_in), tn=N0 // 4).reshape(B, S, N0)
    a_out = hgrn2(proj, lb, hg_gain)
    kvb = C_KV // LANES
    qn, ksn, kwn, vsb, vwb = prep_heads(
        proj, [C_NQ // (NSA_HEADS * HD), kvb + 2, kvb + 4, kvb + 3, kvb + 5],
        [NSA_HEADS, NSA_KV, NSA_KV, NSA_KV, NSA_KV],
        [q_gain, k_gain[1], k_gain[2], None, None],
        [True, True, True, False, False], [HD ** -0.5, 1.0, 1.0, 1.0, 1.0], S)
    kcb, vcb = nsa_compress(proj, kvb, kvb + 1, cmp_pe, cmp_w1, cmp_w2, k_gain[0])
    b_out = nsa_attention(qn, kcb, vcb, ksn, vsb, kwn, vwb, proj)
    return proj_res(a_out.reshape(T, HG_WIDTH), 0, b_out.reshape(T, NSA_HEADS * HD), 0, w_out.astype(BF16), x)


def odd_layer(x, B, S, norm, w_in, q_gain, k_gain, sinks, w_out):
    T = B * S
    proj = rms_matmul(x, norm, w_in.astype(BF16), tn=N1 // 2).reshape(B, S, N1)
    qd = SWA_HEADS * HD
    qn, kn, vb = prep_heads(
        proj, [0, qd // LANES, qd // LANES + 1], [SWA_HEADS, SWA_KV, SWA_KV],
        [q_gain, k_gain, None], [True, True, False], [HD ** -0.5, 1.0, 1.0], S)
    att = swa_attention(qn, kn, vb, sinks).reshape(T, qd)
    return proj_res(att, 0, att, 1, w_out.astype(BF16), x)


def kernel(x, ev_norm, ev_w_in, hg_lb_logits, hg_out_gain, nsa_q_gain, nsa_k_gain, nsa_cmp_pe, nsa_cmp_w1,
           nsa_cmp_w2, ev_w_out, od_norm, od_w_in, swa_q_gain, swa_k_gain, swa_sinks, od_w_out, moe_norm,
           moe_w_grp, moe_b_grp, moe_w_exp, moe_b_exp, moe_w_gate, moe_w_up, moe_w_down):
    B, S, D = x.shape
    lower_bounds = jnp.cumsum(jax.nn.softmax(hg_lb_logits.astype(F32), axis=0), axis=0)
    h = x.reshape(B * S, D)
    h = even_layer(h, B, S, ev_norm[0], ev_w_in[0], lower_bounds[0], hg_out_gain[0], nsa_q_gain[0],
                   nsa_k_gain[0], nsa_cmp_pe[0], nsa_cmp_w1[0], nsa_cmp_w2[0], ev_w_out[0])
    h = hier_moe_block(h, moe_norm[0], moe_w_grp[0], moe_b_grp[0], moe_w_exp[0], moe_b_exp[0],
                       moe_w_gate[0], moe_w_up[0], moe_w_down[0])
    h = odd_layer(h, B, S, od_norm[0], od_w_in[0], swa_q_gain[0], swa_k_gain[0], swa_sinks[0], od_w_out[0])
    h = hier_moe_block(h, moe_norm[1], moe_w_grp[1], moe_b_grp[1], moe_w_exp[1], moe_b_exp[1],
                       moe_w_gate[1], moe_w_up[1], moe_w_down[1])
    return h.reshape(B, S, D)
```

```python
import functools

import numpy as np
import jax
import jax.numpy as jnp
from jax import lax
from jax.experimental import pallas as pl
from jax.experimental.pallas import tpu as pltpu

F32 = jnp.float32
BF16 = jnp.bfloat16
I32 = jnp.int32
HI = lax.Precision.HIGHEST

D_MODEL = 1024
EPS = 1e-6
ROPE_THETA = 500000.0
HD = 64
ROT = HD // 4
HALF = ROT // 2

HG_DIM = 128
HG_WIDTH = 512
HG_HEADS = 4
HG_CHUNK = 128

NSA_HEADS = 8
NSA_KV = 2
NSA_HPG = NSA_HEADS // NSA_KV
CMP_LEN = 32
CMP_STRIDE = 16
CMP_HIDDEN = 256
SEL_LEN = 64
SEL_TOPN = 8
NSA_WINDOW = 512
FORCE_BONUS = 1e4

SWA_HEADS = 16
SWA_KV = 2
SWA_HPG = SWA_HEADS // SWA_KV
SWA_WINDOW = 128

N_GROUPS = 4
EXP_PER_GROUP = 8
N_EXPERTS = 32
EXPERT_FF = 512
MOE_BLOCK = 512

LANES = 128
NEG = -1e30
VMEM_LIMIT = 56 * 1024 * 1024

C_HG = 0
C_NQ = 2048
C_KV = 2560
C_GATE = 3328
N0 = 3584
N1 = 1280


def _cparams(sem):
    return pltpu.CompilerParams(dimension_semantics=sem, vmem_limit_bytes=VMEM_LIMIT)


def _dot(a, b):
    return jnp.dot(a, b, preferred_element_type=F32)


def _dot_nt(a, b):
    return lax.dot_general(a, b, (((1,), (1,)), ((), ())), preferred_element_type=F32)


def _rms_matmul_kernel(tn, x_ref, g_ref, w_ref, o_ref):
    x = x_ref[...]
    ms = jnp.mean(x * x, axis=-1, keepdims=True)
    xn = (x * lax.rsqrt(ms + EPS) * g_ref[...]).astype(BF16)
    for j in range(o_ref.shape[1] // tn):
        o_ref[:, j * tn:(j + 1) * tn] = _dot(xn, w_ref[:, j * tn:(j + 1) * tn])


def rms_matmul(x, gain, w, tm=512, tn=None):
    T, D = x.shape
    N = w.shape[1]
    tn = N if tn is None else tn
    return pl.pallas_call(
        functools.partial(_rms_matmul_kernel, tn),
        grid=(T // tm,),
        in_specs=[pl.BlockSpec((tm, D), lambda i: (i, 0)),
                  pl.BlockSpec((1, D), lambda i: (0, 0)),
                  pl.BlockSpec((D, N), lambda i: (0, 0))],
        out_specs=pl.BlockSpec((tm, N), lambda i: (i, 0)),
        out_shape=jax.ShapeDtypeStruct((T, N), F32),
        compiler_params=_cparams(("parallel",)),
        name="rms_matmul",
    )(x, gain.reshape(1, D), w)


def _proj_res_kernel(a_ref, b_ref, wa_ref, wb_ref, r_ref, o_ref):
    acc = _dot(a_ref[...], wa_ref[...]) + _dot(b_ref[...], wb_ref[...])
    o_ref[...] = r_ref[...] + acc


def proj_res(a, a_blk, b, b_blk, w, res, tm=512):
    T, D = res.shape
    K = w.shape[0]
    kh = K // 2
    return pl.pallas_call(
        _proj_res_kernel,
        grid=(T // tm,),
        in_specs=[pl.BlockSpec((tm, kh), lambda i: (i, a_blk)),
                  pl.BlockSpec((tm, kh), lambda i: (i, b_blk)),
                  pl.BlockSpec((kh, D), lambda i: (0, 0)),
                  pl.BlockSpec((kh, D), lambda i: (1, 0)),
                  pl.BlockSpec((tm, D), lambda i: (i, 0))],
        out_specs=pl.BlockSpec((tm, D), lambda i: (i, 0)),
        out_shape=jax.ShapeDtypeStruct((T, D), F32),
        compiler_params=_cparams(("parallel",)),
        name="proj_res",
    )(a, b, w, w, res)


def _rope_tables(pos, width):
    inv = jnp.power(ROPE_THETA, -jnp.arange(HALF, dtype=F32) * 2.0 / ROT)
    ang = pos.astype(F32)[:, None] * inv[None, :]
    cos, sin = jnp.cos(ang), jnp.sin(ang)
    n = pos.shape[0]
    c = jnp.concatenate([cos, cos, jnp.ones((n, HD - ROT), F32)], axis=1)
    s = jnp.concatenate([-sin, sin, jnp.zeros((n, HD - ROT), F32)], axis=1)
    reps = width // HD
    return jnp.tile(c, (1, reps)), jnp.tile(s, (1, reps))


def _block_ones(width):
    idx = np.arange(width) // HD
    return jnp.asarray((idx[:, None] == idx[None, :]).astype(np.float32), dtype=BF16)


def _split_dot(a, b):
    hi = a.astype(BF16)
    lo = (a - hi.astype(F32)).astype(BF16)
    return _dot(hi, b) + _dot(lo, b)


def _head_norm_rope(x, gain, cos, sin, bones, scale):
    ms = _split_dot(x * x, bones) * (1.0 / HD)
    y = x * lax.rsqrt(ms + EPS) * gain
    lane = lax.broadcasted_iota(I32, y.shape, 1) % HD
    partner = jnp.where(lane < HALF, pltpu.roll(y, LANES - HALF, axis=1), pltpu.roll(y, HALF, axis=1))
    out = y * cos + partner * sin
    return out * scale if scale != 1.0 else out


def _prep_kernel(n_heads, kinds, scales, *refs):
    nt = len(n_heads)
    x_refs = refs[:nt]
    gain_ref, cos_ref, sin_ref, bones_ref = refs[nt:nt + 4]
    o_refs = refs[nt + 4:]
    cos = cos_ref[...]
    sin = sin_ref[...]
    bones = bones_ref[...]
    ts = cos.shape[0]
    lane = lax.broadcasted_iota(I32, cos.shape, 1)
    pos = pl.program_id(1) * ts + lax.broadcasted_iota(I32, cos.shape, 0)
    for t in range(nt):
        for c in range(n_heads[t] // 2):
            x = x_refs[t][0, :, c * LANES:(c + 1) * LANES]
            if kinds[t] != 'v':
                x = _head_norm_rope(x, gain_ref[t:t + 1, :], cos, sin, bones, scales[t])
            for j, xh in enumerate((x, pltpu.roll(x, HD, axis=1))):
                if kinds[t] == 'qk':
                    y = xh[:, :HD]
                elif kinds[t] == 'pad':
                    y = jnp.where(lane < HD, xh, 0.0)
                elif kinds[t] == 'blk':
                    y = jnp.where(lane < HD, xh, jnp.where(lane - HD == (pos >> 6), 1.0, 0.0))
                else:
                    y = jnp.where(lane < HD, xh, jnp.where(lane == HD, 1.0, 0.0)).T
                o_refs[t][0, 2 * c + j] = y.astype(BF16)


def prep_heads(proj, col_blocks, n_heads, gains, kinds, scales, S, ts=512):
    B = proj.shape[0]
    nt = len(n_heads)
    cos, sin = _rope_tables(jnp.arange(S), LANES)
    gain_rows = []
    for t in range(nt):
        g = gains[t] if gains[t] is not None else jnp.ones((HD,), F32)
        gain_rows.append(jnp.tile(g.astype(F32), 2))
    gain_arr = jnp.stack(gain_rows)
    in_specs = []
    for t in range(nt):
        w = n_heads[t] * HD
        in_specs.append(pl.BlockSpec((1, ts, w), functools.partial(lambda b, s, cb: (b, s, cb), cb=col_blocks[t])))
    in_specs += [pl.BlockSpec((nt, LANES), lambda b, s: (0, 0)),
                 pl.BlockSpec((ts, LANES), lambda b, s: (s, 0)),
                 pl.BlockSpec((ts, LANES), lambda b, s: (s, 0)),
                 pl.BlockSpec((LANES, LANES), lambda b, s: (0, 0))]
    assert S // SEL_LEN <= LANES - HD
    out_specs, out_shape = [], []
    for t in range(nt):
        if kinds[t] == 'v':
            out_specs.append(pl.BlockSpec((1, n_heads[t], LANES, ts), lambda b, s: (b, 0, 0, s)))
            out_shape.append(jax.ShapeDtypeStruct((B, n_heads[t], LANES, S), BF16))
        else:
            width = HD if kinds[t] == 'qk' else LANES
            out_specs.append(pl.BlockSpec((1, n_heads[t], ts, width), lambda b, s: (b, 0, s, 0)))
            out_shape.append(jax.ShapeDtypeStruct((B, n_heads[t], S, width), BF16))
    return pl.pallas_call(
        functools.partial(_prep_kernel, tuple(n_heads), tuple(kinds), tuple(scales)),
        grid=(B, S // ts),
        in_specs=in_specs,
        out_specs=out_specs,
        out_shape=out_shape,
        compiler_params=_cparams(("parallel", "parallel")),
        name="prep_heads",
    )(*([proj] * nt), gain_arr, cos, sin, _block_ones(LANES))


_N_LEVELS = HG_CHUNK.bit_length() - 1


def _hgrn_constants():
    C = HG_CHUNK
    r = np.arange(C)
    mats = [(r[None, :] <= r[:, None]).astype(np.float32),
            (r[None, :] > r[:, None]).astype(np.float32)]
    masks = []
    for lv in range(_N_LEVELS):
        n = 1 << lv
        blk = r // (2 * n)
        right = (r // n) % 2 == 1
        bnd = blk * 2 * n + n - 1
        m = np.zeros((C, C), np.float32)
        for t in range(C):
            if right[t]:
                m[t, bnd[t] + 1:t + 1] = 1.0
            else:
                m[t, t + 1:bnd[t] + 1] = 1.0
        mats.append(m)
        masks.append((blk[:, None] == blk[None, :]).astype(np.float32))
    masks.append(np.eye(C, dtype=np.float32))
    return jnp.asarray(np.concatenate(mats, axis=0), dtype=BF16), jnp.asarray(np.stack(masks))


def _hgrn_kernel(q_ref, f_ref, i_ref, g_ref, lb_ref, gain_ref, mall_ref, masks_ref, o_ref, st_ref):
    C = HG_CHUNK
    ts = q_ref.shape[1]

    @pl.when(pl.program_id(1) == 0)
    def _():
        st_ref[...] = jnp.zeros_like(st_ref)

    mall = mall_ref[...]
    row = lax.broadcasted_iota(I32, (C, HG_DIM), 0)
    gain = gain_ref[...]

    def chunk(c, carry):
        r0 = pl.multiple_of(c * C, C)
        for h in range(HG_HEADS):
            ls = slice(h * HG_DIM, (h + 1) * HG_DIM)
            q = q_ref[0, pl.ds(r0, C), ls]
            z = f_ref[0, pl.ds(r0, C), ls]
            v = i_ref[0, pl.ds(r0, C), ls]
            g = g_ref[0, pl.ds(r0, C), ls]
            lb = lb_ref[:, ls]
            lf = jnp.log(lb + (1.0 - lb) * jax.nn.sigmoid(z))
            k = (1.0 - lb) * jax.nn.sigmoid(-z)
            lf_hi = lf.astype(BF16)
            lf_lo = (lf - lf_hi.astype(F32)).astype(BF16)
            seg = _dot(mall, jnp.concatenate([lf_hi, lf_lo], axis=1))
            ex = jnp.exp(seg[:, :HG_DIM] + seg[:, HG_DIM:])
            qs = q * ex[0:C]
            ks = k * ex[C:2 * C]
            st = st_ref[h]
            o = _dot_nt(qs.astype(BF16), st.astype(BF16))
            a = jnp.where(masks_ref[_N_LEVELS] > 0.5, _dot_nt(q.astype(BF16), k.astype(BF16)), 0.0)
            for lv in range(_N_LEVELS):
                e = ex[(2 + lv) * C:(3 + lv) * C]
                right = ((row >> lv) & 1) == 1
                qe = jnp.where(right, q * e, 0.0).astype(BF16)
                ke = jnp.where(right, 0.0, k * e).astype(BF16)
                a = a + jnp.where(masks_ref[lv] > 0.5, _dot_nt(qe, ke), 0.0)
            vb = v.astype(BF16)
            o = o + _dot(a.astype(BF16), vb)
            decay = ex[C - 1:C]
            st_ref[h] = st * decay + _dot(v.T.astype(BF16), ks.astype(BF16))
            ms = jnp.mean(o * o, axis=-1, keepdims=True)
            y = o * lax.rsqrt(ms + EPS) * gain * (g * jax.nn.sigmoid(g))
            o_ref[0, pl.ds(r0, C), ls] = y.astype(BF16)
        return carry

    lax.fori_loop(0, ts // C, chunk, 0)


def hgrn2(proj, lb, out_gain, ts=512):
    B, S, _ = proj.shape
    mall, masks = _hgrn_constants()
    W = HG_WIDTH
    return pl.pallas_call(
        _hgrn_kernel,
        grid=(B, S // ts),
        in_specs=[pl.BlockSpec((1, ts, W), lambda b, s: (b, s, 0)),
                  pl.BlockSpec((1, ts, W), lambda b, s: (b, s, 1)),
                  pl.BlockSpec((1, ts, W), lambda b, s: (b, s, 2)),
                  pl.BlockSpec((1, ts, W), lambda b, s: (b, s, 3)),
                  pl.BlockSpec((1, W), lambda b, s: (0, 0)),
                  pl.BlockSpec((1, HG_DIM), lambda b, s: (0, 0)),
                  pl.BlockSpec(mall.shape, lambda b, s: (0, 0)),
                  pl.BlockSpec(masks.shape, lambda b, s: (0, 0, 0))],
        out_specs=pl.BlockSpec((1, ts, W), lambda b, s: (b, s, 0)),
        out_shape=jax.ShapeDtypeStruct((B, S, W), BF16),
        scratch_shapes=[pltpu.VMEM((HG_HEADS, HG_DIM, HG_DIM), F32)],
        compiler_params=_cparams(("parallel", "arbitrary")),
        name="hgrn2",
    )(proj, proj, proj, proj, lb.reshape(1, W), out_gain.reshape(1, HG_DIM), mall, masks)


def _compress_kernel(xk_ref, xv_ref, w1k_ref, w1v_ref, pek_ref, pev_ref, w2k_ref, w2v_ref,
                     gain_ref, cos_ref, sin_ref, bones_ref, ok_ref, ov_ref):
    nc = xk_ref.shape[1] // CMP_STRIDE

    def hidden(x_ref, w1_ref, pe_ref):
        width = NSA_KV * CMP_HIDDEN
        u0 = jnp.zeros((nc, width), F32)
        u1 = jnp.zeros((nc, width), F32)
        const = jnp.zeros((8, width), F32)
        for p in range(CMP_STRIDE):
            xp = x_ref[0, pl.ds(p, nc, stride=CMP_STRIDE), :].astype(BF16)
            u0 = u0 + _dot(xp, w1_ref[p])
            u1 = u1 + _dot(xp, w1_ref[CMP_STRIDE + p])
            const = const + _dot(pe_ref[p].astype(BF16), w1_ref[p])
            const = const + _dot(pe_ref[CMP_STRIDE + p].astype(BF16), w1_ref[CMP_STRIDE + p])
        h = u0 + pltpu.roll(u1, nc - 1, axis=0) + const[0:1]
        return jax.nn.gelu(h).astype(BF16)

    yk = _dot(hidden(xk_ref, w1k_ref, pek_ref), w2k_ref[...])
    yk = _head_norm_rope(yk, gain_ref[...], cos_ref[...], sin_ref[...], bones_ref[...], 1.0)
    lane = lax.broadcasted_iota(I32, yk.shape, 1)
    ok_ref[0] = jnp.where(lane < HD, yk, 0.0).astype(BF16)
    ok_ref[1] = jnp.where(lane < HD, pltpu.roll(yk, HD, axis=1), 0.0).astype(BF16)
    yv_t = _dot_nt(w2v_ref[...], hidden(xv_ref, w1v_ref, pev_ref)).astype(BF16)
    ov_ref[0] = yv_t[:HD]
    ov_ref[1] = yv_t[HD:]


def _group_diag(w):
    z = jnp.zeros_like(w)
    return jnp.concatenate([jnp.concatenate([w, z], axis=-1), jnp.concatenate([z, w], axis=-1)], axis=-2)


def nsa_compress(proj, kc_blk, vc_blk, pe, w1, w2, k_gain):
    B, S, _ = proj.shape
    nc = S // CMP_STRIDE
    pos_end = jnp.arange(nc) * CMP_STRIDE + CMP_LEN - 1
    cos, sin = _rope_tables(pos_end, LANES)
    w1d = _group_diag(w1.reshape(2, CMP_LEN, HD, CMP_HIDDEN)).astype(BF16)
    w2d = _group_diag(w2).astype(BF16)
    pe2 = jnp.broadcast_to(jnp.tile(pe, (1, 1, NSA_KV))[:, :, None, :], (2, CMP_LEN, 8, LANES))
    full = lambda shape: pl.BlockSpec(shape, lambda b: (0,) * len(shape))
    w1_shape = (CMP_LEN, LANES, NSA_KV * CMP_HIDDEN)
    return pl.pallas_call(
        _compress_kernel,
        grid=(B,),
        in_specs=[pl.BlockSpec((1, S, LANES), lambda b: (b, 0, kc_blk)),
                  pl.BlockSpec((1, S, LANES), lambda b: (b, 0, vc_blk)),
                  full(w1_shape), full(w1_shape),
                  full((CMP_LEN, 8, LANES)), full((CMP_LEN, 8, LANES)),
                  full((NSA_KV * CMP_HIDDEN, LANES)), full((LANES, NSA_KV * CMP_HIDDEN)),
                  full((1, LANES)), full((nc, LANES)), full((nc, LANES)), full((LANES, LANES))],
        out_specs=[pl.BlockSpec((NSA_KV, nc, LANES), lambda b: (b, 0, 0)),
                   pl.BlockSpec((NSA_KV, HD, nc), lambda b: (b, 0, 0))],
        out_shape=[jax.ShapeDtypeStruct((B * NSA_KV, nc, LANES), BF16),
                   jax.ShapeDtypeStruct((B * NSA_KV, HD, nc), BF16)],
        compiler_params=_cparams(("parallel",)),
        name="nsa_compress",
    )(proj, proj, w1d[0], w1d[1], pe2[0], pe2[1], w2d[0], w2d[1].T,
      jnp.tile(k_gain.astype(F32), 2).reshape(1, LANES), cos, sin, _block_ones(LANES))


NSA_TQ = 256
NSA_TK = 1024


def _nsa_kernel(q_ref, kc_ref, vc_ref, ks_ref, vs_ref, kw_ref, vw_ref, gate_ref, cover_ref, place_ref,
                o_ref, acc_ref):
    TQ, TK, H = NSA_TQ, NSA_TK, NSA_HPG
    R = H * TQ
    t0 = pl.program_id(2) * TQ
    q = q_ref[0].reshape(R, LANES)
    tl = t0 + lax.broadcasted_iota(I32, (1, TQ), 1)

    def all_heads(x):
        return jnp.concatenate([x] * H, axis=1)

    kc = kc_ref[0]
    ncmp = kc.shape[0]
    cmp_end = lax.broadcasted_iota(I32, (ncmp, 1), 0) * CMP_STRIDE + (CMP_LEN - 1)
    valid = all_heads(jnp.where(cmp_end <= tl, 1.0, 0.0))
    s = jnp.where(valid > 0.5, _dot_nt(kc, q), NEG)
    p = jnp.exp(s - jnp.max(s, axis=0, keepdims=True)) * valid
    p_c = p / jnp.maximum(jnp.sum(p, axis=0, keepdims=True), 1e-30)
    o_c = _dot(vc_ref[0], p_c.astype(BF16))

    p_tok = p_c[:, 0:TQ]
    for h in range(1, H):
        p_tok = p_tok + p_c[:, h * TQ:(h + 1) * TQ]
    p_hi = p_tok.astype(BF16)
    p_lo = (p_tok - p_hi.astype(F32)).astype(BF16)
    imp = _dot(cover_ref[...], p_hi) + _dot(cover_ref[...], p_lo)
    nsel = imp.shape[0]
    jj = lax.broadcasted_iota(I32, (nsel, TQ), 0)
    jf = jj.astype(F32)
    cur = tl >> 6
    allowed = jj * SEL_LEN <= tl
    forced = (jj == 0) | (jj == cur) | (jj == cur - 1)
    score = jnp.where(allowed, imp + jnp.where(forced, FORCE_BONUS, 0.0), NEG)
    sel = jnp.zeros((nsel, TQ), F32)
    for _ in range(min(SEL_TOPN, nsel)):
        mx = jnp.max(score, axis=0, keepdims=True)
        first = jnp.min(jnp.where(score == mx, jf, float(nsel)), axis=0, keepdims=True)
        hit = jf == first
        sel = jnp.where(hit & (mx > 0.5 * NEG), 1.0, sel)
        score = jnp.where(hit, NEG, score)

    block_bias = jnp.where(sel > 0.5, 0.0, NEG).T.astype(BF16)
    placed = _dot(block_bias, place_ref[...]).astype(BF16)
    q_sel = q + jnp.concatenate([placed] * H, axis=0)
    acc_ref[...] = jnp.zeros(acc_ref.shape, F32)

    def sweep_tile(kt, m_old, causal):
        k0 = pl.multiple_of(kt * TK, TK)
        s = _dot_nt(ks_ref[0, 0, pl.ds(k0, TK), :], q_sel)
        if causal:
            kpos = k0 + lax.broadcasted_iota(I32, (TK, 1), 0)
            s = s + all_heads(jnp.where(kpos <= tl, 0.0, NEG))
        m_new = jnp.maximum(m_old, jnp.max(s, axis=0, keepdims=True))
        p = jnp.exp(s - m_new).astype(BF16)
        acc_ref[...] = jnp.exp(m_old - m_new) * acc_ref[...] + _dot(vs_ref[0, 0, :, pl.ds(k0, TK)], p)
        return m_new

    last = t0 // TK
    m_run = lax.fori_loop(0, last, lambda kt, m: sweep_tile(kt, m, False), jnp.full((1, R), NEG, F32))
    sweep_tile(last, m_run, True)
    acc = acc_ref[...]
    o_s = acc[:HD] / jnp.maximum(acc[HD:HD + 1], 1e-30)

    WK = NSA_WINDOW + TQ
    ws = pl.multiple_of(jnp.maximum(t0 - NSA_WINDOW, 0), TQ)
    kp = ws + lax.broadcasted_iota(I32, (WK, 1), 0)
    wbias = jnp.where((kp <= tl) & (kp > tl - NSA_WINDOW), 0.0, NEG)
    s = _dot_nt(kw_ref[0, 0, pl.ds(ws, WK), :], q) + all_heads(wbias)
    p = jnp.exp(s - jnp.max(s, axis=0, keepdims=True)).astype(BF16)
    acc_w = _dot(vw_ref[0, 0, :, pl.ds(ws, WK)], p)
    o_w = acc_w[:HD] / acc_w[HD:HD + 1]

    sg = jax.nn.sigmoid(gate_ref[0]).T
    heads_out = []
    for h in range(H):
        cols = slice(h * TQ, (h + 1) * TQ)
        heads_out.append(sg[3 * h:3 * h + 1] * o_c[:, cols] + sg[3 * h + 1:3 * h + 2] * o_s[:, cols]
                         + sg[3 * h + 2:3 * h + 3] * o_w[:, cols])
    o_ref[0] = jnp.concatenate(heads_out, axis=0).T.astype(BF16)


def nsa_attention(qn, kcb, vcb, ksn, vsb, kwn, vwb, proj):
    B, _, S, _ = qn.shape
    G, H, TQ = NSA_KV, NSA_HPG, NSA_TQ
    assert S % NSA_TK == 0 and S >= NSA_WINDOW + TQ
    ncmp = kcb.shape[1]
    nsel = S // SEL_LEN
    n = np.arange(ncmp)[:, None] * CMP_STRIDE
    j = np.arange(nsel)[None, :] * SEL_LEN
    cover = jnp.asarray(((n < j + SEL_LEN) & (n + CMP_LEN > j)).astype(np.float32).T, dtype=BF16)
    place = np.zeros((nsel, LANES), np.float32)
    place[np.arange(nsel), HD + np.arange(nsel)] = 1.0
    R = H * TQ
    k_spec = pl.BlockSpec((1, 1, S, LANES), lambda b, g, i: (b, g, 0, 0))
    v_spec = pl.BlockSpec((1, 1, LANES, S), lambda b, g, i: (b, g, 0, 0))
    return pl.pallas_call(
        _nsa_kernel,
        grid=(B, G, S // TQ),
        in_specs=[pl.BlockSpec((1, H, TQ, LANES), lambda b, g, i: (b, g, i, 0)),
                  pl.BlockSpec((1, ncmp, LANES), lambda b, g, i: (b * G + g, 0, 0)),
                  pl.BlockSpec((1, HD, ncmp), lambda b, g, i: (b * G + g, 0, 0)),
                  k_spec, v_spec, k_spec, v_spec,
                  pl.BlockSpec((1, TQ, LANES), lambda b, g, i: (b, i, C_GATE // LANES + g)),
                  pl.BlockSpec((nsel, ncmp), lambda b, g, i: (0, 0)),
                  pl.BlockSpec((nsel, LANES), lambda b, g, i: (0, 0))],
        out_specs=pl.BlockSpec((1, TQ, H * HD), lambda b, g, i: (b, i, g)),
        out_shape=jax.ShapeDtypeStruct((B, S, NSA_HEADS * HD), BF16),
        scratch_shapes=[pltpu.VMEM((LANES, R), F32)],
        compiler_params=_cparams(("parallel", "parallel", "arbitrary")),
        name="nsa_attention",
    )(qn, kcb, vcb, ksn, vsb, kwn, vwb, proj, cover, jnp.asarray(place, dtype=BF16))


SWA_TQ = 256


def _swa_kernel(sink_ref, q_ref, k_ref, v_ref, o_ref):
    TQ, H = SWA_TQ, SWA_HPG
    R = H * TQ
    g = pl.program_id(1)
    t0 = pl.program_id(2) * TQ
    q = q_ref[0].reshape(R, HD)
    tl = t0 + lax.broadcasted_iota(I32, (1, TQ), 1)
    WK = SWA_WINDOW + TQ
    ws = pl.multiple_of(jnp.maximum(t0 - SWA_WINDOW, 0), LANES)
    kp = ws + lax.broadcasted_iota(I32, (WK, 1), 0)
    bias = jnp.where((kp <= tl) & (kp > tl - SWA_WINDOW), 0.0, NEG)
    s = _dot_nt(k_ref[0, 0, pl.ds(ws, WK), :], q) + jnp.concatenate([bias] * H, axis=1)
    head = lax.broadcasted_iota(I32, (1, R), 1) // TQ
    sink = jnp.zeros((1, R), F32)
    for h in range(H):
        sink = jnp.where(head == h, sink_ref[g * H + h], sink)
    m = jnp.maximum(jnp.max(s, axis=0, keepdims=True), sink)
    p = jnp.exp(s - m).astype(BF16)
    acc = _dot(v_ref[0, 0, :, pl.ds(ws, WK)], p)
    o = acc[:HD] / (acc[HD:HD + 1] + jnp.exp(sink - m))
    o_ref[0] = jnp.concatenate([o[:, h * TQ:(h + 1) * TQ] for h in range(H)], axis=0).T.astype(BF16)


def swa_attention(qn, kn, vb, sinks):
    B, _, S, _ = qn.shape
    G, H, TQ = SWA_KV, SWA_HPG, SWA_TQ
    assert S >= SWA_WINDOW + TQ
    k_spec = pl.BlockSpec((1, 1, S, HD), lambda b, g, i, sk: (b, g, 0, 0))
    v_spec = pl.BlockSpec((1, 1, LANES, S), lambda b, g, i, sk: (b, g, 0, 0))
    return pl.pallas_call(
        _swa_kernel,
        grid_spec=pltpu.PrefetchScalarGridSpec(
            num_scalar_prefetch=1,
            grid=(B, G, S // TQ),
            in_specs=[pl.BlockSpec((1, H, TQ, HD), lambda b, g, i, sk: (b, g, i, 0)), k_spec, v_spec],
            out_specs=pl.BlockSpec((1, TQ, H * HD), lambda b, g, i, sk: (b, i, g))),
        out_shape=jax.ShapeDtypeStruct((B, S, SWA_HEADS * HD), BF16),
        compiler_params=_cparams(("parallel", "parallel", "arbitrary")),
        name="swa_attention",
    )(sinks.astype(F32), qn, kn, vb)


R_GRP = 0
R_EXP = 32


SUBLANES = 8


def _store_row_tiles(ref, val, base=0):
    for s in range(SUBLANES):
        ref[pl.ds(base * SUBLANES + s, val.shape[0], stride=SUBLANES), :] = val[:, s * LANES:(s + 1) * LANES]


def _load_row_tiles(ref, rows, base=0):
    return [ref[pl.ds(base * SUBLANES + s, rows, stride=SUBLANES), :] for s in range(SUBLANES)]


def _router_kernel(x_ref, g_ref, w_ref, b_ref, tri_ref, hn_ref, route_ref, cnt_ref, base_ref):
    i = pl.program_id(0)
    last = pl.num_programs(0) - 1

    @pl.when(i == 0)
    def _():
        base_ref[...] = jnp.zeros_like(base_ref)

    @pl.when(i < last)
    def _():
        _route_tile(x_ref, g_ref, w_ref, b_ref, tri_ref, hn_ref, route_ref, base_ref)

    @pl.when(i == last)
    def _():
        hn_ref[...] = jnp.zeros_like(hn_ref)
        route_ref[...] = jnp.zeros_like(route_ref)

    cnt_ref[...] = base_ref[...]


def _route_tile(x_ref, g_ref, w_ref, b_ref, tri_ref, hn_ref, route_ref, base_ref):
    x = x_ref[...]
    ms = jnp.mean(x * x, axis=-1, keepdims=True)
    h = x * lax.rsqrt(ms + EPS) * g_ref[...]
    _store_row_tiles(hn_ref, h)
    h_hi = h.astype(BF16)
    h_lo = (h - h_hi.astype(F32)).astype(BF16)
    logits = _dot(h_hi, w_ref[0]) + _dot(h_lo, w_ref[0]) + _dot(h_hi, w_ref[1]) + b_ref[...]
    lane = lax.broadcasted_iota(I32, logits.shape, 1)
    is_grp = lane < N_GROUPS
    gl = jnp.where(is_grp, logits, NEG)
    gmax = jnp.max(gl, axis=-1, keepdims=True)
    lanef = lane.astype(F32)
    gidx = jnp.min(jnp.where(gl == gmax, lanef, float(LANES)), axis=-1, keepdims=True)
    gw = 1.0 / jnp.sum(jnp.where(is_grp, jnp.exp(logits - gmax), 0.0), axis=-1, keepdims=True)
    in_grp = ((lane >> 3) - R_EXP // EXP_PER_GROUP).astype(F32) == gidx
    w0 = jnp.where(in_grp, logits, NEG)
    m1 = jnp.max(w0, axis=-1, keepdims=True)
    i1 = jnp.min(jnp.where(w0 == m1, lanef, float(LANES)), axis=-1, keepdims=True)
    w1 = jnp.where(lanef == i1, NEG, w0)
    m2 = jnp.max(w1, axis=-1, keepdims=True)
    i2 = jnp.min(jnp.where(w1 == m2, lanef, float(LANES)), axis=-1, keepdims=True)
    e2 = jnp.exp(m2 - m1)
    p1 = gw / (1.0 + e2)
    p2 = gw * e2 / (1.0 + e2)
    oh1 = jnp.where(lanef == i1, 1.0, 0.0)
    oh2 = jnp.where(lanef == i2, 1.0, 0.0)
    oh = oh1 + oh2
    before = _dot(tri_ref[...], oh.astype(BF16)) + base_ref[0:1, :]
    r1 = jnp.sum(before * oh1, axis=-1, keepdims=True)
    r2 = jnp.sum(before * oh2, axis=-1, keepdims=True)
    base_ref[...] = base_ref[...] + jnp.sum(oh, axis=0, keepdims=True)
    route = jnp.zeros_like(logits)
    for ln, val in enumerate((i1 - R_EXP, i2 - R_EXP, p1, p2, r1, r2)):
        route = jnp.where(lane == ln, val, route)
    route_ref[...] = route


def moe_router(x, gain, w_grp, b_grp, w_exp, b_exp, tm=256):
    T, D = x.shape
    nt = T // tm
    w = jnp.zeros((D, LANES), F32).at[:, R_GRP:R_GRP + N_GROUPS].set(w_grp).at[:, R_EXP:R_EXP + N_EXPERTS].set(w_exp)
    b = jnp.zeros((1, LANES), F32).at[0, R_GRP:R_GRP + N_GROUPS].set(b_grp).at[0, R_EXP:R_EXP + N_EXPERTS].set(b_exp)
    w_hi = w.astype(BF16)
    w = jnp.stack([w_hi, (w - w_hi.astype(F32)).astype(BF16)])
    tri = jnp.asarray(np.tril(np.ones((tm, tm), np.float32), -1), dtype=BF16)
    return pl.pallas_call(
        _router_kernel,
        grid=(nt + 1,),
        in_specs=[pl.BlockSpec((tm, D), lambda i: (jnp.minimum(i, nt - 1), 0)),
                  pl.BlockSpec((1, D), lambda i: (0, 0)),
                  pl.BlockSpec((2, D, LANES), lambda i: (0, 0, 0)),
                  pl.BlockSpec((1, LANES), lambda i: (0, 0)),
                  pl.BlockSpec((tm, tm), lambda i: (0, 0))],
        out_specs=[pl.BlockSpec((tm * SUBLANES, LANES), lambda i: (i, 0)),
                   pl.BlockSpec((tm, LANES), lambda i: (i, 0)),
                   pl.BlockSpec((8, LANES), lambda i: (0, 0))],
        out_shape=[jax.ShapeDtypeStruct(((T + tm) * SUBLANES, LANES), F32),
                   jax.ShapeDtypeStruct((T + tm, LANES), F32),
                   jax.ShapeDtypeStruct((8, LANES), F32)],
        scratch_shapes=[pltpu.VMEM((8, LANES), F32)],
        compiler_params=_cparams(("arbitrary",)),
        name="moe_router",
    )(x, gain.reshape(1, D), w, b, tri)


def _row_copy(src, src_row, dst, dst_row, sem):
    def tile(ref, row):
        return ref.at[pl.ds(pl.multiple_of(row * SUBLANES, SUBLANES), SUBLANES)]

    return pltpu.make_async_copy(tile(src, src_row), tile(dst, dst_row), sem)


DISPATCH_TILE = 512
ROW_UNROLL = 4


def _wait_rows(hbm_ref, sem, n):
    view = hbm_ref.at[pl.ds(0, n * SUBLANES)]
    pltpu.make_async_copy(view, view, sem).wait()


def _dispatch_kernel(dest_ref, x_ref, xb_ref, stage, sems):
    i = pl.program_id(0)
    last = pl.num_programs(0) - 1
    slot = i % 2

    @pl.when(i >= 2)
    def _():
        _wait_rows(xb_ref, sems.at[slot], DISPATCH_TILE)

    stage[slot] = x_ref[...]

    def issue(t, c):
        _row_copy(stage.at[slot], t, xb_ref, dest_ref[0, 0, 2 * t], sems.at[slot]).start(priority=0)
        _row_copy(stage.at[slot], t, xb_ref, dest_ref[0, 0, 2 * t + 1], sems.at[slot]).start(priority=1)
        return c

    lax.fori_loop(0, DISPATCH_TILE // 2, issue, 0, unroll=ROW_UNROLL)

    @pl.when(i == last)
    def _():
        _wait_rows(xb_ref, sems.at[slot], DISPATCH_TILE)

    @pl.when((i == last) & (i >= 1))
    def _():
        _wait_rows(xb_ref, sems.at[1 - slot], DISPATCH_TILE)


def moe_dispatch(hn, dest_all, n_tok):
    n_rows = dest_all.shape[0]
    tm = DISPATCH_TILE // 2
    nt = n_rows // DISPATCH_TILE
    n_tok_tiles = n_tok // tm
    return pl.pallas_call(
        _dispatch_kernel,
        grid=(nt,),
        in_specs=[pl.BlockSpec((1, 1, DISPATCH_TILE), lambda i: (i, 0, 0), memory_space=pltpu.SMEM),
                  pl.BlockSpec((tm * SUBLANES, LANES), lambda i: (jnp.minimum(i, n_tok_tiles), 0))],
        out_specs=pl.BlockSpec(memory_space=pl.ANY),
        out_shape=jax.ShapeDtypeStruct((n_rows * SUBLANES, LANES), F32),
        scratch_shapes=[pltpu.VMEM((2, tm * SUBLANES, LANES), F32), pltpu.SemaphoreType.DMA((2,))],
        compiler_params=_cparams(("arbitrary",)),
        name="moe_dispatch",
    )(dest_all.reshape(nt, 1, DISPATCH_TILE), hn)


def _expert_kernel(be_ref, x_ref, wg_ref, wu_ref, wd_ref, y_ref, wg_b, wu_b, wd_b):
    i = pl.program_id(0)

    @pl.when((i == 0) | (be_ref[i] != be_ref[jnp.maximum(i - 1, 0)]))
    def _():
        wg_b[...] = wg_ref[0].astype(BF16)
        wu_b[...] = wu_ref[0].astype(BF16)
        wd_b[...] = wd_ref[0].astype(BF16)

    x = jnp.concatenate(_load_row_tiles(x_ref, MOE_BLOCK), axis=1).astype(BF16)
    hid = _dot(x, wg_b[...])
    hid = hid * jax.nn.sigmoid(hid) * _dot(x, wu_b[...])
    _store_row_tiles(y_ref, _dot(hid.astype(BF16), wd_b[...]))


def moe_experts(xb, blk_expert, w_gate, w_up, w_down):
    n_rows = xb.shape[0] // SUBLANES
    D = w_gate.shape[1]
    blk = (MOE_BLOCK * SUBLANES, LANES)
    wspec = lambda shape: pl.BlockSpec((1,) + shape, lambda i, be: (be[i], 0, 0))
    return pl.pallas_call(
        _expert_kernel,
        grid_spec=pltpu.PrefetchScalarGridSpec(
            num_scalar_prefetch=1,
            grid=(n_rows // MOE_BLOCK,),
            in_specs=[pl.BlockSpec(blk, lambda i, be: (i, 0)),
                      wspec((D, EXPERT_FF)), wspec((D, EXPERT_FF)), wspec((EXPERT_FF, D))],
            out_specs=pl.BlockSpec(blk, lambda i, be: (i, 0)),
            scratch_shapes=[pltpu.VMEM((D, EXPERT_FF), BF16), pltpu.VMEM((D, EXPERT_FF), BF16),
                            pltpu.VMEM((EXPERT_FF, D), BF16)]),
        out_shape=jax.ShapeDtypeStruct((n_rows * SUBLANES, LANES), F32),
        compiler_params=_cparams(("arbitrary",)),
        name="moe_experts",
    )(blk_expert, xb, w_gate, w_up, w_down)


def _combine_kernel(dest_ref, dest_next_ref, x_ref, route_ref, yb_ref, o_ref, ybuf, sems):
    tm = x_ref.shape[0]
    i = pl.program_id(0)
    slot = i % 2

    def gather(d_ref, s):
        def body(t, c):
            _row_copy(yb_ref, d_ref[0, 0, 2 * t], ybuf.at[s], t, sems.at[s]).start(priority=0)
            _row_copy(yb_ref, d_ref[0, 0, 2 * t + 1], ybuf.at[s], tm + t, sems.at[s]).start(priority=1)
            return c

        lax.fori_loop(0, tm, body, 0, unroll=ROW_UNROLL)

    @pl.when(i == 0)
    def _():
        gather(dest_ref, 0)

    @pl.when(i + 1 < pl.num_programs(0))
    def _():
        gather(dest_next_ref, 1 - slot)

    _wait_rows(yb_ref, sems.at[slot], 2 * tm)
    r = route_ref[...]
    w0, w1 = r[:, 2:3], r[:, 3:4]
    y0 = _load_row_tiles(ybuf.at[slot], tm)
    y1 = _load_row_tiles(ybuf.at[slot], tm, base=tm)
    for s in range(SUBLANES):
        cols = slice(s * LANES, (s + 1) * LANES)
        o_ref[:, cols] = x_ref[:, cols] + w0 * y0[s] + w1 * y1[s]


def moe_combine(x, yb, dest, route):
    T, D = x.shape
    tm = DISPATCH_TILE // 2
    nt = T // tm
    dest3 = dest.reshape(nt, 1, DISPATCH_TILE)
    return pl.pallas_call(
        _combine_kernel,
        grid=(nt,),
        in_specs=[pl.BlockSpec((1, 1, DISPATCH_TILE), lambda i: (i, 0, 0), memory_space=pltpu.SMEM),
                  pl.BlockSpec((1, 1, DISPATCH_TILE), lambda i: (jnp.minimum(i + 1, nt - 1), 0, 0),
                               memory_space=pltpu.SMEM),
                  pl.BlockSpec((tm, D), lambda i: (i, 0)),
                  pl.BlockSpec((tm, LANES), lambda i: (i, 0)),
                  pl.BlockSpec(memory_space=pl.ANY)],
        out_specs=pl.BlockSpec((tm, D), lambda i: (i, 0)),
        out_shape=jax.ShapeDtypeStruct((T, D), F32),
        scratch_shapes=[pltpu.VMEM((2, DISPATCH_TILE * SUBLANES, LANES), F32), pltpu.SemaphoreType.DMA((2,))],
        compiler_params=_cparams(("arbitrary",)),
        name="moe_combine",
    )(dest3, dest3, x, route, yb)


def hier_moe_block(x, gain, w_grp, b_grp, w_exp, b_exp, w_gate, w_up, w_down):
    T, D = x.shape
    hn, route, cnt = moe_router(x, gain, w_grp, b_grp, w_exp, b_exp)
    A = 2 * T
    n_rows = -(-A // MOE_BLOCK) * MOE_BLOCK + N_EXPERTS * MOE_BLOCK
    n_blocks = n_rows // MOE_BLOCK
    counts = cnt[0, R_EXP:R_EXP + N_EXPERTS].astype(I32)
    pcounts = (counts + MOE_BLOCK - 1) // MOE_BLOCK * MOE_BLOCK
    pend = jnp.cumsum(pcounts)
    pstart = pend - pcounts
    experts = jnp.arange(N_EXPERTS, dtype=I32)
    flat_e = route[:T, 0:2].astype(I32).reshape(A)
    rank = route[:T, 4:6].astype(I32).reshape(A)
    dest = rank + jnp.sum(jnp.where(flat_e[:, None] == experts[None, :], pstart[None, :], 0), axis=1)
    n_pad = n_rows - A
    pad_cnt = jnp.concatenate([pcounts - counts, (n_rows - pend[-1])[None]])
    pad_start = jnp.concatenate([pstart + counts, pend[-1:]])
    pad_end = jnp.cumsum(pad_cnt)
    k = jnp.arange(n_pad, dtype=I32)
    seg = (k[:, None] >= pad_end[None, :]).astype(I32).sum(axis=1)
    seg_hot = seg[:, None] == jnp.arange(N_EXPERTS + 1, dtype=I32)[None, :]
    pad_dest = k + jnp.sum(jnp.where(seg_hot, (pad_start - (pad_end - pad_cnt))[None, :], 0), axis=1)
    dest_all = jnp.concatenate([dest, pad_dest]).astype(I32)
    blk_start = jnp.arange(n_blocks, dtype=I32) * MOE_BLOCK
    blk_expert = jnp.minimum((pend[None, :] <= blk_start[:, None]).astype(I32).sum(axis=1), N_EXPERTS - 1)
    xb = moe_dispatch(hn, dest_all, T)
    yb = moe_experts(xb, blk_expert, w_gate, w_up, w_down)
    return moe_combine(x, yb, dest, route)


def _even_w_in(w_in):
    ng = 3 * NSA_HPG
    pad = jnp.zeros((w_in.shape[0], LANES - ng), w_in.dtype)
    cols = [w_in[:, :C_GATE]]
    for g in range(NSA_KV):
        cols += [w_in[:, C_GATE + g * ng:C_GATE + (g + 1) * ng], pad]
    return jnp.concatenate(cols, axis=1).astype(BF16)


def even_layer(x, B, S, norm, w_in, lb, hg_gain, q_gain, k_gain, cmp_pe, cmp_w1, cmp_w2, w_out):
    T = B * S
    proj = rms_matmul(x, norm, _even_w_in(w_in), tn=N0 // 4).reshape(B, S, N0)
    a_out = hgrn2(proj, lb, hg_gain)
    kvb = C_KV // LANES
    qn, ksn, kwn, vsb, vwb = prep_heads(
        proj, [C_NQ // (NSA_HEADS * HD), kvb + 2, kvb + 4, kvb + 3, kvb + 5],
        [NSA_HEADS, NSA_KV, NSA_KV, NSA_KV, NSA_KV],
        [q_gain, k_gain[1], k_gain[2], None, None],
        ['pad', 'blk', 'pad', 'v', 'v'], [HD ** -0.5, 1.0, 1.0, 1.0, 1.0], S)
    kcb, vcb = nsa_compress(proj, kvb, kvb + 1, cmp_pe, cmp_w1, cmp_w2, k_gain[0])
    b_out = nsa_attention(qn, kcb, vcb, ksn, vsb, kwn, vwb, proj)
    return proj_res(a_out.reshape(T, HG_WIDTH), 0, b_out.reshape(T, NSA_HEADS * HD), 0, w_out.astype(BF16), x)


def odd_layer(x, B, S, norm, w_in, q_gain, k_gain, sinks, w_out):
    T = B * S
    proj = rms_matmul(x, norm, w_in.astype(BF16), tn=N1 // 2).reshape(B, S, N1)
    qd = SWA_HEADS * HD
    qn, kn, vb = prep_heads(
        proj, [0, qd // LANES, qd // LANES + 1], [SWA_HEADS, SWA_KV, SWA_KV],
        [q_gain, k_gain, None], ['qk', 'qk', 'v'], [HD ** -0.5, 1.0, 1.0], S)
    att = swa_attention(qn, kn, vb, sinks).reshape(T, qd)
    return proj_res(att, 0, att, 1, w_out.astype(BF16), x)


def kernel(x, ev_norm, ev_w_in, hg_lb_logits, hg_out_gain, nsa_q_gain, nsa_k_gain, nsa_cmp_pe, nsa_cmp_w1,
           nsa_cmp_w2, ev_w_out, od_norm, od_w_in, swa_q_gain, swa_k_gain, swa_sinks, od_w_out, moe_norm,
           moe_w_grp, moe_b_grp, moe_w_exp, moe_b_exp, moe_w_gate, moe_w_up, moe_w_down):
    B, S, D = x.shape
    lower_bounds = jnp.cumsum(jax.nn.softmax(hg_lb_logits.astype(F32), axis=0), axis=0)
    h = x.reshape(B * S, D)
    h = even_layer(h, B, S, ev_norm[0], ev_w_in[0], lower_bounds[0], hg_out_gain[0], nsa_q_gain[0],
                   nsa_k_gain[0], nsa_cmp_pe[0], nsa_cmp_w1[0], nsa_cmp_w2[0], ev_w_out[0])
    h = hier_moe_block(h, moe_norm[0], moe_w_grp[0], moe_b_grp[0], moe_w_exp[0], moe_b_exp[0],
                       moe_w_gate[0], moe_w_up[0], moe_w_down[0])
    h = odd_layer(h, B, S, od_norm[0], od_w_in[0], swa_q_gain[0], swa_k_gain[0], swa_sinks[0], od_w_out[0])
    h = hier_moe_block(h, moe_norm[1], moe_w_grp[1], moe_b_grp[1], moe_w_exp[1], moe_b_exp[1],
                       moe_w_gate[1], moe_w_up[1], moe_w_down[1])
    return h.reshape(B, S, D)
```

```python
import functools

import numpy as np
import jax
import jax.numpy as jnp
from jax import lax
from jax.experimental import pallas as pl
from jax.experimental.pallas import tpu as pltpu

F32 = jnp.float32
BF16 = jnp.bfloat16
I32 = jnp.int32
HI = lax.Precision.HIGHEST

D_MODEL = 1024
EPS = 1e-6
ROPE_THETA = 500000.0
HD = 64
ROT = HD // 4
HALF = ROT // 2

HG_DIM = 128
HG_WIDTH = 512
HG_HEADS = 4
HG_CHUNK = 128

NSA_HEADS = 8
NSA_KV = 2
NSA_HPG = NSA_HEADS // NSA_KV
CMP_LEN = 32
CMP_STRIDE = 16
CMP_HIDDEN = 256
SEL_LEN = 64
SEL_TOPN = 8
NSA_WINDOW = 512
FORCE_BONUS = 1e4

SWA_HEADS = 16
SWA_KV = 2
SWA_HPG = SWA_HEADS // SWA_KV
SWA_WINDOW = 128

N_GROUPS = 4
EXP_PER_GROUP = 8
N_EXPERTS = 32
EXPERT_FF = 512
MOE_BLOCK = 512

LANES = 128
NEG = -1e30
VMEM_LIMIT = 56 * 1024 * 1024

C_HG = 0
C_NQ = 2048
C_KV = 2560
C_GATE = 3328
N0 = 3584
N1 = 1280


def _cparams(sem):
    return pltpu.CompilerParams(dimension_semantics=sem, vmem_limit_bytes=VMEM_LIMIT)


def _dot(a, b):
    return jnp.dot(a, b, preferred_element_type=F32)


def _dot_nt(a, b):
    return lax.dot_general(a, b, (((1,), (1,)), ((), ())), preferred_element_type=F32)


def _rms_matmul_kernel(tn, x_ref, g_ref, w_ref, o_ref):
    x = x_ref[...]
    ms = jnp.mean(x * x, axis=-1, keepdims=True)
    xn = (x * lax.rsqrt(ms + EPS) * g_ref[...]).astype(BF16)
    for j in range(o_ref.shape[1] // tn):
        o_ref[:, j * tn:(j + 1) * tn] = _dot(xn, w_ref[:, j * tn:(j + 1) * tn])


def rms_matmul(x, gain, w, tm=512, tn=None):
    T, D = x.shape
    N = w.shape[1]
    tn = N if tn is None else tn
    return pl.pallas_call(
        functools.partial(_rms_matmul_kernel, tn),
        grid=(T // tm,),
        in_specs=[pl.BlockSpec((tm, D), lambda i: (i, 0)),
                  pl.BlockSpec((1, D), lambda i: (0, 0)),
                  pl.BlockSpec((D, N), lambda i: (0, 0))],
        out_specs=pl.BlockSpec((tm, N), lambda i: (i, 0)),
        out_shape=jax.ShapeDtypeStruct((T, N), F32),
        compiler_params=_cparams(("parallel",)),
        name="rms_matmul",
    )(x, gain.reshape(1, D), w)


def _proj_res_kernel(a_ref, b_ref, wa_ref, wb_ref, r_ref, o_ref):
    acc = _dot(a_ref[...], wa_ref[...]) + _dot(b_ref[...], wb_ref[...])
    o_ref[...] = r_ref[...] + acc


def proj_res(a, a_blk, b, b_blk, w, res, tm=512):
    T, D = res.shape
    K = w.shape[0]
    kh = K // 2
    return pl.pallas_call(
        _proj_res_kernel,
        grid=(T // tm,),
        in_specs=[pl.BlockSpec((tm, kh), lambda i: (i, a_blk)),
                  pl.BlockSpec((tm, kh), lambda i: (i, b_blk)),
                  pl.BlockSpec((kh, D), lambda i: (0, 0)),
                  pl.BlockSpec((kh, D), lambda i: (1, 0)),
                  pl.BlockSpec((tm, D), lambda i: (i, 0))],
        out_specs=pl.BlockSpec((tm, D), lambda i: (i, 0)),
        out_shape=jax.ShapeDtypeStruct((T, D), F32),
        compiler_params=_cparams(("parallel",)),
        name="proj_res",
    )(a, b, w, w, res)


def _rope_tables(pos, width):
    inv = jnp.power(ROPE_THETA, -jnp.arange(HALF, dtype=F32) * 2.0 / ROT)
    ang = pos.astype(F32)[:, None] * inv[None, :]
    cos, sin = jnp.cos(ang), jnp.sin(ang)
    n = pos.shape[0]
    c = jnp.concatenate([cos, cos, jnp.ones((n, HD - ROT), F32)], axis=1)
    s = jnp.concatenate([-sin, sin, jnp.zeros((n, HD - ROT), F32)], axis=1)
    reps = width // HD
    return jnp.tile(c, (1, reps)), jnp.tile(s, (1, reps))


def _block_ones(width):
    idx = np.arange(width) // HD
    return jnp.asarray((idx[:, None] == idx[None, :]).astype(np.float32), dtype=BF16)


def _split_dot(a, b):
    hi = a.astype(BF16)
    lo = (a - hi.astype(F32)).astype(BF16)
    return _dot(hi, b) + _dot(lo, b)


def _head_norm_rope(x, gain, cos, sin, bones, scale):
    ms = _split_dot(x * x, bones) * (1.0 / HD)
    y = x * lax.rsqrt(ms + EPS) * gain
    lane = lax.broadcasted_iota(I32, y.shape, 1) % HD
    partner = jnp.where(lane < HALF, pltpu.roll(y, LANES - HALF, axis=1), pltpu.roll(y, HALF, axis=1))
    out = y * cos + partner * sin
    return out * scale if scale != 1.0 else out


def _prep_kernel(n_heads, kinds, scales, *refs):
    nt = len(n_heads)
    x_refs = refs[:nt]
    gain_ref, cos_ref, sin_ref, bones_ref = refs[nt:nt + 4]
    o_refs = refs[nt + 4:]
    cos = cos_ref[...]
    sin = sin_ref[...]
    bones = bones_ref[...]
    ts = cos.shape[0]
    lane = lax.broadcasted_iota(I32, cos.shape, 1)
    pos = pl.program_id(1) * ts + lax.broadcasted_iota(I32, cos.shape, 0)
    for t in range(nt):
        for c in range(n_heads[t] // 2):
            x = x_refs[t][0, :, c * LANES:(c + 1) * LANES]
            if kinds[t] != 'v':
                x = _head_norm_rope(x, gain_ref[t:t + 1, :], cos, sin, bones, scales[t])
            for j, xh in enumerate((x, pltpu.roll(x, HD, axis=1))):
                if kinds[t] == 'qk':
                    y = xh[:, :HD]
                elif kinds[t] == 'pad':
                    y = jnp.where(lane < HD, xh, 0.0)
                elif kinds[t] == 'blk':
                    y = jnp.where(lane < HD, xh, jnp.where(lane - HD == (pos >> 6), 1.0, 0.0))
                else:
                    y = jnp.where(lane < HD, xh, jnp.where(lane == HD, 1.0, 0.0)).T
                o_refs[t][0, 2 * c + j] = y.astype(BF16)


def prep_heads(proj, col_blocks, n_heads, gains, kinds, scales, S, ts=512):
    B = proj.shape[0]
    nt = len(n_heads)
    cos, sin = _rope_tables(jnp.arange(S), LANES)
    gain_rows = []
    for t in range(nt):
        g = gains[t] if gains[t] is not None else jnp.ones((HD,), F32)
        gain_rows.append(jnp.tile(g.astype(F32), 2))
    gain_arr = jnp.stack(gain_rows)
    in_specs = []
    for t in range(nt):
        w = n_heads[t] * HD
        in_specs.append(pl.BlockSpec((1, ts, w), functools.partial(lambda b, s, cb: (b, s, cb), cb=col_blocks[t])))
    in_specs += [pl.BlockSpec((nt, LANES), lambda b, s: (0, 0)),
                 pl.BlockSpec((ts, LANES), lambda b, s: (s, 0)),
                 pl.BlockSpec((ts, LANES), lambda b, s: (s, 0)),
                 pl.BlockSpec((LANES, LANES), lambda b, s: (0, 0))]
    assert S // SEL_LEN <= LANES - HD
    out_specs, out_shape = [], []
    for t in range(nt):
        if kinds[t] == 'v':
            out_specs.append(pl.BlockSpec((1, n_heads[t], LANES, ts), lambda b, s: (b, 0, 0, s)))
            out_shape.append(jax.ShapeDtypeStruct((B, n_heads[t], LANES, S), BF16))
        else:
            width = HD if kinds[t] == 'qk' else LANES
            out_specs.append(pl.BlockSpec((1, n_heads[t], ts, width), lambda b, s: (b, 0, s, 0)))
            out_shape.append(jax.ShapeDtypeStruct((B, n_heads[t], S, width), BF16))
    return pl.pallas_call(
        functools.partial(_prep_kernel, tuple(n_heads), tuple(kinds), tuple(scales)),
        grid=(B, S // ts),
        in_specs=in_specs,
        out_specs=out_specs,
        out_shape=out_shape,
        compiler_params=_cparams(("parallel", "parallel")),
        name="prep_heads",
    )(*([proj] * nt), gain_arr, cos, sin, _block_ones(LANES))


_N_LEVELS = HG_CHUNK.bit_length() - 1


def _hgrn_constants():
    C = HG_CHUNK
    r = np.arange(C)
    mats = [(r[None, :] <= r[:, None]).astype(np.float32),
            (r[None, :] > r[:, None]).astype(np.float32)]
    masks = []
    for lv in range(_N_LEVELS):
        n = 1 << lv
        blk = r // (2 * n)
        right = (r // n) % 2 == 1
        bnd = blk * 2 * n + n - 1
        m = np.zeros((C, C), np.float32)
        for t in range(C):
            if right[t]:
                m[t, bnd[t] + 1:t + 1] = 1.0
            else:
                m[t, t + 1:bnd[t] + 1] = 1.0
        mats.append(m)
        masks.append((blk[:, None] == blk[None, :]).astype(np.float32))
    masks.append(np.eye(C, dtype=np.float32))
    return jnp.asarray(np.concatenate(mats, axis=0), dtype=BF16), jnp.asarray(np.stack(masks))


def _hgrn_kernel(q_ref, f_ref, i_ref, g_ref, lb_ref, gain_ref, mall_ref, masks_ref, o_ref, st_ref):
    C = HG_CHUNK
    ts = q_ref.shape[1]

    @pl.when(pl.program_id(1) == 0)
    def _():
        st_ref[...] = jnp.zeros_like(st_ref)

    mall = mall_ref[...]
    row = lax.broadcasted_iota(I32, (C, HG_DIM), 0)
    gain = gain_ref[...]

    def chunk(c, carry):
        r0 = pl.multiple_of(c * C, C)
        for h in range(HG_HEADS):
            ls = slice(h * HG_DIM, (h + 1) * HG_DIM)
            q = q_ref[0, pl.ds(r0, C), ls]
            z = f_ref[0, pl.ds(r0, C), ls]
            v = i_ref[0, pl.ds(r0, C), ls]
            g = g_ref[0, pl.ds(r0, C), ls]
            lb = lb_ref[:, ls]
            lf = jnp.log(lb + (1.0 - lb) * jax.nn.sigmoid(z))
            k = (1.0 - lb) * jax.nn.sigmoid(-z)
            lf_hi = lf.astype(BF16)
            lf_lo = (lf - lf_hi.astype(F32)).astype(BF16)
            seg = _dot(mall, jnp.concatenate([lf_hi, lf_lo], axis=1))
            ex = jnp.exp(seg[:, :HG_DIM] + seg[:, HG_DIM:])
            qs = q * ex[0:C]
            ks = k * ex[C:2 * C]
            st = st_ref[h]
            o = _dot_nt(qs.astype(BF16), st.astype(BF16))
            a = jnp.where(masks_ref[_N_LEVELS] > 0.5, _dot_nt(q.astype(BF16), k.astype(BF16)), 0.0)
            for lv in range(_N_LEVELS):
                e = ex[(2 + lv) * C:(3 + lv) * C]
                right = ((row >> lv) & 1) == 1
                qe = jnp.where(right, q * e, 0.0).astype(BF16)
                ke = jnp.where(right, 0.0, k * e).astype(BF16)
                a = a + jnp.where(masks_ref[lv] > 0.5, _dot_nt(qe, ke), 0.0)
            vb = v.astype(BF16)
            o = o + _dot(a.astype(BF16), vb)
            decay = ex[C - 1:C]
            st_ref[h] = st * decay + _dot(v.T.astype(BF16), ks.astype(BF16))
            ms = jnp.mean(o * o, axis=-1, keepdims=True)
            y = o * lax.rsqrt(ms + EPS) * gain * (g * jax.nn.sigmoid(g))
            o_ref[0, pl.ds(r0, C), ls] = y.astype(BF16)
        return carry

    lax.fori_loop(0, ts // C, chunk, 0)


def hgrn2(proj, lb, out_gain, ts=512):
    B, S, _ = proj.shape
    mall, masks = _hgrn_constants()
    W = HG_WIDTH
    return pl.pallas_call(
        _hgrn_kernel,
        grid=(B, S // ts),
        in_specs=[pl.BlockSpec((1, ts, W), lambda b, s: (b, s, 0)),
                  pl.BlockSpec((1, ts, W), lambda b, s: (b, s, 1)),
                  pl.BlockSpec((1, ts, W), lambda b, s: (b, s, 2)),
                  pl.BlockSpec((1, ts, W), lambda b, s: (b, s, 3)),
                  pl.BlockSpec((1, W), lambda b, s: (0, 0)),
                  pl.BlockSpec((1, HG_DIM), lambda b, s: (0, 0)),
                  pl.BlockSpec(mall.shape, lambda b, s: (0, 0)),
                  pl.BlockSpec(masks.shape, lambda b, s: (0, 0, 0))],
        out_specs=pl.BlockSpec((1, ts, W), lambda b, s: (b, s, 0)),
        out_shape=jax.ShapeDtypeStruct((B, S, W), BF16),
        scratch_shapes=[pltpu.VMEM((HG_HEADS, HG_DIM, HG_DIM), F32)],
        compiler_params=_cparams(("parallel", "arbitrary")),
        name="hgrn2",
    )(proj, proj, proj, proj, lb.reshape(1, W), out_gain.reshape(1, HG_DIM), mall, masks)


def _compress_kernel(xk_ref, xv_ref, w1k_ref, w1v_ref, pek_ref, pev_ref, w2k_ref, w2v_ref,
                     gain_ref, cos_ref, sin_ref, bones_ref, ok_ref, ov_ref):
    nc = xk_ref.shape[1] // CMP_STRIDE

    def hidden(x_ref, w1_ref, pe_ref):
        width = NSA_KV * CMP_HIDDEN
        u0 = jnp.zeros((nc, width), F32)
        u1 = jnp.zeros((nc, width), F32)
        const = jnp.zeros((8, width), F32)
        for p in range(CMP_STRIDE):
            xp = x_ref[0, pl.ds(p, nc, stride=CMP_STRIDE), :].astype(BF16)
            u0 = u0 + _dot(xp, w1_ref[p])
            u1 = u1 + _dot(xp, w1_ref[CMP_STRIDE + p])
            const = const + _dot(pe_ref[p].astype(BF16), w1_ref[p])
            const = const + _dot(pe_ref[CMP_STRIDE + p].astype(BF16), w1_ref[CMP_STRIDE + p])
        h = u0 + pltpu.roll(u1, nc - 1, axis=0) + const[0:1]
        return jax.nn.gelu(h).astype(BF16)

    yk = _dot(hidden(xk_ref, w1k_ref, pek_ref), w2k_ref[...])
    yk = _head_norm_rope(yk, gain_ref[...], cos_ref[...], sin_ref[...], bones_ref[...], 1.0)
    lane = lax.broadcasted_iota(I32, yk.shape, 1)
    ok_ref[0] = jnp.where(lane < HD, yk, 0.0).astype(BF16)
    ok_ref[1] = jnp.where(lane < HD, pltpu.roll(yk, HD, axis=1), 0.0).astype(BF16)
    yv_t = _dot_nt(w2v_ref[...], hidden(xv_ref, w1v_ref, pev_ref)).astype(BF16)
    ov_ref[0] = yv_t[:HD]
    ov_ref[1] = yv_t[HD:]


def _group_diag(w):
    z = jnp.zeros_like(w)
    return jnp.concatenate([jnp.concatenate([w, z], axis=-1), jnp.concatenate([z, w], axis=-1)], axis=-2)


def nsa_compress(proj, kc_blk, vc_blk, pe, w1, w2, k_gain):
    B, S, _ = proj.shape
    nc = S // CMP_STRIDE
    pos_end = jnp.arange(nc) * CMP_STRIDE + CMP_LEN - 1
    cos, sin = _rope_tables(pos_end, LANES)
    w1d = _group_diag(w1.reshape(2, CMP_LEN, HD, CMP_HIDDEN)).astype(BF16)
    w2d = _group_diag(w2).astype(BF16)
    pe2 = jnp.broadcast_to(jnp.tile(pe, (1, 1, NSA_KV))[:, :, None, :], (2, CMP_LEN, 8, LANES))
    full = lambda shape: pl.BlockSpec(shape, lambda b: (0,) * len(shape))
    w1_shape = (CMP_LEN, LANES, NSA_KV * CMP_HIDDEN)
    return pl.pallas_call(
        _compress_kernel,
        grid=(B,),
        in_specs=[pl.BlockSpec((1, S, LANES), lambda b: (b, 0, kc_blk)),
                  pl.BlockSpec((1, S, LANES), lambda b: (b, 0, vc_blk)),
                  full(w1_shape), full(w1_shape),
                  full((CMP_LEN, 8, LANES)), full((CMP_LEN, 8, LANES)),
                  full((NSA_KV * CMP_HIDDEN, LANES)), full((LANES, NSA_KV * CMP_HIDDEN)),
                  full((1, LANES)), full((nc, LANES)), full((nc, LANES)), full((LANES, LANES))],
        out_specs=[pl.BlockSpec((NSA_KV, nc, LANES), lambda b: (b, 0, 0)),
                   pl.BlockSpec((NSA_KV, HD, nc), lambda b: (b, 0, 0))],
        out_shape=[jax.ShapeDtypeStruct((B * NSA_KV, nc, LANES), BF16),
                   jax.ShapeDtypeStruct((B * NSA_KV, HD, nc), BF16)],
        compiler_params=_cparams(("parallel",)),
        name="nsa_compress",
    )(proj, proj, w1d[0], w1d[1], pe2[0], pe2[1], w2d[0], w2d[1].T,
      jnp.tile(k_gain.astype(F32), 2).reshape(1, LANES), cos, sin, _block_ones(LANES))


NSA_TQ = 256
NSA_TK = 1024


def _nsa_kernel(q_ref, kc_ref, vc_ref, ks_ref, vs_ref, kw_ref, vw_ref, gate_ref, cover_ref, place_ref,
                o_ref, acc_ref):
    TQ, TK, H = NSA_TQ, NSA_TK, NSA_HPG
    R = H * TQ
    t0 = pl.program_id(1) * TQ
    tl = t0 + lax.broadcasted_iota(I32, (1, TQ), 1)
    ncmp = kc_ref.shape[1]
    nsel = cover_ref.shape[0]

    def all_heads(x):
        return jnp.concatenate([x] * H, axis=1)

    def select(g):
        q = q_ref[0, g * H:(g + 1) * H].reshape(R, LANES)
        cmp_end = lax.broadcasted_iota(I32, (ncmp, 1), 0) * CMP_STRIDE + (CMP_LEN - 1)
        valid = all_heads(jnp.where(cmp_end <= tl, 1.0, 0.0))
        s = jnp.where(valid > 0.5, _dot_nt(kc_ref[g], q), NEG)
        p = jnp.exp(s - jnp.max(s, axis=0, keepdims=True)) * valid
        p_c = p / jnp.maximum(jnp.sum(p, axis=0, keepdims=True), 1e-30)
        o_c = _dot(vc_ref[g], p_c.astype(BF16))
        p_tok = p_c[:, 0:TQ]
        for h in range(1, H):
            p_tok = p_tok + p_c[:, h * TQ:(h + 1) * TQ]
        p_hi = p_tok.astype(BF16)
        p_lo = (p_tok - p_hi.astype(F32)).astype(BF16)
        imp = _dot(cover_ref[...], p_hi) + _dot(cover_ref[...], p_lo)
        jj = lax.broadcasted_iota(I32, (nsel, TQ), 0)
        jf = jj.astype(F32)
        cur = tl >> 6
        allowed = jj * SEL_LEN <= tl
        forced = (jj == 0) | (jj == cur) | (jj == cur - 1)
        score = jnp.where(allowed, imp + jnp.where(forced, FORCE_BONUS, 0.0), NEG)
        sel = jnp.zeros((nsel, TQ), F32)
        for _ in range(min(SEL_TOPN, nsel)):
            mx = jnp.max(score, axis=0, keepdims=True)
            first = jnp.min(jnp.where(score == mx, jf, float(nsel)), axis=0, keepdims=True)
            hit = jf == first
            sel = jnp.where(hit & (mx > 0.5 * NEG), 1.0, sel)
            score = jnp.where(hit, NEG, score)
        block_bias = jnp.where(sel > 0.5, 0.0, NEG).T.astype(BF16)
        placed = _dot(block_bias, place_ref[...]).astype(BF16)
        return o_c, q, q + jnp.concatenate([placed] * H, axis=0)

    def sweep_tile(g, q_sel, kt, m_old, causal):
        k0 = pl.multiple_of(kt * TK, TK)
        s = _dot_nt(ks_ref[0, g, pl.ds(k0, TK), :], q_sel)
        if causal:
            kpos = k0 + lax.broadcasted_iota(I32, (TK, 1), 0)
            s = s + all_heads(jnp.where(kpos <= tl, 0.0, NEG))
        m_new = jnp.maximum(m_old, jnp.max(s, axis=0, keepdims=True))
        p = jnp.exp(s - m_new).astype(BF16)
        acc_ref[g] = jnp.exp(m_old - m_new) * acc_ref[g] + _dot(vs_ref[0, g, :, pl.ds(k0, TK)], p)
        return m_new

    def window(g, q):
        WK = NSA_WINDOW + TQ
        ws = pl.multiple_of(jnp.maximum(t0 - NSA_WINDOW, 0), TQ)
        kp = ws + lax.broadcasted_iota(I32, (WK, 1), 0)
        wbias = jnp.where((kp <= tl) & (kp > tl - NSA_WINDOW), 0.0, NEG)
        s = _dot_nt(kw_ref[0, g, pl.ds(ws, WK), :], q) + all_heads(wbias)
        p = jnp.exp(s - jnp.max(s, axis=0, keepdims=True)).astype(BF16)
        acc_w = _dot(vw_ref[0, g, :, pl.ds(ws, WK)], p)
        return acc_w[:HD] / acc_w[HD:HD + 1]

    groups = range(NSA_KV)
    chosen = [select(g) for g in groups]
    acc_ref[...] = jnp.zeros(acc_ref.shape, F32)
    last = t0 // TK
    m_run = lax.fori_loop(
        0, last, lambda kt, ms: tuple(sweep_tile(g, chosen[g][2], kt, ms[g], False) for g in groups),
        tuple(jnp.full((1, R), NEG, F32) for _ in groups))
    heads_out = []
    for g in groups:
        o_c, q, q_sel = chosen[g]
        sweep_tile(g, q_sel, last, m_run[g], True)
        acc = acc_ref[g]
        o_s = acc[:HD] / jnp.maximum(acc[HD:HD + 1], 1e-30)
        o_w = window(g, q)
        sg = jax.nn.sigmoid(gate_ref[0, :, g * LANES:(g + 1) * LANES]).T
        for h in range(H):
            cols = slice(h * TQ, (h + 1) * TQ)
            heads_out.append(sg[3 * h:3 * h + 1] * o_c[:, cols] + sg[3 * h + 1:3 * h + 2] * o_s[:, cols]
                             + sg[3 * h + 2:3 * h + 3] * o_w[:, cols])
    o_ref[0] = jnp.concatenate(heads_out, axis=0).T.astype(BF16)


def nsa_attention(qn, kcb, vcb, ksn, vsb, kwn, vwb, proj):
    B, _, S, _ = qn.shape
    G, H, TQ = NSA_KV, NSA_HPG, NSA_TQ
    assert S % NSA_TK == 0 and S >= NSA_WINDOW + TQ
    ncmp = kcb.shape[1]
    nsel = S // SEL_LEN
    n = np.arange(ncmp)[:, None] * CMP_STRIDE
    j = np.arange(nsel)[None, :] * SEL_LEN
    cover = jnp.asarray(((n < j + SEL_LEN) & (n + CMP_LEN > j)).astype(np.float32).T, dtype=BF16)
    place = np.zeros((nsel, LANES), np.float32)
    place[np.arange(nsel), HD + np.arange(nsel)] = 1.0
    R = H * TQ
    k_spec = pl.BlockSpec((1, G, S, LANES), lambda b, i: (b, 0, 0, 0))
    v_spec = pl.BlockSpec((1, G, LANES, S), lambda b, i: (b, 0, 0, 0))
    assert C_GATE % (G * LANES) == 0
    return pl.pallas_call(
        _nsa_kernel,
        grid=(B, S // TQ),
        in_specs=[pl.BlockSpec((1, G * H, TQ, LANES), lambda b, i: (b, 0, i, 0)),
                  pl.BlockSpec((G, ncmp, LANES), lambda b, i: (b, 0, 0)),
                  pl.BlockSpec((G, HD, ncmp), lambda b, i: (b, 0, 0)),
                  k_spec, v_spec, k_spec, v_spec,
                  pl.BlockSpec((1, TQ, G * LANES), lambda b, i: (b, i, C_GATE // (G * LANES))),
                  pl.BlockSpec((nsel, ncmp), lambda b, i: (0, 0)),
                  pl.BlockSpec((nsel, LANES), lambda b, i: (0, 0))],
        out_specs=pl.BlockSpec((1, TQ, G * H * HD), lambda b, i: (b, i, 0)),
        out_shape=jax.ShapeDtypeStruct((B, S, NSA_HEADS * HD), BF16),
        scratch_shapes=[pltpu.VMEM((G, LANES, R), F32)],
        compiler_params=_cparams(("parallel", "arbitrary")),
        name="nsa_attention",
    )(qn, kcb, vcb, ksn, vsb, kwn, vwb, proj, cover, jnp.asarray(place, dtype=BF16))


SWA_TQ = 256


def _swa_kernel(sink_ref, q_ref, k_ref, v_ref, o_ref):
    TQ, H = SWA_TQ, SWA_HPG
    R = H * TQ
    g = pl.program_id(1)
    t0 = pl.program_id(2) * TQ
    q = q_ref[0].reshape(R, HD)
    tl = t0 + lax.broadcasted_iota(I32, (1, TQ), 1)
    WK = SWA_WINDOW + TQ
    ws = pl.multiple_of(jnp.maximum(t0 - SWA_WINDOW, 0), LANES)
    kp = ws + lax.broadcasted_iota(I32, (WK, 1), 0)
    bias = jnp.where((kp <= tl) & (kp > tl - SWA_WINDOW), 0.0, NEG)
    s = _dot_nt(k_ref[0, 0, pl.ds(ws, WK), :], q) + jnp.concatenate([bias] * H, axis=1)
    head = lax.broadcasted_iota(I32, (1, R), 1) // TQ
    sink = jnp.zeros((1, R), F32)
    for h in range(H):
        sink = jnp.where(head == h, sink_ref[g * H + h], sink)
    m = jnp.maximum(jnp.max(s, axis=0, keepdims=True), sink)
    p = jnp.exp(s - m).astype(BF16)
    acc = _dot(v_ref[0, 0, :, pl.ds(ws, WK)], p)
    o = acc[:HD] / (acc[HD:HD + 1] + jnp.exp(sink - m))
    o_ref[0] = jnp.concatenate([o[:, h * TQ:(h + 1) * TQ] for h in range(H)], axis=0).T.astype(BF16)


def swa_attention(qn, kn, vb, sinks):
    B, _, S, _ = qn.shape
    G, H, TQ = SWA_KV, SWA_HPG, SWA_TQ
    assert S >= SWA_WINDOW + TQ
    k_spec = pl.BlockSpec((1, 1, S, HD), lambda b, g, i, sk: (b, g, 0, 0))
    v_spec = pl.BlockSpec((1, 1, LANES, S), lambda b, g, i, sk: (b, g, 0, 0))
    return pl.pallas_call(
        _swa_kernel,
        grid_spec=pltpu.PrefetchScalarGridSpec(
            num_scalar_prefetch=1,
            grid=(B, G, S // TQ),
            in_specs=[pl.BlockSpec((1, H, TQ, HD), lambda b, g, i, sk: (b, g, i, 0)), k_spec, v_spec],
            out_specs=pl.BlockSpec((1, TQ, H * HD), lambda b, g, i, sk: (b, i, g))),
        out_shape=jax.ShapeDtypeStruct((B, S, SWA_HEADS * HD), BF16),
        compiler_params=_cparams(("parallel", "parallel", "arbitrary")),
        name="swa_attention",
    )(sinks.astype(F32), qn, kn, vb)


R_GRP = 0
R_EXP = 32


SUBLANES = 8


def _store_row_tiles(ref, val, base=0):
    for s in range(SUBLANES):
        ref[pl.ds(base * SUBLANES + s, val.shape[0], stride=SUBLANES), :] = val[:, s * LANES:(s + 1) * LANES]


def _load_row_tiles(ref, rows, base=0):
    return [ref[pl.ds(base * SUBLANES + s, rows, stride=SUBLANES), :] for s in range(SUBLANES)]


def _router_kernel(x_ref, g_ref, w_ref, b_ref, tri_ref, hn_ref, route_ref, cnt_ref, base_ref):
    i = pl.program_id(0)
    last = pl.num_programs(0) - 1

    @pl.when(i == 0)
    def _():
        base_ref[...] = jnp.zeros_like(base_ref)

    @pl.when(i < last)
    def _():
        _route_tile(x_ref, g_ref, w_ref, b_ref, tri_ref, hn_ref, route_ref, base_ref)

    @pl.when(i == last)
    def _():
        hn_ref[...] = jnp.zeros_like(hn_ref)
        route_ref[...] = jnp.zeros_like(route_ref)

    cnt_ref[...] = base_ref[...]


def _route_tile(x_ref, g_ref, w_ref, b_ref, tri_ref, hn_ref, route_ref, base_ref):
    x = x_ref[...]
    ms = jnp.mean(x * x, axis=-1, keepdims=True)
    h = x * lax.rsqrt(ms + EPS) * g_ref[...]
    _store_row_tiles(hn_ref, h)
    h_hi = h.astype(BF16)
    h_lo = (h - h_hi.astype(F32)).astype(BF16)
    logits = _dot(h_hi, w_ref[0]) + _dot(h_lo, w_ref[0]) + _dot(h_hi, w_ref[1]) + b_ref[...]
    lane = lax.broadcasted_iota(I32, logits.shape, 1)
    is_grp = lane < N_GROUPS
    gl = jnp.where(is_grp, logits, NEG)
    gmax = jnp.max(gl, axis=-1, keepdims=True)
    lanef = lane.astype(F32)
    gidx = jnp.min(jnp.where(gl == gmax, lanef, float(LANES)), axis=-1, keepdims=True)
    gw = 1.0 / jnp.sum(jnp.where(is_grp, jnp.exp(logits - gmax), 0.0), axis=-1, keepdims=True)
    in_grp = ((lane >> 3) - R_EXP // EXP_PER_GROUP).astype(F32) == gidx
    w0 = jnp.where(in_grp, logits, NEG)
    m1 = jnp.max(w0, axis=-1, keepdims=True)
    i1 = jnp.min(jnp.where(w0 == m1, lanef, float(LANES)), axis=-1, keepdims=True)
    w1 = jnp.where(lanef == i1, NEG, w0)
    m2 = jnp.max(w1, axis=-1, keepdims=True)
    i2 = jnp.min(jnp.where(w1 == m2, lanef, float(LANES)), axis=-1, keepdims=True)
    e2 = jnp.exp(m2 - m1)
    p1 = gw / (1.0 + e2)
    p2 = gw * e2 / (1.0 + e2)
    oh1 = jnp.where(lanef == i1, 1.0, 0.0)
    oh2 = jnp.where(lanef == i2, 1.0, 0.0)
    oh = oh1 + oh2
    before = _dot(tri_ref[...], oh.astype(BF16)) + base_ref[0:1, :]
    r1 = jnp.sum(before * oh1, axis=-1, keepdims=True)
    r2 = jnp.sum(before * oh2, axis=-1, keepdims=True)
    base_ref[...] = base_ref[...] + jnp.sum(oh, axis=0, keepdims=True)
    route = jnp.zeros_like(logits)
    for ln, val in enumerate((i1 - R_EXP, i2 - R_EXP, p1, p2, r1, r2)):
        route = jnp.where(lane == ln, val, route)
    route_ref[...] = route


def moe_router(x, gain, w_grp, b_grp, w_exp, b_exp, tm=256):
    T, D = x.shape
    nt = T // tm
    w = jnp.zeros((D, LANES), F32).at[:, R_GRP:R_GRP + N_GROUPS].set(w_grp).at[:, R_EXP:R_EXP + N_EXPERTS].set(w_exp)
    b = jnp.zeros((1, LANES), F32).at[0, R_GRP:R_GRP + N_GROUPS].set(b_grp).at[0, R_EXP:R_EXP + N_EXPERTS].set(b_exp)
    w_hi = w.astype(BF16)
    w = jnp.stack([w_hi, (w - w_hi.astype(F32)).astype(BF16)])
    tri = jnp.asarray(np.tril(np.ones((tm, tm), np.float32), -1), dtype=BF16)
    return pl.pallas_call(
        _router_kernel,
        grid=(nt + 1,),
        in_specs=[pl.BlockSpec((tm, D), lambda i: (jnp.minimum(i, nt - 1), 0)),
                  pl.BlockSpec((1, D), lambda i: (0, 0)),
                  pl.BlockSpec((2, D, LANES), lambda i: (0, 0, 0)),
                  pl.BlockSpec((1, LANES), lambda i: (0, 0)),
                  pl.BlockSpec((tm, tm), lambda i: (0, 0))],
        out_specs=[pl.BlockSpec((tm * SUBLANES, LANES), lambda i: (i, 0)),
                   pl.BlockSpec((tm, LANES), lambda i: (i, 0)),
                   pl.BlockSpec((8, LANES), lambda i: (0, 0))],
        out_shape=[jax.ShapeDtypeStruct(((T + tm) * SUBLANES, LANES), F32),
                   jax.ShapeDtypeStruct((T + tm, LANES), F32),
                   jax.ShapeDtypeStruct((8, LANES), F32)],
        scratch_shapes=[pltpu.VMEM((8, LANES), F32)],
        compiler_params=_cparams(("arbitrary",)),
        name="moe_router",
    )(x, gain.reshape(1, D), w, b, tri)


def _row_copy(src, src_row, dst, dst_row, sem):
    def tile(ref, row):
        return ref.at[pl.ds(pl.multiple_of(row * SUBLANES, SUBLANES), SUBLANES)]

    return pltpu.make_async_copy(tile(src, src_row), tile(dst, dst_row), sem)


DISPATCH_TILE = 512
ROW_UNROLL = 4


def _wait_rows(hbm_ref, sem, n):
    view = hbm_ref.at[pl.ds(0, n * SUBLANES)]
    pltpu.make_async_copy(view, view, sem).wait()


def _dispatch_kernel(dest_ref, x_ref, xb_ref, stage, sems):
    i = pl.program_id(0)
    last = pl.num_programs(0) - 1
    slot = i % 2

    @pl.when(i >= 2)
    def _():
        _wait_rows(xb_ref, sems.at[slot], DISPATCH_TILE)

    stage[slot] = x_ref[...]

    def issue(t, c):
        _row_copy(stage.at[slot], t, xb_ref, dest_ref[0, 0, 2 * t], sems.at[slot]).start(priority=0)
        _row_copy(stage.at[slot], t, xb_ref, dest_ref[0, 0, 2 * t + 1], sems.at[slot]).start(priority=1)
        return c

    lax.fori_loop(0, DISPATCH_TILE // 2, issue, 0, unroll=ROW_UNROLL)

    @pl.when(i == last)
    def _():
        _wait_rows(xb_ref, sems.at[slot], DISPATCH_TILE)

    @pl.when((i == last) & (i >= 1))
    def _():
        _wait_rows(xb_ref, sems.at[1 - slot], DISPATCH_TILE)


def moe_dispatch(hn, dest_all, n_tok):
    n_rows = dest_all.shape[0]
    tm = DISPATCH_TILE // 2
    nt = n_rows // DISPATCH_TILE
    n_tok_tiles = n_tok // tm
    return pl.pallas_call(
        _dispatch_kernel,
        grid=(nt,),
        in_specs=[pl.BlockSpec((1, 1, DISPATCH_TILE), lambda i: (i, 0, 0), memory_space=pltpu.SMEM),
                  pl.BlockSpec((tm * SUBLANES, LANES), lambda i: (jnp.minimum(i, n_tok_tiles), 0))],
        out_specs=pl.BlockSpec(memory_space=pl.ANY),
        out_shape=jax.ShapeDtypeStruct((n_rows * SUBLANES, LANES), F32),
        scratch_shapes=[pltpu.VMEM((2, tm * SUBLANES, LANES), F32), pltpu.SemaphoreType.DMA((2,))],
        compiler_params=_cparams(("arbitrary",)),
        name="moe_dispatch",
    )(dest_all.reshape(nt, 1, DISPATCH_TILE), hn)


def _expert_kernel(be_ref, x_ref, wg_ref, wu_ref, wd_ref, y_ref, wg_b, wu_b, wd_b):
    i = pl.program_id(0)

    @pl.when((i == 0) | (be_ref[i] != be_ref[jnp.maximum(i - 1, 0)]))
    def _():
        wg_b[...] = wg_ref[0].astype(BF16)
        wu_b[...] = wu_ref[0].astype(BF16)
        wd_b[...] = wd_ref[0].astype(BF16)

    x = jnp.concatenate(_load_row_tiles(x_ref, MOE_BLOCK), axis=1).astype(BF16)
    hid = _dot(x, wg_b[...])
    hid = hid * jax.nn.sigmoid(hid) * _dot(x, wu_b[...])
    _store_row_tiles(y_ref, _dot(hid.astype(BF16), wd_b[...]))


def moe_experts(xb, blk_expert, w_gate, w_up, w_down):
    n_rows = xb.shape[0] // SUBLANES
    D = w_gate.shape[1]
    blk = (MOE_BLOCK * SUBLANES, LANES)
    wspec = lambda shape: pl.BlockSpec((1,) + shape, lambda i, be: (be[i], 0, 0))
    return pl.pallas_call(
        _expert_kernel,
        grid_spec=pltpu.PrefetchScalarGridSpec(
            num_scalar_prefetch=1,
            grid=(n_rows // MOE_BLOCK,),
            in_specs=[pl.BlockSpec(blk, lambda i, be: (i, 0)),
                      wspec((D, EXPERT_FF)), wspec((D, EXPERT_FF)), wspec((EXPERT_FF, D))],
            out_specs=pl.BlockSpec(blk, lambda i, be: (i, 0)),
            scratch_shapes=[pltpu.VMEM((D, EXPERT_FF), BF16), pltpu.VMEM((D, EXPERT_FF), BF16),
                            pltpu.VMEM((EXPERT_FF, D), BF16)]),
        out_shape=jax.ShapeDtypeStruct((n_rows * SUBLANES, LANES), F32),
        compiler_params=_cparams(("arbitrary",)),
        name="moe_experts",
    )(blk_expert, xb, w_gate, w_up, w_down)


def _combine_kernel(dest_ref, dest_next_ref, x_ref, route_ref, yb_ref, o_ref, ybuf, sems):
    tm = x_ref.shape[0]
    i = pl.program_id(0)
    slot = i % 2

    def gather(d_ref, s):
        def body(t, c):
            _row_copy(yb_ref, d_ref[0, 0, 2 * t], ybuf.at[s], t, sems.at[s]).start(priority=0)
            _row_copy(yb_ref, d_ref[0, 0, 2 * t + 1], ybuf.at[s], tm + t, sems.at[s]).start(priority=1)
            return c

        lax.fori_loop(0, tm, body, 0, unroll=ROW_UNROLL)

    @pl.when(i == 0)
    def _():
        gather(dest_ref, 0)

    @pl.when(i + 1 < pl.num_programs(0))
    def _():
        gather(dest_next_ref, 1 - slot)

    _wait_rows(yb_ref, sems.at[slot], 2 * tm)
    r = route_ref[...]
    w0, w1 = r[:, 2:3], r[:, 3:4]
    y0 = _load_row_tiles(ybuf.at[slot], tm)
    y1 = _load_row_tiles(ybuf.at[slot], tm, base=tm)
    for s in range(SUBLANES):
        cols = slice(s * LANES, (s + 1) * LANES)
        o_ref[:, cols] = x_ref[:, cols] + w0 * y0[s] + w1 * y1[s]


def moe_combine(x, yb, dest, route):
    T, D = x.shape
    tm = DISPATCH_TILE // 2
    nt = T // tm
    dest3 = dest.reshape(nt, 1, DISPATCH_TILE)
    return pl.pallas_call(
        _combine_kernel,
        grid=(nt,),
        in_specs=[pl.BlockSpec((1, 1, DISPATCH_TILE), lambda i: (i, 0, 0), memory_space=pltpu.SMEM),
                  pl.BlockSpec((1, 1, DISPATCH_TILE), lambda i: (jnp.minimum(i + 1, nt - 1), 0, 0),
                               memory_space=pltpu.SMEM),
                  pl.BlockSpec((tm, D), lambda i: (i, 0)),
                  pl.BlockSpec((tm, LANES), lambda i: (i, 0)),
                  pl.BlockSpec(memory_space=pl.ANY)],
        out_specs=pl.BlockSpec((tm, D), lambda i: (i, 0)),
        out_shape=jax.ShapeDtypeStruct((T, D), F32),
        scratch_shapes=[pltpu.VMEM((2, DISPATCH_TILE * SUBLANES, LANES), F32), pltpu.SemaphoreType.DMA((2,))],
        compiler_params=_cparams(("arbitrary",)),
        name="moe_combine",
    )(dest3, dest3, x, route, yb)


def hier_moe_block(x, gain, w_grp, b_grp, w_exp, b_exp, w_gate, w_up, w_down):
    T, D = x.shape
    hn, route, cnt = moe_router(x, gain, w_grp, b_grp, w_exp, b_exp)
    A = 2 * T
    n_rows = -(-A // MOE_BLOCK) * MOE_BLOCK + N_EXPERTS * MOE_BLOCK
    n_blocks = n_rows // MOE_BLOCK
    counts = cnt[0, R_EXP:R_EXP + N_EXPERTS].astype(I32)
    pcounts = (counts + MOE_BLOCK - 1) // MOE_BLOCK * MOE_BLOCK
    pend = jnp.cumsum(pcounts)
    pstart = pend - pcounts
    experts = jnp.arange(N_EXPERTS, dtype=I32)
    flat_e = route[:T, 0:2].astype(I32).reshape(A)
    rank = route[:T, 4:6].astype(I32).reshape(A)
    dest = rank + jnp.sum(jnp.where(flat_e[:, None] == experts[None, :], pstart[None, :], 0), axis=1)
    n_pad = n_rows - A
    pad_cnt = jnp.concatenate([pcounts - counts, (n_rows - pend[-1])[None]])
    pad_start = jnp.concatenate([pstart + counts, pend[-1:]])
    pad_end = jnp.cumsum(pad_cnt)
    k = jnp.arange(n_pad, dtype=I32)
    seg = (k[:, None] >= pad_end[None, :]).astype(I32).sum(axis=1)
    seg_hot = seg[:, None] == jnp.arange(N_EXPERTS + 1, dtype=I32)[None, :]
    pad_dest = k + jnp.sum(jnp.where(seg_hot, (pad_start - (pad_end - pad_cnt))[None, :], 0), axis=1)
    dest_all = jnp.concatenate([dest, pad_dest]).astype(I32)
    blk_start = jnp.arange(n_blocks, dtype=I32) * MOE_BLOCK
    blk_expert = jnp.minimum((pend[None, :] <= blk_start[:, None]).astype(I32).sum(axis=1), N_EXPERTS - 1)
    xb = moe_dispatch(hn, dest_all, T)
    yb = moe_experts(xb, blk_expert, w_gate, w_up, w_down)
    return moe_combine(x, yb, dest, route)


def _even_w_in(w_in):
    ng = 3 * NSA_HPG
    pad = jnp.zeros((w_in.shape[0], LANES - ng), w_in.dtype)
    cols = [w_in[:, :C_GATE]]
    for g in range(NSA_KV):
        cols += [w_in[:, C_GATE + g * ng:C_GATE + (g + 1) * ng], pad]
    return jnp.concatenate(cols, axis=1).astype(BF16)


def even_layer(x, B, S, norm, w_in, lb, hg_gain, q_gain, k_gain, cmp_pe, cmp_w1, cmp_w2, w_out):
    T = B * S
    proj = rms_matmul(x, norm, _even_w_in(w_in), tn=N0 // 4).reshape(B, S, N0)
    a_out = hgrn2(proj, lb, hg_gain)
    kvb = C_KV // LANES
    qn, ksn, kwn, vsb, vwb = prep_heads(
        proj, [C_NQ // (NSA_HEADS * HD), kvb + 2, kvb + 4, kvb + 3, kvb + 5],
        [NSA_HEADS, NSA_KV, NSA_KV, NSA_KV, NSA_KV],
        [q_gain, k_gain[1], k_gain[2], None, None],
        ['pad', 'blk', 'pad', 'v', 'v'], [HD ** -0.5, 1.0, 1.0, 1.0, 1.0], S)
    kcb, vcb = nsa_compress(proj, kvb, kvb + 1, cmp_pe, cmp_w1, cmp_w2, k_gain[0])
    b_out = nsa_attention(qn, kcb, vcb, ksn, vsb, kwn, vwb, proj)
    return proj_res(a_out.reshape(T, HG_WIDTH), 0, b_out.reshape(T, NSA_HEADS * HD), 0, w_out.astype(BF16), x)


def odd_layer(x, B, S, norm, w_in, q_gain, k_gain, sinks, w_out):
    T = B * S
    proj = rms_matmul(x, norm, w_in.astype(BF16), tn=N1 // 2).reshape(B, S, N1)
    qd = SWA_HEADS * HD
    qn, kn, vb = prep_heads(
        proj, [0, qd // LANES, qd // LANES + 1], [SWA_HEADS, SWA_KV, SWA_KV],
        [q_gain, k_gain, None], ['qk', 'qk', 'v'], [HD ** -0.5, 1.0, 1.0], S)
    att = swa_attention(qn, kn, vb, sinks).reshape(T, qd)
    return proj_res(att, 0, att, 1, w_out.astype(BF16), x)


def kernel(x, ev_norm, ev_w_in, hg_lb_logits, hg_out_gain, nsa_q_gain, nsa_k_gain, nsa_cmp_pe, nsa_cmp_w1,
           nsa_cmp_w2, ev_w_out, od_norm, od_w_in, swa_q_gain, swa_k_gain, swa_sinks, od_w_out, moe_norm,
           moe_w_grp, moe_b_grp, moe_w_exp, moe_b_exp, moe_w_gate, moe_w_up, moe_w_down):
    B, S, D = x.shape
    lower_bounds = jnp.cumsum(jax.nn.softmax(hg_lb_logits.astype(F32), axis=0), axis=0)
    h = x.reshape(B * S, D)
    h = even_layer(h, B, S, ev_norm[0], ev_w_in[0], lower_bounds[0], hg_out_gain[0], nsa_q_gain[0],
                   nsa_k_gain[0], nsa_cmp_pe[0], nsa_cmp_w1[0], nsa_cmp_w2[0], ev_w_out[0])
    h = hier_moe_block(h, moe_norm[0], moe_w_grp[0], moe_b_grp[0], moe_w_exp[0], moe_b_exp[0],
                       moe_w_gate[0], moe_w_up[0], moe_w_down[0])
    h = odd_layer(h, B, S, od_norm[0], od_w_in[0], swa_q_gain[0], swa_k_gain[0], swa_sinks[0], od_w_out[0])
    h = hier_moe_block(h, moe_norm[1], moe_w_grp[1], moe_b_grp[1], moe_w_exp[1], moe_b_exp[1],
                       moe_w_gate[1], moe_w_up[1], moe_w_down[1])
    return h.reshape(B, S, D)
```

```python
import functools

import numpy as np
import jax
import jax.numpy as jnp
from jax import lax
from jax.experimental import pallas as pl
from jax.experimental.pallas import tpu as pltpu

F32 = jnp.float32
BF16 = jnp.bfloat16
I32 = jnp.int32
HI = lax.Precision.HIGHEST

D_MODEL = 1024
EPS = 1e-6
ROPE_THETA = 500000.0
HD = 64
ROT = HD // 4
HALF = ROT // 2

HG_DIM = 128
HG_WIDTH = 512
HG_HEADS = 4
HG_CHUNK = 128

NSA_HEADS = 8
NSA_KV = 2
NSA_HPG = NSA_HEADS // NSA_KV
CMP_LEN = 32
CMP_STRIDE = 16
CMP_HIDDEN = 256
SEL_LEN = 64
SEL_TOPN = 8
NSA_WINDOW = 512
FORCE_BONUS = 1e4

SWA_HEADS = 16
SWA_KV = 2
SWA_HPG = SWA_HEADS // SWA_KV
SWA_WINDOW = 128

N_GROUPS = 4
EXP_PER_GROUP = 8
N_EXPERTS = 32
EXPERT_FF = 512
MOE_BLOCK = 512

LANES = 128
NEG = -1e30
VMEM_LIMIT = 56 * 1024 * 1024

C_HG = 0
C_NQ = 2048
C_KV = 2560
C_GATE = 3328
N0 = 3584
N1 = 1280


def _cparams(sem):
    return pltpu.CompilerParams(dimension_semantics=sem, vmem_limit_bytes=VMEM_LIMIT)


def _dot(a, b):
    return jnp.dot(a, b, preferred_element_type=F32)


def _dot_nt(a, b):
    return lax.dot_general(a, b, (((1,), (1,)), ((), ())), preferred_element_type=F32)


def _rms_matmul_kernel(tn, x_ref, g_ref, w_ref, o_ref):
    x = x_ref[...]
    ms = jnp.mean(x * x, axis=-1, keepdims=True)
    xn = (x * lax.rsqrt(ms + EPS) * g_ref[...]).astype(BF16)
    for j in range(o_ref.shape[1] // tn):
        o_ref[:, j * tn:(j + 1) * tn] = _dot(xn, w_ref[:, j * tn:(j + 1) * tn])


def rms_matmul(x, gain, w, tm=512, tn=None):
    T, D = x.shape
    N = w.shape[1]
    tn = N if tn is None else tn
    return pl.pallas_call(
        functools.partial(_rms_matmul_kernel, tn),
        grid=(T // tm,),
        in_specs=[pl.BlockSpec((tm, D), lambda i: (i, 0)),
                  pl.BlockSpec((1, D), lambda i: (0, 0)),
                  pl.BlockSpec((D, N), lambda i: (0, 0))],
        out_specs=pl.BlockSpec((tm, N), lambda i: (i, 0)),
        out_shape=jax.ShapeDtypeStruct((T, N), F32),
        compiler_params=_cparams(("parallel",)),
        name="rms_matmul",
    )(x, gain.reshape(1, D), w)


def _proj_res_kernel(a_ref, b_ref, wa_ref, wb_ref, r_ref, o_ref):
    acc = _dot(a_ref[...], wa_ref[...]) + _dot(b_ref[...], wb_ref[...])
    o_ref[...] = r_ref[...] + acc


def proj_res(a, a_blk, b, b_blk, w, res, tm=512):
    T, D = res.shape
    K = w.shape[0]
    kh = K // 2
    return pl.pallas_call(
        _proj_res_kernel,
        grid=(T // tm,),
        in_specs=[pl.BlockSpec((tm, kh), lambda i: (i, a_blk)),
                  pl.BlockSpec((tm, kh), lambda i: (i, b_blk)),
                  pl.BlockSpec((kh, D), lambda i: (0, 0)),
                  pl.BlockSpec((kh, D), lambda i: (1, 0)),
                  pl.BlockSpec((tm, D), lambda i: (i, 0))],
        out_specs=pl.BlockSpec((tm, D), lambda i: (i, 0)),
        out_shape=jax.ShapeDtypeStruct((T, D), F32),
        compiler_params=_cparams(("parallel",)),
        name="proj_res",
    )(a, b, w, w, res)


def _rope_tables(pos, width):
    inv = jnp.power(ROPE_THETA, -jnp.arange(HALF, dtype=F32) * 2.0 / ROT)
    ang = pos.astype(F32)[:, None] * inv[None, :]
    cos, sin = jnp.cos(ang), jnp.sin(ang)
    n = pos.shape[0]
    c = jnp.concatenate([cos, cos, jnp.ones((n, HD - ROT), F32)], axis=1)
    s = jnp.concatenate([-sin, sin, jnp.zeros((n, HD - ROT), F32)], axis=1)
    reps = width // HD
    return jnp.tile(c, (1, reps)), jnp.tile(s, (1, reps))


def _block_ones(width):
    idx = np.arange(width) // HD
    return jnp.asarray((idx[:, None] == idx[None, :]).astype(np.float32), dtype=BF16)


def _split_dot(a, b):
    hi = a.astype(BF16)
    lo = (a - hi.astype(F32)).astype(BF16)
    return _dot(hi, b) + _dot(lo, b)


def _head_norm_rope(x, gain, cos, sin, bones, scale):
    ms = _split_dot(x * x, bones) * (1.0 / HD)
    y = x * lax.rsqrt(ms + EPS) * gain
    lane = lax.broadcasted_iota(I32, y.shape, 1) % HD
    partner = jnp.where(lane < HALF, pltpu.roll(y, LANES - HALF, axis=1), pltpu.roll(y, HALF, axis=1))
    out = y * cos + partner * sin
    return out * scale if scale != 1.0 else out


def _prep_kernel(n_heads, kinds, scales, *refs):
    nt = len(n_heads)
    x_refs = refs[:nt]
    gain_ref, cos_ref, sin_ref, bones_ref = refs[nt:nt + 4]
    o_refs = refs[nt + 4:]
    cos = cos_ref[...]
    sin = sin_ref[...]
    bones = bones_ref[...]
    ts = cos.shape[0]
    lane = lax.broadcasted_iota(I32, cos.shape, 1)
    pos = pl.program_id(1) * ts + lax.broadcasted_iota(I32, cos.shape, 0)
    for t in range(nt):
        for c in range(n_heads[t] // 2):
            x = x_refs[t][0, :, c * LANES:(c + 1) * LANES]
            if kinds[t] != 'v':
                x = _head_norm_rope(x, gain_ref[t:t + 1, :], cos, sin, bones, scales[t])
            for j, xh in enumerate((x, pltpu.roll(x, HD, axis=1))):
                if kinds[t] == 'qk':
                    y = xh[:, :HD]
                elif kinds[t] == 'pad':
                    y = jnp.where(lane < HD, xh, 0.0)
                elif kinds[t] == 'blk':
                    y = jnp.where(lane < HD, xh, jnp.where(lane - HD == (pos >> 6), 1.0, 0.0))
                else:
                    y = jnp.where(lane < HD, xh, jnp.where(lane == HD, 1.0, 0.0)).T
                o_refs[t][0, 2 * c + j] = y.astype(BF16)


def prep_heads(proj, col_blocks, n_heads, gains, kinds, scales, S, ts=512):
    B = proj.shape[0]
    nt = len(n_heads)
    cos, sin = _rope_tables(jnp.arange(S), LANES)
    gain_rows = []
    for t in range(nt):
        g = gains[t] if gains[t] is not None else jnp.ones((HD,), F32)
        gain_rows.append(jnp.tile(g.astype(F32), 2))
    gain_arr = jnp.stack(gain_rows)
    in_specs = []
    for t in range(nt):
        w = n_heads[t] * HD
        in_specs.append(pl.BlockSpec((1, ts, w), functools.partial(lambda b, s, cb: (b, s, cb), cb=col_blocks[t])))
    in_specs += [pl.BlockSpec((nt, LANES), lambda b, s: (0, 0)),
                 pl.BlockSpec((ts, LANES), lambda b, s: (s, 0)),
                 pl.BlockSpec((ts, LANES), lambda b, s: (s, 0)),
                 pl.BlockSpec((LANES, LANES), lambda b, s: (0, 0))]
    assert S // SEL_LEN <= LANES - HD
    out_specs, out_shape = [], []
    for t in range(nt):
        if kinds[t] == 'v':
            out_specs.append(pl.BlockSpec((1, n_heads[t], LANES, ts), lambda b, s: (b, 0, 0, s)))
            out_shape.append(jax.ShapeDtypeStruct((B, n_heads[t], LANES, S), BF16))
        else:
            width = HD if kinds[t] == 'qk' else LANES
            out_specs.append(pl.BlockSpec((1, n_heads[t], ts, width), lambda b, s: (b, 0, s, 0)))
            out_shape.append(jax.ShapeDtypeStruct((B, n_heads[t], S, width), BF16))
    return pl.pallas_call(
        functools.partial(_prep_kernel, tuple(n_heads), tuple(kinds), tuple(scales)),
        grid=(B, S // ts),
        in_specs=in_specs,
        out_specs=out_specs,
        out_shape=out_shape,
        compiler_params=_cparams(("parallel", "parallel")),
        name="prep_heads",
    )(*([proj] * nt), gain_arr, cos, sin, _block_ones(LANES))


_N_LEVELS = HG_CHUNK.bit_length() - 1


def _hgrn_constants():
    C = HG_CHUNK
    r = np.arange(C)
    mats = [(r[None, :] <= r[:, None]).astype(np.float32),
            (r[None, :] > r[:, None]).astype(np.float32)]
    masks = []
    for lv in range(_N_LEVELS):
        n = 1 << lv
        blk = r // (2 * n)
        right = (r // n) % 2 == 1
        bnd = blk * 2 * n + n - 1
        m = np.zeros((C, C), np.float32)
        for t in range(C):
            if right[t]:
                m[t, bnd[t] + 1:t + 1] = 1.0
            else:
                m[t, t + 1:bnd[t] + 1] = 1.0
        mats.append(m)
        masks.append((blk[:, None] == blk[None, :]).astype(np.float32))
    masks.append(np.eye(C, dtype=np.float32))
    return jnp.asarray(np.concatenate(mats, axis=0), dtype=BF16), jnp.asarray(np.stack(masks))


def _hgrn_kernel(q_ref, f_ref, i_ref, g_ref, lb_ref, gain_ref, mall_ref, masks_ref, o_ref, st_ref):
    C = HG_CHUNK
    ts = q_ref.shape[1]

    @pl.when(pl.program_id(1) == 0)
    def _():
        st_ref[...] = jnp.zeros_like(st_ref)

    mall = mall_ref[...]
    row = lax.broadcasted_iota(I32, (C, HG_DIM), 0)
    gain = gain_ref[...]

    def chunk(c, carry):
        r0 = pl.multiple_of(c * C, C)
        for h in range(HG_HEADS):
            ls = slice(h * HG_DIM, (h + 1) * HG_DIM)
            q = q_ref[0, pl.ds(r0, C), ls]
            z = f_ref[0, pl.ds(r0, C), ls]
            v = i_ref[0, pl.ds(r0, C), ls]
            g = g_ref[0, pl.ds(r0, C), ls]
            lb = lb_ref[:, ls]
            lf = jnp.log(lb + (1.0 - lb) * jax.nn.sigmoid(z))
            k = (1.0 - lb) * jax.nn.sigmoid(-z)
            lf_hi = lf.astype(BF16)
            lf_lo = (lf - lf_hi.astype(F32)).astype(BF16)
            seg = _dot(mall, jnp.concatenate([lf_hi, lf_lo], axis=1))
            ex = jnp.exp(seg[:, :HG_DIM] + seg[:, HG_DIM:])
            qs = q * ex[0:C]
            ks = k * ex[C:2 * C]
            st = st_ref[h]
            o = _dot_nt(qs.astype(BF16), st.astype(BF16))
            a = masks_ref[_N_LEVELS] * _dot_nt(q.astype(BF16), k.astype(BF16))
            for lv in range(_N_LEVELS):
                e = ex[(2 + lv) * C:(3 + lv) * C]
                right = ((row >> lv) & 1) == 1
                qe = jnp.where(right, q * e, 0.0).astype(BF16)
                ke = jnp.where(right, 0.0, k * e).astype(BF16)
                a = a + masks_ref[lv] * _dot_nt(qe, ke)
            vb = v.astype(BF16)
            o = o + _dot(a.astype(BF16), vb)
            decay = ex[C - 1:C]
            st_ref[h] = st * decay + _dot(v.T.astype(BF16), ks.astype(BF16))
            ms = jnp.mean(o * o, axis=-1, keepdims=True)
            y = o * lax.rsqrt(ms + EPS) * gain * (g * jax.nn.sigmoid(g))
            o_ref[0, pl.ds(r0, C), ls] = y.astype(BF16)
        return carry

    lax.fori_loop(0, ts // C, chunk, 0)


def hgrn2(proj, lb, out_gain, ts=512):
    B, S, _ = proj.shape
    mall, masks = _hgrn_constants()
    W = HG_WIDTH
    return pl.pallas_call(
        _hgrn_kernel,
        grid=(B, S // ts),
        in_specs=[pl.BlockSpec((1, ts, W), lambda b, s: (b, s, 0)),
                  pl.BlockSpec((1, ts, W), lambda b, s: (b, s, 1)),
                  pl.BlockSpec((1, ts, W), lambda b, s: (b, s, 2)),
                  pl.BlockSpec((1, ts, W), lambda b, s: (b, s, 3)),
                  pl.BlockSpec((1, W), lambda b, s: (0, 0)),
                  pl.BlockSpec((1, HG_DIM), lambda b, s: (0, 0)),
                  pl.BlockSpec(mall.shape, lambda b, s: (0, 0)),
                  pl.BlockSpec(masks.shape, lambda b, s: (0, 0, 0))],
        out_specs=pl.BlockSpec((1, ts, W), lambda b, s: (b, s, 0)),
        out_shape=jax.ShapeDtypeStruct((B, S, W), BF16),
        scratch_shapes=[pltpu.VMEM((HG_HEADS, HG_DIM, HG_DIM), F32)],
        compiler_params=_cparams(("parallel", "arbitrary")),
        name="hgrn2",
    )(proj, proj, proj, proj, lb.reshape(1, W), out_gain.reshape(1, HG_DIM), mall, masks)


def _compress_kernel(xk_ref, xv_ref, w1k_ref, w1v_ref, pek_ref, pev_ref, w2k_ref, w2v_ref,
                     gain_ref, cos_ref, sin_ref, bones_ref, ok_ref, ov_ref):
    nc = xk_ref.shape[1] // CMP_STRIDE

    def hidden(x_ref, w1_ref, pe_ref):
        width = NSA_KV * CMP_HIDDEN
        u0 = jnp.zeros((nc, width), F32)
        u1 = jnp.zeros((nc, width), F32)
        const = jnp.zeros((8, width), F32)
        for p in range(CMP_STRIDE):
            xp = x_ref[0, pl.ds(p, nc, stride=CMP_STRIDE), :].astype(BF16)
            u0 = u0 + _dot(xp, w1_ref[p])
            u1 = u1 + _dot(xp, w1_ref[CMP_STRIDE + p])
            const = const + _dot(pe_ref[p].astype(BF16), w1_ref[p])
            const = const + _dot(pe_ref[CMP_STRIDE + p].astype(BF16), w1_ref[CMP_STRIDE + p])
        h = u0 + pltpu.roll(u1, nc - 1, axis=0) + const[0:1]
        return jax.nn.gelu(h).astype(BF16)

    yk = _dot(hidden(xk_ref, w1k_ref, pek_ref), w2k_ref[...])
    yk = _head_norm_rope(yk, gain_ref[...], cos_ref[...], sin_ref[...], bones_ref[...], 1.0)
    lane = lax.broadcasted_iota(I32, yk.shape, 1)
    ok_ref[0] = jnp.where(lane < HD, yk, 0.0).astype(BF16)
    ok_ref[1] = jnp.where(lane < HD, pltpu.roll(yk, HD, axis=1), 0.0).astype(BF16)
    yv_t = _dot_nt(w2v_ref[...], hidden(xv_ref, w1v_ref, pev_ref)).astype(BF16)
    ov_ref[0] = yv_t[:HD]
    ov_ref[1] = yv_t[HD:]


def _group_diag(w):
    z = jnp.zeros_like(w)
    return jnp.concatenate([jnp.concatenate([w, z], axis=-1), jnp.concatenate([z, w], axis=-1)], axis=-2)


def nsa_compress(proj, kc_blk, vc_blk, pe, w1, w2, k_gain):
    B, S, _ = proj.shape
    nc = S // CMP_STRIDE
    pos_end = jnp.arange(nc) * CMP_STRIDE + CMP_LEN - 1
    cos, sin = _rope_tables(pos_end, LANES)
    w1d = _group_diag(w1.reshape(2, CMP_LEN, HD, CMP_HIDDEN)).astype(BF16)
    w2d = _group_diag(w2).astype(BF16)
    pe2 = jnp.broadcast_to(jnp.tile(pe, (1, 1, NSA_KV))[:, :, None, :], (2, CMP_LEN, 8, LANES))
    full = lambda shape: pl.BlockSpec(shape, lambda b: (0,) * len(shape))
    w1_shape = (CMP_LEN, LANES, NSA_KV * CMP_HIDDEN)
    return pl.pallas_call(
        _compress_kernel,
        grid=(B,),
        in_specs=[pl.BlockSpec((1, S, LANES), lambda b: (b, 0, kc_blk)),
                  pl.BlockSpec((1, S, LANES), lambda b: (b, 0, vc_blk)),
                  full(w1_shape), full(w1_shape),
                  full((CMP_LEN, 8, LANES)), full((CMP_LEN, 8, LANES)),
                  full((NSA_KV * CMP_HIDDEN, LANES)), full((LANES, NSA_KV * CMP_HIDDEN)),
                  full((1, LANES)), full((nc, LANES)), full((nc, LANES)), full((LANES, LANES))],
        out_specs=[pl.BlockSpec((NSA_KV, nc, LANES), lambda b: (b, 0, 0)),
                   pl.BlockSpec((NSA_KV, HD, nc), lambda b: (b, 0, 0))],
        out_shape=[jax.ShapeDtypeStruct((B * NSA_KV, nc, LANES), BF16),
                   jax.ShapeDtypeStruct((B * NSA_KV, HD, nc), BF16)],
        compiler_params=_cparams(("parallel",)),
        name="nsa_compress",
    )(proj, proj, w1d[0], w1d[1], pe2[0], pe2[1], w2d[0], w2d[1].T,
      jnp.tile(k_gain.astype(F32), 2).reshape(1, LANES), cos, sin, _block_ones(LANES))


NSA_TQ = 256
NSA_TK = 1024


def _nsa_kernel(q_ref, kc_ref, vc_ref, ks_ref, vs_ref, kw_ref, vw_ref, gate_ref, cover_ref, place_ref,
                o_ref, acc_ref):
    TQ, TK, H = NSA_TQ, NSA_TK, NSA_HPG
    R = H * TQ
    t0 = pl.program_id(1) * TQ
    tl = t0 + lax.broadcasted_iota(I32, (1, TQ), 1)
    ncmp = kc_ref.shape[1]
    nsel = cover_ref.shape[0]

    def all_heads(x):
        return jnp.concatenate([x] * H, axis=1)

    def select(g):
        q = q_ref[0, g * H:(g + 1) * H].reshape(R, LANES)
        cmp_end = lax.broadcasted_iota(I32, (ncmp, 1), 0) * CMP_STRIDE + (CMP_LEN - 1)
        valid = all_heads(jnp.where(cmp_end <= tl, 1.0, 0.0))
        s = jnp.where(valid > 0.5, _dot_nt(kc_ref[g], q), NEG)
        p = jnp.exp(s - jnp.max(s, axis=0, keepdims=True)) * valid
        p_c = p / jnp.maximum(jnp.sum(p, axis=0, keepdims=True), 1e-30)
        o_c = _dot(vc_ref[g], p_c.astype(BF16))
        p_tok = p_c[:, 0:TQ]
        for h in range(1, H):
            p_tok = p_tok + p_c[:, h * TQ:(h + 1) * TQ]
        p_hi = p_tok.astype(BF16)
        p_lo = (p_tok - p_hi.astype(F32)).astype(BF16)
        imp = _dot(cover_ref[...], p_hi) + _dot(cover_ref[...], p_lo)
        jj = lax.broadcasted_iota(I32, (nsel, TQ), 0)
        jf = jj.astype(F32)
        cur = tl >> 6
        allowed = jj * SEL_LEN <= tl
        forced = (jj == 0) | (jj == cur) | (jj == cur - 1)
        score = jnp.where(allowed, imp + jnp.where(forced, FORCE_BONUS, 0.0), NEG)
        sel = jnp.zeros((nsel, TQ), F32)
        for _ in range(min(SEL_TOPN, nsel)):
            mx = jnp.max(score, axis=0, keepdims=True)
            first = jnp.min(jnp.where(score == mx, jf, float(nsel)), axis=0, keepdims=True)
            hit = jf == first
            sel = jnp.where(hit & (mx > 0.5 * NEG), 1.0, sel)
            score = jnp.where(hit, NEG, score)
        block_bias = jnp.where(sel > 0.5, 0.0, NEG).T.astype(BF16)
        placed = _dot(block_bias, place_ref[...]).astype(BF16)
        return o_c, q, q + jnp.concatenate([placed] * H, axis=0)

    def sweep_tile(g, q_sel, kt, m_old, causal):
        k0 = pl.multiple_of(kt * TK, TK)
        s = _dot_nt(ks_ref[0, g, pl.ds(k0, TK), :], q_sel)
        if causal:
            kpos = k0 + lax.broadcasted_iota(I32, (TK, 1), 0)
            s = s + all_heads(jnp.where(kpos <= tl, 0.0, NEG))
        m_new = jnp.maximum(m_old, jnp.max(s, axis=0, keepdims=True))
        p = jnp.exp(s - m_new).astype(BF16)
        acc_ref[g] = jnp.exp(m_old - m_new) * acc_ref[g] + _dot(vs_ref[0, g, :, pl.ds(k0, TK)], p)
        return m_new

    def window(g, q):
        WK = NSA_WINDOW + TQ
        ws = pl.multiple_of(jnp.maximum(t0 - NSA_WINDOW, 0), TQ)
        kp = ws + lax.broadcasted_iota(I32, (WK, 1), 0)
        wbias = jnp.where((kp <= tl) & (kp > tl - NSA_WINDOW), 0.0, NEG)
        s = _dot_nt(kw_ref[0, g, pl.ds(ws, WK), :], q) + all_heads(wbias)
        p = jnp.exp(s - jnp.max(s, axis=0, keepdims=True)).astype(BF16)
        acc_w = _dot(vw_ref[0, g, :, pl.ds(ws, WK)], p)
        return acc_w[:HD] / acc_w[HD:HD + 1]

    groups = range(NSA_KV)
    chosen = [select(g) for g in groups]
    acc_ref[...] = jnp.zeros(acc_ref.shape, F32)
    last = t0 // TK
    m_run = lax.fori_loop(
        0, last, lambda kt, ms: tuple(sweep_tile(g, chosen[g][2], kt, ms[g], False) for g in groups),
        tuple(jnp.full((1, R), NEG, F32) for _ in groups))
    heads_out = []
    for g in groups:
        o_c, q, q_sel = chosen[g]
        sweep_tile(g, q_sel, last, m_run[g], True)
        acc = acc_ref[g]
        o_s = acc[:HD] / jnp.maximum(acc[HD:HD + 1], 1e-30)
        o_w = window(g, q)
        sg = jax.nn.sigmoid(gate_ref[0, :, g * LANES:(g + 1) * LANES]).T
        for h in range(H):
            cols = slice(h * TQ, (h + 1) * TQ)
            heads_out.append(sg[3 * h:3 * h + 1] * o_c[:, cols] + sg[3 * h + 1:3 * h + 2] * o_s[:, cols]
                             + sg[3 * h + 2:3 * h + 3] * o_w[:, cols])
    o_ref[0] = jnp.concatenate(heads_out, axis=0).T.astype(BF16)


def nsa_attention(qn, kcb, vcb, ksn, vsb, kwn, vwb, proj):
    B, _, S, _ = qn.shape
    G, H, TQ = NSA_KV, NSA_HPG, NSA_TQ
    assert S % NSA_TK == 0 and S >= NSA_WINDOW + TQ
    ncmp = kcb.shape[1]
    nsel = S // SEL_LEN
    n = np.arange(ncmp)[:, None] * CMP_STRIDE
    j = np.arange(nsel)[None, :] * SEL_LEN
    cover = jnp.asarray(((n < j + SEL_LEN) & (n + CMP_LEN > j)).astype(np.float32).T, dtype=BF16)
    place = np.zeros((nsel, LANES), np.float32)
    place[np.arange(nsel), HD + np.arange(nsel)] = 1.0
    R = H * TQ
    k_spec = pl.BlockSpec((1, G, S, LANES), lambda b, i: (b, 0, 0, 0))
    v_spec = pl.BlockSpec((1, G, LANES, S), lambda b, i: (b, 0, 0, 0))
    assert C_GATE % (G * LANES) == 0
    return pl.pallas_call(
        _nsa_kernel,
        grid=(B, S // TQ),
        in_specs=[pl.BlockSpec((1, G * H, TQ, LANES), lambda b, i: (b, 0, i, 0)),
                  pl.BlockSpec((G, ncmp, LANES), lambda b, i: (b, 0, 0)),
                  pl.BlockSpec((G, HD, ncmp), lambda b, i: (b, 0, 0)),
                  k_spec, v_spec, k_spec, v_spec,
                  pl.BlockSpec((1, TQ, G * LANES), lambda b, i: (b, i, C_GATE // (G * LANES))),
                  pl.BlockSpec((nsel, ncmp), lambda b, i: (0, 0)),
                  pl.BlockSpec((nsel, LANES), lambda b, i: (0, 0))],
        out_specs=pl.BlockSpec((1, TQ, G * H * HD), lambda b, i: (b, i, 0)),
        out_shape=jax.ShapeDtypeStruct((B, S, NSA_HEADS * HD), BF16),
        scratch_shapes=[pltpu.VMEM((G, LANES, R), F32)],
        compiler_params=_cparams(("parallel", "arbitrary")),
        name="nsa_attention",
    )(qn, kcb, vcb, ksn, vsb, kwn, vwb, proj, cover, jnp.asarray(place, dtype=BF16))


SWA_TQ = 256


def _swa_kernel(sink_ref, q_ref, k_ref, v_ref, o_ref):
    TQ, H = SWA_TQ, SWA_HPG
    R = H * TQ
    g = pl.program_id(1)
    t0 = pl.program_id(2) * TQ
    q = q_ref[0].reshape(R, HD)
    tl = t0 + lax.broadcasted_iota(I32, (1, TQ), 1)
    WK = SWA_WINDOW + TQ
    ws = pl.multiple_of(jnp.maximum(t0 - SWA_WINDOW, 0), LANES)
    kp = ws + lax.broadcasted_iota(I32, (WK, 1), 0)
    bias = jnp.where((kp <= tl) & (kp > tl - SWA_WINDOW), 0.0, NEG)
    s = _dot_nt(k_ref[0, 0, pl.ds(ws, WK), :], q) + jnp.concatenate([bias] * H, axis=1)
    head = lax.broadcasted_iota(I32, (1, R), 1) // TQ
    sink = jnp.zeros((1, R), F32)
    for h in range(H):
        sink = jnp.where(head == h, sink_ref[g * H + h], sink)
    m = jnp.maximum(jnp.max(s, axis=0, keepdims=True), sink)
    p = jnp.exp(s - m).astype(BF16)
    acc = _dot(v_ref[0, 0, :, pl.ds(ws, WK)], p)
    o = acc[:HD] / (acc[HD:HD + 1] + jnp.exp(sink - m))
    o_ref[0] = jnp.concatenate([o[:, h * TQ:(h + 1) * TQ] for h in range(H)], axis=0).T.astype(BF16)


def swa_attention(qn, kn, vb, sinks):
    B, _, S, _ = qn.shape
    G, H, TQ = SWA_KV, SWA_HPG, SWA_TQ
    assert S >= SWA_WINDOW + TQ
    k_spec = pl.BlockSpec((1, 1, S, HD), lambda b, g, i, sk: (b, g, 0, 0))
    v_spec = pl.BlockSpec((1, 1, LANES, S), lambda b, g, i, sk: (b, g, 0, 0))
    return pl.pallas_call(
        _swa_kernel,
        grid_spec=pltpu.PrefetchScalarGridSpec(
            num_scalar_prefetch=1,
            grid=(B, G, S // TQ),
            in_specs=[pl.BlockSpec((1, H, TQ, HD), lambda b, g, i, sk: (b, g, i, 0)), k_spec, v_spec],
            out_specs=pl.BlockSpec((1, TQ, H * HD), lambda b, g, i, sk: (b, i, g))),
        out_shape=jax.ShapeDtypeStruct((B, S, SWA_HEADS * HD), BF16),
        compiler_params=_cparams(("parallel", "parallel", "arbitrary")),
        name="swa_attention",
    )(sinks.astype(F32), qn, kn, vb)


R_GRP = 0
R_EXP = 32


SUBLANES = 8


def _store_row_tiles(ref, val, base=0):
    for s in range(SUBLANES):
        ref[pl.ds(base * SUBLANES + s, val.shape[0], stride=SUBLANES), :] = val[:, s * LANES:(s + 1) * LANES]


def _load_row_tiles(ref, rows, base=0):
    return [ref[pl.ds(base * SUBLANES + s, rows, stride=SUBLANES), :] for s in range(SUBLANES)]


def _router_kernel(x_ref, g_ref, w_ref, b_ref, tri_ref, hn_ref, route_ref, compact_ref, cnt_ref, base_ref):
    i = pl.program_id(0)
    last = pl.num_programs(0) - 1

    @pl.when(i == 0)
    def _():
        base_ref[...] = jnp.zeros_like(base_ref)

    @pl.when(i < last)
    def _():
        _route_tile(x_ref, g_ref, w_ref, b_ref, tri_ref, hn_ref, route_ref, compact_ref, base_ref)

    @pl.when(i == last)
    def _():
        hn_ref[...] = jnp.zeros_like(hn_ref)
        route_ref[...] = jnp.zeros_like(route_ref)
        compact_ref[...] = jnp.zeros_like(compact_ref)

    cnt_ref[...] = base_ref[...]


def _route_tile(x_ref, g_ref, w_ref, b_ref, tri_ref, hn_ref, route_ref, compact_ref, base_ref):
    x = x_ref[...]
    ms = jnp.mean(x * x, axis=-1, keepdims=True)
    h = x * lax.rsqrt(ms + EPS) * g_ref[...]
    _store_row_tiles(hn_ref, h)
    h_hi = h.astype(BF16)
    h_lo = (h - h_hi.astype(F32)).astype(BF16)
    logits = _dot(h_hi, w_ref[0]) + _dot(h_lo, w_ref[0]) + _dot(h_hi, w_ref[1]) + b_ref[...]
    lane = lax.broadcasted_iota(I32, logits.shape, 1)
    is_grp = lane < N_GROUPS
    gl = jnp.where(is_grp, logits, NEG)
    gmax = jnp.max(gl, axis=-1, keepdims=True)
    lanef = lane.astype(F32)
    gidx = jnp.min(jnp.where(gl == gmax, lanef, float(LANES)), axis=-1, keepdims=True)
    gw = 1.0 / jnp.sum(jnp.where(is_grp, jnp.exp(logits - gmax), 0.0), axis=-1, keepdims=True)
    in_grp = ((lane >> 3) - R_EXP // EXP_PER_GROUP).astype(F32) == gidx
    w0 = jnp.where(in_grp, logits, NEG)
    m1 = jnp.max(w0, axis=-1, keepdims=True)
    i1 = jnp.min(jnp.where(w0 == m1, lanef, float(LANES)), axis=-1, keepdims=True)
    w1 = jnp.where(lanef == i1, NEG, w0)
    m2 = jnp.max(w1, axis=-1, keepdims=True)
    i2 = jnp.min(jnp.where(w1 == m2, lanef, float(LANES)), axis=-1, keepdims=True)
    e2 = jnp.exp(m2 - m1)
    p1 = gw / (1.0 + e2)
    p2 = gw * e2 / (1.0 + e2)
    oh1 = jnp.where(lanef == i1, 1.0, 0.0)
    oh2 = jnp.where(lanef == i2, 1.0, 0.0)
    oh = oh1 + oh2
    before = _dot(tri_ref[...], oh.astype(BF16)) + base_ref[0:1, :]
    r1 = jnp.sum(before * oh1, axis=-1, keepdims=True)
    r2 = jnp.sum(before * oh2, axis=-1, keepdims=True)
    base_ref[...] = base_ref[...] + jnp.sum(oh, axis=0, keepdims=True)
    route = jnp.zeros_like(logits)
    for ln, val in enumerate((i1 - R_EXP, i2 - R_EXP, p1, p2, r1, r2)):
        route = jnp.where(lane == ln, val, route)
    route_ref[...] = route
    compact_ref[0] = route.T[0:SUBLANES]


def moe_router(x, gain, w_grp, b_grp, w_exp, b_exp, tm=256):
    T, D = x.shape
    nt = T // tm
    w = jnp.zeros((D, LANES), F32).at[:, R_GRP:R_GRP + N_GROUPS].set(w_grp).at[:, R_EXP:R_EXP + N_EXPERTS].set(w_exp)
    b = jnp.zeros((1, LANES), F32).at[0, R_GRP:R_GRP + N_GROUPS].set(b_grp).at[0, R_EXP:R_EXP + N_EXPERTS].set(b_exp)
    w_hi = w.astype(BF16)
    w = jnp.stack([w_hi, (w - w_hi.astype(F32)).astype(BF16)])
    tri = jnp.asarray(np.tril(np.ones((tm, tm), np.float32), -1), dtype=BF16)
    return pl.pallas_call(
        _router_kernel,
        grid=(nt + 1,),
        in_specs=[pl.BlockSpec((tm, D), lambda i: (jnp.minimum(i, nt - 1), 0)),
                  pl.BlockSpec((1, D), lambda i: (0, 0)),
                  pl.BlockSpec((2, D, LANES), lambda i: (0, 0, 0)),
                  pl.BlockSpec((1, LANES), lambda i: (0, 0)),
                  pl.BlockSpec((tm, tm), lambda i: (0, 0))],
        out_specs=[pl.BlockSpec((tm * SUBLANES, LANES), lambda i: (i, 0)),
                   pl.BlockSpec((tm, LANES), lambda i: (i, 0)),
                   pl.BlockSpec((1, SUBLANES, tm), lambda i: (i, 0, 0)),
                   pl.BlockSpec((8, LANES), lambda i: (0, 0))],
        out_shape=[jax.ShapeDtypeStruct(((T + tm) * SUBLANES, LANES), F32),
                   jax.ShapeDtypeStruct((T + tm, LANES), F32),
                   jax.ShapeDtypeStruct((nt + 1, SUBLANES, tm), F32),
                   jax.ShapeDtypeStruct((8, LANES), F32)],
        scratch_shapes=[pltpu.VMEM((8, LANES), F32)],
        compiler_params=_cparams(("arbitrary",)),
        name="moe_router",
    )(x, gain.reshape(1, D), w, b, tri)


def _row_copy(src, src_row, dst, dst_row, sem):
    def tile(ref, row):
        return ref.at[pl.ds(pl.multiple_of(row * SUBLANES, SUBLANES), SUBLANES)]

    return pltpu.make_async_copy(tile(src, src_row), tile(dst, dst_row), sem)


DISPATCH_TILE = 512
ROW_UNROLL = 4


def _wait_rows(hbm_ref, sem, n):
    view = hbm_ref.at[pl.ds(0, n * SUBLANES)]
    pltpu.make_async_copy(view, view, sem).wait()


def _dispatch_kernel(dest_ref, x_ref, xb_ref, stage, sems):
    i = pl.program_id(0)
    last = pl.num_programs(0) - 1
    slot = i % 2

    @pl.when(i >= 2)
    def _():
        _wait_rows(xb_ref, sems.at[slot], DISPATCH_TILE)

    stage[slot] = x_ref[...]

    def issue(t, c):
        _row_copy(stage.at[slot], t, xb_ref, dest_ref[0, 0, 2 * t], sems.at[slot]).start(priority=0)
        _row_copy(stage.at[slot], t, xb_ref, dest_ref[0, 0, 2 * t + 1], sems.at[slot]).start(priority=1)
        return c

    lax.fori_loop(0, DISPATCH_TILE // 2, issue, 0, unroll=ROW_UNROLL)

    @pl.when(i == last)
    def _():
        _wait_rows(xb_ref, sems.at[slot], DISPATCH_TILE)

    @pl.when((i == last) & (i >= 1))
    def _():
        _wait_rows(xb_ref, sems.at[1 - slot], DISPATCH_TILE)


def moe_dispatch(hn, dest_all, n_tok):
    n_rows = dest_all.shape[0]
    tm = DISPATCH_TILE // 2
    nt = n_rows // DISPATCH_TILE
    n_tok_tiles = n_tok // tm
    return pl.pallas_call(
        _dispatch_kernel,
        grid=(nt,),
        in_specs=[pl.BlockSpec((1, 1, DISPATCH_TILE), lambda i: (i, 0, 0), memory_space=pltpu.SMEM),
                  pl.BlockSpec((tm * SUBLANES, LANES), lambda i: (jnp.minimum(i, n_tok_tiles), 0))],
        out_specs=pl.BlockSpec(memory_space=pl.ANY),
        out_shape=jax.ShapeDtypeStruct((n_rows * SUBLANES, LANES), F32),
        scratch_shapes=[pltpu.VMEM((2, tm * SUBLANES, LANES), F32), pltpu.SemaphoreType.DMA((2,))],
        compiler_params=_cparams(("arbitrary",)),
        name="moe_dispatch",
    )(dest_all.reshape(nt, 1, DISPATCH_TILE), hn)


def _expert_kernel(be_ref, x_ref, wg_ref, wu_ref, wd_ref, y_ref, wg_b, wu_b, wd_b):
    i = pl.program_id(0)

    @pl.when((i == 0) | (be_ref[i] != be_ref[jnp.maximum(i - 1, 0)]))
    def _():
        wg_b[...] = wg_ref[0, 0].astype(BF16)
        wu_b[...] = wu_ref[0, 0].astype(BF16)
        wd_b[...] = wd_ref[0, 0].astype(BF16)

    x = jnp.concatenate(_load_row_tiles(x_ref, MOE_BLOCK), axis=1).astype(BF16)
    hid = _dot(x, wg_b[...])
    hid = hid * jax.nn.sigmoid(hid) * _dot(x, wu_b[...])
    _store_row_tiles(y_ref, _dot(hid.astype(BF16), wd_b[...]))


def moe_experts(xb, blk_expert, layer, w_gate, w_up, w_down):
    n_rows = xb.shape[0] // SUBLANES
    D = w_gate.shape[2]
    blk = (MOE_BLOCK * SUBLANES, LANES)
    wspec = lambda shape: pl.BlockSpec((1, 1) + shape, lambda i, be: (layer, be[i], 0, 0))
    return pl.pallas_call(
        _expert_kernel,
        grid_spec=pltpu.PrefetchScalarGridSpec(
            num_scalar_prefetch=1,
            grid=(n_rows // MOE_BLOCK,),
            in_specs=[pl.BlockSpec(blk, lambda i, be: (i, 0)),
                      wspec((D, EXPERT_FF)), wspec((D, EXPERT_FF)), wspec((EXPERT_FF, D))],
            out_specs=pl.BlockSpec(blk, lambda i, be: (i, 0)),
            scratch_shapes=[pltpu.VMEM((D, EXPERT_FF), BF16), pltpu.VMEM((D, EXPERT_FF), BF16),
                            pltpu.VMEM((EXPERT_FF, D), BF16)]),
        out_shape=jax.ShapeDtypeStruct((n_rows * SUBLANES, LANES), F32),
        compiler_params=_cparams(("arbitrary",)),
        name="moe_experts",
    )(blk_expert, xb, w_gate, w_up, w_down)


def _combine_kernel(dest_ref, dest_next_ref, x_ref, route_ref, yb_ref, o_ref, ybuf, sems):
    tm = x_ref.shape[0]
    i = pl.program_id(0)
    slot = i % 2

    def gather(d_ref, s):
        def body(t, c):
            _row_copy(yb_ref, d_ref[0, 0, 2 * t], ybuf.at[s], t, sems.at[s]).start(priority=0)
            _row_copy(yb_ref, d_ref[0, 0, 2 * t + 1], ybuf.at[s], tm + t, sems.at[s]).start(priority=1)
            return c

        lax.fori_loop(0, tm, body, 0, unroll=ROW_UNROLL)

    @pl.when(i == 0)
    def _():
        gather(dest_ref, 0)

    @pl.when(i + 1 < pl.num_programs(0))
    def _():
        gather(dest_next_ref, 1 - slot)

    _wait_rows(yb_ref, sems.at[slot], 2 * tm)
    r = route_ref[...]
    w0, w1 = r[:, 2:3], r[:, 3:4]
    y0 = _load_row_tiles(ybuf.at[slot], tm)
    y1 = _load_row_tiles(ybuf.at[slot], tm, base=tm)
    for s in range(SUBLANES):
        cols = slice(s * LANES, (s + 1) * LANES)
        o_ref[:, cols] = x_ref[:, cols] + w0 * y0[s] + w1 * y1[s]


def moe_combine(x, yb, dest, route):
    T, D = x.shape
    tm = DISPATCH_TILE // 2
    nt = T // tm
    dest3 = dest.reshape(nt, 1, DISPATCH_TILE)
    return pl.pallas_call(
        _combine_kernel,
        grid=(nt,),
        in_specs=[pl.BlockSpec((1, 1, DISPATCH_TILE), lambda i: (i, 0, 0), memory_space=pltpu.SMEM),
                  pl.BlockSpec((1, 1, DISPATCH_TILE), lambda i: (jnp.minimum(i + 1, nt - 1), 0, 0),
                               memory_space=pltpu.SMEM),
                  pl.BlockSpec((tm, D), lambda i: (i, 0)),
                  pl.BlockSpec((tm, LANES), lambda i: (i, 0)),
                  pl.BlockSpec(memory_space=pl.ANY)],
        out_specs=pl.BlockSpec((tm, D), lambda i: (i, 0)),
        out_shape=jax.ShapeDtypeStruct((T, D), F32),
        scratch_shapes=[pltpu.VMEM((2, DISPATCH_TILE * SUBLANES, LANES), F32), pltpu.SemaphoreType.DMA((2,))],
        compiler_params=_cparams(("arbitrary",)),
        name="moe_combine",
    )(dest3, dest3, x, route, yb)


def hier_moe_block(x, layer, gain, w_grp, b_grp, w_exp, b_exp, w_gate, w_up, w_down):
    T, D = x.shape
    hn, route, compact, cnt = moe_router(x, gain, w_grp, b_grp, w_exp, b_exp)
    A = 2 * T
    n_rows = -(-A // MOE_BLOCK) * MOE_BLOCK + N_EXPERTS * MOE_BLOCK
    n_blocks = n_rows // MOE_BLOCK
    counts = cnt[0, R_EXP:R_EXP + N_EXPERTS].astype(I32)
    pcounts = (counts + MOE_BLOCK - 1) // MOE_BLOCK * MOE_BLOCK
    pend = jnp.cumsum(pcounts)
    pstart = pend - pcounts
    experts = jnp.arange(N_EXPERTS, dtype=I32)
    fields = compact[:-1].astype(I32)
    slot_e = jnp.stack([fields[:, 0], fields[:, 1]], axis=-1).reshape(T, 2)
    slot_rank = jnp.stack([fields[:, 4], fields[:, 5]], axis=-1).reshape(T, 2)
    seg_start = jnp.sum(jnp.where(slot_e[..., None] == experts, pstart, 0), axis=-1)
    dest = (slot_rank + seg_start).reshape(A)
    n_pad = n_rows - A
    pad_cnt = jnp.concatenate([pcounts - counts, (n_rows - pend[-1])[None]])
    pad_start = jnp.concatenate([pstart + counts, pend[-1:]])
    pad_end = jnp.cumsum(pad_cnt)
    k = jnp.arange(n_pad, dtype=I32)
    seg = (k[:, None] >= pad_end[None, :]).astype(I32).sum(axis=1)
    seg_hot = seg[:, None] == jnp.arange(N_EXPERTS + 1, dtype=I32)[None, :]
    pad_dest = k + jnp.sum(jnp.where(seg_hot, (pad_start - (pad_end - pad_cnt))[None, :], 0), axis=1)
    dest_all = jnp.concatenate([dest, pad_dest]).astype(I32)
    blk_start = jnp.arange(n_blocks, dtype=I32) * MOE_BLOCK
    blk_expert = jnp.minimum((pend[None, :] <= blk_start[:, None]).astype(I32).sum(axis=1), N_EXPERTS - 1)
    xb = moe_dispatch(hn, dest_all, T)
    yb = moe_experts(xb, blk_expert, layer, w_gate, w_up, w_down)
    return moe_combine(x, yb, dest, route)


def _even_w_in(w_in):
    ng = 3 * NSA_HPG
    pad = jnp.zeros((w_in.shape[0], LANES - ng), w_in.dtype)
    cols = [w_in[:, :C_GATE]]
    for g in range(NSA_KV):
        cols += [w_in[:, C_GATE + g * ng:C_GATE + (g + 1) * ng], pad]
    return jnp.concatenate(cols, axis=1).astype(BF16)


def even_layer(x, B, S, norm, w_in, lb, hg_gain, q_gain, k_gain, cmp_pe, cmp_w1, cmp_w2, w_out):
    T = B * S
    proj = rms_matmul(x, norm, _even_w_in(w_in), tn=N0 // 4).reshape(B, S, N0)
    a_out = hgrn2(proj, lb, hg_gain)
    kvb = C_KV // LANES
    qn, ksn, kwn, vsb, vwb = prep_heads(
        proj, [C_NQ // (NSA_HEADS * HD), kvb + 2, kvb + 4, kvb + 3, kvb + 5],
        [NSA_HEADS, NSA_KV, NSA_KV, NSA_KV, NSA_KV],
        [q_gain, k_gain[1], k_gain[2], None, None],
        ['pad', 'blk', 'pad', 'v', 'v'], [HD ** -0.5, 1.0, 1.0, 1.0, 1.0], S)
    kcb, vcb = nsa_compress(proj, kvb, kvb + 1, cmp_pe, cmp_w1, cmp_w2, k_gain[0])
    b_out = nsa_attention(qn, kcb, vcb, ksn, vsb, kwn, vwb, proj)
    return proj_res(a_out.reshape(T, HG_WIDTH), 0, b_out.reshape(T, NSA_HEADS * HD), 0, w_out.astype(BF16), x)


def odd_layer(x, B, S, norm, w_in, q_gain, k_gain, sinks, w_out):
    T = B * S
    proj = rms_matmul(x, norm, w_in.astype(BF16), tn=N1 // 2).reshape(B, S, N1)
    qd = SWA_HEADS * HD
    qn, kn, vb = prep_heads(
        proj, [0, qd // LANES, qd // LANES + 1], [SWA_HEADS, SWA_KV, SWA_KV],
        [q_gain, k_gain, None], ['qk', 'qk', 'v'], [HD ** -0.5, 1.0, 1.0], S)
    att = swa_attention(qn, kn, vb, sinks).reshape(T, qd)
    return proj_res(att, 0, att, 1, w_out.astype(BF16), x)


def kernel(x, ev_norm, ev_w_in, hg_lb_logits, hg_out_gain, nsa_q_gain, nsa_k_gain, nsa_cmp_pe, nsa_cmp_w1,
           nsa_cmp_w2, ev_w_out, od_norm, od_w_in, swa_q_gain, swa_k_gain, swa_sinks, od_w_out, moe_norm,
           moe_w_grp, moe_b_grp, moe_w_exp, moe_b_exp, moe_w_gate, moe_w_up, moe_w_down):
    B, S, D = x.shape
    lower_bounds = jnp.cumsum(jax.nn.softmax(hg_lb_logits.astype(F32), axis=0), axis=0)
    h = x.reshape(B * S, D)
    h = even_layer(h, B, S, ev_norm[0], ev_w_in[0], lower_bounds[0], hg_out_gain[0], nsa_q_gain[0],
                   nsa_k_gain[0], nsa_cmp_pe[0], nsa_cmp_w1[0], nsa_cmp_w2[0], ev_w_out[0])
    h = hier_moe_block(h, 0, moe_norm[0], moe_w_grp[0], moe_b_grp[0], moe_w_exp[0], moe_b_exp[0],
                       moe_w_gate, moe_w_up, moe_w_down)
    h = odd_layer(h, B, S, od_norm[0], od_w_in[0], swa_q_gain[0], swa_k_gain[0], swa_sinks[0], od_w_out[0])
    h = hier_moe_block(h, 1, moe_norm[1], moe_w_grp[1], moe_b_grp[1], moe_w_exp[1], moe_b_exp[1],
                       moe_w_gate, moe_w_up, moe_w_down)
    return h.reshape(B, S, D)
```

```python
import functools

import numpy as np
import jax
import jax.numpy as jnp
from jax import lax
from jax.experimental import pallas as pl
from jax.experimental.pallas import tpu as pltpu

F32 = jnp.float32
BF16 = jnp.bfloat16
I32 = jnp.int32

EPS = 1e-6
ROPE_THETA = 500000.0
HD = 64
ROT = HD // 4
HALF = ROT // 2

HG_DIM = 128
HG_WIDTH = 512
HG_HEADS = 4
HG_CHUNK = 256

NSA_HEADS = 8
NSA_KV = 2
NSA_HPG = NSA_HEADS // NSA_KV
CMP_LEN = 32
CMP_STRIDE = 16
CMP_HIDDEN = 256
SEL_LEN = 64
SEL_TOPN = 8
NSA_WINDOW = 512
FORCE_BONUS = 1e4

SWA_HEADS = 16
SWA_KV = 2
SWA_HPG = SWA_HEADS // SWA_KV
SWA_WINDOW = 128

N_GROUPS = 4
EXP_PER_GROUP = 8
N_EXPERTS = 32
EXPERT_FF = 512
MOE_BLOCK = 512

LANES = 128
NEG = -1e30
VMEM_V7X = 64 * 1024 * 1024
VMEM_LIMIT = VMEM_V7X * 7 // 8

C_NQ = 2048
C_KV = 2560
C_GATE = 3328
N0 = 3584
N1 = 1280


def _cparams(sem):
    return pltpu.CompilerParams(dimension_semantics=sem, vmem_limit_bytes=VMEM_LIMIT)


def _dot(a, b):
    return jnp.dot(a, b, preferred_element_type=F32)


def _dot_nt(a, b):
    return lax.dot_general(a, b, (((1,), (1,)), ((), ())), preferred_element_type=F32)


def _rms_matmul_kernel(tn, x_ref, g_ref, w_ref, o_ref):
    x = x_ref[...]
    ms = jnp.mean(x * x, axis=-1, keepdims=True)
    xn = (x * lax.rsqrt(ms + EPS) * g_ref[...]).astype(BF16)
    for j in range(o_ref.shape[1] // tn):
        o_ref[:, j * tn:(j + 1) * tn] = _dot(xn, w_ref[:, j * tn:(j + 1) * tn])


def rms_matmul(x, gain, w, tm=512, tn=None):
    T, D = x.shape
    N = w.shape[1]
    tn = N if tn is None else tn
    return pl.pallas_call(
        functools.partial(_rms_matmul_kernel, tn),
        grid=(T // tm,),
        in_specs=[pl.BlockSpec((tm, D), lambda i: (i, 0)),
                  pl.BlockSpec((1, D), lambda i: (0, 0)),
                  pl.BlockSpec((D, N), lambda i: (0, 0))],
        out_specs=pl.BlockSpec((tm, N), lambda i: (i, 0)),
        out_shape=jax.ShapeDtypeStruct((T, N), F32),
        compiler_params=_cparams(("parallel",)),
        name="rms_matmul",
    )(x, gain.reshape(1, D), w)


def _proj_res_kernel(a_ref, b_ref, wa_ref, wb_ref, r_ref, o_ref):
    acc = _dot(a_ref[...], wa_ref[...]) + _dot(b_ref[...], wb_ref[...])
    o_ref[...] = r_ref[...] + acc


def proj_res(a, a_blk, b, b_blk, w, res, tm=512):
    T, D = res.shape
    K = w.shape[0]
    kh = K // 2
    return pl.pallas_call(
        _proj_res_kernel,
        grid=(T // tm,),
        in_specs=[pl.BlockSpec((tm, kh), lambda i: (i, a_blk)),
                  pl.BlockSpec((tm, kh), lambda i: (i, b_blk)),
                  pl.BlockSpec((kh, D), lambda i: (0, 0)),
                  pl.BlockSpec((kh, D), lambda i: (1, 0)),
                  pl.BlockSpec((tm, D), lambda i: (i, 0))],
        out_specs=pl.BlockSpec((tm, D), lambda i: (i, 0)),
        out_shape=jax.ShapeDtypeStruct((T, D), F32),
        compiler_params=_cparams(("parallel",)),
        name="proj_res",
    )(a, b, w, w, res)


def _rope_tables(pos, width):
    inv = jnp.power(ROPE_THETA, -jnp.arange(HALF, dtype=F32) * 2.0 / ROT)
    ang = pos.astype(F32)[:, None] * inv[None, :]
    cos, sin = jnp.cos(ang), jnp.sin(ang)
    n = pos.shape[0]
    c = jnp.concatenate([cos, cos, jnp.ones((n, HD - ROT), F32)], axis=1)
    s = jnp.concatenate([-sin, sin, jnp.zeros((n, HD - ROT), F32)], axis=1)
    reps = width // HD
    return jnp.tile(c, (1, reps)), jnp.tile(s, (1, reps))


def _block_ones(width):
    idx = np.arange(width) // HD
    return jnp.asarray((idx[:, None] == idx[None, :]).astype(np.float32), dtype=BF16)


def _split_dot(a, b):
    hi = a.astype(BF16)
    lo = (a - hi.astype(F32)).astype(BF16)
    return _dot(hi, b) + _dot(lo, b)


def _head_norm_rope(x, gain, cos, sin, bones, scale):
    ms = _split_dot(x * x, bones) * (1.0 / HD)
    y = x * lax.rsqrt(ms + EPS) * gain
    lane = lax.broadcasted_iota(I32, y.shape, 1) % HD
    partner = jnp.where(lane < HALF, pltpu.roll(y, LANES - HALF, axis=1), pltpu.roll(y, HALF, axis=1))
    out = y * cos + partner * sin
    return out * scale if scale != 1.0 else out


def _prep_kernel(n_heads, kinds, scales, *refs):
    nt = len(n_heads)
    x_refs = refs[:nt]
    gain_ref, cos_ref, sin_ref, bones_ref = refs[nt:nt + 4]
    o_refs = refs[nt + 4:]
    cos = cos_ref[...]
    sin = sin_ref[...]
    bones = bones_ref[...]
    ts = cos.shape[0]
    lane = lax.broadcasted_iota(I32, cos.shape, 1)
    pos = pl.program_id(1) * ts + lax.broadcasted_iota(I32, cos.shape, 0)
    for t in range(nt):
        for c in range(n_heads[t] // 2):
            x = x_refs[t][0, :, c * LANES:(c + 1) * LANES]
            if kinds[t] != 'v':
                x = _head_norm_rope(x, gain_ref[t:t + 1, :], cos, sin, bones, scales[t])
            for j, xh in enumerate((x, pltpu.roll(x, HD, axis=1))):
                if kinds[t] == 'qk':
                    y = xh[:, :HD]
                elif kinds[t] == 'pad':
                    y = jnp.where(lane < HD, xh, 0.0)
                elif kinds[t] == 'blk':
                    y = jnp.where(lane < HD, xh, jnp.where(lane - HD == pos // SEL_LEN, 1.0, 0.0))
                else:
                    y = jnp.where(lane < HD, xh, jnp.where(lane == HD, 1.0, 0.0)).T
                o_refs[t][0, 2 * c + j] = y.astype(BF16)


def prep_heads(proj, col_blocks, n_heads, gains, kinds, scales, S, ts=512):
    B = proj.shape[0]
    nt = len(n_heads)
    cos, sin = _rope_tables(jnp.arange(S), LANES)
    gain_rows = []
    for t in range(nt):
        g = gains[t] if gains[t] is not None else jnp.ones((HD,), F32)
        gain_rows.append(jnp.tile(g.astype(F32), 2))
    gain_arr = jnp.stack(gain_rows)
    in_specs = []
    for t in range(nt):
        w = n_heads[t] * HD
        in_specs.append(pl.BlockSpec((1, ts, w), functools.partial(lambda b, s, cb: (b, s, cb), cb=col_blocks[t])))
    in_specs += [pl.BlockSpec((nt, LANES), lambda b, s: (0, 0)),
                 pl.BlockSpec((ts, LANES), lambda b, s: (s, 0)),
                 pl.BlockSpec((ts, LANES), lambda b, s: (s, 0)),
                 pl.BlockSpec((LANES, LANES), lambda b, s: (0, 0))]
    assert S // SEL_LEN <= LANES - HD
    out_specs, out_shape = [], []
    for t in range(nt):
        if kinds[t] == 'v':
            out_specs.append(pl.BlockSpec((1, n_heads[t], LANES, ts), lambda b, s: (b, 0, 0, s)))
            out_shape.append(jax.ShapeDtypeStruct((B, n_heads[t], LANES, S), BF16))
        else:
            width = HD if kinds[t] == 'qk' else LANES
            out_specs.append(pl.BlockSpec((1, n_heads[t], ts, width), lambda b, s: (b, 0, s, 0)))
            out_shape.append(jax.ShapeDtypeStruct((B, n_heads[t], S, width), BF16))
    return pl.pallas_call(
        functools.partial(_prep_kernel, tuple(n_heads), tuple(kinds), tuple(scales)),
        grid=(B, S // ts),
        in_specs=in_specs,
        out_specs=out_specs,
        out_shape=out_shape,
        compiler_params=_cparams(("parallel", "parallel")),
        name="prep_heads",
    )(*([proj] * nt), gain_arr, cos, sin, _block_ones(LANES))


_N_LEVELS = HG_CHUNK.bit_length() - 1


def _hgrn_constants():
    C = HG_CHUNK
    r = np.arange(C)
    mats = [(r[None, :] <= r[:, None]).astype(np.float32),
            (r[None, :] > r[:, None]).astype(np.float32)]
    masks = []
    for lv in range(_N_LEVELS):
        n = 1 << lv
        blk = r // (2 * n)
        right = (r // n) % 2 == 1
        bnd = blk * 2 * n + n - 1
        m = np.zeros((C, C), np.float32)
        for t in range(C):
            if right[t]:
                m[t, bnd[t] + 1:t + 1] = 1.0
            else:
                m[t, t + 1:bnd[t] + 1] = 1.0
        mats.append(m)
        masks.append((blk[:, None] == blk[None, :]).astype(np.float32))
    masks.append(np.eye(C, dtype=np.float32))
    return jnp.asarray(np.concatenate(mats, axis=0), dtype=BF16), jnp.asarray(np.stack(masks))


def _hgrn_kernel(q_ref, f_ref, i_ref, g_ref, lb_ref, gain_ref, mall_ref, masks_ref, o_ref, st_ref):
    C = HG_CHUNK
    ts = q_ref.shape[1]

    @pl.when(pl.program_id(1) == 0)
    def _():
        st_ref[...] = jnp.zeros_like(st_ref)

    mall = mall_ref[...]
    row = lax.broadcasted_iota(I32, (C, HG_DIM), 0)
    gain = gain_ref[...]

    def chunk(c, carry):
        r0 = pl.multiple_of(c * C, C)
        for h in range(HG_HEADS):
            ls = slice(h * HG_DIM, (h + 1) * HG_DIM)
            q = q_ref[0, pl.ds(r0, C), ls]
            z = f_ref[0, pl.ds(r0, C), ls]
            v = i_ref[0, pl.ds(r0, C), ls]
            g = g_ref[0, pl.ds(r0, C), ls]
            lb = lb_ref[:, ls]
            lf = jnp.log(lb + (1.0 - lb) * jax.nn.sigmoid(z))
            k = (1.0 - lb) * jax.nn.sigmoid(-z)
            lf_hi = lf.astype(BF16)
            lf_lo = (lf - lf_hi.astype(F32)).astype(BF16)
            seg = _dot(mall, jnp.concatenate([lf_hi, lf_lo], axis=1))
            ex = jnp.exp(seg[:, :HG_DIM] + seg[:, HG_DIM:])
            qs = q * ex[0:C]
            ks = k * ex[C:2 * C]
            st = st_ref[h]
            o = _dot_nt(qs.astype(BF16), st.astype(BF16))
            a = masks_ref[_N_LEVELS] * _dot_nt(q.astype(BF16), k.astype(BF16))
            for lv in range(_N_LEVELS):
                e = ex[(2 + lv) * C:(3 + lv) * C]
                right = ((row >> lv) & 1) == 1
                qe = jnp.where(right, q * e, 0.0).astype(BF16)
                ke = jnp.where(right, 0.0, k * e).astype(BF16)
                a = a + masks_ref[lv] * _dot_nt(qe, ke)
            vb = v.astype(BF16)
            o = o + _dot(a.astype(BF16), vb)
            decay = ex[C - 1:C]
            st_ref[h] = st * decay + _dot(v.T.astype(BF16), ks.astype(BF16))
            ms = jnp.mean(o * o, axis=-1, keepdims=True)
            y = o * lax.rsqrt(ms + EPS) * gain * (g * jax.nn.sigmoid(g))
            o_ref[0, pl.ds(r0, C), ls] = y.astype(BF16)
        return carry

    lax.fori_loop(0, ts // C, chunk, 0)


def hgrn2(proj, lb, out_gain, ts=512):
    B, S, _ = proj.shape
    mall, masks = _hgrn_constants()
    W = HG_WIDTH
    return pl.pallas_call(
        _hgrn_kernel,
        grid=(B, S // ts),
        in_specs=[pl.BlockSpec((1, ts, W), lambda b, s: (b, s, 0)),
                  pl.BlockSpec((1, ts, W), lambda b, s: (b, s, 1)),
                  pl.BlockSpec((1, ts, W), lambda b, s: (b, s, 2)),
                  pl.BlockSpec((1, ts, W), lambda b, s: (b, s, 3)),
                  pl.BlockSpec((1, W), lambda b, s: (0, 0)),
                  pl.BlockSpec((1, HG_DIM), lambda b, s: (0, 0)),
                  pl.BlockSpec(mall.shape, lambda b, s: (0, 0)),
                  pl.BlockSpec(masks.shape, lambda b, s: (0, 0, 0))],
        out_specs=pl.BlockSpec((1, ts, W), lambda b, s: (b, s, 0)),
        out_shape=jax.ShapeDtypeStruct((B, S, W), BF16),
        scratch_shapes=[pltpu.VMEM((HG_HEADS, HG_DIM, HG_DIM), F32)],
        compiler_params=_cparams(("parallel", "arbitrary")),
        name="hgrn2",
    )(proj, proj, proj, proj, lb.reshape(1, W), out_gain.reshape(1, HG_DIM), mall, masks)


def _compress_kernel(xk_ref, xv_ref, w1k_ref, w1v_ref, pek_ref, pev_ref, w2k_ref, w2v_ref,
                     gain_ref, cos_ref, sin_ref, bones_ref, ok_ref, ov_ref):
    nc = xk_ref.shape[1] // CMP_STRIDE

    def hidden(x_ref, w1_ref, pe_ref):
        width = NSA_KV * CMP_HIDDEN
        u0 = jnp.zeros((nc, width), F32)
        u1 = jnp.zeros((nc, width), F32)
        const = jnp.zeros((8, width), F32)
        for p in range(CMP_STRIDE):
            xp = x_ref[0, pl.ds(p, nc, stride=CMP_STRIDE), :].astype(BF16)
            u0 = u0 + _dot(xp, w1_ref[p])
            u1 = u1 + _dot(xp, w1_ref[CMP_STRIDE + p])
            const = const + _dot(pe_ref[p].astype(BF16), w1_ref[p])
            const = const + _dot(pe_ref[CMP_STRIDE + p].astype(BF16), w1_ref[CMP_STRIDE + p])
        h = u0 + pltpu.roll(u1, nc - 1, axis=0) + const[0:1]
        return jax.nn.gelu(h).astype(BF16)

    yk = _dot(hidden(xk_ref, w1k_ref, pek_ref), w2k_ref[...])
    yk = _head_norm_rope(yk, gain_ref[...], cos_ref[...], sin_ref[...], bones_ref[...], 1.0)
    lane = lax.broadcasted_iota(I32, yk.shape, 1)
    ok_ref[0] = jnp.where(lane < HD, yk, 0.0).astype(BF16)
    ok_ref[1] = jnp.where(lane < HD, pltpu.roll(yk, HD, axis=1), 0.0).astype(BF16)
    yv_t = _dot_nt(w2v_ref[...], hidden(xv_ref, w1v_ref, pev_ref)).astype(BF16)
    ov_ref[0] = yv_t[:HD]
    ov_ref[1] = yv_t[HD:]


def _group_diag(w):
    z = jnp.zeros_like(w)
    return jnp.concatenate([jnp.concatenate([w, z], axis=-1), jnp.concatenate([z, w], axis=-1)], axis=-2)


def nsa_compress(proj, kc_blk, vc_blk, pe, w1, w2, k_gain):
    B, S, _ = proj.shape
    nc = S // CMP_STRIDE
    pos_end = jnp.arange(nc) * CMP_STRIDE + CMP_LEN - 1
    cos, sin = _rope_tables(pos_end, LANES)
    w1d = _group_diag(w1.reshape(2, CMP_LEN, HD, CMP_HIDDEN)).astype(BF16)
    w2d = _group_diag(w2).astype(BF16)
    pe2 = jnp.broadcast_to(jnp.tile(pe, (1, 1, NSA_KV))[:, :, None, :], (2, CMP_LEN, 8, LANES))
    full = lambda shape: pl.BlockSpec(shape, lambda b: (0,) * len(shape))
    w1_shape = (CMP_LEN, LANES, NSA_KV * CMP_HIDDEN)
    return pl.pallas_call(
        _compress_kernel,
        grid=(B,),
        in_specs=[pl.BlockSpec((1, S, LANES), lambda b: (b, 0, kc_blk)),
                  pl.BlockSpec((1, S, LANES), lambda b: (b, 0, vc_blk)),
                  full(w1_shape), full(w1_shape),
                  full((CMP_LEN, 8, LANES)), full((CMP_LEN, 8, LANES)),
                  full((NSA_KV * CMP_HIDDEN, LANES)), full((LANES, NSA_KV * CMP_HIDDEN)),
                  full((1, LANES)), full((nc, LANES)), full((nc, LANES)), full((LANES, LANES))],
        out_specs=[pl.BlockSpec((NSA_KV, nc, LANES), lambda b: (b, 0, 0)),
                   pl.BlockSpec((NSA_KV, HD, nc), lambda b: (b, 0, 0))],
        out_shape=[jax.ShapeDtypeStruct((B * NSA_KV, nc, LANES), BF16),
                   jax.ShapeDtypeStruct((B * NSA_KV, HD, nc), BF16)],
        compiler_params=_cparams(("parallel",)),
        name="nsa_compress",
    )(proj, proj, w1d[0], w1d[1], pe2[0], pe2[1], w2d[0], w2d[1].T,
      jnp.tile(k_gain.astype(F32), 2).reshape(1, LANES), cos, sin, _block_ones(LANES))


NSA_TQ = 256
NSA_TK = 1024


def _nsa_kernel(q_ref, kc_ref, vc_ref, ks_ref, vs_ref, kw_ref, vw_ref, gate_ref, cover_ref, place_ref,
                o_ref, acc_ref):
    TQ, TK, H = NSA_TQ, NSA_TK, NSA_HPG
    R = H * TQ
    t0 = pl.program_id(1) * TQ
    tl = t0 + lax.broadcasted_iota(I32, (1, TQ), 1)
    ncmp = kc_ref.shape[1]
    nsel = cover_ref.shape[0]

    def all_heads(x):
        return jnp.concatenate([x] * H, axis=1)

    def select(g):
        q = q_ref[0, g * H:(g + 1) * H].reshape(R, LANES)
        cmp_end = lax.broadcasted_iota(I32, (ncmp, 1), 0) * CMP_STRIDE + (CMP_LEN - 1)
        valid = all_heads(jnp.where(cmp_end <= tl, 1.0, 0.0))
        s = jnp.where(valid > 0.5, _dot_nt(kc_ref[g], q), NEG)
        p = jnp.exp(s - jnp.max(s, axis=0, keepdims=True)) * valid
        p_c = p / jnp.maximum(jnp.sum(p, axis=0, keepdims=True), 1e-30)
        o_c = _dot(vc_ref[g], p_c.astype(BF16))
        p_tok = p_c[:, 0:TQ]
        for h in range(1, H):
            p_tok = p_tok + p_c[:, h * TQ:(h + 1) * TQ]
        p_hi = p_tok.astype(BF16)
        p_lo = (p_tok - p_hi.astype(F32)).astype(BF16)
        imp = _dot(cover_ref[...], p_hi) + _dot(cover_ref[...], p_lo)
        jj = lax.broadcasted_iota(I32, (nsel, TQ), 0)
        jf = jj.astype(F32)
        cur = tl // SEL_LEN
        allowed = jj * SEL_LEN <= tl
        forced = (jj == 0) | (jj == cur) | (jj == cur - 1)
        score = jnp.where(allowed, imp + jnp.where(forced, FORCE_BONUS, 0.0), NEG)
        sel = jnp.zeros((nsel, TQ), F32)
        for _ in range(min(SEL_TOPN, nsel)):
            mx = jnp.max(score, axis=0, keepdims=True)
            first = jnp.min(jnp.where(score == mx, jf, float(nsel)), axis=0, keepdims=True)
            hit = jf == first
            sel = jnp.where(hit & (mx > 0.5 * NEG), 1.0, sel)
            score = jnp.where(hit, NEG, score)
        block_bias = jnp.where(sel > 0.5, 0.0, NEG).T.astype(BF16)
        placed = _dot(block_bias, place_ref[...]).astype(BF16)
        return o_c, q, q + jnp.concatenate([placed] * H, axis=0)

    def sweep_tile(g, q_sel, kt, m_old, causal):
        k0 = pl.multiple_of(kt * TK, TK)
        s = _dot_nt(ks_ref[0, g, pl.ds(k0, TK), :], q_sel)
        if causal:
            kpos = k0 + lax.broadcasted_iota(I32, (TK, 1), 0)
            s = s + all_heads(jnp.where(kpos <= tl, 0.0, NEG))
        m_new = jnp.maximum(m_old, jnp.max(s, axis=0, keepdims=True))
        p = jnp.exp(s - m_new).astype(BF16)
        acc_ref[g] = jnp.exp(m_old - m_new) * acc_ref[g] + _dot(vs_ref[0, g, :, pl.ds(k0, TK)], p)
        return m_new

    def window(g, q):
        WK = NSA_WINDOW + TQ
        ws = pl.multiple_of(jnp.maximum(t0 - NSA_WINDOW, 0), TQ)
        kp = ws + lax.broadcasted_iota(I32, (WK, 1), 0)
        wbias = jnp.where((kp <= tl) & (kp > tl - NSA_WINDOW), 0.0, NEG)
        s = _dot_nt(kw_ref[0, g, pl.ds(ws, WK), :], q) + all_heads(wbias)
        p = jnp.exp(s - jnp.max(s, axis=0, keepdims=True)).astype(BF16)
        acc_w = _dot(vw_ref[0, g, :, pl.ds(ws, WK)], p)
        return acc_w[:HD] / acc_w[HD:HD + 1]

    groups = range(NSA_KV)
    chosen = [select(g) for g in groups]
    acc_ref[...] = jnp.zeros(acc_ref.shape, F32)
    last = t0 // TK
    m_run = lax.fori_loop(
        0, last, lambda kt, ms: tuple(sweep_tile(g, chosen[g][2], kt, ms[g], False) for g in groups),
        tuple(jnp.full((1, R), NEG, F32) for _ in groups))
    heads_out = []
    for g in groups:
        o_c, q, q_sel = chosen[g]
        sweep_tile(g, q_sel, last, m_run[g], True)
        acc = acc_ref[g]
        o_s = acc[:HD] / jnp.maximum(acc[HD:HD + 1], 1e-30)
        o_w = window(g, q)
        sg = jax.nn.sigmoid(gate_ref[0, :, g * LANES:(g + 1) * LANES]).T
        for h in range(H):
            cols = slice(h * TQ, (h + 1) * TQ)
            heads_out.append(sg[3 * h:3 * h + 1] * o_c[:, cols] + sg[3 * h + 1:3 * h + 2] * o_s[:, cols]
                             + sg[3 * h + 2:3 * h + 3] * o_w[:, cols])
    o_ref[0] = jnp.concatenate(heads_out, axis=0).T.astype(BF16)


def nsa_attention(qn, kcb, vcb, ksn, vsb, kwn, vwb, proj):
    B, _, S, _ = qn.shape
    G, H, TQ = NSA_KV, NSA_HPG, NSA_TQ
    assert S % NSA_TK == 0 and S >= NSA_WINDOW + TQ
    ncmp = kcb.shape[1]
    nsel = S // SEL_LEN
    n = np.arange(ncmp)[:, None] * CMP_STRIDE
    j = np.arange(nsel)[None, :] * SEL_LEN
    cover = jnp.asarray(((n < j + SEL_LEN) & (n + CMP_LEN > j)).astype(np.float32).T, dtype=BF16)
    place = np.zeros((nsel, LANES), np.float32)
    place[np.arange(nsel), HD + np.arange(nsel)] = 1.0
    R = H * TQ
    k_spec = pl.BlockSpec((1, G, S, LANES), lambda b, i: (b, 0, 0, 0))
    v_spec = pl.BlockSpec((1, G, LANES, S), lambda b, i: (b, 0, 0, 0))
    assert C_GATE % (G * LANES) == 0
    return pl.pallas_call(
        _nsa_kernel,
        grid=(B, S // TQ),
        in_specs=[pl.BlockSpec((1, G * H, TQ, LANES), lambda b, i: (b, 0, i, 0)),
                  pl.BlockSpec((G, ncmp, LANES), lambda b, i: (b, 0, 0)),
                  pl.BlockSpec((G, HD, ncmp), lambda b, i: (b, 0, 0)),
                  k_spec, v_spec, k_spec, v_spec,
                  pl.BlockSpec((1, TQ, G * LANES), lambda b, i: (b, i, C_GATE // (G * LANES))),
                  pl.BlockSpec((nsel, ncmp), lambda b, i: (0, 0)),
                  pl.BlockSpec((nsel, LANES), lambda b, i: (0, 0))],
        out_specs=pl.BlockSpec((1, TQ, G * H * HD), lambda b, i: (b, i, 0)),
        out_shape=jax.ShapeDtypeStruct((B, S, NSA_HEADS * HD), BF16),
        scratch_shapes=[pltpu.VMEM((G, LANES, R), F32)],
        compiler_params=_cparams(("parallel", "arbitrary")),
        name="nsa_attention",
    )(qn, kcb, vcb, ksn, vsb, kwn, vwb, proj, cover, jnp.asarray(place, dtype=BF16))


SWA_TQ = 256


def _swa_kernel(sink_ref, q_ref, k_ref, v_ref, o_ref):
    TQ, H = SWA_TQ, SWA_HPG
    R = H * TQ
    g = pl.program_id(1)
    t0 = pl.program_id(2) * TQ
    q = q_ref[0].reshape(R, HD)
    tl = t0 + lax.broadcasted_iota(I32, (1, TQ), 1)
    WK = SWA_WINDOW + TQ
    ws = pl.multiple_of(jnp.maximum(t0 - SWA_WINDOW, 0), LANES)
    kp = ws + lax.broadcasted_iota(I32, (WK, 1), 0)
    bias = jnp.where((kp <= tl) & (kp > tl - SWA_WINDOW), 0.0, NEG)
    s = _dot_nt(k_ref[0, 0, pl.ds(ws, WK), :], q) + jnp.concatenate([bias] * H, axis=1)
    head = lax.broadcasted_iota(I32, (1, R), 1) // TQ
    sink = jnp.zeros((1, R), F32)
    for h in range(H):
        sink = jnp.where(head == h, sink_ref[g * H + h], sink)
    m = jnp.maximum(jnp.max(s, axis=0, keepdims=True), sink)
    p = jnp.exp(s - m).astype(BF16)
    acc = _dot(v_ref[0, 0, :, pl.ds(ws, WK)], p)
    o = acc[:HD] / (acc[HD:HD + 1] + jnp.exp(sink - m))
    o_ref[0] = jnp.concatenate([o[:, h * TQ:(h + 1) * TQ] for h in range(H)], axis=0).T.astype(BF16)


def swa_attention(qn, kn, vb, sinks):
    B, _, S, _ = qn.shape
    G, H, TQ = SWA_KV, SWA_HPG, SWA_TQ
    assert S >= SWA_WINDOW + TQ
    k_spec = pl.BlockSpec((1, 1, S, HD), lambda b, g, i, sk: (b, g, 0, 0))
    v_spec = pl.BlockSpec((1, 1, LANES, S), lambda b, g, i, sk: (b, g, 0, 0))
    return pl.pallas_call(
        _swa_kernel,
        grid_spec=pltpu.PrefetchScalarGridSpec(
            num_scalar_prefetch=1,
            grid=(B, G, S // TQ),
            in_specs=[pl.BlockSpec((1, H, TQ, HD), lambda b, g, i, sk: (b, g, i, 0)), k_spec, v_spec],
            out_specs=pl.BlockSpec((1, TQ, H * HD), lambda b, g, i, sk: (b, i, g))),
        out_shape=jax.ShapeDtypeStruct((B, S, SWA_HEADS * HD), BF16),
        compiler_params=_cparams(("parallel", "parallel", "arbitrary")),
        name="swa_attention",
    )(sinks.astype(F32), qn, kn, vb)


R_GRP = 0
R_EXP = 32


SUBLANES = 8


def _store_row_tiles(ref, val, base=0):
    for s in range(SUBLANES):
        ref[pl.ds(base * SUBLANES + s, val.shape[0], stride=SUBLANES), :] = val[:, s * LANES:(s + 1) * LANES]


def _load_row_tiles(ref, rows, base=0):
    return [ref[pl.ds(base * SUBLANES + s, rows, stride=SUBLANES), :] for s in range(SUBLANES)]


def _router_kernel(x_ref, g_ref, w_ref, b_ref, tri_ref, hn_ref, route_ref, compact_ref, cnt_ref, base_ref):
    i = pl.program_id(0)
    last = pl.num_programs(0) - 1

    @pl.when(i == 0)
    def _():
        base_ref[...] = jnp.zeros_like(base_ref)

    @pl.when(i < last)
    def _():
        _route_tile(x_ref, g_ref, w_ref, b_ref, tri_ref, hn_ref, route_ref, compact_ref, base_ref)

    @pl.when(i == last)
    def _():
        hn_ref[...] = jnp.zeros_like(hn_ref)
        route_ref[...] = jnp.zeros_like(route_ref)
        compact_ref[...] = jnp.zeros_like(compact_ref)

    cnt_ref[...] = base_ref[...]


def _route_tile(x_ref, g_ref, w_ref, b_ref, tri_ref, hn_ref, route_ref, compact_ref, base_ref):
    x = x_ref[...]
    ms = jnp.mean(x * x, axis=-1, keepdims=True)
    h = x * lax.rsqrt(ms + EPS) * g_ref[...]
    _store_row_tiles(hn_ref, h)
    h_hi = h.astype(BF16)
    h_lo = (h - h_hi.astype(F32)).astype(BF16)
    logits = _dot(h_hi, w_ref[0]) + _dot(h_lo, w_ref[0]) + _dot(h_hi, w_ref[1]) + b_ref[...]
    lane = lax.broadcasted_iota(I32, logits.shape, 1)
    is_grp = lane < N_GROUPS
    gl = jnp.where(is_grp, logits, NEG)
    gmax = jnp.max(gl, axis=-1, keepdims=True)
    lanef = lane.astype(F32)
    gidx = jnp.min(jnp.where(gl == gmax, lanef, float(LANES)), axis=-1, keepdims=True)
    gw = 1.0 / jnp.sum(jnp.where(is_grp, jnp.exp(logits - gmax), 0.0), axis=-1, keepdims=True)
    in_grp = ((lane - R_EXP) // EXP_PER_GROUP).astype(F32) == gidx
    w0 = jnp.where(in_grp, logits, NEG)
    m1 = jnp.max(w0, axis=-1, keepdims=True)
    i1 = jnp.min(jnp.where(w0 == m1, lanef, float(LANES)), axis=-1, keepdims=True)
    w1 = jnp.where(lanef == i1, NEG, w0)
    m2 = jnp.max(w1, axis=-1, keepdims=True)
    i2 = jnp.min(jnp.where(w1 == m2, lanef, float(LANES)), axis=-1, keepdims=True)
    e2 = jnp.exp(m2 - m1)
    p1 = gw / (1.0 + e2)
    p2 = gw * e2 / (1.0 + e2)
    oh1 = jnp.where(lanef == i1, 1.0, 0.0)
    oh2 = jnp.where(lanef == i2, 1.0, 0.0)
    oh = oh1 + oh2
    before = _dot(tri_ref[...], oh.astype(BF16)) + base_ref[0:1, :]
    r1 = jnp.sum(before * oh1, axis=-1, keepdims=True)
    r2 = jnp.sum(before * oh2, axis=-1, keepdims=True)
    base_ref[...] = base_ref[...] + jnp.sum(oh, axis=0, keepdims=True)
    route = jnp.zeros_like(logits)
    for ln, val in enumerate((i1 - R_EXP, i2 - R_EXP, p1, p2, r1, r2)):
        route = jnp.where(lane == ln, val, route)
    route_ref[...] = route
    compact_ref[0] = route.T[0:SUBLANES]


def moe_router(x, gain, w_grp, b_grp, w_exp, b_exp, tm=256):
    T, D = x.shape
    nt = T // tm
    w = jnp.zeros((D, LANES), F32).at[:, R_GRP:R_GRP + N_GROUPS].set(w_grp).at[:, R_EXP:R_EXP + N_EXPERTS].set(w_exp)
    b = jnp.zeros((1, LANES), F32).at[0, R_GRP:R_GRP + N_GROUPS].set(b_grp).at[0, R_EXP:R_EXP + N_EXPERTS].set(b_exp)
    w_hi = w.astype(BF16)
    w = jnp.stack([w_hi, (w - w_hi.astype(F32)).astype(BF16)])
    tri = jnp.asarray(np.tril(np.ones((tm, tm), np.float32), -1), dtype=BF16)
    return pl.pallas_call(
        _router_kernel,
        grid=(nt + 1,),
        in_specs=[pl.BlockSpec((tm, D), lambda i: (jnp.minimum(i, nt - 1), 0)),
                  pl.BlockSpec((1, D), lambda i: (0, 0)),
                  pl.BlockSpec((2, D, LANES), lambda i: (0, 0, 0)),
                  pl.BlockSpec((1, LANES), lambda i: (0, 0)),
                  pl.BlockSpec((tm, tm), lambda i: (0, 0))],
        out_specs=[pl.BlockSpec((tm * SUBLANES, LANES), lambda i: (i, 0)),
                   pl.BlockSpec((tm, LANES), lambda i: (i, 0)),
                   pl.BlockSpec((1, SUBLANES, tm), lambda i: (i, 0, 0)),
                   pl.BlockSpec((8, LANES), lambda i: (0, 0))],
        out_shape=[jax.ShapeDtypeStruct(((T + tm) * SUBLANES, LANES), F32),
                   jax.ShapeDtypeStruct((T + tm, LANES), F32),
                   jax.ShapeDtypeStruct((nt + 1, SUBLANES, tm), F32),
                   jax.ShapeDtypeStruct((8, LANES), F32)],
        scratch_shapes=[pltpu.VMEM((8, LANES), F32)],
        compiler_params=_cparams(("arbitrary",)),
        name="moe_router",
    )(x, gain.reshape(1, D), w, b, tri)


def _row_copy(src, src_row, dst, dst_row, sem):
    def tile(ref, row):
        return ref.at[pl.ds(pl.multiple_of(row * SUBLANES, SUBLANES), SUBLANES)]

    return pltpu.make_async_copy(tile(src, src_row), tile(dst, dst_row), sem)


DISPATCH_TILE = 512
ROW_UNROLL = 4


def _wait_rows(hbm_ref, sem, n):
    view = hbm_ref.at[pl.ds(0, n * SUBLANES)]
    pltpu.make_async_copy(view, view, sem).wait()


def _dispatch_kernel(dest_ref, x_ref, xb_ref, stage, sems):
    i = pl.program_id(0)
    last = pl.num_programs(0) - 1
    slot = i % 2

    @pl.when(i >= 2)
    def _():
        _wait_rows(xb_ref, sems.at[slot], DISPATCH_TILE)

    stage[slot] = x_ref[...]

    def issue(t, c):
        _row_copy(stage.at[slot], t, xb_ref, dest_ref[0, 0, 2 * t], sems.at[slot]).start(priority=0)
        _row_copy(stage.at[slot], t, xb_ref, dest_ref[0, 0, 2 * t + 1], sems.at[slot]).start(priority=1)
        return c

    lax.fori_loop(0, DISPATCH_TILE // 2, issue, 0, unroll=ROW_UNROLL)

    @pl.when(i == last)
    def _():
        _wait_rows(xb_ref, sems.at[slot], DISPATCH_TILE)

    @pl.when((i == last) & (i >= 1))
    def _():
        _wait_rows(xb_ref, sems.at[1 - slot], DISPATCH_TILE)


def moe_dispatch(hn, dest_all, n_tok):
    n_rows = dest_all.shape[0]
    tm = DISPATCH_TILE // 2
    nt = n_rows // DISPATCH_TILE
    n_tok_tiles = n_tok // tm
    return pl.pallas_call(
        _dispatch_kernel,
        grid=(nt,),
        in_specs=[pl.BlockSpec((1, 1, DISPATCH_TILE), lambda i: (i, 0, 0), memory_space=pltpu.SMEM),
                  pl.BlockSpec((tm * SUBLANES, LANES), lambda i: (jnp.minimum(i, n_tok_tiles), 0))],
        out_specs=pl.BlockSpec(memory_space=pl.ANY),
        out_shape=jax.ShapeDtypeStruct((n_rows * SUBLANES, LANES), F32),
        scratch_shapes=[pltpu.VMEM((2, tm * SUBLANES, LANES), F32), pltpu.SemaphoreType.DMA((2,))],
        compiler_params=_cparams(("arbitrary",)),
        name="moe_dispatch",
    )(dest_all.reshape(nt, 1, DISPATCH_TILE), hn)


def _expert_kernel(be_ref, x_ref, wg_ref, wu_ref, wd_ref, y_ref, wg_b, wu_b, wd_b):
    i = pl.program_id(0)

    @pl.when((i == 0) | (be_ref[i] != be_ref[jnp.maximum(i - 1, 0)]))
    def _():
        wg_b[...] = wg_ref[0, 0].astype(BF16)
        wu_b[...] = wu_ref[0, 0].astype(BF16)
        wd_b[...] = wd_ref[0, 0].astype(BF16)

    x = jnp.concatenate(_load_row_tiles(x_ref, MOE_BLOCK), axis=1).astype(BF16)
    hid = _dot(x, wg_b[...])
    hid = hid * jax.nn.sigmoid(hid) * _dot(x, wu_b[...])
    _store_row_tiles(y_ref, _dot(hid.astype(BF16), wd_b[...]))


def moe_experts(xb, blk_expert, layer, w_gate, w_up, w_down):
    n_rows = xb.shape[0] // SUBLANES
    D = w_gate.shape[2]
    blk = (MOE_BLOCK * SUBLANES, LANES)
    wspec = lambda shape: pl.BlockSpec((1, 1) + shape, lambda i, be: (layer, be[i], 0, 0))
    return pl.pallas_call(
        _expert_kernel,
        grid_spec=pltpu.PrefetchScalarGridSpec(
            num_scalar_prefetch=1,
            grid=(n_rows // MOE_BLOCK,),
            in_specs=[pl.BlockSpec(blk, lambda i, be: (i, 0)),
                      wspec((D, EXPERT_FF)), wspec((D, EXPERT_FF)), wspec((EXPERT_FF, D))],
            out_specs=pl.BlockSpec(blk, lambda i, be: (i, 0)),
            scratch_shapes=[pltpu.VMEM((D, EXPERT_FF), BF16), pltpu.VMEM((D, EXPERT_FF), BF16),
                            pltpu.VMEM((EXPERT_FF, D), BF16)]),
        out_shape=jax.ShapeDtypeStruct((n_rows * SUBLANES, LANES), F32),
        compiler_params=_cparams(("arbitrary",)),
        name="moe_experts",
    )(blk_expert, xb, w_gate, w_up, w_down)


def _combine_kernel(dest_ref, dest_next_ref, x_ref, route_ref, yb_ref, o_ref, ybuf, sems):
    tm = x_ref.shape[0]
    i = pl.program_id(0)
    slot = i % 2

    def gather(d_ref, s):
        def body(t, c):
            _row_copy(yb_ref, d_ref[0, 0, 2 * t], ybuf.at[s], t, sems.at[s]).start(priority=0)
            _row_copy(yb_ref, d_ref[0, 0, 2 * t + 1], ybuf.at[s], tm + t, sems.at[s]).start(priority=1)
            return c

        lax.fori_loop(0, tm, body, 0, unroll=ROW_UNROLL)

    @pl.when(i == 0)
    def _():
        gather(dest_ref, 0)

    @pl.when(i + 1 < pl.num_programs(0))
    def _():
        gather(dest_next_ref, 1 - slot)

    _wait_rows(yb_ref, sems.at[slot], 2 * tm)
    r = route_ref[...]
    w0, w1 = r[:, 2:3], r[:, 3:4]
    y0 = _load_row_tiles(ybuf.at[slot], tm)
    y1 = _load_row_tiles(ybuf.at[slot], tm, base=tm)
    for s in range(SUBLANES):
        cols = slice(s * LANES, (s + 1) * LANES)
        o_ref[:, cols] = x_ref[:, cols] + w0 * y0[s] + w1 * y1[s]


def moe_combine(x, yb, dest, route):
    T, D = x.shape
    tm = DISPATCH_TILE // 2
    nt = T // tm
    dest3 = dest.reshape(nt, 1, DISPATCH_TILE)
    return pl.pallas_call(
        _combine_kernel,
        grid=(nt,),
        in_specs=[pl.BlockSpec((1, 1, DISPATCH_TILE), lambda i: (i, 0, 0), memory_space=pltpu.SMEM),
                  pl.BlockSpec((1, 1, DISPATCH_TILE), lambda i: (jnp.minimum(i + 1, nt - 1), 0, 0),
                               memory_space=pltpu.SMEM),
                  pl.BlockSpec((tm, D), lambda i: (i, 0)),
                  pl.BlockSpec((tm, LANES), lambda i: (i, 0)),
                  pl.BlockSpec(memory_space=pl.ANY)],
        out_specs=pl.BlockSpec((tm, D), lambda i: (i, 0)),
        out_shape=jax.ShapeDtypeStruct((T, D), F32),
        scratch_shapes=[pltpu.VMEM((2, DISPATCH_TILE * SUBLANES, LANES), F32), pltpu.SemaphoreType.DMA((2,))],
        compiler_params=_cparams(("arbitrary",)),
        name="moe_combine",
    )(dest3, dest3, x, route, yb)


def hier_moe_block(x, layer, gain, w_grp, b_grp, w_exp, b_exp, w_gate, w_up, w_down):
    T, D = x.shape
    hn, route, compact, cnt = moe_router(x, gain, w_grp, b_grp, w_exp, b_exp)
    A = 2 * T
    n_rows = -(-A // MOE_BLOCK) * MOE_BLOCK + N_EXPERTS * MOE_BLOCK
    n_blocks = n_rows // MOE_BLOCK
    counts = cnt[0, R_EXP:R_EXP + N_EXPERTS].astype(I32)
    pcounts = (counts + MOE_BLOCK - 1) // MOE_BLOCK * MOE_BLOCK
    pend = jnp.cumsum(pcounts)
    pstart = pend - pcounts
    experts = jnp.arange(N_EXPERTS, dtype=I32)
    fields = compact[:-1].astype(I32)
    slot_e = jnp.stack([fields[:, 0], fields[:, 1]], axis=-1).reshape(T, 2)
    slot_rank = jnp.stack([fields[:, 4], fields[:, 5]], axis=-1).reshape(T, 2)
    seg_start = jnp.sum(jnp.where(slot_e[..., None] == experts, pstart, 0), axis=-1)
    dest = (slot_rank + seg_start).reshape(A)
    n_pad = n_rows - A
    pad_cnt = jnp.concatenate([pcounts - counts, (n_rows - pend[-1])[None]])
    pad_start = jnp.concatenate([pstart + counts, pend[-1:]])
    pad_end = jnp.cumsum(pad_cnt)
    k = jnp.arange(n_pad, dtype=I32)
    seg = (k[:, None] >= pad_end[None, :]).astype(I32).sum(axis=1)
    seg_hot = seg[:, None] == jnp.arange(N_EXPERTS + 1, dtype=I32)[None, :]
    pad_dest = k + jnp.sum(jnp.where(seg_hot, (pad_start - (pad_end - pad_cnt))[None, :], 0), axis=1)
    dest_all = jnp.concatenate([dest, pad_dest]).astype(I32)
    blk_start = jnp.arange(n_blocks, dtype=I32) * MOE_BLOCK
    blk_expert = jnp.minimum((pend[None, :] <= blk_start[:, None]).astype(I32).sum(axis=1), N_EXPERTS - 1)
    xb = moe_dispatch(hn, dest_all, T)
    yb = moe_experts(xb, blk_expert, layer, w_gate, w_up, w_down)
    return moe_combine(x, yb, dest, route)


def _even_w_in(w_in):
    ng = 3 * NSA_HPG
    pad = jnp.zeros((w_in.shape[0], LANES - ng), w_in.dtype)
    cols = [w_in[:, :C_GATE]]
    for g in range(NSA_KV):
        cols += [w_in[:, C_GATE + g * ng:C_GATE + (g + 1) * ng], pad]
    return jnp.concatenate(cols, axis=1).astype(BF16)


def even_layer(x, B, S, norm, w_in, lb, hg_gain, q_gain, k_gain, cmp_pe, cmp_w1, cmp_w2, w_out):
    T = B * S
    proj = rms_matmul(x, norm, _even_w_in(w_in), tn=N0 // 4).reshape(B, S, N0)
    a_out = hgrn2(proj, lb, hg_gain)
    kvb = C_KV // LANES
    qn, ksn, kwn, vsb, vwb = prep_heads(
        proj, [C_NQ // (NSA_HEADS * HD), kvb + 2, kvb + 4, kvb + 3, kvb + 5],
        [NSA_HEADS, NSA_KV, NSA_KV, NSA_KV, NSA_KV],
        [q_gain, k_gain[1], k_gain[2], None, None],
        ['pad', 'blk', 'pad', 'v', 'v'], [HD ** -0.5, 1.0, 1.0, 1.0, 1.0], S)
    kcb, vcb = nsa_compress(proj, kvb, kvb + 1, cmp_pe, cmp_w1, cmp_w2, k_gain[0])
    b_out = nsa_attention(qn, kcb, vcb, ksn, vsb, kwn, vwb, proj)
    return proj_res(a_out.reshape(T, HG_WIDTH), 0, b_out.reshape(T, NSA_HEADS * HD), 0, w_out.astype(BF16), x)


def odd_layer(x, B, S, norm, w_in, q_gain, k_gain, sinks, w_out):
    T = B * S
    proj = rms_matmul(x, norm, w_in.astype(BF16), tn=N1 // 2).reshape(B, S, N1)
    qd = SWA_HEADS * HD
    qn, kn, vb = prep_heads(
        proj, [0, qd // LANES, qd // LANES + 1], [SWA_HEADS, SWA_KV, SWA_KV],
        [q_gain, k_gain, None], ['qk', 'qk', 'v'], [HD ** -0.5, 1.0, 1.0], S)
    att = swa_attention(qn, kn, vb, sinks).reshape(T, qd)
    return proj_res(att, 0, att, 1, w_out.astype(BF16), x)


def kernel(x, ev_norm, ev_w_in, hg_lb_logits, hg_out_gain, nsa_q_gain, nsa_k_gain, nsa_cmp_pe, nsa_cmp_w1,
           nsa_cmp_w2, ev_w_out, od_norm, od_w_in, swa_q_gain, swa_k_gain, swa_sinks, od_w_out, moe_norm,
           moe_w_grp, moe_b_grp, moe_w_exp, moe_b_exp, moe_w_gate, moe_w_up, moe_w_down):
    B, S, D = x.shape
    lower_bounds = jnp.cumsum(jax.nn.softmax(hg_lb_logits.astype(F32), axis=0), axis=0)
    h = x.reshape(B * S, D)
    h = even_layer(h, B, S, ev_norm[0], ev_w_in[0], lower_bounds[0], hg_out_gain[0], nsa_q_gain[0],
                   nsa_k_gain[0], nsa_cmp_pe[0], nsa_cmp_w1[0], nsa_cmp_w2[0], ev_w_out[0])
    h = hier_moe_block(h, 0, moe_norm[0], moe_w_grp[0], moe_b_grp[0], moe_w_exp[0], moe_b_exp[0],
                       moe_w_gate, moe_w_up, moe_w_down)
    h = odd_layer(h, B, S, od_norm[0], od_w_in[0], swa_q_gain[0], swa_k_gain[0], swa_sinks[0], od_w_out[0])
    h = hier_moe_block(h, 1, moe_norm[1], moe_w_grp[1], moe_b_grp[1], moe_w_exp[1], moe_b_exp[1],
                       moe_w_gate, moe_w_up, moe_w_down)
    return h.reshape(B, S, D)
```

```python
import functools

import numpy as np
import jax
import jax.numpy as jnp
from jax import lax
from jax.experimental import pallas as pl
from jax.experimental.pallas import tpu as pltpu

F32 = jnp.float32
BF16 = jnp.bfloat16
I32 = jnp.int32

EPS = 1e-6
ROPE_THETA = 500000.0
HD = 64
ROT = HD // 4
HALF = ROT // 2

HG_DIM = 128
HG_WIDTH = 512
HG_HEADS = 4
HG_CHUNK = 256

NSA_HEADS = 8
NSA_KV = 2
NSA_HPG = NSA_HEADS // NSA_KV
CMP_LEN = 32
CMP_STRIDE = 16
CMP_HIDDEN = 256
SEL_LEN = 64
SEL_TOPN = 8
NSA_WINDOW = 512
FORCE_BONUS = 1e4

SWA_HEADS = 16
SWA_KV = 2
SWA_HPG = SWA_HEADS // SWA_KV
SWA_WINDOW = 128

N_GROUPS = 4
EXP_PER_GROUP = 8
N_EXPERTS = 32
EXPERT_FF = 512
MOE_BLOCK = 512

LANES = 128
NEG = -1e30
VMEM_V7X = 64 * 1024 * 1024
VMEM_LIMIT = VMEM_V7X * 7 // 8

C_KC = 2048
C_GATE = 2304
N_PLAIN = 2560


def _cparams(sem):
    return pltpu.CompilerParams(dimension_semantics=sem, vmem_limit_bytes=VMEM_LIMIT)


def _dot(a, b):
    return jnp.dot(a, b, preferred_element_type=F32)


def _dot_nt(a, b):
    return lax.dot_general(a, b, (((1,), (1,)), ((), ())), preferred_element_type=F32)


def _proj_res_kernel(a_ref, b_ref, wa_ref, wb_ref, r_ref, o_ref):
    acc = _dot(a_ref[...], wa_ref[...]) + _dot(b_ref[...], wb_ref[...])
    o_ref[...] = r_ref[...] + acc


def proj_res(a, a_blk, b, b_blk, w, res, tm=512):
    T, D = res.shape
    K = w.shape[0]
    kh = K // 2
    return pl.pallas_call(
        _proj_res_kernel,
        grid=(T // tm,),
        in_specs=[pl.BlockSpec((tm, kh), lambda i: (i, a_blk)),
                  pl.BlockSpec((tm, kh), lambda i: (i, b_blk)),
                  pl.BlockSpec((kh, D), lambda i: (0, 0)),
                  pl.BlockSpec((kh, D), lambda i: (1, 0)),
                  pl.BlockSpec((tm, D), lambda i: (i, 0))],
        out_specs=pl.BlockSpec((tm, D), lambda i: (i, 0)),
        out_shape=jax.ShapeDtypeStruct((T, D), F32),
        compiler_params=_cparams(("parallel",)),
        name="proj_res",
    )(a, b, w, w, res)


def _rope_tables(pos, width):
    inv = jnp.power(ROPE_THETA, -jnp.arange(HALF, dtype=F32) * 2.0 / ROT)
    ang = pos.astype(F32)[:, None] * inv[None, :]
    cos, sin = jnp.cos(ang), jnp.sin(ang)
    n = pos.shape[0]
    c = jnp.concatenate([cos, cos, jnp.ones((n, HD - ROT), F32)], axis=1)
    s = jnp.concatenate([-sin, sin, jnp.zeros((n, HD - ROT), F32)], axis=1)
    reps = width // HD
    return jnp.tile(c, (1, reps)), jnp.tile(s, (1, reps))


def _block_ones(width):
    idx = np.arange(width) // HD
    return jnp.asarray((idx[:, None] == idx[None, :]).astype(np.float32), dtype=BF16)


def _split_dot(a, b):
    hi = a.astype(BF16)
    lo = (a - hi.astype(F32)).astype(BF16)
    return _dot(hi, b) + _dot(lo, b)


def _head_norm_rope(x, gain, cos, sin, bones, scale):
    ms = _split_dot(x * x, bones) * (1.0 / HD)
    y = x * lax.rsqrt(ms + EPS) * gain
    lane = lax.broadcasted_iota(I32, y.shape, 1) % HD
    partner = jnp.where(lane < HALF, pltpu.roll(y, LANES - HALF, axis=1), pltpu.roll(y, HALF, axis=1))
    out = y * cos + partner * sin
    return out * scale if scale != 1.0 else out


def _in_proj_kernel(n_plain, tn, n_heads, kinds, scales, s_tiles, x_ref, g_ref, w_ref,
                    gain_ref, cos_ref, sin_ref, bones_ref, *o_refs):
    x = x_ref[...]
    ms = jnp.mean(x * x, axis=-1, keepdims=True)
    xn = (x * lax.rsqrt(ms + EPS) * g_ref[...]).astype(BF16)
    if n_plain:
        plain_ref, o_refs = o_refs[0], o_refs[1:]
        for j in range(n_plain // tn):
            plain_ref[:, j * tn:(j + 1) * tn] = _dot(xn, w_ref[:, j * tn:(j + 1) * tn])
    cos = cos_ref[...]
    sin = sin_ref[...]
    bones = bones_ref[...]
    tm = cos.shape[0]
    lane = lax.broadcasted_iota(I32, cos.shape, 1)
    pos = (pl.program_id(0) % s_tiles) * tm + lax.broadcasted_iota(I32, cos.shape, 0)
    col = n_plain
    for t in range(len(n_heads)):
        for c in range(n_heads[t] // 2):
            if c % (tn // LANES) == 0:
                width = min(tn, (n_heads[t] // 2 - c) * LANES)
                y_wide = _dot(xn, w_ref[:, col:col + width])
                col += width
            k = c % (tn // LANES)
            y2 = y_wide[:, k * LANES:(k + 1) * LANES]
            if kinds[t] != 'v':
                y2 = _head_norm_rope(y2, gain_ref[t:t + 1, :], cos, sin, bones, scales[t])
            for j, yh in enumerate((y2, pltpu.roll(y2, HD, axis=1))):
                if kinds[t] == 'qk':
                    y = yh[:, :HD]
                elif kinds[t] == 'pad':
                    y = jnp.where(lane < HD, yh, 0.0)
                elif kinds[t] == 'blk':
                    y = jnp.where(lane < HD, yh, jnp.where(lane - HD == pos // SEL_LEN, 1.0, 0.0))
                else:
                    y = jnp.where(lane < HD, yh, jnp.where(lane == HD, 1.0, 0.0)).T
                o_refs[t][0, 2 * c + j] = y.astype(BF16)


def in_proj(x, gain, w, n_plain, n_heads, head_gains, kinds, scales, B, S, tm=512, tn=512):
    T, D = x.shape
    N = w.shape[1]
    nt = len(n_heads)
    s_tiles = S // tm
    assert S // SEL_LEN <= LANES - HD and n_plain % tn == 0
    cos, sin = _rope_tables(jnp.arange(S), LANES)
    gain_arr = jnp.stack([jnp.tile((jnp.ones((HD,), F32) if g is None else g).astype(F32), 2) for g in head_gains])
    out_specs, out_shape = [], []
    if n_plain:
        out_specs.append(pl.BlockSpec((tm, n_plain), lambda i: (i, 0)))
        out_shape.append(jax.ShapeDtypeStruct((T, n_plain), F32))
    for t in range(nt):
        if kinds[t] == 'v':
            out_specs.append(pl.BlockSpec((1, n_heads[t], LANES, tm), lambda i: (i // s_tiles, 0, 0, i % s_tiles)))
            out_shape.append(jax.ShapeDtypeStruct((B, n_heads[t], LANES, S), BF16))
        else:
            width = HD if kinds[t] == 'qk' else LANES
            out_specs.append(pl.BlockSpec((1, n_heads[t], tm, width), lambda i: (i // s_tiles, 0, i % s_tiles, 0)))
            out_shape.append(jax.ShapeDtypeStruct((B, n_heads[t], S, width), BF16))
    return pl.pallas_call(
        functools.partial(_in_proj_kernel, n_plain, tn, tuple(n_heads), tuple(kinds), tuple(scales), s_tiles),
        grid=(T // tm,),
        in_specs=[pl.BlockSpec((tm, D), lambda i: (i, 0)),
                  pl.BlockSpec((1, D), lambda i: (0, 0)),
                  pl.BlockSpec((D, N), lambda i: (0, 0)),
                  pl.BlockSpec((nt, LANES), lambda i: (0, 0)),
                  pl.BlockSpec((tm, LANES), lambda i: (i % s_tiles, 0)),
                  pl.BlockSpec((tm, LANES), lambda i: (i % s_tiles, 0)),
                  pl.BlockSpec((LANES, LANES), lambda i: (0, 0))],
        out_specs=out_specs,
        out_shape=out_shape,
        compiler_params=_cparams(("parallel",)),
        name="in_proj",
    )(x, gain.reshape(1, D), w, gain_arr, cos, sin, _block_ones(LANES))


_N_LEVELS = HG_CHUNK.bit_length() - 1


def _hgrn_constants():
    C = HG_CHUNK
    r = np.arange(C)
    mats = [(r[None, :] <= r[:, None]).astype(np.float32),
            (r[None, :] > r[:, None]).astype(np.float32)]
    masks = []
    for lv in range(_N_LEVELS):
        n = 1 << lv
        blk = r // (2 * n)
        right = (r // n) % 2 == 1
        bnd = blk * 2 * n + n - 1
        m = np.zeros((C, C), np.float32)
        for t in range(C):
            if right[t]:
                m[t, bnd[t] + 1:t + 1] = 1.0
            else:
                m[t, t + 1:bnd[t] + 1] = 1.0
        mats.append(m)
        masks.append((blk[:, None] == blk[None, :]).astype(np.float32))
    masks.append(np.eye(C, dtype=np.float32))
    return jnp.asarray(np.concatenate(mats, axis=0), dtype=BF16), jnp.asarray(np.stack(masks))


def _hgrn_kernel(q_ref, f_ref, i_ref, g_ref, lb_ref, gain_ref, mall_ref, masks_ref, o_ref, st_ref):
    C = HG_CHUNK
    ts = q_ref.shape[1]

    @pl.when(pl.program_id(1) == 0)
    def _():
        st_ref[...] = jnp.zeros_like(st_ref)

    mall = mall_ref[...]
    row = lax.broadcasted_iota(I32, (C, HG_DIM), 0)
    gain = gain_ref[...]

    def chunk(c, carry):
        r0 = pl.multiple_of(c * C, C)
        for h in range(HG_HEADS):
            ls = slice(h * HG_DIM, (h + 1) * HG_DIM)
            q = q_ref[0, pl.ds(r0, C), ls]
            z = f_ref[0, pl.ds(r0, C), ls]
            v = i_ref[0, pl.ds(r0, C), ls]
            g = g_ref[0, pl.ds(r0, C), ls]
            lb = lb_ref[:, ls]
            lf = jnp.log(lb + (1.0 - lb) * jax.nn.sigmoid(z))
            k = (1.0 - lb) * jax.nn.sigmoid(-z)
            lf_hi = lf.astype(BF16)
            lf_lo = (lf - lf_hi.astype(F32)).astype(BF16)
            seg = _dot(mall, jnp.concatenate([lf_hi, lf_lo], axis=1))
            ex = jnp.exp(seg[:, :HG_DIM] + seg[:, HG_DIM:])
            qs = q * ex[0:C]
            ks = k * ex[C:2 * C]
            st = st_ref[h]
            o = _dot_nt(qs.astype(BF16), st.astype(BF16))
            a = masks_ref[_N_LEVELS] * _dot_nt(q.astype(BF16), k.astype(BF16))
            for lv in range(_N_LEVELS):
                e = ex[(2 + lv) * C:(3 + lv) * C]
                right = ((row >> lv) & 1) == 1
                qe = jnp.where(right, q * e, 0.0).astype(BF16)
                ke = jnp.where(right, 0.0, k * e).astype(BF16)
                a = a + masks_ref[lv] * _dot_nt(qe, ke)
            vb = v.astype(BF16)
            o = o + _dot(a.astype(BF16), vb)
            decay = ex[C - 1:C]
            st_ref[h] = st * decay + _dot(v.T.astype(BF16), ks.astype(BF16))
            ms = jnp.mean(o * o, axis=-1, keepdims=True)
            y = o * lax.rsqrt(ms + EPS) * gain * (g * jax.nn.sigmoid(g))
            o_ref[0, pl.ds(r0, C), ls] = y.astype(BF16)
        return carry

    lax.fori_loop(0, ts // C, chunk, 0)


def hgrn2(proj, lb, out_gain, ts=512):
    B, S, _ = proj.shape
    mall, masks = _hgrn_constants()
    W = HG_WIDTH
    return pl.pallas_call(
        _hgrn_kernel,
        grid=(B, S // ts),
        in_specs=[pl.BlockSpec((1, ts, W), lambda b, s: (b, s, 0)),
                  pl.BlockSpec((1, ts, W), lambda b, s: (b, s, 1)),
                  pl.BlockSpec((1, ts, W), lambda b, s: (b, s, 2)),
                  pl.BlockSpec((1, ts, W), lambda b, s: (b, s, 3)),
                  pl.BlockSpec((1, W), lambda b, s: (0, 0)),
                  pl.BlockSpec((1, HG_DIM), lambda b, s: (0, 0)),
                  pl.BlockSpec(mall.shape, lambda b, s: (0, 0)),
                  pl.BlockSpec(masks.shape, lambda b, s: (0, 0, 0))],
        out_specs=pl.BlockSpec((1, ts, W), lambda b, s: (b, s, 0)),
        out_shape=jax.ShapeDtypeStruct((B, S, W), BF16),
        scratch_shapes=[pltpu.VMEM((HG_HEADS, HG_DIM, HG_DIM), F32)],
        compiler_params=_cparams(("parallel", "arbitrary")),
        name="hgrn2",
    )(proj, proj, proj, proj, lb.reshape(1, W), out_gain.reshape(1, HG_DIM), mall, masks)


def _compress_kernel(xk_ref, xv_ref, w1k_ref, w1v_ref, pek_ref, pev_ref, w2k_ref, w2v_ref,
                     gain_ref, cos_ref, sin_ref, bones_ref, ok_ref, ov_ref):
    nc = xk_ref.shape[1] // CMP_STRIDE

    def hidden(x_ref, w1_ref, pe_ref):
        width = NSA_KV * CMP_HIDDEN
        u0 = jnp.zeros((nc, width), F32)
        u1 = jnp.zeros((nc, width), F32)
        const = jnp.zeros((8, width), F32)
        for p in range(CMP_STRIDE):
            xp = x_ref[0, pl.ds(p, nc, stride=CMP_STRIDE), :].astype(BF16)
            u0 = u0 + _dot(xp, w1_ref[p])
            u1 = u1 + _dot(xp, w1_ref[CMP_STRIDE + p])
            const = const + _dot(pe_ref[p].astype(BF16), w1_ref[p])
            const = const + _dot(pe_ref[CMP_STRIDE + p].astype(BF16), w1_ref[CMP_STRIDE + p])
        h = u0 + pltpu.roll(u1, nc - 1, axis=0) + const[0:1]
        return jax.nn.gelu(h).astype(BF16)

    yk = _dot(hidden(xk_ref, w1k_ref, pek_ref), w2k_ref[...])
    yk = _head_norm_rope(yk, gain_ref[...], cos_ref[...], sin_ref[...], bones_ref[...], 1.0)
    lane = lax.broadcasted_iota(I32, yk.shape, 1)
    ok_ref[0] = jnp.where(lane < HD, yk, 0.0).astype(BF16)
    ok_ref[1] = jnp.where(lane < HD, pltpu.roll(yk, HD, axis=1), 0.0).astype(BF16)
    yv_t = _dot_nt(w2v_ref[...], hidden(xv_ref, w1v_ref, pev_ref)).astype(BF16)
    ov_ref[0] = yv_t[:HD]
    ov_ref[1] = yv_t[HD:]


def _group_diag(w):
    z = jnp.zeros_like(w)
    return jnp.concatenate([jnp.concatenate([w, z], axis=-1), jnp.concatenate([z, w], axis=-1)], axis=-2)


def nsa_compress(proj, kc_blk, vc_blk, pe, w1, w2, k_gain):
    B, S, _ = proj.shape
    nc = S // CMP_STRIDE
    pos_end = jnp.arange(nc) * CMP_STRIDE + CMP_LEN - 1
    cos, sin = _rope_tables(pos_end, LANES)
    w1d = _group_diag(w1.reshape(2, CMP_LEN, HD, CMP_HIDDEN)).astype(BF16)
    w2d = _group_diag(w2).astype(BF16)
    pe2 = jnp.broadcast_to(jnp.tile(pe, (1, 1, NSA_KV))[:, :, None, :], (2, CMP_LEN, 8, LANES))
    full = lambda shape: pl.BlockSpec(shape, lambda b: (0,) * len(shape))
    w1_shape = (CMP_LEN, LANES, NSA_KV * CMP_HIDDEN)
    return pl.pallas_call(
        _compress_kernel,
        grid=(B,),
        in_specs=[pl.BlockSpec((1, S, LANES), lambda b: (b, 0, kc_blk)),
                  pl.BlockSpec((1, S, LANES), lambda b: (b, 0, vc_blk)),
                  full(w1_shape), full(w1_shape),
                  full((CMP_LEN, 8, LANES)), full((CMP_LEN, 8, LANES)),
                  full((NSA_KV * CMP_HIDDEN, LANES)), full((LANES, NSA_KV * CMP_HIDDEN)),
                  full((1, LANES)), full((nc, LANES)), full((nc, LANES)), full((LANES, LANES))],
        out_specs=[pl.BlockSpec((NSA_KV, nc, LANES), lambda b: (b, 0, 0)),
                   pl.BlockSpec((NSA_KV, HD, nc), lambda b: (b, 0, 0))],
        out_shape=[jax.ShapeDtypeStruct((B * NSA_KV, nc, LANES), BF16),
                   jax.ShapeDtypeStruct((B * NSA_KV, HD, nc), BF16)],
        compiler_params=_cparams(("parallel",)),
        name="nsa_compress",
    )(proj, proj, w1d[0], w1d[1], pe2[0], pe2[1], w2d[0], w2d[1].T,
      jnp.tile(k_gain.astype(F32), 2).reshape(1, LANES), cos, sin, _block_ones(LANES))


NSA_TQ = 256
NSA_TK = 1024


def _nsa_kernel(q_ref, kc_ref, vc_ref, ks_ref, vs_ref, kw_ref, vw_ref, gate_ref, cover_ref, place_ref,
                o_ref, acc_ref):
    TQ, TK, H = NSA_TQ, NSA_TK, NSA_HPG
    R = H * TQ
    t0 = pl.program_id(1) * TQ
    tl = t0 + lax.broadcasted_iota(I32, (1, TQ), 1)
    ncmp = kc_ref.shape[1]
    nsel = cover_ref.shape[0]

    def all_heads(x):
        return jnp.concatenate([x] * H, axis=1)

    def select(g):
        q = q_ref[0, g * H:(g + 1) * H].reshape(R, LANES)
        cmp_end = lax.broadcasted_iota(I32, (ncmp, 1), 0) * CMP_STRIDE + (CMP_LEN - 1)
        valid = all_heads(jnp.where(cmp_end <= tl, 1.0, 0.0))
        s = jnp.where(valid > 0.5, _dot_nt(kc_ref[g], q), NEG)
        p = jnp.exp(s - jnp.max(s, axis=0, keepdims=True)) * valid
        p_c = p / jnp.maximum(jnp.sum(p, axis=0, keepdims=True), 1e-30)
        o_c = _dot(vc_ref[g], p_c.astype(BF16))
        p_tok = p_c[:, 0:TQ]
        for h in range(1, H):
            p_tok = p_tok + p_c[:, h * TQ:(h + 1) * TQ]
        p_hi = p_tok.astype(BF16)
        p_lo = (p_tok - p_hi.astype(F32)).astype(BF16)
        imp = _dot(cover_ref[...], p_hi) + _dot(cover_ref[...], p_lo)
        jj = lax.broadcasted_iota(I32, (nsel, TQ), 0)
        jf = jj.astype(F32)
        cur = tl // SEL_LEN
        allowed = jj * SEL_LEN <= tl
        forced = (jj == 0) | (jj == cur) | (jj == cur - 1)
        score = jnp.where(allowed, imp + jnp.where(forced, FORCE_BONUS, 0.0), NEG)
        sel = jnp.zeros((nsel, TQ), F32)
        for _ in range(min(SEL_TOPN, nsel)):
            mx = jnp.max(score, axis=0, keepdims=True)
            first = jnp.min(jnp.where(score == mx, jf, float(nsel)), axis=0, keepdims=True)
            hit = jf == first
            sel = jnp.where(hit & (mx > 0.5 * NEG), 1.0, sel)
            score = jnp.where(hit, NEG, score)
        block_bias = jnp.where(sel > 0.5, 0.0, NEG).T.astype(BF16)
        placed = _dot(block_bias, place_ref[...]).astype(BF16)
        return o_c, q, q + jnp.concatenate([placed] * H, axis=0)

    def sweep_tile(g, q_sel, kt, m_old, causal):
        k0 = pl.multiple_of(kt * TK, TK)
        s = _dot_nt(ks_ref[0, g, pl.ds(k0, TK), :], q_sel)
        if causal:
            kpos = k0 + lax.broadcasted_iota(I32, (TK, 1), 0)
            s = s + all_heads(jnp.where(kpos <= tl, 0.0, NEG))
        m_new = jnp.maximum(m_old, jnp.max(s, axis=0, keepdims=True))
        p = jnp.exp(s - m_new).astype(BF16)
        acc_ref[g] = jnp.exp(m_old - m_new) * acc_ref[g] + _dot(vs_ref[0, g, :, pl.ds(k0, TK)], p)
        return m_new

    def window(g, q):
        WK = NSA_WINDOW + TQ
        ws = pl.multiple_of(jnp.maximum(t0 - NSA_WINDOW, 0), TQ)
        kp = ws + lax.broadcasted_iota(I32, (WK, 1), 0)
        wbias = jnp.where((kp <= tl) & (kp > tl - NSA_WINDOW), 0.0, NEG)
        s = _dot_nt(kw_ref[0, g, pl.ds(ws, WK), :], q) + all_heads(wbias)
        p = jnp.exp(s - jnp.max(s, axis=0, keepdims=True)).astype(BF16)
        acc_w = _dot(vw_ref[0, g, :, pl.ds(ws, WK)], p)
        return acc_w[:HD] / acc_w[HD:HD + 1]

    groups = range(NSA_KV)
    chosen = [select(g) for g in groups]
    acc_ref[...] = jnp.zeros(acc_ref.shape, F32)
    last = t0 // TK
    m_run = lax.fori_loop(
        0, last, lambda kt, ms: tuple(sweep_tile(g, chosen[g][2], kt, ms[g], False) for g in groups),
        tuple(jnp.full((1, R), NEG, F32) for _ in groups))
    heads_out = []
    for g in groups:
        o_c, q, q_sel = chosen[g]
        sweep_tile(g, q_sel, last, m_run[g], True)
        acc = acc_ref[g]
        o_s = acc[:HD] / jnp.maximum(acc[HD:HD + 1], 1e-30)
        o_w = window(g, q)
        sg = jax.nn.sigmoid(gate_ref[0, :, g * LANES:(g + 1) * LANES]).T
        for h in range(H):
            cols = slice(h * TQ, (h + 1) * TQ)
            heads_out.append(sg[3 * h:3 * h + 1] * o_c[:, cols] + sg[3 * h + 1:3 * h + 2] * o_s[:, cols]
                             + sg[3 * h + 2:3 * h + 3] * o_w[:, cols])
    o_ref[0] = jnp.concatenate(heads_out, axis=0).T.astype(BF16)


def nsa_attention(qn, kcb, vcb, ksn, vsb, kwn, vwb, proj):
    B, _, S, _ = qn.shape
    G, H, TQ = NSA_KV, NSA_HPG, NSA_TQ
    assert S % NSA_TK == 0 and S >= NSA_WINDOW + TQ
    ncmp = kcb.shape[1]
    nsel = S // SEL_LEN
    n = np.arange(ncmp)[:, None] * CMP_STRIDE
    j = np.arange(nsel)[None, :] * SEL_LEN
    cover = jnp.asarray(((n < j + SEL_LEN) & (n + CMP_LEN > j)).astype(np.float32).T, dtype=BF16)
    place = np.zeros((nsel, LANES), np.float32)
    place[np.arange(nsel), HD + np.arange(nsel)] = 1.0
    R = H * TQ
    k_spec = pl.BlockSpec((1, G, S, LANES), lambda b, i: (b, 0, 0, 0))
    v_spec = pl.BlockSpec((1, G, LANES, S), lambda b, i: (b, 0, 0, 0))
    assert C_GATE % (G * LANES) == 0
    return pl.pallas_call(
        _nsa_kernel,
        grid=(B, S // TQ),
        in_specs=[pl.BlockSpec((1, G * H, TQ, LANES), lambda b, i: (b, 0, i, 0)),
                  pl.BlockSpec((G, ncmp, LANES), lambda b, i: (b, 0, 0)),
                  pl.BlockSpec((G, HD, ncmp), lambda b, i: (b, 0, 0)),
                  k_spec, v_spec, k_spec, v_spec,
                  pl.BlockSpec((1, TQ, G * LANES), lambda b, i: (b, i, C_GATE // (G * LANES))),
                  pl.BlockSpec((nsel, ncmp), lambda b, i: (0, 0)),
                  pl.BlockSpec((nsel, LANES), lambda b, i: (0, 0))],
        out_specs=pl.BlockSpec((1, TQ, G * H * HD), lambda b, i: (b, i, 0)),
        out_shape=jax.ShapeDtypeStruct((B, S, NSA_HEADS * HD), BF16),
        scratch_shapes=[pltpu.VMEM((G, LANES, R), F32)],
        compiler_params=_cparams(("parallel", "arbitrary")),
        name="nsa_attention",
    )(qn, kcb, vcb, ksn, vsb, kwn, vwb, proj, cover, jnp.asarray(place, dtype=BF16))


SWA_TQ = 256


def _swa_kernel(sink_ref, q_ref, k_ref, v_ref, o_ref):
    TQ, H = SWA_TQ, SWA_HPG
    R = H * TQ
    g = pl.program_id(1)
    t0 = pl.program_id(2) * TQ
    q = q_ref[0].reshape(R, HD)
    tl = t0 + lax.broadcasted_iota(I32, (1, TQ), 1)
    WK = SWA_WINDOW + TQ
    ws = pl.multiple_of(jnp.maximum(t0 - SWA_WINDOW, 0), LANES)
    kp = ws + lax.broadcasted_iota(I32, (WK, 1), 0)
    bias = jnp.where((kp <= tl) & (kp > tl - SWA_WINDOW), 0.0, NEG)
    s = _dot_nt(k_ref[0, 0, pl.ds(ws, WK), :], q) + jnp.concatenate([bias] * H, axis=1)
    head = lax.broadcasted_iota(I32, (1, R), 1) // TQ
    sink = jnp.zeros((1, R), F32)
    for h in range(H):
        sink = jnp.where(head == h, sink_ref[g * H + h], sink)
    m = jnp.maximum(jnp.max(s, axis=0, keepdims=True), sink)
    p = jnp.exp(s - m).astype(BF16)
    acc = _dot(v_ref[0, 0, :, pl.ds(ws, WK)], p)
    o = acc[:HD] / (acc[HD:HD + 1] + jnp.exp(sink - m))
    o_ref[0] = jnp.concatenate([o[:, h * TQ:(h + 1) * TQ] for h in range(H)], axis=0).T.astype(BF16)


def swa_attention(qn, kn, vb, sinks):
    B, _, S, _ = qn.shape
    G, H, TQ = SWA_KV, SWA_HPG, SWA_TQ
    assert S >= SWA_WINDOW + TQ
    k_spec = pl.BlockSpec((1, 1, S, HD), lambda b, g, i, sk: (b, g, 0, 0))
    v_spec = pl.BlockSpec((1, 1, LANES, S), lambda b, g, i, sk: (b, g, 0, 0))
    return pl.pallas_call(
        _swa_kernel,
        grid_spec=pltpu.PrefetchScalarGridSpec(
            num_scalar_prefetch=1,
            grid=(B, G, S // TQ),
            in_specs=[pl.BlockSpec((1, H, TQ, HD), lambda b, g, i, sk: (b, g, i, 0)), k_spec, v_spec],
            out_specs=pl.BlockSpec((1, TQ, H * HD), lambda b, g, i, sk: (b, i, g))),
        out_shape=jax.ShapeDtypeStruct((B, S, SWA_HEADS * HD), BF16),
        compiler_params=_cparams(("parallel", "parallel", "arbitrary")),
        name="swa_attention",
    )(sinks.astype(F32), qn, kn, vb)


R_GRP = 0
R_EXP = 32


SUBLANES = 8


def _store_row_tiles(ref, val, base=0):
    for s in range(SUBLANES):
        ref[pl.ds(base * SUBLANES + s, val.shape[0], stride=SUBLANES), :] = val[:, s * LANES:(s + 1) * LANES]


def _load_row_tiles(ref, rows, base=0):
    return [ref[pl.ds(base * SUBLANES + s, rows, stride=SUBLANES), :] for s in range(SUBLANES)]


def _router_kernel(x_ref, g_ref, w_ref, b_ref, tri_ref, hn_ref, route_ref, compact_ref, cnt_ref, base_ref):
    i = pl.program_id(0)
    last = pl.num_programs(0) - 1

    @pl.when(i == 0)
    def _():
        base_ref[...] = jnp.zeros_like(base_ref)

    @pl.when(i < last)
    def _():
        _route_tile(x_ref, g_ref, w_ref, b_ref, tri_ref, hn_ref, route_ref, compact_ref, base_ref)

    @pl.when(i == last)
    def _():
        hn_ref[...] = jnp.zeros_like(hn_ref)
        route_ref[...] = jnp.zeros_like(route_ref)
        compact_ref[...] = jnp.zeros_like(compact_ref)

    cnt_ref[...] = base_ref[...]


def _route_tile(x_ref, g_ref, w_ref, b_ref, tri_ref, hn_ref, route_ref, compact_ref, base_ref):
    x = x_ref[...]
    ms = jnp.mean(x * x, axis=-1, keepdims=True)
    h = x * lax.rsqrt(ms + EPS) * g_ref[...]
    _store_row_tiles(hn_ref, h)
    h_hi = h.astype(BF16)
    h_lo = (h - h_hi.astype(F32)).astype(BF16)
    logits = _dot(h_hi, w_ref[0]) + _dot(h_lo, w_ref[0]) + _dot(h_hi, w_ref[1]) + b_ref[...]
    lane = lax.broadcasted_iota(I32, logits.shape, 1)
    is_grp = lane < N_GROUPS
    gl = jnp.where(is_grp, logits, NEG)
    gmax = jnp.max(gl, axis=-1, keepdims=True)
    lanef = lane.astype(F32)
    gidx = jnp.min(jnp.where(gl == gmax, lanef, float(LANES)), axis=-1, keepdims=True)
    gw = 1.0 / jnp.sum(jnp.where(is_grp, jnp.exp(logits - gmax), 0.0), axis=-1, keepdims=True)
    in_grp = ((lane - R_EXP) // EXP_PER_GROUP).astype(F32) == gidx
    w0 = jnp.where(in_grp, logits, NEG)
    m1 = jnp.max(w0, axis=-1, keepdims=True)
    i1 = jnp.min(jnp.where(w0 == m1, lanef, float(LANES)), axis=-1, keepdims=True)
    w1 = jnp.where(lanef == i1, NEG, w0)
    m2 = jnp.max(w1, axis=-1, keepdims=True)
    i2 = jnp.min(jnp.where(w1 == m2, lanef, float(LANES)), axis=-1, keepdims=True)
    e2 = jnp.exp(m2 - m1)
    p1 = gw / (1.0 + e2)
    p2 = gw * e2 / (1.0 + e2)
    oh1 = jnp.where(lanef == i1, 1.0, 0.0)
    oh2 = jnp.where(lanef == i2, 1.0, 0.0)
    oh = oh1 + oh2
    before = _dot(tri_ref[...], oh.astype(BF16)) + base_ref[0:1, :]
    r1 = jnp.sum(before * oh1, axis=-1, keepdims=True)
    r2 = jnp.sum(before * oh2, axis=-1, keepdims=True)
    base_ref[...] = base_ref[...] + jnp.sum(oh, axis=0, keepdims=True)
    route = jnp.zeros_like(logits)
    for ln, val in enumerate((i1 - R_EXP, i2 - R_EXP, p1, p2, r1, r2)):
        route = jnp.where(lane == ln, val, route)
    route_ref[...] = route
    compact_ref[0] = route.T[0:SUBLANES]


def moe_router(x, gain, w_grp, b_grp, w_exp, b_exp, tm=256):
    T, D = x.shape
    nt = T // tm
    w = jnp.zeros((D, LANES), F32).at[:, R_GRP:R_GRP + N_GROUPS].set(w_grp).at[:, R_EXP:R_EXP + N_EXPERTS].set(w_exp)
    b = jnp.zeros((1, LANES), F32).at[0, R_GRP:R_GRP + N_GROUPS].set(b_grp).at[0, R_EXP:R_EXP + N_EXPERTS].set(b_exp)
    w_hi = w.astype(BF16)
    w = jnp.stack([w_hi, (w - w_hi.astype(F32)).astype(BF16)])
    tri = jnp.asarray(np.tril(np.ones((tm, tm), np.float32), -1), dtype=BF16)
    return pl.pallas_call(
        _router_kernel,
        grid=(nt + 1,),
        in_specs=[pl.BlockSpec((tm, D), lambda i: (jnp.minimum(i, nt - 1), 0)),
                  pl.BlockSpec((1, D), lambda i: (0, 0)),
                  pl.BlockSpec((2, D, LANES), lambda i: (0, 0, 0)),
                  pl.BlockSpec((1, LANES), lambda i: (0, 0)),
                  pl.BlockSpec((tm, tm), lambda i: (0, 0))],
        out_specs=[pl.BlockSpec((tm * SUBLANES, LANES), lambda i: (i, 0)),
                   pl.BlockSpec((tm, LANES), lambda i: (i, 0)),
                   pl.BlockSpec((1, SUBLANES, tm), lambda i: (i, 0, 0)),
                   pl.BlockSpec((8, LANES), lambda i: (0, 0))],
        out_shape=[jax.ShapeDtypeStruct(((T + tm) * SUBLANES, LANES), F32),
                   jax.ShapeDtypeStruct((T + tm, LANES), F32),
                   jax.ShapeDtypeStruct((nt + 1, SUBLANES, tm), F32),
                   jax.ShapeDtypeStruct((8, LANES), F32)],
        scratch_shapes=[pltpu.VMEM((8, LANES), F32)],
        compiler_params=_cparams(("arbitrary",)),
        name="moe_router",
    )(x, gain.reshape(1, D), w, b, tri)


def _row_copy(src, src_row, dst, dst_row, sem):
    def tile(ref, row):
        return ref.at[pl.ds(pl.multiple_of(row * SUBLANES, SUBLANES), SUBLANES)]

    return pltpu.make_async_copy(tile(src, src_row), tile(dst, dst_row), sem)


DISPATCH_TILE = 512
ROW_UNROLL = 4


def _wait_rows(hbm_ref, sem, n):
    view = hbm_ref.at[pl.ds(0, n * SUBLANES)]
    pltpu.make_async_copy(view, view, sem).wait()


def _dispatch_kernel(dest_ref, x_ref, xb_ref, stage, sems):
    i = pl.program_id(0)
    last = pl.num_programs(0) - 1
    slot = i % 2

    @pl.when(i >= 2)
    def _():
        _wait_rows(xb_ref, sems.at[slot], DISPATCH_TILE)

    stage[slot] = x_ref[...]

    def issue(t, c):
        _row_copy(stage.at[slot], t, xb_ref, dest_ref[0, 0, 2 * t], sems.at[slot]).start(priority=0)
        _row_copy(stage.at[slot], t, xb_ref, dest_ref[0, 0, 2 * t + 1], sems.at[slot]).start(priority=1)
        return c

    lax.fori_loop(0, DISPATCH_TILE // 2, issue, 0, unroll=ROW_UNROLL)

    @pl.when(i == last)
    def _():
        _wait_rows(xb_ref, sems.at[slot], DISPATCH_TILE)

    @pl.when((i == last) & (i >= 1))
    def _():
        _wait_rows(xb_ref, sems.at[1 - slot], DISPATCH_TILE)


def moe_dispatch(hn, dest_all, n_tok):
    n_rows = dest_all.shape[0]
    tm = DISPATCH_TILE // 2
    nt = n_rows // DISPATCH_TILE
    n_tok_tiles = n_tok // tm
    return pl.pallas_call(
        _dispatch_kernel,
        grid=(nt,),
        in_specs=[pl.BlockSpec((1, 1, DISPATCH_TILE), lambda i: (i, 0, 0), memory_space=pltpu.SMEM),
                  pl.BlockSpec((tm * SUBLANES, LANES), lambda i: (jnp.minimum(i, n_tok_tiles), 0))],
        out_specs=pl.BlockSpec(memory_space=pl.ANY),
        out_shape=jax.ShapeDtypeStruct((n_rows * SUBLANES, LANES), F32),
        scratch_shapes=[pltpu.VMEM((2, tm * SUBLANES, LANES), F32), pltpu.SemaphoreType.DMA((2,))],
        compiler_params=_cparams(("arbitrary",)),
        name="moe_dispatch",
    )(dest_all.reshape(nt, 1, DISPATCH_TILE), hn)


def _expert_kernel(be_ref, x_ref, wg_ref, wu_ref, wd_ref, y_ref, wg_b, wu_b, wd_b):
    i = pl.program_id(0)

    @pl.when((i == 0) | (be_ref[i] != be_ref[jnp.maximum(i - 1, 0)]))
    def _():
        wg_b[...] = wg_ref[0, 0].astype(BF16)
        wu_b[...] = wu_ref[0, 0].astype(BF16)
        wd_b[...] = wd_ref[0, 0].astype(BF16)

    x = jnp.concatenate(_load_row_tiles(x_ref, MOE_BLOCK), axis=1).astype(BF16)
    hid = _dot(x, wg_b[...])
    hid = hid * jax.nn.sigmoid(hid) * _dot(x, wu_b[...])
    _store_row_tiles(y_ref, _dot(hid.astype(BF16), wd_b[...]))


def moe_experts(xb, blk_expert, layer, w_gate, w_up, w_down):
    n_rows = xb.shape[0] // SUBLANES
    D = w_gate.shape[2]
    blk = (MOE_BLOCK * SUBLANES, LANES)
    wspec = lambda shape: pl.BlockSpec((1, 1) + shape, lambda i, be: (layer, be[i], 0, 0))
    return pl.pallas_call(
        _expert_kernel,
        grid_spec=pltpu.PrefetchScalarGridSpec(
            num_scalar_prefetch=1,
            grid=(n_rows // MOE_BLOCK,),
            in_specs=[pl.BlockSpec(blk, lambda i, be: (i, 0)),
                      wspec((D, EXPERT_FF)), wspec((D, EXPERT_FF)), wspec((EXPERT_FF, D))],
            out_specs=pl.BlockSpec(blk, lambda i, be: (i, 0)),
            scratch_shapes=[pltpu.VMEM((D, EXPERT_FF), BF16), pltpu.VMEM((D, EXPERT_FF), BF16),
                            pltpu.VMEM((EXPERT_FF, D), BF16)]),
        out_shape=jax.ShapeDtypeStruct((n_rows * SUBLANES, LANES), F32),
        compiler_params=_cparams(("arbitrary",)),
        name="moe_experts",
    )(blk_expert, xb, w_gate, w_up, w_down)


def _combine_kernel(dest_ref, dest_next_ref, x_ref, route_ref, yb_ref, o_ref, ybuf, sems):
    tm = x_ref.shape[0]
    i = pl.program_id(0)
    slot = i % 2

    def gather(d_ref, s):
        def body(t, c):
            _row_copy(yb_ref, d_ref[0, 0, 2 * t], ybuf.at[s], t, sems.at[s]).start(priority=0)
            _row_copy(yb_ref, d_ref[0, 0, 2 * t + 1], ybuf.at[s], tm + t, sems.at[s]).start(priority=1)
            return c

        lax.fori_loop(0, tm, body, 0, unroll=ROW_UNROLL)

    @pl.when(i == 0)
    def _():
        gather(dest_ref, 0)

    @pl.when(i + 1 < pl.num_programs(0))
    def _():
        gather(dest_next_ref, 1 - slot)

    _wait_rows(yb_ref, sems.at[slot], 2 * tm)
    r = route_ref[...]
    w0, w1 = r[:, 2:3], r[:, 3:4]
    y0 = _load_row_tiles(ybuf.at[slot], tm)
    y1 = _load_row_tiles(ybuf.at[slot], tm, base=tm)
    for s in range(SUBLANES):
        cols = slice(s * LANES, (s + 1) * LANES)
        o_ref[:, cols] = x_ref[:, cols] + w0 * y0[s] + w1 * y1[s]


def moe_combine(x, yb, dest, route):
    T, D = x.shape
    tm = DISPATCH_TILE // 2
    nt = T // tm
    dest3 = dest.reshape(nt, 1, DISPATCH_TILE)
    return pl.pallas_call(
        _combine_kernel,
        grid=(nt,),
        in_specs=[pl.BlockSpec((1, 1, DISPATCH_TILE), lambda i: (i, 0, 0), memory_space=pltpu.SMEM),
                  pl.BlockSpec((1, 1, DISPATCH_TILE), lambda i: (jnp.minimum(i + 1, nt - 1), 0, 0),
                               memory_space=pltpu.SMEM),
                  pl.BlockSpec((tm, D), lambda i: (i, 0)),
                  pl.BlockSpec((tm, LANES), lambda i: (i, 0)),
                  pl.BlockSpec(memory_space=pl.ANY)],
        out_specs=pl.BlockSpec((tm, D), lambda i: (i, 0)),
        out_shape=jax.ShapeDtypeStruct((T, D), F32),
        scratch_shapes=[pltpu.VMEM((2, DISPATCH_TILE * SUBLANES, LANES), F32), pltpu.SemaphoreType.DMA((2,))],
        compiler_params=_cparams(("arbitrary",)),
        name="moe_combine",
    )(dest3, dest3, x, route, yb)


def hier_moe_block(x, layer, gain, w_grp, b_grp, w_exp, b_exp, w_gate, w_up, w_down):
    T, D = x.shape
    hn, route, compact, cnt = moe_router(x, gain, w_grp, b_grp, w_exp, b_exp)
    A = 2 * T
    n_rows = -(-A // MOE_BLOCK) * MOE_BLOCK + N_EXPERTS * MOE_BLOCK
    n_blocks = n_rows // MOE_BLOCK
    counts = cnt[0, R_EXP:R_EXP + N_EXPERTS].astype(I32)
    pcounts = (counts + MOE_BLOCK - 1) // MOE_BLOCK * MOE_BLOCK
    pend = jnp.cumsum(pcounts)
    pstart = pend - pcounts
    experts = jnp.arange(N_EXPERTS, dtype=I32)
    fields = compact[:-1].astype(I32)
    slot_e = jnp.stack([fields[:, 0], fields[:, 1]], axis=-1).reshape(T, 2)
    slot_rank = jnp.stack([fields[:, 4], fields[:, 5]], axis=-1).reshape(T, 2)
    seg_start = jnp.sum(jnp.where(slot_e[..., None] == experts, pstart, 0), axis=-1)
    dest = (slot_rank + seg_start).reshape(A)
    n_pad = n_rows - A
    pad_cnt = jnp.concatenate([pcounts - counts, (n_rows - pend[-1])[None]])
    pad_start = jnp.concatenate([pstart + counts, pend[-1:]])
    pad_end = jnp.cumsum(pad_cnt)
    k = jnp.arange(n_pad, dtype=I32)
    seg = (k[:, None] >= pad_end[None, :]).astype(I32).sum(axis=1)
    seg_hot = seg[:, None] == jnp.arange(N_EXPERTS + 1, dtype=I32)[None, :]
    pad_dest = k + jnp.sum(jnp.where(seg_hot, (pad_start - (pad_end - pad_cnt))[None, :], 0), axis=1)
    dest_all = jnp.concatenate([dest, pad_dest]).astype(I32)
    blk_start = jnp.arange(n_blocks, dtype=I32) * MOE_BLOCK
    blk_expert = jnp.minimum((pend[None, :] <= blk_start[:, None]).astype(I32).sum(axis=1), N_EXPERTS - 1)
    xb = moe_dispatch(hn, dest_all, T)
    yb = moe_experts(xb, blk_expert, layer, w_gate, w_up, w_down)
    return moe_combine(x, yb, dest, route)


def _even_w_in(w_in):
    hgw, qw, kvw, ng = 4 * HG_WIDTH, NSA_HEADS * HD, NSA_KV * HD, 3 * NSA_HPG
    kc, vc, ks, vs, kw, vw = (hgw + qw + i * kvw for i in range(6))
    gate = hgw + qw + 6 * kvw
    pad = jnp.zeros((w_in.shape[0], LANES - ng), w_in.dtype)
    cols = [w_in[:, :hgw], w_in[:, kc:kc + kvw], w_in[:, vc:vc + kvw]]
    for g in range(NSA_KV):
        cols += [w_in[:, gate + g * ng:gate + (g + 1) * ng], pad]
    cols += [w_in[:, hgw:hgw + qw]] + [w_in[:, c:c + kvw] for c in (ks, kw, vs, vw)]
    return jnp.concatenate(cols, axis=1).astype(BF16)


def even_layer(x, B, S, norm, w_in, lb, hg_gain, q_gain, k_gain, cmp_pe, cmp_w1, cmp_w2, w_out):
    T = B * S
    proj, qn, ksn, kwn, vsb, vwb = in_proj(
        x, norm, _even_w_in(w_in), N_PLAIN, [NSA_HEADS, NSA_KV, NSA_KV, NSA_KV, NSA_KV],
        [q_gain, k_gain[1], k_gain[2], None, None], ['pad', 'blk', 'pad', 'v', 'v'],
        [HD ** -0.5, 1.0, 1.0, 1.0, 1.0], B, S)
    proj = proj.reshape(B, S, N_PLAIN)
    a_out = hgrn2(proj, lb, hg_gain)
    kcb, vcb = nsa_compress(proj, C_KC // LANES, C_KC // LANES + 1, cmp_pe, cmp_w1, cmp_w2, k_gain[0])
    b_out = nsa_attention(qn, kcb, vcb, ksn, vsb, kwn, vwb, proj)
    return proj_res(a_out.reshape(T, HG_WIDTH), 0, b_out.reshape(T, NSA_HEADS * HD), 0, w_out.astype(BF16), x)


def odd_layer(x, B, S, norm, w_in, q_gain, k_gain, sinks, w_out):
    T = B * S
    qn, kn, vb = in_proj(x, norm, w_in.astype(BF16), 0, [SWA_HEADS, SWA_KV, SWA_KV], [q_gain, k_gain, None],
                         ['qk', 'qk', 'v'], [HD ** -0.5, 1.0, 1.0], B, S)
    att = swa_attention(qn, kn, vb, sinks).reshape(T, SWA_HEADS * HD)
    return proj_res(att, 0, att, 1, w_out.astype(BF16), x)


def kernel(x, ev_norm, ev_w_in, hg_lb_logits, hg_out_gain, nsa_q_gain, nsa_k_gain, nsa_cmp_pe, nsa_cmp_w1,
           nsa_cmp_w2, ev_w_out, od_norm, od_w_in, swa_q_gain, swa_k_gain, swa_sinks, od_w_out, moe_norm,
           moe_w_grp, moe_b_grp, moe_w_exp, moe_b_exp, moe_w_gate, moe_w_up, moe_w_down):
    B, S, D = x.shape
    lower_bounds = jnp.cumsum(jax.nn.softmax(hg_lb_logits.astype(F32), axis=0), axis=0)
    h = x.reshape(B * S, D)
    h = even_layer(h, B, S, ev_norm[0], ev_w_in[0], lower_bounds[0], hg_out_gain[0], nsa_q_gain[0],
                   nsa_k_gain[0], nsa_cmp_pe[0], nsa_cmp_w1[0], nsa_cmp_w2[0], ev_w_out[0])
    h = hier_moe_block(h, 0, moe_norm[0], moe_w_grp[0], moe_b_grp[0], moe_w_exp[0], moe_b_exp[0],
                       moe_w_gate, moe_w_up, moe_w_down)
    h = odd_layer(h, B, S, od_norm[0], od_w_in[0], swa_q_gain[0], swa_k_gain[0], swa_sinks[0], od_w_out[0])
    h = hier_moe_block(h, 1, moe_norm[1], moe_w_grp[1], moe_b_grp[1], moe_w_exp[1], moe_b_exp[1],
                       moe_w_gate, moe_w_up, moe_w_down)
    return h.reshape(B, S, D)
```

```python
import functools

import numpy as np
import jax
import jax.numpy as jnp
from jax import lax
from jax.experimental import pallas as pl
from jax.experimental.pallas import tpu as pltpu

F32 = jnp.float32
BF16 = jnp.bfloat16
I32 = jnp.int32

EPS = 1e-6
ROPE_THETA = 500000.0
HD = 64
ROT = HD // 4
HALF = ROT // 2

HG_DIM = 128
HG_WIDTH = 512
HG_HEADS = 4
HG_CHUNK = 256

NSA_HEADS = 8
NSA_KV = 2
NSA_HPG = NSA_HEADS // NSA_KV
CMP_LEN = 32
CMP_STRIDE = 16
CMP_HIDDEN = 256
SEL_LEN = 64
SEL_TOPN = 8
NSA_WINDOW = 512
FORCE_BONUS = 1e4

SWA_HEADS = 16
SWA_KV = 2
SWA_HPG = SWA_HEADS // SWA_KV
SWA_WINDOW = 128

N_GROUPS = 4
EXP_PER_GROUP = 8
N_EXPERTS = 32
EXPERT_FF = 512
MOE_BLOCK = 512

LANES = 128
NEG = -1e30
VMEM_V7X = 64 * 1024 * 1024
VMEM_LIMIT = VMEM_V7X * 7 // 8

C_KC = 2048
C_GATE = 2304
N_PLAIN = 2560


def _cparams(sem):
    return pltpu.CompilerParams(dimension_semantics=sem, vmem_limit_bytes=VMEM_LIMIT)


def _dot(a, b):
    return jnp.dot(a, b, preferred_element_type=F32)


def _dot_nt(a, b):
    return lax.dot_general(a, b, (((1,), (1,)), ((), ())), preferred_element_type=F32)


def _proj_res_kernel(a_ref, b_ref, wa_ref, wb_ref, r_ref, o_ref):
    acc = _dot(a_ref[...], wa_ref[...]) + _dot(b_ref[...], wb_ref[...])
    o_ref[...] = r_ref[...] + acc


def proj_res(a, a_blk, b, b_blk, w, res, tm=512):
    T, D = res.shape
    K = w.shape[0]
    kh = K // 2
    return pl.pallas_call(
        _proj_res_kernel,
        grid=(T // tm,),
        in_specs=[pl.BlockSpec((tm, kh), lambda i: (i, a_blk)),
                  pl.BlockSpec((tm, kh), lambda i: (i, b_blk)),
                  pl.BlockSpec((kh, D), lambda i: (0, 0)),
                  pl.BlockSpec((kh, D), lambda i: (1, 0)),
                  pl.BlockSpec((tm, D), lambda i: (i, 0))],
        out_specs=pl.BlockSpec((tm, D), lambda i: (i, 0)),
        out_shape=jax.ShapeDtypeStruct((T, D), F32),
        compiler_params=_cparams(("parallel",)),
        name="proj_res",
    )(a, b, w, w, res)


def _rope_tables(pos, width):
    inv = jnp.power(ROPE_THETA, -jnp.arange(HALF, dtype=F32) * 2.0 / ROT)
    ang = pos.astype(F32)[:, None] * inv[None, :]
    cos, sin = jnp.cos(ang), jnp.sin(ang)
    n = pos.shape[0]
    c = jnp.concatenate([cos, cos, jnp.ones((n, HD - ROT), F32)], axis=1)
    s = jnp.concatenate([-sin, sin, jnp.zeros((n, HD - ROT), F32)], axis=1)
    reps = width // HD
    return jnp.tile(c, (1, reps)), jnp.tile(s, (1, reps))


def _block_ones(width):
    idx = np.arange(width) // HD
    return jnp.asarray((idx[:, None] == idx[None, :]).astype(np.float32), dtype=BF16)


def _split_dot(a, b):
    hi = a.astype(BF16)
    lo = (a - hi.astype(F32)).astype(BF16)
    return _dot(hi, b) + _dot(lo, b)


def _head_norm_rope(x, gain, cos, sin, bones, scale):
    ms = _split_dot(x * x, bones) * (1.0 / HD)
    y = x * lax.rsqrt(ms + EPS) * gain
    lane = lax.broadcasted_iota(I32, y.shape, 1) % HD
    partner = jnp.where(lane < HALF, pltpu.roll(y, LANES - HALF, axis=1), pltpu.roll(y, HALF, axis=1))
    out = y * cos + partner * sin
    return out * scale if scale != 1.0 else out


def _in_proj_kernel(n_plain, tn, n_heads, kinds, scales, s_tiles, x_ref, g_ref, w_ref,
                    gain_ref, cos_ref, sin_ref, bones_ref, *o_refs):
    x = x_ref[...]
    ms = jnp.mean(x * x, axis=-1, keepdims=True)
    xn = (x * lax.rsqrt(ms + EPS) * g_ref[...]).astype(BF16)
    if n_plain:
        plain_ref, o_refs = o_refs[0], o_refs[1:]
        for j in range(n_plain // tn):
            plain_ref[:, j * tn:(j + 1) * tn] = _dot(xn, w_ref[:, j * tn:(j + 1) * tn])
    cos = cos_ref[...]
    sin = sin_ref[...]
    bones = bones_ref[...]
    tm = cos.shape[0]
    lane = lax.broadcasted_iota(I32, cos.shape, 1)
    pos = (pl.program_id(0) % s_tiles) * tm + lax.broadcasted_iota(I32, cos.shape, 0)
    col = n_plain
    for t in range(len(n_heads)):
        for c in range(n_heads[t] // 2):
            if c % (tn // LANES) == 0:
                width = min(tn, (n_heads[t] // 2 - c) * LANES)
                y_wide = _dot(xn, w_ref[:, col:col + width])
                col += width
            k = c % (tn // LANES)
            y2 = y_wide[:, k * LANES:(k + 1) * LANES]
            if kinds[t] != 'v':
                y2 = _head_norm_rope(y2, gain_ref[t:t + 1, :], cos, sin, bones, scales[t])
            for j, yh in enumerate((y2, pltpu.roll(y2, HD, axis=1))):
                if kinds[t] == 'qk':
                    y = yh[:, :HD]
                elif kinds[t] == 'pad':
                    y = jnp.where(lane < HD, yh, 0.0)
                elif kinds[t] == 'blk':
                    y = jnp.where(lane < HD, yh, jnp.where(lane - HD == pos // SEL_LEN, 1.0, 0.0))
                else:
                    y = jnp.where(lane < HD, yh, jnp.where(lane == HD, 1.0, 0.0)).T
                o_refs[t][0, 2 * c + j] = y.astype(BF16)


def in_proj(x, gain, w, n_plain, n_heads, head_gains, kinds, scales, B, S, tm=512, tn=512):
    T, D = x.shape
    N = w.shape[1]
    nt = len(n_heads)
    s_tiles = S // tm
    assert S // SEL_LEN <= LANES - HD and n_plain % tn == 0
    cos, sin = _rope_tables(jnp.arange(S), LANES)
    gain_arr = jnp.stack([jnp.tile((jnp.ones((HD,), F32) if g is None else g).astype(F32), 2) for g in head_gains])
    out_specs, out_shape = [], []
    if n_plain:
        out_specs.append(pl.BlockSpec((tm, n_plain), lambda i: (i, 0)))
        out_shape.append(jax.ShapeDtypeStruct((T, n_plain), F32))
    for t in range(nt):
        if kinds[t] == 'v':
            out_specs.append(pl.BlockSpec((1, n_heads[t], LANES, tm), lambda i: (i // s_tiles, 0, 0, i % s_tiles)))
            out_shape.append(jax.ShapeDtypeStruct((B, n_heads[t], LANES, S), BF16))
        else:
            width = HD if kinds[t] == 'qk' else LANES
            out_specs.append(pl.BlockSpec((1, n_heads[t], tm, width), lambda i: (i // s_tiles, 0, i % s_tiles, 0)))
            out_shape.append(jax.ShapeDtypeStruct((B, n_heads[t], S, width), BF16))
    return pl.pallas_call(
        functools.partial(_in_proj_kernel, n_plain, tn, tuple(n_heads), tuple(kinds), tuple(scales), s_tiles),
        grid=(T // tm,),
        in_specs=[pl.BlockSpec((tm, D), lambda i: (i, 0)),
                  pl.BlockSpec((1, D), lambda i: (0, 0)),
                  pl.BlockSpec((D, N), lambda i: (0, 0)),
                  pl.BlockSpec((nt, LANES), lambda i: (0, 0)),
                  pl.BlockSpec((tm, LANES), lambda i: (i % s_tiles, 0)),
                  pl.BlockSpec((tm, LANES), lambda i: (i % s_tiles, 0)),
                  pl.BlockSpec((LANES, LANES), lambda i: (0, 0))],
        out_specs=out_specs,
        out_shape=out_shape,
        compiler_params=_cparams(("parallel",)),
        name="in_proj",
    )(x, gain.reshape(1, D), w, gain_arr, cos, sin, _block_ones(LANES))


_N_LEVELS = HG_CHUNK.bit_length() - 1


def _hgrn_constants():
    C = HG_CHUNK
    r = np.arange(C)
    mats = [(r[None, :] <= r[:, None]).astype(np.float32),
            (r[None, :] > r[:, None]).astype(np.float32)]
    masks = []
    for lv in range(_N_LEVELS):
        n = 1 << lv
        blk = r // (2 * n)
        right = (r // n) % 2 == 1
        bnd = blk * 2 * n + n - 1
        m = np.zeros((C, C), np.float32)
        for t in range(C):
            if right[t]:
                m[t, bnd[t] + 1:t + 1] = 1.0
            else:
                m[t, t + 1:bnd[t] + 1] = 1.0
        mats.append(m)
        masks.append((blk[:, None] == blk[None, :]).astype(np.float32))
    masks.append(np.eye(C, dtype=np.float32))
    return jnp.asarray(np.concatenate(mats, axis=0), dtype=BF16), jnp.asarray(np.stack(masks))


def _hgrn_kernel(q_ref, f_ref, i_ref, g_ref, lb_ref, gain_ref, mall_ref, masks_ref, o_ref, st_ref):
    C = HG_CHUNK
    ts = q_ref.shape[1]

    @pl.when(pl.program_id(1) == 0)
    def _():
        st_ref[...] = jnp.zeros_like(st_ref)

    mall = mall_ref[...]
    row = lax.broadcasted_iota(I32, (C, HG_DIM), 0)
    gain = gain_ref[...]

    def chunk(c, carry):
        r0 = pl.multiple_of(c * C, C)
        for h in range(HG_HEADS):
            ls = slice(h * HG_DIM, (h + 1) * HG_DIM)
            q = q_ref[0, pl.ds(r0, C), ls]
            z = f_ref[0, pl.ds(r0, C), ls]
            v = i_ref[0, pl.ds(r0, C), ls]
            g = g_ref[0, pl.ds(r0, C), ls]
            lb = lb_ref[:, ls]
            lf = jnp.log(lb + (1.0 - lb) * jax.nn.sigmoid(z))
            k = (1.0 - lb) * jax.nn.sigmoid(-z)
            lf_hi = lf.astype(BF16)
            lf_lo = (lf - lf_hi.astype(F32)).astype(BF16)
            seg = _dot(mall, jnp.concatenate([lf_hi, lf_lo], axis=1))
            ex = jnp.exp(seg[:, :HG_DIM] + seg[:, HG_DIM:])
            qs = q * ex[0:C]
            ks = k * ex[C:2 * C]
            st = st_ref[h]
            o = _dot_nt(qs.astype(BF16), st.astype(BF16))
            a = masks_ref[_N_LEVELS] * _dot_nt(q.astype(BF16), k.astype(BF16))
            for lv in range(_N_LEVELS):
                e = ex[(2 + lv) * C:(3 + lv) * C]
                right = ((row >> lv) & 1) == 1
                qe = jnp.where(right, q * e, 0.0).astype(BF16)
                ke = jnp.where(right, 0.0, k * e).astype(BF16)
                a = a + masks_ref[lv] * _dot_nt(qe, ke)
            vb = v.astype(BF16)
            o = o + _dot(a.astype(BF16), vb)
            decay = ex[C - 1:C]
            st_ref[h] = st * decay + _dot(v.T.astype(BF16), ks.astype(BF16))
            ms = jnp.mean(o * o, axis=-1, keepdims=True)
            y = o * lax.rsqrt(ms + EPS) * gain * (g * jax.nn.sigmoid(g))
            o_ref[0, pl.ds(r0, C), ls] = y.astype(BF16)
        return carry

    lax.fori_loop(0, ts // C, chunk, 0)


def hgrn2(proj, lb, out_gain, ts=512):
    B, S, _ = proj.shape
    mall, masks = _hgrn_constants()
    W = HG_WIDTH
    return pl.pallas_call(
        _hgrn_kernel,
        grid=(B, S // ts),
        in_specs=[pl.BlockSpec((1, ts, W), lambda b, s: (b, s, 0)),
                  pl.BlockSpec((1, ts, W), lambda b, s: (b, s, 1)),
                  pl.BlockSpec((1, ts, W), lambda b, s: (b, s, 2)),
                  pl.BlockSpec((1, ts, W), lambda b, s: (b, s, 3)),
                  pl.BlockSpec((1, W), lambda b, s: (0, 0)),
                  pl.BlockSpec((1, HG_DIM), lambda b, s: (0, 0)),
                  pl.BlockSpec(mall.shape, lambda b, s: (0, 0)),
                  pl.BlockSpec(masks.shape, lambda b, s: (0, 0, 0))],
        out_specs=pl.BlockSpec((1, ts, W), lambda b, s: (b, s, 0)),
        out_shape=jax.ShapeDtypeStruct((B, S, W), BF16),
        scratch_shapes=[pltpu.VMEM((HG_HEADS, HG_DIM, HG_DIM), F32)],
        compiler_params=_cparams(("parallel", "arbitrary")),
        name="hgrn2",
    )(proj, proj, proj, proj, lb.reshape(1, W), out_gain.reshape(1, HG_DIM), mall, masks)


def _compress_kernel(xk_ref, xv_ref, w1k_ref, w1v_ref, pek_ref, pev_ref, w2k_ref, w2v_ref,
                     gain_ref, cos_ref, sin_ref, bones_ref, ok_ref, ov_ref):
    nc = xk_ref.shape[1] // CMP_STRIDE

    def hidden(x_ref, w1_ref, pe_ref):
        width = NSA_KV * CMP_HIDDEN
        u0 = jnp.zeros((nc, width), F32)
        u1 = jnp.zeros((nc, width), F32)
        const = jnp.zeros((8, width), F32)
        for p in range(CMP_STRIDE):
            xp = x_ref[0, pl.ds(p, nc, stride=CMP_STRIDE), :].astype(BF16)
            u0 = u0 + _dot(xp, w1_ref[p])
            u1 = u1 + _dot(xp, w1_ref[CMP_STRIDE + p])
            const = const + _dot(pe_ref[p].astype(BF16), w1_ref[p])
            const = const + _dot(pe_ref[CMP_STRIDE + p].astype(BF16), w1_ref[CMP_STRIDE + p])
        h = u0 + pltpu.roll(u1, nc - 1, axis=0) + const[0:1]
        return jax.nn.gelu(h).astype(BF16)

    yk = _dot(hidden(xk_ref, w1k_ref, pek_ref), w2k_ref[...])
    yk = _head_norm_rope(yk, gain_ref[...], cos_ref[...], sin_ref[...], bones_ref[...], 1.0)
    lane = lax.broadcasted_iota(I32, yk.shape, 1)
    ok_ref[0] = jnp.where(lane < HD, yk, 0.0).astype(BF16)
    ok_ref[1] = jnp.where(lane < HD, pltpu.roll(yk, HD, axis=1), 0.0).astype(BF16)
    yv_t = _dot_nt(w2v_ref[...], hidden(xv_ref, w1v_ref, pev_ref)).astype(BF16)
    ov_ref[0] = yv_t[:HD]
    ov_ref[1] = yv_t[HD:]


def _group_diag(w):
    z = jnp.zeros_like(w)
    return jnp.concatenate([jnp.concatenate([w, z], axis=-1), jnp.concatenate([z, w], axis=-1)], axis=-2)


def nsa_compress(proj, kc_blk, vc_blk, pe, w1, w2, k_gain):
    B, S, _ = proj.shape
    nc = S // CMP_STRIDE
    pos_end = jnp.arange(nc) * CMP_STRIDE + CMP_LEN - 1
    cos, sin = _rope_tables(pos_end, LANES)
    w1d = _group_diag(w1.reshape(2, CMP_LEN, HD, CMP_HIDDEN)).astype(BF16)
    w2d = _group_diag(w2).astype(BF16)
    pe2 = jnp.broadcast_to(jnp.tile(pe, (1, 1, NSA_KV))[:, :, None, :], (2, CMP_LEN, 8, LANES))
    full = lambda shape: pl.BlockSpec(shape, lambda b: (0,) * len(shape))
    w1_shape = (CMP_LEN, LANES, NSA_KV * CMP_HIDDEN)
    return pl.pallas_call(
        _compress_kernel,
        grid=(B,),
        in_specs=[pl.BlockSpec((1, S, LANES), lambda b: (b, 0, kc_blk)),
                  pl.BlockSpec((1, S, LANES), lambda b: (b, 0, vc_blk)),
                  full(w1_shape), full(w1_shape),
                  full((CMP_LEN, 8, LANES)), full((CMP_LEN, 8, LANES)),
                  full((NSA_KV * CMP_HIDDEN, LANES)), full((LANES, NSA_KV * CMP_HIDDEN)),
                  full((1, LANES)), full((nc, LANES)), full((nc, LANES)), full((LANES, LANES))],
        out_specs=[pl.BlockSpec((NSA_KV, nc, LANES), lambda b: (b, 0, 0)),
                   pl.BlockSpec((NSA_KV, HD, nc), lambda b: (b, 0, 0))],
        out_shape=[jax.ShapeDtypeStruct((B * NSA_KV, nc, LANES), BF16),
                   jax.ShapeDtypeStruct((B * NSA_KV, HD, nc), BF16)],
        compiler_params=_cparams(("parallel",)),
        name="nsa_compress",
    )(proj, proj, w1d[0], w1d[1], pe2[0], pe2[1], w2d[0], w2d[1].T,
      jnp.tile(k_gain.astype(F32), 2).reshape(1, LANES), cos, sin, _block_ones(LANES))


NSA_TQ = 256
NSA_TK = 1024


def _nsa_kernel(q_ref, kc_ref, vc_ref, ks_ref, vs_ref, kw_ref, vw_ref, gate_ref, cover_ref, place_ref,
                o_ref, acc_ref):
    TQ, TK, H = NSA_TQ, NSA_TK, NSA_HPG
    R = H * TQ
    t0 = pl.program_id(1) * TQ
    tl = t0 + lax.broadcasted_iota(I32, (1, TQ), 1)
    ncmp = kc_ref.shape[1]
    nsel = cover_ref.shape[0]

    def all_heads(x):
        return jnp.concatenate([x] * H, axis=1)

    def select(g):
        q = q_ref[0, g * H:(g + 1) * H].reshape(R, LANES)
        cmp_end = lax.broadcasted_iota(I32, (ncmp, 1), 0) * CMP_STRIDE + (CMP_LEN - 1)
        valid = all_heads(jnp.where(cmp_end <= tl, 1.0, 0.0))
        s = jnp.where(valid > 0.5, _dot_nt(kc_ref[g], q), NEG)
        p = jnp.exp(s - jnp.max(s, axis=0, keepdims=True)) * valid
        p_c = p / jnp.maximum(jnp.sum(p, axis=0, keepdims=True), 1e-30)
        o_c = _dot(vc_ref[g], p_c.astype(BF16))
        p_tok = p_c[:, 0:TQ]
        for h in range(1, H):
            p_tok = p_tok + p_c[:, h * TQ:(h + 1) * TQ]
        p_hi = p_tok.astype(BF16)
        p_lo = (p_tok - p_hi.astype(F32)).astype(BF16)
        imp = _dot(cover_ref[...], p_hi) + _dot(cover_ref[...], p_lo)
        jj = lax.broadcasted_iota(I32, (nsel, TQ), 0)
        jf = jj.astype(F32)
        cur = tl // SEL_LEN
        allowed = jj * SEL_LEN <= tl
        forced = (jj == 0) | (jj == cur) | (jj == cur - 1)
        score = jnp.where(allowed, imp + jnp.where(forced, FORCE_BONUS, 0.0), NEG)
        sel = jnp.zeros((nsel, TQ), F32)
        for _ in range(min(SEL_TOPN, nsel)):
            mx = jnp.max(score, axis=0, keepdims=True)
            first = jnp.min(jnp.where(score == mx, jf, float(nsel)), axis=0, keepdims=True)
            hit = jf == first
            sel = jnp.where(hit & (mx > 0.5 * NEG), 1.0, sel)
            score = jnp.where(hit, NEG, score)
        block_bias = jnp.where(sel > 0.5, 0.0, NEG).T.astype(BF16)
        placed = _dot(block_bias, place_ref[...]).astype(BF16)
        return o_c, q, q + jnp.concatenate([placed] * H, axis=0)

    def sweep_tile(g, q_sel, k0, size, m_old, causal):
        s = _dot_nt(ks_ref[0, g, pl.ds(k0, size), :], q_sel)
        if causal:
            kpos = k0 + lax.broadcasted_iota(I32, (size, 1), 0)
            s = s + all_heads(jnp.where(kpos <= tl, 0.0, NEG))
        m_new = jnp.maximum(m_old, jnp.max(s, axis=0, keepdims=True))
        p = jnp.exp(s - m_new).astype(BF16)
        acc_ref[g] = jnp.exp(m_old - m_new) * acc_ref[g] + _dot(vs_ref[0, g, :, pl.ds(k0, size)], p)
        return m_new

    def sweep(n, start, size, ms):
        def body(j, ms):
            k0 = pl.multiple_of(start + j * size, size)
            return tuple(sweep_tile(g, chosen[g][2], k0, size, ms[g], False) for g in groups)

        return lax.fori_loop(0, n, body, ms)

    def window(g, q):
        WK = NSA_WINDOW + TQ
        ws = pl.multiple_of(jnp.maximum(t0 - NSA_WINDOW, 0), TQ)
        kp = ws + lax.broadcasted_iota(I32, (WK, 1), 0)
        wbias = jnp.where((kp <= tl) & (kp > tl - NSA_WINDOW), 0.0, NEG)
        s = _dot_nt(kw_ref[0, g, pl.ds(ws, WK), :], q) + all_heads(wbias)
        p = jnp.exp(s - jnp.max(s, axis=0, keepdims=True)).astype(BF16)
        acc_w = _dot(vw_ref[0, g, :, pl.ds(ws, WK)], p)
        return acc_w[:HD] / acc_w[HD:HD + 1]

    groups = range(NSA_KV)
    chosen = [select(g) for g in groups]
    acc_ref[...] = jnp.zeros(acc_ref.shape, F32)
    n_big = t0 // TK
    m_run = sweep(n_big, 0, TK, tuple(jnp.full((1, R), NEG, F32) for _ in groups))
    m_run = sweep((t0 - n_big * TK) // TQ, n_big * TK, TQ, m_run)
    heads_out = []
    for g in groups:
        o_c, q, q_sel = chosen[g]
        sweep_tile(g, q_sel, pl.multiple_of(t0, TQ), TQ, m_run[g], True)
        acc = acc_ref[g]
        o_s = acc[:HD] / jnp.maximum(acc[HD:HD + 1], 1e-30)
        o_w = window(g, q)
        sg = jax.nn.sigmoid(gate_ref[0, :, g * LANES:(g + 1) * LANES]).T
        for h in range(H):
            cols = slice(h * TQ, (h + 1) * TQ)
            heads_out.append(sg[3 * h:3 * h + 1] * o_c[:, cols] + sg[3 * h + 1:3 * h + 2] * o_s[:, cols]
                             + sg[3 * h + 2:3 * h + 3] * o_w[:, cols])
    o_ref[0] = jnp.concatenate(heads_out, axis=0).T.astype(BF16)


def nsa_attention(qn, kcb, vcb, ksn, vsb, kwn, vwb, proj):
    B, _, S, _ = qn.shape
    G, H, TQ = NSA_KV, NSA_HPG, NSA_TQ
    assert S % NSA_TK == 0 and S >= NSA_WINDOW + TQ
    ncmp = kcb.shape[1]
    nsel = S // SEL_LEN
    n = np.arange(ncmp)[:, None] * CMP_STRIDE
    j = np.arange(nsel)[None, :] * SEL_LEN
    cover = jnp.asarray(((n < j + SEL_LEN) & (n + CMP_LEN > j)).astype(np.float32).T, dtype=BF16)
    place = np.zeros((nsel, LANES), np.float32)
    place[np.arange(nsel), HD + np.arange(nsel)] = 1.0
    R = H * TQ
    k_spec = pl.BlockSpec((1, G, S, LANES), lambda b, i: (b, 0, 0, 0))
    v_spec = pl.BlockSpec((1, G, LANES, S), lambda b, i: (b, 0, 0, 0))
    assert C_GATE % (G * LANES) == 0
    return pl.pallas_call(
        _nsa_kernel,
        grid=(B, S // TQ),
        in_specs=[pl.BlockSpec((1, G * H, TQ, LANES), lambda b, i: (b, 0, i, 0)),
                  pl.BlockSpec((G, ncmp, LANES), lambda b, i: (b, 0, 0)),
                  pl.BlockSpec((G, HD, ncmp), lambda b, i: (b, 0, 0)),
                  k_spec, v_spec, k_spec, v_spec,
                  pl.BlockSpec((1, TQ, G * LANES), lambda b, i: (b, i, C_GATE // (G * LANES))),
                  pl.BlockSpec((nsel, ncmp), lambda b, i: (0, 0)),
                  pl.BlockSpec((nsel, LANES), lambda b, i: (0, 0))],
        out_specs=pl.BlockSpec((1, TQ, G * H * HD), lambda b, i: (b, i, 0)),
        out_shape=jax.ShapeDtypeStruct((B, S, NSA_HEADS * HD), BF16),
        scratch_shapes=[pltpu.VMEM((G, LANES, R), F32)],
        compiler_params=_cparams(("parallel", "arbitrary")),
        name="nsa_attention",
    )(qn, kcb, vcb, ksn, vsb, kwn, vwb, proj, cover, jnp.asarray(place, dtype=BF16))


SWA_TQ = 256


def _swa_kernel(sink_ref, q_ref, k_ref, v_ref, o_ref):
    TQ, H = SWA_TQ, SWA_HPG
    R = H * TQ
    g = pl.program_id(1)
    t0 = pl.program_id(2) * TQ
    q = q_ref[0].reshape(R, HD)
    tl = t0 + lax.broadcasted_iota(I32, (1, TQ), 1)
    WK = SWA_WINDOW + TQ
    ws = pl.multiple_of(jnp.maximum(t0 - SWA_WINDOW, 0), LANES)
    kp = ws + lax.broadcasted_iota(I32, (WK, 1), 0)
    bias = jnp.where((kp <= tl) & (kp > tl - SWA_WINDOW), 0.0, NEG)
    s = _dot_nt(k_ref[0, 0, pl.ds(ws, WK), :], q) + jnp.concatenate([bias] * H, axis=1)
    head = lax.broadcasted_iota(I32, (1, R), 1) // TQ
    sink = jnp.zeros((1, R), F32)
    for h in range(H):
        sink = jnp.where(head == h, sink_ref[g * H + h], sink)
    m = jnp.maximum(jnp.max(s, axis=0, keepdims=True), sink)
    p = jnp.exp(s - m).astype(BF16)
    acc = _dot(v_ref[0, 0, :, pl.ds(ws, WK)], p)
    o = acc[:HD] / (acc[HD:HD + 1] + jnp.exp(sink - m))
    o_ref[0] = jnp.concatenate([o[:, h * TQ:(h + 1) * TQ] for h in range(H)], axis=0).T.astype(BF16)


def swa_attention(qn, kn, vb, sinks):
    B, _, S, _ = qn.shape
    G, H, TQ = SWA_KV, SWA_HPG, SWA_TQ
    assert S >= SWA_WINDOW + TQ
    k_spec = pl.BlockSpec((1, 1, S, HD), lambda b, g, i, sk: (b, g, 0, 0))
    v_spec = pl.BlockSpec((1, 1, LANES, S), lambda b, g, i, sk: (b, g, 0, 0))
    return pl.pallas_call(
        _swa_kernel,
        grid_spec=pltpu.PrefetchScalarGridSpec(
            num_scalar_prefetch=1,
            grid=(B, G, S // TQ),
            in_specs=[pl.BlockSpec((1, H, TQ, HD), lambda b, g, i, sk: (b, g, i, 0)), k_spec, v_spec],
            out_specs=pl.BlockSpec((1, TQ, H * HD), lambda b, g, i, sk: (b, i, g))),
        out_shape=jax.ShapeDtypeStruct((B, S, SWA_HEADS * HD), BF16),
        compiler_params=_cparams(("parallel", "parallel", "arbitrary")),
        name="swa_attention",
    )(sinks.astype(F32), qn, kn, vb)


R_GRP = 0
R_EXP = 32


SUBLANES = 8


def _store_row_tiles(ref, val, base=0):
    for s in range(SUBLANES):
        ref[pl.ds(base * SUBLANES + s, val.shape[0], stride=SUBLANES), :] = val[:, s * LANES:(s + 1) * LANES]


def _load_row_tiles(ref, rows, base=0):
    return [ref[pl.ds(base * SUBLANES + s, rows, stride=SUBLANES), :] for s in range(SUBLANES)]


def _router_kernel(x_ref, g_ref, w_ref, b_ref, tri_ref, hn_ref, route_ref, compact_ref, cnt_ref, base_ref):
    i = pl.program_id(0)
    last = pl.num_programs(0) - 1

    @pl.when(i == 0)
    def _():
        base_ref[...] = jnp.zeros_like(base_ref)

    @pl.when(i < last)
    def _():
        _route_tile(x_ref, g_ref, w_ref, b_ref, tri_ref, hn_ref, route_ref, compact_ref, base_ref)

    @pl.when(i == last)
    def _():
        hn_ref[...] = jnp.zeros_like(hn_ref)
        route_ref[...] = jnp.zeros_like(route_ref)
        compact_ref[...] = jnp.zeros_like(compact_ref)

    cnt_ref[...] = base_ref[...]


def _route_tile(x_ref, g_ref, w_ref, b_ref, tri_ref, hn_ref, route_ref, compact_ref, base_ref):
    x = x_ref[...]
    ms = jnp.mean(x * x, axis=-1, keepdims=True)
    h = x * lax.rsqrt(ms + EPS) * g_ref[...]
    _store_row_tiles(hn_ref, h)
    h_hi = h.astype(BF16)
    h_lo = (h - h_hi.astype(F32)).astype(BF16)
    logits = _dot(h_hi, w_ref[0]) + _dot(h_lo, w_ref[0]) + _dot(h_hi, w_ref[1]) + b_ref[...]
    lane = lax.broadcasted_iota(I32, logits.shape, 1)
    is_grp = lane < N_GROUPS
    gl = jnp.where(is_grp, logits, NEG)
    gmax = jnp.max(gl, axis=-1, keepdims=True)
    lanef = lane.astype(F32)
    gidx = jnp.min(jnp.where(gl == gmax, lanef, float(LANES)), axis=-1, keepdims=True)
    gw = 1.0 / jnp.sum(jnp.where(is_grp, jnp.exp(logits - gmax), 0.0), axis=-1, keepdims=True)
    in_grp = ((lane - R_EXP) // EXP_PER_GROUP).astype(F32) == gidx
    w0 = jnp.where(in_grp, logits, NEG)
    m1 = jnp.max(w0, axis=-1, keepdims=True)
    i1 = jnp.min(jnp.where(w0 == m1, lanef, float(LANES)), axis=-1, keepdims=True)
    w1 = jnp.where(lanef == i1, NEG, w0)
    m2 = jnp.max(w1, axis=-1, keepdims=True)
    i2 = jnp.min(jnp.where(w1 == m2, lanef, float(LANES)), axis=-1, keepdims=True)
    e2 = jnp.exp(m2 - m1)
    p1 = gw / (1.0 + e2)
    p2 = gw * e2 / (1.0 + e2)
    oh1 = jnp.where(lanef == i1, 1.0, 0.0)
    oh2 = jnp.where(lanef == i2, 1.0, 0.0)
    oh = oh1 + oh2
    before = _dot(tri_ref[...], oh.astype(BF16)) + base_ref[0:1, :]
    r1 = jnp.sum(before * oh1, axis=-1, keepdims=True)
    r2 = jnp.sum(before * oh2, axis=-1, keepdims=True)
    base_ref[...] = base_ref[...] + jnp.sum(oh, axis=0, keepdims=True)
    route = jnp.zeros_like(logits)
    for ln, val in enumerate((i1 - R_EXP, i2 - R_EXP, p1, p2, r1, r2)):
        route = jnp.where(lane == ln, val, route)
    route_ref[...] = route
    compact_ref[0] = route.T[0:SUBLANES]


def moe_router(x, gain, w_grp, b_grp, w_exp, b_exp, tm=256):
    T, D = x.shape
    nt = T // tm
    w = jnp.zeros((D, LANES), F32).at[:, R_GRP:R_GRP + N_GROUPS].set(w_grp).at[:, R_EXP:R_EXP + N_EXPERTS].set(w_exp)
    b = jnp.zeros((1, LANES), F32).at[0, R_GRP:R_GRP + N_GROUPS].set(b_grp).at[0, R_EXP:R_EXP + N_EXPERTS].set(b_exp)
    w_hi = w.astype(BF16)
    w = jnp.stack([w_hi, (w - w_hi.astype(F32)).astype(BF16)])
    tri = jnp.asarray(np.tril(np.ones((tm, tm), np.float32), -1), dtype=BF16)
    return pl.pallas_call(
        _router_kernel,
        grid=(nt + 1,),
        in_specs=[pl.BlockSpec((tm, D), lambda i: (jnp.minimum(i, nt - 1), 0)),
                  pl.BlockSpec((1, D), lambda i: (0, 0)),
                  pl.BlockSpec((2, D, LANES), lambda i: (0, 0, 0)),
                  pl.BlockSpec((1, LANES), lambda i: (0, 0)),
                  pl.BlockSpec((tm, tm), lambda i: (0, 0))],
        out_specs=[pl.BlockSpec((tm * SUBLANES, LANES), lambda i: (i, 0)),
                   pl.BlockSpec((tm, LANES), lambda i: (i, 0)),
                   pl.BlockSpec((1, SUBLANES, tm), lambda i: (i, 0, 0)),
                   pl.BlockSpec((8, LANES), lambda i: (0, 0))],
        out_shape=[jax.ShapeDtypeStruct(((T + tm) * SUBLANES, LANES), F32),
                   jax.ShapeDtypeStruct((T + tm, LANES), F32),
                   jax.ShapeDtypeStruct((nt + 1, SUBLANES, tm), F32),
                   jax.ShapeDtypeStruct((8, LANES), F32)],
        scratch_shapes=[pltpu.VMEM((8, LANES), F32)],
        compiler_params=_cparams(("arbitrary",)),
        name="moe_router",
    )(x, gain.reshape(1, D), w, b, tri)


def _row_copy(src, src_row, dst, dst_row, sem):
    def tile(ref, row):
        return ref.at[pl.ds(pl.multiple_of(row * SUBLANES, SUBLANES), SUBLANES)]

    return pltpu.make_async_copy(tile(src, src_row), tile(dst, dst_row), sem)


DISPATCH_TILE = 512
ROW_UNROLL = 4


def _wait_rows(hbm_ref, sem, n):
    view = hbm_ref.at[pl.ds(0, n * SUBLANES)]
    pltpu.make_async_copy(view, view, sem).wait()


def _dispatch_kernel(dest_ref, x_ref, xb_ref, stage, sems):
    i = pl.program_id(0)
    last = pl.num_programs(0) - 1
    slot = i % 2

    @pl.when(i >= 2)
    def _():
        _wait_rows(xb_ref, sems.at[slot], DISPATCH_TILE)

    stage[slot] = x_ref[...]

    def issue(t, c):
        _row_copy(stage.at[slot], t, xb_ref, dest_ref[0, 0, 2 * t], sems.at[slot]).start(priority=0)
        _row_copy(stage.at[slot], t, xb_ref, dest_ref[0, 0, 2 * t + 1], sems.at[slot]).start(priority=1)
        return c

    lax.fori_loop(0, DISPATCH_TILE // 2, issue, 0, unroll=ROW_UNROLL)

    @pl.when(i == last)
    def _():
        _wait_rows(xb_ref, sems.at[slot], DISPATCH_TILE)

    @pl.when((i == last) & (i >= 1))
    def _():
        _wait_rows(xb_ref, sems.at[1 - slot], DISPATCH_TILE)


def moe_dispatch(hn, dest_all, n_tok):
    n_rows = dest_all.shape[0]
    tm = DISPATCH_TILE // 2
    nt = n_rows // DISPATCH_TILE
    n_tok_tiles = n_tok // tm
    return pl.pallas_call(
        _dispatch_kernel,
        grid=(nt,),
        in_specs=[pl.BlockSpec((1, 1, DISPATCH_TILE), lambda i: (i, 0, 0), memory_space=pltpu.SMEM),
                  pl.BlockSpec((tm * SUBLANES, LANES), lambda i: (jnp.minimum(i, n_tok_tiles), 0))],
        out_specs=pl.BlockSpec(memory_space=pl.ANY),
        out_shape=jax.ShapeDtypeStruct((n_rows * SUBLANES, LANES), F32),
        scratch_shapes=[pltpu.VMEM((2, tm * SUBLANES, LANES), F32), pltpu.SemaphoreType.DMA((2,))],
        compiler_params=_cparams(("arbitrary",)),
        name="moe_dispatch",
    )(dest_all.reshape(nt, 1, DISPATCH_TILE), hn)


def _expert_kernel(be_ref, x_ref, wg_ref, wu_ref, wd_ref, y_ref, wg_b, wu_b, wd_b):
    i = pl.program_id(0)

    @pl.when((i == 0) | (be_ref[i] != be_ref[jnp.maximum(i - 1, 0)]))
    def _():
        wg_b[...] = wg_ref[0, 0].astype(BF16)
        wu_b[...] = wu_ref[0, 0].astype(BF16)
        wd_b[...] = wd_ref[0, 0].astype(BF16)

    x = jnp.concatenate(_load_row_tiles(x_ref, MOE_BLOCK), axis=1).astype(BF16)
    hid = _dot(x, wg_b[...])
    hid = hid * jax.nn.sigmoid(hid) * _dot(x, wu_b[...])
    _store_row_tiles(y_ref, _dot(hid.astype(BF16), wd_b[...]))


def moe_experts(xb, blk_expert, layer, w_gate, w_up, w_down):
    n_rows = xb.shape[0] // SUBLANES
    D = w_gate.shape[2]
    blk = (MOE_BLOCK * SUBLANES, LANES)
    wspec = lambda shape: pl.BlockSpec((1, 1) + shape, lambda i, be: (layer, be[i], 0, 0))
    return pl.pallas_call(
        _expert_kernel,
        grid_spec=pltpu.PrefetchScalarGridSpec(
            num_scalar_prefetch=1,
            grid=(n_rows // MOE_BLOCK,),
            in_specs=[pl.BlockSpec(blk, lambda i, be: (i, 0)),
                      wspec((D, EXPERT_FF)), wspec((D, EXPERT_FF)), wspec((EXPERT_FF, D))],
            out_specs=pl.BlockSpec(blk, lambda i, be: (i, 0)),
            scratch_shapes=[pltpu.VMEM((D, EXPERT_FF), BF16), pltpu.VMEM((D, EXPERT_FF), BF16),
                            pltpu.VMEM((EXPERT_FF, D), BF16)]),
        out_shape=jax.ShapeDtypeStruct((n_rows * SUBLANES, LANES), F32),
        compiler_params=_cparams(("arbitrary",)),
        name="moe_experts",
    )(blk_expert, xb, w_gate, w_up, w_down)


def _combine_kernel(dest_ref, dest_next_ref, x_ref, route_ref, yb_ref, o_ref, ybuf, sems):
    tm = x_ref.shape[0]
    i = pl.program_id(0)
    slot = i % 2

    def gather(d_ref, s):
        def body(t, c):
            _row_copy(yb_ref, d_ref[0, 0, 2 * t], ybuf.at[s], t, sems.at[s]).start(priority=0)
            _row_copy(yb_ref, d_ref[0, 0, 2 * t + 1], ybuf.at[s], tm + t, sems.at[s]).start(priority=1)
            return c

        lax.fori_loop(0, tm, body, 0, unroll=ROW_UNROLL)

    @pl.when(i == 0)
    def _():
        gather(dest_ref, 0)

    @pl.when(i + 1 < pl.num_programs(0))
    def _():
        gather(dest_next_ref, 1 - slot)

    _wait_rows(yb_ref, sems.at[slot], 2 * tm)
    r = route_ref[...]
    w0, w1 = r[:, 2:3], r[:, 3:4]
    y0 = _load_row_tiles(ybuf.at[slot], tm)
    y1 = _load_row_tiles(ybuf.at[slot], tm, base=tm)
    for s in range(SUBLANES):
        cols = slice(s * LANES, (s + 1) * LANES)
        o_ref[:, cols] = x_ref[:, cols] + w0 * y0[s] + w1 * y1[s]


def moe_combine(x, yb, dest, route):
    T, D = x.shape
    tm = DISPATCH_TILE // 2
    nt = T // tm
    dest3 = dest.reshape(nt, 1, DISPATCH_TILE)
    return pl.pallas_call(
        _combine_kernel,
        grid=(nt,),
        in_specs=[pl.BlockSpec((1, 1, DISPATCH_TILE), lambda i: (i, 0, 0), memory_space=pltpu.SMEM),
                  pl.BlockSpec((1, 1, DISPATCH_TILE), lambda i: (jnp.minimum(i + 1, nt - 1), 0, 0),
                               memory_space=pltpu.SMEM),
                  pl.BlockSpec((tm, D), lambda i: (i, 0)),
                  pl.BlockSpec((tm, LANES), lambda i: (i, 0)),
                  pl.BlockSpec(memory_space=pl.ANY)],
        out_specs=pl.BlockSpec((tm, D), lambda i: (i, 0)),
        out_shape=jax.ShapeDtypeStruct((T, D), F32),
        scratch_shapes=[pltpu.VMEM((2, DISPATCH_TILE * SUBLANES, LANES), F32), pltpu.SemaphoreType.DMA((2,))],
        compiler_params=_cparams(("arbitrary",)),
        name="moe_combine",
    )(dest3, dest3, x, route, yb)


def hier_moe_block(x, layer, gain, w_grp, b_grp, w_exp, b_exp, w_gate, w_up, w_down):
    T, D = x.shape
    hn, route, compact, cnt = moe_router(x, gain, w_grp, b_grp, w_exp, b_exp)
    A = 2 * T
    n_rows = -(-A // MOE_BLOCK) * MOE_BLOCK + N_EXPERTS * MOE_BLOCK
    n_blocks = n_rows // MOE_BLOCK
    counts = cnt[0, R_EXP:R_EXP + N_EXPERTS].astype(I32)
    pcounts = (counts + MOE_BLOCK - 1) // MOE_BLOCK * MOE_BLOCK
    pend = jnp.cumsum(pcounts)
    pstart = pend - pcounts
    experts = jnp.arange(N_EXPERTS, dtype=I32)
    fields = compact[:-1].astype(I32)
    slot_e = jnp.stack([fields[:, 0], fields[:, 1]], axis=-1).reshape(T, 2)
    slot_rank = jnp.stack([fields[:, 4], fields[:, 5]], axis=-1).reshape(T, 2)
    seg_start = jnp.sum(jnp.where(slot_e[..., None] == experts, pstart, 0), axis=-1)
    dest = (slot_rank + seg_start).reshape(A)
    n_pad = n_rows - A
    pad_cnt = jnp.concatenate([pcounts - counts, (n_rows - pend[-1])[None]])
    pad_start = jnp.concatenate([pstart + counts, pend[-1:]])
    pad_end = jnp.cumsum(pad_cnt)
    k = jnp.arange(n_pad, dtype=I32)
    seg = (k[:, None] >= pad_end[None, :]).astype(I32).sum(axis=1)
    seg_hot = seg[:, None] == jnp.arange(N_EXPERTS + 1, dtype=I32)[None, :]
    pad_dest = k + jnp.sum(jnp.where(seg_hot, (pad_start - (pad_end - pad_cnt))[None, :], 0), axis=1)
    dest_all = jnp.concatenate([dest, pad_dest]).astype(I32)
    blk_start = jnp.arange(n_blocks, dtype=I32) * MOE_BLOCK
    blk_expert = jnp.minimum((pend[None, :] <= blk_start[:, None]).astype(I32).sum(axis=1), N_EXPERTS - 1)
    xb = moe_dispatch(hn, dest_all, T)
    yb = moe_experts(xb, blk_expert, layer, w_gate, w_up, w_down)
    return moe_combine(x, yb, dest, route)


def _even_w_in(w_in):
    hgw, qw, kvw, ng = 4 * HG_WIDTH, NSA_HEADS * HD, NSA_KV * HD, 3 * NSA_HPG
    kc, vc, ks, vs, kw, vw = (hgw + qw + i * kvw for i in range(6))
    gate = hgw + qw + 6 * kvw
    pad = jnp.zeros((w_in.shape[0], LANES - ng), w_in.dtype)
    cols = [w_in[:, :hgw], w_in[:, kc:kc + kvw], w_in[:, vc:vc + kvw]]
    for g in range(NSA_KV):
        cols += [w_in[:, gate + g * ng:gate + (g + 1) * ng], pad]
    cols += [w_in[:, hgw:hgw + qw]] + [w_in[:, c:c + kvw] for c in (ks, kw, vs, vw)]
    return jnp.concatenate(cols, axis=1).astype(BF16)


def even_layer(x, B, S, norm, w_in, lb, hg_gain, q_gain, k_gain, cmp_pe, cmp_w1, cmp_w2, w_out):
    T = B * S
    proj, qn, ksn, kwn, vsb, vwb = in_proj(
        x, norm, _even_w_in(w_in), N_PLAIN, [NSA_HEADS, NSA_KV, NSA_KV, NSA_KV, NSA_KV],
        [q_gain, k_gain[1], k_gain[2], None, None], ['pad', 'blk', 'pad', 'v', 'v'],
        [HD ** -0.5, 1.0, 1.0, 1.0, 1.0], B, S)
    proj = proj.reshape(B, S, N_PLAIN)
    a_out = hgrn2(proj, lb, hg_gain)
    kcb, vcb = nsa_compress(proj, C_KC // LANES, C_KC // LANES + 1, cmp_pe, cmp_w1, cmp_w2, k_gain[0])
    b_out = nsa_attention(qn, kcb, vcb, ksn, vsb, kwn, vwb, proj)
    return proj_res(a_out.reshape(T, HG_WIDTH), 0, b_out.reshape(T, NSA_HEADS * HD), 0, w_out.astype(BF16), x)


def odd_layer(x, B, S, norm, w_in, q_gain, k_gain, sinks, w_out):
    T = B * S
    qn, kn, vb = in_proj(x, norm, w_in.astype(BF16), 0, [SWA_HEADS, SWA_KV, SWA_KV], [q_gain, k_gain, None],
                         ['qk', 'qk', 'v'], [HD ** -0.5, 1.0, 1.0], B, S)
    att = swa_attention(qn, kn, vb, sinks).reshape(T, SWA_HEADS * HD)
    return proj_res(att, 0, att, 1, w_out.astype(BF16), x)


def kernel(x, ev_norm, ev_w_in, hg_lb_logits, hg_out_gain, nsa_q_gain, nsa_k_gain, nsa_cmp_pe, nsa_cmp_w1,
           nsa_cmp_w2, ev_w_out, od_norm, od_w_in, swa_q_gain, swa_k_gain, swa_sinks, od_w_out, moe_norm,
           moe_w_grp, moe_b_grp, moe_w_exp, moe_b_exp, moe_w_gate, moe_w_up, moe_w_down):
    B, S, D = x.shape
    lower_bounds = jnp.cumsum(jax.nn.softmax(hg_lb_logits.astype(F32), axis=0), axis=0)
    h = x.reshape(B * S, D)
    h = even_layer(h, B, S, ev_norm[0], ev_w_in[0], lower_bounds[0], hg_out_gain[0], nsa_q_gain[0],
                   nsa_k_gain[0], nsa_cmp_pe[0], nsa_cmp_w1[0], nsa_cmp_w2[0], ev_w_out[0])
    h = hier_moe_block(h, 0, moe_norm[0], moe_w_grp[0], moe_b_grp[0], moe_w_exp[0], moe_b_exp[0],
                       moe_w_gate, moe_w_up, moe_w_down)
    h = odd_layer(h, B, S, od_norm[0], od_w_in[0], swa_q_gain[0], swa_k_gain[0], swa_sinks[0], od_w_out[0])
    h = hier_moe_block(h, 1, moe_norm[1], moe_w_grp[1], moe_b_grp[1], moe_w_exp[1], moe_b_exp[1],
                       moe_w_gate, moe_w_up, moe_w_down)
    return h.reshape(B, S, D)
```

```python
import functools

import numpy as np
import jax
import jax.numpy as jnp
from jax import lax
from jax.experimental import pallas as pl
from jax.experimental.pallas import tpu as pltpu

F32 = jnp.float32
BF16 = jnp.bfloat16
I32 = jnp.int32

EPS = 1e-6
ROPE_THETA = 500000.0
HD = 64
ROT = HD // 4
HALF = ROT // 2

HG_DIM = 128
HG_WIDTH = 512
HG_HEADS = 4
HG_CHUNK = 256

NSA_HEADS = 8
NSA_KV = 2
NSA_HPG = NSA_HEADS // NSA_KV
CMP_LEN = 32
CMP_STRIDE = 16
CMP_HIDDEN = 256
SEL_LEN = 64
SEL_TOPN = 8
NSA_WINDOW = 512
FORCE_BONUS = 1e4

SWA_HEADS = 16
SWA_KV = 2
SWA_HPG = SWA_HEADS // SWA_KV
SWA_WINDOW = 128

N_GROUPS = 4
EXP_PER_GROUP = 8
N_EXPERTS = 32
EXPERT_FF = 512
MOE_BLOCK = 512

LANES = 128
NEG = -1e30
VMEM_V7X = 64 * 1024 * 1024
VMEM_LIMIT = VMEM_V7X * 7 // 8

C_KC = 2048
C_GATE = 2304
N_PLAIN = 2560


def _cparams(sem):
    return pltpu.CompilerParams(dimension_semantics=sem, vmem_limit_bytes=VMEM_LIMIT)


def _dot(a, b):
    return jnp.dot(a, b, preferred_element_type=F32)


def _dot_nt(a, b):
    return lax.dot_general(a, b, (((1,), (1,)), ((), ())), preferred_element_type=F32)


def _proj_res_kernel(a_ref, b_ref, wa_ref, wb_ref, r_ref, o_ref):
    acc = _dot(a_ref[...], wa_ref[...]) + _dot(b_ref[...], wb_ref[...])
    o_ref[...] = r_ref[...] + acc


def proj_res(a, a_blk, b, b_blk, w, res, tm=512):
    T, D = res.shape
    K = w.shape[0]
    kh = K // 2
    return pl.pallas_call(
        _proj_res_kernel,
        grid=(T // tm,),
        in_specs=[pl.BlockSpec((tm, kh), lambda i: (i, a_blk)),
                  pl.BlockSpec((tm, kh), lambda i: (i, b_blk)),
                  pl.BlockSpec((kh, D), lambda i: (0, 0)),
                  pl.BlockSpec((kh, D), lambda i: (1, 0)),
                  pl.BlockSpec((tm, D), lambda i: (i, 0))],
        out_specs=pl.BlockSpec((tm, D), lambda i: (i, 0)),
        out_shape=jax.ShapeDtypeStruct((T, D), F32),
        compiler_params=_cparams(("parallel",)),
        name="proj_res",
    )(a, b, w, w, res)


def _rope_tables(pos, width):
    inv = jnp.power(ROPE_THETA, -jnp.arange(HALF, dtype=F32) * 2.0 / ROT)
    ang = pos.astype(F32)[:, None] * inv[None, :]
    cos, sin = jnp.cos(ang), jnp.sin(ang)
    n = pos.shape[0]
    c = jnp.concatenate([cos, cos, jnp.ones((n, HD - ROT), F32)], axis=1)
    s = jnp.concatenate([-sin, sin, jnp.zeros((n, HD - ROT), F32)], axis=1)
    reps = width // HD
    return jnp.tile(c, (1, reps)), jnp.tile(s, (1, reps))


def _block_ones(width):
    idx = np.arange(width) // HD
    return jnp.asarray((idx[:, None] == idx[None, :]).astype(np.float32), dtype=BF16)


def _split_dot(a, b):
    hi = a.astype(BF16)
    lo = (a - hi.astype(F32)).astype(BF16)
    return _dot(hi, b) + _dot(lo, b)


def _head_norm_rope(x, gain, cos, sin, bones, scale):
    ms = _split_dot(x * x, bones) * (1.0 / HD)
    y = x * lax.rsqrt(ms + EPS) * gain
    lane = lax.broadcasted_iota(I32, y.shape, 1) % HD
    partner = jnp.where(lane < HALF, pltpu.roll(y, LANES - HALF, axis=1), pltpu.roll(y, HALF, axis=1))
    out = y * cos + partner * sin
    return out * scale if scale != 1.0 else out


def _in_proj_kernel(n_plain, tn, n_heads, kinds, scales, s_tiles, x_ref, g_ref, w_ref,
                    gain_ref, cos_ref, sin_ref, bones_ref, *o_refs):
    x = x_ref[...]
    ms = jnp.mean(x * x, axis=-1, keepdims=True)
    xn = (x * lax.rsqrt(ms + EPS) * g_ref[...]).astype(BF16)
    if n_plain:
        plain_ref, o_refs = o_refs[0], o_refs[1:]
        for j in range(n_plain // tn):
            plain_ref[:, j * tn:(j + 1) * tn] = _dot(xn, w_ref[:, j * tn:(j + 1) * tn])
    cos = cos_ref[...]
    sin = sin_ref[...]
    bones = bones_ref[...]
    tm = cos.shape[0]
    lane = lax.broadcasted_iota(I32, cos.shape, 1)
    pos = (pl.program_id(0) % s_tiles) * tm + lax.broadcasted_iota(I32, cos.shape, 0)
    col = n_plain
    for t in range(len(n_heads)):
        for c in range(n_heads[t] // 2):
            if c % (tn // LANES) == 0:
                width = min(tn, (n_heads[t] // 2 - c) * LANES)
                y_wide = _dot(xn, w_ref[:, col:col + width])
                col += width
            k = c % (tn // LANES)
            y2 = y_wide[:, k * LANES:(k + 1) * LANES]
            if kinds[t] != 'v':
                y2 = _head_norm_rope(y2, gain_ref[t:t + 1, :], cos, sin, bones, scales[t])
            for j, yh in enumerate((y2, pltpu.roll(y2, HD, axis=1))):
                if kinds[t] == 'qk':
                    y = yh[:, :HD]
                elif kinds[t] == 'pad':
                    y = jnp.where(lane < HD, yh, 0.0)
                elif kinds[t] == 'blk':
                    y = jnp.where(lane < HD, yh, jnp.where(lane - HD == pos // SEL_LEN, 1.0, 0.0))
                else:
                    y = jnp.where(lane < HD, yh, jnp.where(lane == HD, 1.0, 0.0)).T
                o_refs[t][0, 2 * c + j] = y.astype(BF16)


def in_proj(x, gain, w, n_plain, n_heads, head_gains, kinds, scales, B, S, tm=512, tn=512):
    T, D = x.shape
    N = w.shape[1]
    nt = len(n_heads)
    s_tiles = S // tm
    assert S // SEL_LEN <= LANES - HD and n_plain % tn == 0
    cos, sin = _rope_tables(jnp.arange(S), LANES)
    gain_arr = jnp.stack([jnp.tile((jnp.ones((HD,), F32) if g is None else g).astype(F32), 2) for g in head_gains])
    out_specs, out_shape = [], []
    if n_plain:
        out_specs.append(pl.BlockSpec((tm, n_plain), lambda i: (i, 0)))
        out_shape.append(jax.ShapeDtypeStruct((T, n_plain), F32))
    for t in range(nt):
        if kinds[t] == 'v':
            out_specs.append(pl.BlockSpec((1, n_heads[t], LANES, tm), lambda i: (i // s_tiles, 0, 0, i % s_tiles)))
            out_shape.append(jax.ShapeDtypeStruct((B, n_heads[t], LANES, S), BF16))
        else:
            width = HD if kinds[t] == 'qk' else LANES
            out_specs.append(pl.BlockSpec((1, n_heads[t], tm, width), lambda i: (i // s_tiles, 0, i % s_tiles, 0)))
            out_shape.append(jax.ShapeDtypeStruct((B, n_heads[t], S, width), BF16))
    return pl.pallas_call(
        functools.partial(_in_proj_kernel, n_plain, tn, tuple(n_heads), tuple(kinds), tuple(scales), s_tiles),
        grid=(T // tm,),
        in_specs=[pl.BlockSpec((tm, D), lambda i: (i, 0)),
                  pl.BlockSpec((1, D), lambda i: (0, 0)),
                  pl.BlockSpec((D, N), lambda i: (0, 0)),
                  pl.BlockSpec((nt, LANES), lambda i: (0, 0)),
                  pl.BlockSpec((tm, LANES), lambda i: (i % s_tiles, 0)),
                  pl.BlockSpec((tm, LANES), lambda i: (i % s_tiles, 0)),
                  pl.BlockSpec((LANES, LANES), lambda i: (0, 0))],
        out_specs=out_specs,
        out_shape=out_shape,
        compiler_params=_cparams(("parallel",)),
        name="in_proj",
    )(x, gain.reshape(1, D), w, gain_arr, cos, sin, _block_ones(LANES))


_N_LEVELS = HG_CHUNK.bit_length() - 1


def _hgrn_constants():
    C = HG_CHUNK
    r = np.arange(C)
    mats = [(r[None, :] <= r[:, None]).astype(np.float32),
            (r[None, :] > r[:, None]).astype(np.float32)]
    masks = []
    for lv in range(_N_LEVELS):
        n = 1 << lv
        blk = r // (2 * n)
        right = (r // n) % 2 == 1
        bnd = blk * 2 * n + n - 1
        m = np.zeros((C, C), np.float32)
        for t in range(C):
            if right[t]:
                m[t, bnd[t] + 1:t + 1] = 1.0
            else:
                m[t, t + 1:bnd[t] + 1] = 1.0
        mats.append(m)
        masks.append((blk[:, None] == blk[None, :]).astype(np.float32))
    masks.append(np.eye(C, dtype=np.float32))
    return jnp.asarray(np.concatenate(mats, axis=0), dtype=BF16), jnp.asarray(np.stack(masks))


def _hgrn_kernel(q_ref, f_ref, i_ref, g_ref, lb_ref, gain_ref, mall_ref, masks_ref, o_ref, st_ref):
    C = HG_CHUNK
    ts = q_ref.shape[1]

    @pl.when(pl.program_id(1) == 0)
    def _():
        st_ref[...] = jnp.zeros_like(st_ref)

    mall = mall_ref[...]
    row = lax.broadcasted_iota(I32, (C, HG_DIM), 0)
    gain = gain_ref[...]

    def chunk(c, carry):
        r0 = pl.multiple_of(c * C, C)
        for h in range(HG_HEADS):
            ls = slice(h * HG_DIM, (h + 1) * HG_DIM)
            q = q_ref[0, pl.ds(r0, C), ls]
            z = f_ref[0, pl.ds(r0, C), ls]
            v = i_ref[0, pl.ds(r0, C), ls]
            g = g_ref[0, pl.ds(r0, C), ls]
            lb = lb_ref[:, ls]
            lf = jnp.log(lb + (1.0 - lb) * jax.nn.sigmoid(z))
            k = (1.0 - lb) * jax.nn.sigmoid(-z)
            lf_hi = lf.astype(BF16)
            lf_lo = (lf - lf_hi.astype(F32)).astype(BF16)
            seg = _dot(mall, jnp.concatenate([lf_hi, lf_lo], axis=1))
            ex = jnp.exp(seg[:, :HG_DIM] + seg[:, HG_DIM:])
            qs = q * ex[0:C]
            ks = k * ex[C:2 * C]
            st = st_ref[h]
            o = _dot_nt(qs.astype(BF16), st.astype(BF16))
            a = masks_ref[_N_LEVELS] * _dot_nt(q.astype(BF16), k.astype(BF16))
            for lv in range(_N_LEVELS):
                e = ex[(2 + lv) * C:(3 + lv) * C]
                right = ((row >> lv) & 1) == 1
                qe = jnp.where(right, q * e, 0.0).astype(BF16)
                ke = jnp.where(right, 0.0, k * e).astype(BF16)
                a = a + masks_ref[lv] * _dot_nt(qe, ke)
            vb = v.astype(BF16)
            o = o + _dot(a.astype(BF16), vb)
            decay = ex[C - 1:C]
            st_ref[h] = st * decay + _dot(v.T.astype(BF16), ks.astype(BF16))
            ms = jnp.mean(o * o, axis=-1, keepdims=True)
            y = o * lax.rsqrt(ms + EPS) * gain * (g * jax.nn.sigmoid(g))
            o_ref[0, pl.ds(r0, C), ls] = y.astype(BF16)
        return carry

    lax.fori_loop(0, ts // C, chunk, 0)


def hgrn2(proj, lb, out_gain, ts=512):
    B, S, _ = proj.shape
    mall, masks = _hgrn_constants()
    W = HG_WIDTH
    return pl.pallas_call(
        _hgrn_kernel,
        grid=(B, S // ts),
        in_specs=[pl.BlockSpec((1, ts, W), lambda b, s: (b, s, 0)),
                  pl.BlockSpec((1, ts, W), lambda b, s: (b, s, 1)),
                  pl.BlockSpec((1, ts, W), lambda b, s: (b, s, 2)),
                  pl.BlockSpec((1, ts, W), lambda b, s: (b, s, 3)),
                  pl.BlockSpec((1, W), lambda b, s: (0, 0)),
                  pl.BlockSpec((1, HG_DIM), lambda b, s: (0, 0)),
                  pl.BlockSpec(mall.shape, lambda b, s: (0, 0)),
                  pl.BlockSpec(masks.shape, lambda b, s: (0, 0, 0))],
        out_specs=pl.BlockSpec((1, ts, W), lambda b, s: (b, s, 0)),
        out_shape=jax.ShapeDtypeStruct((B, S, W), BF16),
        scratch_shapes=[pltpu.VMEM((HG_HEADS, HG_DIM, HG_DIM), F32)],
        compiler_params=_cparams(("parallel", "arbitrary")),
        name="hgrn2",
    )(proj, proj, proj, proj, lb.reshape(1, W), out_gain.reshape(1, HG_DIM), mall, masks)


def _compress_kernel(xk_ref, xv_ref, w1k_ref, w1v_ref, pek_ref, pev_ref, w2k_ref, w2v_ref,
                     gain_ref, cos_ref, sin_ref, bones_ref, ok_ref, ov_ref):
    nc = xk_ref.shape[1] // CMP_STRIDE

    def hidden(x_ref, w1_ref, pe_ref):
        width = NSA_KV * CMP_HIDDEN
        u0 = jnp.zeros((nc, width), F32)
        u1 = jnp.zeros((nc, width), F32)
        const = jnp.zeros((8, width), F32)
        for p in range(CMP_STRIDE):
            xp = x_ref[0, pl.ds(p, nc, stride=CMP_STRIDE), :].astype(BF16)
            u0 = u0 + _dot(xp, w1_ref[p])
            u1 = u1 + _dot(xp, w1_ref[CMP_STRIDE + p])
            const = const + _dot(pe_ref[p].astype(BF16), w1_ref[p])
            const = const + _dot(pe_ref[CMP_STRIDE + p].astype(BF16), w1_ref[CMP_STRIDE + p])
        h = u0 + pltpu.roll(u1, nc - 1, axis=0) + const[0:1]
        return jax.nn.gelu(h).astype(BF16)

    yk = _dot(hidden(xk_ref, w1k_ref, pek_ref), w2k_ref[...])
    yk = _head_norm_rope(yk, gain_ref[...], cos_ref[...], sin_ref[...], bones_ref[...], 1.0)
    lane = lax.broadcasted_iota(I32, yk.shape, 1)
    ok_ref[0] = jnp.where(lane < HD, yk, 0.0).astype(BF16)
    ok_ref[1] = jnp.where(lane < HD, pltpu.roll(yk, HD, axis=1), 0.0).astype(BF16)
    yv_t = _dot_nt(w2v_ref[...], hidden(xv_ref, w1v_ref, pev_ref)).astype(BF16)
    ov_ref[0] = yv_t[:HD]
    ov_ref[1] = yv_t[HD:]


def _group_diag(w):
    z = jnp.zeros_like(w)
    return jnp.concatenate([jnp.concatenate([w, z], axis=-1), jnp.concatenate([z, w], axis=-1)], axis=-2)


def nsa_compress(proj, kc_blk, vc_blk, pe, w1, w2, k_gain):
    B, S, _ = proj.shape
    nc = S // CMP_STRIDE
    pos_end = jnp.arange(nc) * CMP_STRIDE + CMP_LEN - 1
    cos, sin = _rope_tables(pos_end, LANES)
    w1d = _group_diag(w1.reshape(2, CMP_LEN, HD, CMP_HIDDEN)).astype(BF16)
    w2d = _group_diag(w2).astype(BF16)
    pe2 = jnp.broadcast_to(jnp.tile(pe, (1, 1, NSA_KV))[:, :, None, :], (2, CMP_LEN, 8, LANES))
    full = lambda shape: pl.BlockSpec(shape, lambda b: (0,) * len(shape))
    w1_shape = (CMP_LEN, LANES, NSA_KV * CMP_HIDDEN)
    return pl.pallas_call(
        _compress_kernel,
        grid=(B,),
        in_specs=[pl.BlockSpec((1, S, LANES), lambda b: (b, 0, kc_blk)),
                  pl.BlockSpec((1, S, LANES), lambda b: (b, 0, vc_blk)),
                  full(w1_shape), full(w1_shape),
                  full((CMP_LEN, 8, LANES)), full((CMP_LEN, 8, LANES)),
                  full((NSA_KV * CMP_HIDDEN, LANES)), full((LANES, NSA_KV * CMP_HIDDEN)),
                  full((1, LANES)), full((nc, LANES)), full((nc, LANES)), full((LANES, LANES))],
        out_specs=[pl.BlockSpec((NSA_KV, nc, LANES), lambda b: (b, 0, 0)),
                   pl.BlockSpec((NSA_KV, HD, nc), lambda b: (b, 0, 0))],
        out_shape=[jax.ShapeDtypeStruct((B * NSA_KV, nc, LANES), BF16),
                   jax.ShapeDtypeStruct((B * NSA_KV, HD, nc), BF16)],
        compiler_params=_cparams(("parallel",)),
        name="nsa_compress",
    )(proj, proj, w1d[0], w1d[1], pe2[0], pe2[1], w2d[0], w2d[1].T,
      jnp.tile(k_gain.astype(F32), 2).reshape(1, LANES), cos, sin, _block_ones(LANES))


NSA_TQ = 256
NSA_TK = 1024


def _nsa_kernel(q_ref, kc_ref, vc_ref, ks_ref, vs_ref, kw_ref, vw_ref, gate_ref, cover_ref, place_ref,
                o_ref, acc_ref):
    TQ, TK, H = NSA_TQ, NSA_TK, NSA_HPG
    R = H * TQ
    t0 = pl.program_id(1) * TQ
    tl = t0 + lax.broadcasted_iota(I32, (1, TQ), 1)
    ncmp = kc_ref.shape[1]
    nsel = cover_ref.shape[0]

    def all_heads(x):
        return jnp.concatenate([x] * H, axis=1)

    def select(g):
        q = q_ref[0, g * H:(g + 1) * H].reshape(R, LANES)
        cmp_end = lax.broadcasted_iota(I32, (ncmp, 1), 0) * CMP_STRIDE + (CMP_LEN - 1)
        valid = all_heads(jnp.where(cmp_end <= tl, 1.0, 0.0))
        s = jnp.where(valid > 0.5, _dot_nt(kc_ref[g], q), NEG)
        p = jnp.exp(s - jnp.max(s, axis=0, keepdims=True)) * valid
        p_c = p / jnp.maximum(jnp.sum(p, axis=0, keepdims=True), 1e-30)
        o_c = _dot(vc_ref[g], p_c.astype(BF16))
        p_tok = p_c[:, 0:TQ]
        for h in range(1, H):
            p_tok = p_tok + p_c[:, h * TQ:(h + 1) * TQ]
        p_hi = p_tok.astype(BF16)
        p_lo = (p_tok - p_hi.astype(F32)).astype(BF16)
        imp = _dot(cover_ref[...], p_hi) + _dot(cover_ref[...], p_lo)
        jj = lax.broadcasted_iota(I32, (nsel, TQ), 0)
        jf = jj.astype(F32)
        cur = tl // SEL_LEN
        allowed = jj * SEL_LEN <= tl
        forced = (jj == 0) | (jj == cur) | (jj == cur - 1)
        score = jnp.where(allowed, imp + jnp.where(forced, FORCE_BONUS, 0.0), NEG)
        sel = jnp.zeros((nsel, TQ), F32)
        for _ in range(min(SEL_TOPN, nsel)):
            mx = jnp.max(score, axis=0, keepdims=True)
            first = jnp.min(jnp.where(score == mx, jf, float(nsel)), axis=0, keepdims=True)
            hit = jf == first
            sel = jnp.where(hit & (mx > 0.5 * NEG), 1.0, sel)
            score = jnp.where(hit, NEG, score)
        block_bias = jnp.where(sel > 0.5, 0.0, NEG).T.astype(BF16)
        placed = _dot(block_bias, place_ref[...]).astype(BF16)
        return o_c, q, q + jnp.concatenate([placed] * H, axis=0)

    def sweep_tile(g, q_sel, k0, size, m_old, causal):
        s = _dot_nt(ks_ref[0, g, pl.ds(k0, size), :], q_sel)
        if causal:
            kpos = k0 + lax.broadcasted_iota(I32, (size, 1), 0)
            s = s + all_heads(jnp.where(kpos <= tl, 0.0, NEG))
        m_new = jnp.maximum(m_old, jnp.max(s, axis=0, keepdims=True))
        p = jnp.exp(s - m_new).astype(BF16)
        acc_ref[g] = jnp.exp(m_old - m_new) * acc_ref[g] + _dot(vs_ref[0, g, :, pl.ds(k0, size)], p)
        return m_new

    def sweep(n, start, size, ms):
        def body(j, ms):
            k0 = pl.multiple_of(start + j * size, size)
            return tuple(sweep_tile(g, chosen[g][2], k0, size, ms[g], False) for g in groups)

        return lax.fori_loop(0, n, body, ms)

    def window(g, q):
        WK = NSA_WINDOW + TQ
        ws = pl.multiple_of(jnp.maximum(t0 - NSA_WINDOW, 0), TQ)
        kp = ws + lax.broadcasted_iota(I32, (WK, 1), 0)
        wbias = jnp.where((kp <= tl) & (kp > tl - NSA_WINDOW), 0.0, NEG)
        s = _dot_nt(kw_ref[0, g, pl.ds(ws, WK), :], q) + all_heads(wbias)
        p = jnp.exp(s - jnp.max(s, axis=0, keepdims=True)).astype(BF16)
        acc_w = _dot(vw_ref[0, g, :, pl.ds(ws, WK)], p)
        return acc_w[:HD] / acc_w[HD:HD + 1]

    groups = range(NSA_KV)
    chosen = [select(g) for g in groups]
    acc_ref[...] = jnp.zeros(acc_ref.shape, F32)
    n_big = t0 // TK
    m_run = sweep(n_big, 0, TK, tuple(jnp.full((1, R), NEG, F32) for _ in groups))
    m_run = sweep((t0 - n_big * TK) // TQ, n_big * TK, TQ, m_run)
    heads_out = []
    for g in groups:
        o_c, q, q_sel = chosen[g]
        sweep_tile(g, q_sel, pl.multiple_of(t0, TQ), TQ, m_run[g], True)
        acc = acc_ref[g]
        o_s = acc[:HD] / jnp.maximum(acc[HD:HD + 1], 1e-30)
        o_w = window(g, q)
        sg = jax.nn.sigmoid(gate_ref[0, :, g * LANES:(g + 1) * LANES]).T
        for h in range(H):
            cols = slice(h * TQ, (h + 1) * TQ)
            heads_out.append(sg[3 * h:3 * h + 1] * o_c[:, cols] + sg[3 * h + 1:3 * h + 2] * o_s[:, cols]
                             + sg[3 * h + 2:3 * h + 3] * o_w[:, cols])
    o_ref[0] = jnp.concatenate(heads_out, axis=0).T.astype(BF16)


def nsa_attention(qn, kcb, vcb, ksn, vsb, kwn, vwb, proj):
    B, _, S, _ = qn.shape
    G, H, TQ = NSA_KV, NSA_HPG, NSA_TQ
    assert S % NSA_TK == 0 and S >= NSA_WINDOW + TQ
    ncmp = kcb.shape[1]
    nsel = S // SEL_LEN
    n = np.arange(ncmp)[:, None] * CMP_STRIDE
    j = np.arange(nsel)[None, :] * SEL_LEN
    cover = jnp.asarray(((n < j + SEL_LEN) & (n + CMP_LEN > j)).astype(np.float32).T, dtype=BF16)
    place = np.zeros((nsel, LANES), np.float32)
    place[np.arange(nsel), HD + np.arange(nsel)] = 1.0
    R = H * TQ
    k_spec = pl.BlockSpec((1, G, S, LANES), lambda b, i: (b, 0, 0, 0))
    v_spec = pl.BlockSpec((1, G, LANES, S), lambda b, i: (b, 0, 0, 0))
    assert C_GATE % (G * LANES) == 0
    return pl.pallas_call(
        _nsa_kernel,
        grid=(B, S // TQ),
        in_specs=[pl.BlockSpec((1, G * H, TQ, LANES), lambda b, i: (b, 0, i, 0)),
                  pl.BlockSpec((G, ncmp, LANES), lambda b, i: (b, 0, 0)),
                  pl.BlockSpec((G, HD, ncmp), lambda b, i: (b, 0, 0)),
                  k_spec, v_spec, k_spec, v_spec,
                  pl.BlockSpec((1, TQ, G * LANES), lambda b, i: (b, i, C_GATE // (G * LANES))),
                  pl.BlockSpec((nsel, ncmp), lambda b, i: (0, 0)),
                  pl.BlockSpec((nsel, LANES), lambda b, i: (0, 0))],
        out_specs=pl.BlockSpec((1, TQ, G * H * HD), lambda b, i: (b, i, 0)),
        out_shape=jax.ShapeDtypeStruct((B, S, NSA_HEADS * HD), BF16),
        scratch_shapes=[pltpu.VMEM((G, LANES, R), F32)],
        compiler_params=_cparams(("parallel", "arbitrary")),
        name="nsa_attention",
    )(qn, kcb, vcb, ksn, vsb, kwn, vwb, proj, cover, jnp.asarray(place, dtype=BF16))


SWA_TQ = 256


def _swa_kernel(sink_ref, q_ref, k_ref, v_ref, o_ref):
    TQ, H = SWA_TQ, SWA_HPG
    R = H * TQ
    g = pl.program_id(1)
    t0 = pl.program_id(2) * TQ
    q = q_ref[0].reshape(R, HD)
    tl = t0 + lax.broadcasted_iota(I32, (1, TQ), 1)
    WK = SWA_WINDOW + TQ
    ws = pl.multiple_of(jnp.maximum(t0 - SWA_WINDOW, 0), LANES)
    kp = ws + lax.broadcasted_iota(I32, (WK, 1), 0)
    bias = jnp.where((kp <= tl) & (kp > tl - SWA_WINDOW), 0.0, NEG)
    s = _dot_nt(k_ref[0, 0, pl.ds(ws, WK), :], q) + jnp.concatenate([bias] * H, axis=1)
    head = lax.broadcasted_iota(I32, (1, R), 1) // TQ
    sink = jnp.zeros((1, R), F32)
    for h in range(H):
        sink = jnp.where(head == h, sink_ref[g * H + h], sink)
    m = jnp.maximum(jnp.max(s, axis=0, keepdims=True), sink)
    p = jnp.exp(s - m).astype(BF16)
    acc = _dot(v_ref[0, 0, :, pl.ds(ws, WK)], p)
    o = acc[:HD] / (acc[HD:HD + 1] + jnp.exp(sink - m))
    o_ref[0] = jnp.concatenate([o[:, h * TQ:(h + 1) * TQ] for h in range(H)], axis=0).T.astype(BF16)


def swa_attention(qn, kn, vb, sinks):
    B, _, S, _ = qn.shape
    G, H, TQ = SWA_KV, SWA_HPG, SWA_TQ
    assert S >= SWA_WINDOW + TQ
    k_spec = pl.BlockSpec((1, 1, S, HD), lambda b, g, i, sk: (b, g, 0, 0))
    v_spec = pl.BlockSpec((1, 1, LANES, S), lambda b, g, i, sk: (b, g, 0, 0))
    return pl.pallas_call(
        _swa_kernel,
        grid_spec=pltpu.PrefetchScalarGridSpec(
            num_scalar_prefetch=1,
            grid=(B, G, S // TQ),
            in_specs=[pl.BlockSpec((1, H, TQ, HD), lambda b, g, i, sk: (b, g, i, 0)), k_spec, v_spec],
            out_specs=pl.BlockSpec((1, TQ, H * HD), lambda b, g, i, sk: (b, i, g))),
        out_shape=jax.ShapeDtypeStruct((B, S, SWA_HEADS * HD), BF16),
        compiler_params=_cparams(("parallel", "parallel", "arbitrary")),
        name="swa_attention",
    )(sinks.astype(F32), qn, kn, vb)


R_GRP = 0
R_EXP = 32


SUBLANES = 8


def _store_row_tiles(ref, val, base=0):
    for s in range(SUBLANES):
        ref[pl.ds(base * SUBLANES + s, val.shape[0], stride=SUBLANES), :] = val[:, s * LANES:(s + 1) * LANES]


def _load_row_tiles(ref, rows, base=0):
    return [ref[pl.ds(base * SUBLANES + s, rows, stride=SUBLANES), :] for s in range(SUBLANES)]


def _router_kernel(x_ref, g_ref, w_ref, b_ref, tri_ref, hn_ref, route_ref, compact_ref, cnt_ref, base_ref):
    i = pl.program_id(0)
    last = pl.num_programs(0) - 1

    @pl.when(i == 0)
    def _():
        base_ref[...] = jnp.zeros_like(base_ref)

    @pl.when(i < last)
    def _():
        _route_tile(x_ref, g_ref, w_ref, b_ref, tri_ref, hn_ref, route_ref, compact_ref, base_ref)

    @pl.when(i == last)
    def _():
        hn_ref[...] = jnp.zeros_like(hn_ref)
        route_ref[...] = jnp.zeros_like(route_ref)
        compact_ref[...] = jnp.zeros_like(compact_ref)

    cnt_ref[...] = base_ref[...]


def _route_tile(x_ref, g_ref, w_ref, b_ref, tri_ref, hn_ref, route_ref, compact_ref, base_ref):
    x = x_ref[...]
    ms = jnp.mean(x * x, axis=-1, keepdims=True)
    h = x * lax.rsqrt(ms + EPS) * g_ref[...]
    _store_row_tiles(hn_ref, h)
    h_hi = h.astype(BF16)
    h_lo = (h - h_hi.astype(F32)).astype(BF16)
    logits = _dot(h_hi, w_ref[0]) + _dot(h_lo, w_ref[0]) + _dot(h_hi, w_ref[1]) + b_ref[...]
    lane = lax.broadcasted_iota(I32, logits.shape, 1)
    is_grp = lane < N_GROUPS
    gl = jnp.where(is_grp, logits, NEG)
    gmax = jnp.max(gl, axis=-1, keepdims=True)
    lanef = lane.astype(F32)
    gidx = jnp.min(jnp.where(gl == gmax, lanef, float(LANES)), axis=-1, keepdims=True)
    gw = 1.0 / jnp.sum(jnp.where(is_grp, jnp.exp(logits - gmax), 0.0), axis=-1, keepdims=True)
    in_grp = ((lane - R_EXP) // EXP_PER_GROUP).astype(F32) == gidx
    w0 = jnp.where(in_grp, logits, NEG)
    m1 = jnp.max(w0, axis=-1, keepdims=True)
    i1 = jnp.min(jnp.where(w0 == m1, lanef, float(LANES)), axis=-1, keepdims=True)
    w1 = jnp.where(lanef == i1, NEG, w0)
    m2 = jnp.max(w1, axis=-1, keepdims=True)
    i2 = jnp.min(jnp.where(w1 == m2, lanef, float(LANES)), axis=-1, keepdims=True)
    e2 = jnp.exp(m2 - m1)
    p1 = gw / (1.0 + e2)
    p2 = gw * e2 / (1.0 + e2)
    oh1 = jnp.where(lanef == i1, 1.0, 0.0)
    oh2 = jnp.where(lanef == i2, 1.0, 0.0)
    oh = oh1 + oh2
    before = _dot(tri_ref[...], oh.astype(BF16)) + base_ref[0:1, :]
    r1 = jnp.sum(before * oh1, axis=-1, keepdims=True)
    r2 = jnp.sum(before * oh2, axis=-1, keepdims=True)
    base_ref[...] = base_ref[...] + jnp.sum(oh, axis=0, keepdims=True)
    route = jnp.zeros_like(logits)
    for ln, val in enumerate((i1 - R_EXP, i2 - R_EXP, p1, p2, r1, r2)):
        route = jnp.where(lane == ln, val, route)
    route_ref[...] = route
    compact_ref[0] = route.T[0:SUBLANES]


def moe_router(x, gain, w_grp, b_grp, w_exp, b_exp, tm=512):
    T, D = x.shape
    nt = T // tm
    w = jnp.zeros((D, LANES), F32).at[:, R_GRP:R_GRP + N_GROUPS].set(w_grp).at[:, R_EXP:R_EXP + N_EXPERTS].set(w_exp)
    b = jnp.zeros((1, LANES), F32).at[0, R_GRP:R_GRP + N_GROUPS].set(b_grp).at[0, R_EXP:R_EXP + N_EXPERTS].set(b_exp)
    w_hi = w.astype(BF16)
    w = jnp.stack([w_hi, (w - w_hi.astype(F32)).astype(BF16)])
    tri = jnp.asarray(np.tril(np.ones((tm, tm), np.float32), -1), dtype=BF16)
    return pl.pallas_call(
        _router_kernel,
        grid=(nt + 1,),
        in_specs=[pl.BlockSpec((tm, D), lambda i: (jnp.minimum(i, nt - 1), 0)),
                  pl.BlockSpec((1, D), lambda i: (0, 0)),
                  pl.BlockSpec((2, D, LANES), lambda i: (0, 0, 0)),
                  pl.BlockSpec((1, LANES), lambda i: (0, 0)),
                  pl.BlockSpec((tm, tm), lambda i: (0, 0))],
        out_specs=[pl.BlockSpec((tm * SUBLANES, LANES), lambda i: (i, 0)),
                   pl.BlockSpec((tm, LANES), lambda i: (i, 0)),
                   pl.BlockSpec((1, SUBLANES, tm), lambda i: (i, 0, 0)),
                   pl.BlockSpec((8, LANES), lambda i: (0, 0))],
        out_shape=[jax.ShapeDtypeStruct(((T + tm) * SUBLANES, LANES), F32),
                   jax.ShapeDtypeStruct((T + tm, LANES), F32),
                   jax.ShapeDtypeStruct((nt + 1, SUBLANES, tm), F32),
                   jax.ShapeDtypeStruct((8, LANES), F32)],
        scratch_shapes=[pltpu.VMEM((8, LANES), F32)],
        compiler_params=_cparams(("arbitrary",)),
        name="moe_router",
    )(x, gain.reshape(1, D), w, b, tri)


def _row_copy(src, src_row, dst, dst_row, sem):
    def tile(ref, row):
        return ref.at[pl.ds(pl.multiple_of(row * SUBLANES, SUBLANES), SUBLANES)]

    return pltpu.make_async_copy(tile(src, src_row), tile(dst, dst_row), sem)


DISPATCH_TILE = 512
ROW_UNROLL = 4


def _wait_rows(hbm_ref, sem, n):
    view = hbm_ref.at[pl.ds(0, n * SUBLANES)]
    pltpu.make_async_copy(view, view, sem).wait()


def _dispatch_kernel(dest_ref, x_ref, xb_ref, stage, sems):
    i = pl.program_id(0)
    last = pl.num_programs(0) - 1
    slot = i % 2

    @pl.when(i >= 2)
    def _():
        _wait_rows(xb_ref, sems.at[slot], DISPATCH_TILE)

    stage[slot] = x_ref[...]

    def issue(t, c):
        _row_copy(stage.at[slot], t, xb_ref, dest_ref[0, 0, 2 * t], sems.at[slot]).start(priority=0)
        _row_copy(stage.at[slot], t, xb_ref, dest_ref[0, 0, 2 * t + 1], sems.at[slot]).start(priority=1)
        return c

    lax.fori_loop(0, DISPATCH_TILE // 2, issue, 0, unroll=ROW_UNROLL)

    @pl.when(i == last)
    def _():
        _wait_rows(xb_ref, sems.at[slot], DISPATCH_TILE)

    @pl.when((i == last) & (i >= 1))
    def _():
        _wait_rows(xb_ref, sems.at[1 - slot], DISPATCH_TILE)


def moe_dispatch(hn, dest_all, n_tok):
    n_rows = dest_all.shape[0]
    tm = DISPATCH_TILE // 2
    nt = n_rows // DISPATCH_TILE
    n_tok_tiles = n_tok // tm
    return pl.pallas_call(
        _dispatch_kernel,
        grid=(nt,),
        in_specs=[pl.BlockSpec((1, 1, DISPATCH_TILE), lambda i: (i, 0, 0), memory_space=pltpu.SMEM),
                  pl.BlockSpec((tm * SUBLANES, LANES), lambda i: (jnp.minimum(i, n_tok_tiles), 0))],
        out_specs=pl.BlockSpec(memory_space=pl.ANY),
        out_shape=jax.ShapeDtypeStruct((n_rows * SUBLANES, LANES), F32),
        scratch_shapes=[pltpu.VMEM((2, tm * SUBLANES, LANES), F32), pltpu.SemaphoreType.DMA((2,))],
        compiler_params=_cparams(("arbitrary",)),
        name="moe_dispatch",
    )(dest_all.reshape(nt, 1, DISPATCH_TILE), hn)


def _expert_kernel(be_ref, x_ref, wg_ref, wu_ref, wd_ref, y_ref, wg_b, wu_b, wd_b):
    i = pl.program_id(0)

    @pl.when((i == 0) | (be_ref[i] != be_ref[jnp.maximum(i - 1, 0)]))
    def _():
        wg_b[...] = wg_ref[0, 0].astype(BF16)
        wu_b[...] = wu_ref[0, 0].astype(BF16)
        wd_b[...] = wd_ref[0, 0].astype(BF16)

    x = jnp.concatenate(_load_row_tiles(x_ref, MOE_BLOCK), axis=1).astype(BF16)
    hid = _dot(x, wg_b[...])
    hid = hid * jax.nn.sigmoid(hid) * _dot(x, wu_b[...])
    _store_row_tiles(y_ref, _dot(hid.astype(BF16), wd_b[...]))


def moe_experts(xb, blk_expert, layer, w_gate, w_up, w_down):
    n_rows = xb.shape[0] // SUBLANES
    D = w_gate.shape[2]
    blk = (MOE_BLOCK * SUBLANES, LANES)
    wspec = lambda shape: pl.BlockSpec((1, 1) + shape, lambda i, be: (layer, be[i], 0, 0))
    return pl.pallas_call(
        _expert_kernel,
        grid_spec=pltpu.PrefetchScalarGridSpec(
            num_scalar_prefetch=1,
            grid=(n_rows // MOE_BLOCK,),
            in_specs=[pl.BlockSpec(blk, lambda i, be: (i, 0)),
                      wspec((D, EXPERT_FF)), wspec((D, EXPERT_FF)), wspec((EXPERT_FF, D))],
            out_specs=pl.BlockSpec(blk, lambda i, be: (i, 0)),
            scratch_shapes=[pltpu.VMEM((D, EXPERT_FF), BF16), pltpu.VMEM((D, EXPERT_FF), BF16),
                            pltpu.VMEM((EXPERT_FF, D), BF16)]),
        out_shape=jax.ShapeDtypeStruct((n_rows * SUBLANES, LANES), F32),
        compiler_params=_cparams(("arbitrary",)),
        name="moe_experts",
    )(blk_expert, xb, w_gate, w_up, w_down)


def _combine_kernel(dest_ref, dest_next_ref, x_ref, route_ref, yb_ref, o_ref, ybuf, sems):
    tm = x_ref.shape[0]
    i = pl.program_id(0)
    slot = i % 2

    def gather(d_ref, s):
        def body(t, c):
            _row_copy(yb_ref, d_ref[0, 0, 2 * t], ybuf.at[s], t, sems.at[s]).start(priority=0)
            _row_copy(yb_ref, d_ref[0, 0, 2 * t + 1], ybuf.at[s], tm + t, sems.at[s]).start(priority=1)
            return c

        lax.fori_loop(0, tm, body, 0, unroll=ROW_UNROLL)

    @pl.when(i == 0)
    def _():
        gather(dest_ref, 0)

    @pl.when(i + 1 < pl.num_programs(0))
    def _():
        gather(dest_next_ref, 1 - slot)

    _wait_rows(yb_ref, sems.at[slot], 2 * tm)
    r = route_ref[...]
    w0, w1 = r[:, 2:3], r[:, 3:4]
    y0 = _load_row_tiles(ybuf.at[slot], tm)
    y1 = _load_row_tiles(ybuf.at[slot], tm, base=tm)
    for s in range(SUBLANES):
        cols = slice(s * LANES, (s + 1) * LANES)
        o_ref[:, cols] = x_ref[:, cols] + w0 * y0[s] + w1 * y1[s]


def moe_combine(x, yb, dest, route):
    T, D = x.shape
    tm = DISPATCH_TILE // 2
    nt = T // tm
    dest3 = dest.reshape(nt, 1, DISPATCH_TILE)
    return pl.pallas_call(
        _combine_kernel,
        grid=(nt,),
        in_specs=[pl.BlockSpec((1, 1, DISPATCH_TILE), lambda i: (i, 0, 0), memory_space=pltpu.SMEM),
                  pl.BlockSpec((1, 1, DISPATCH_TILE), lambda i: (jnp.minimum(i + 1, nt - 1), 0, 0),
                               memory_space=pltpu.SMEM),
                  pl.BlockSpec((tm, D), lambda i: (i, 0)),
                  pl.BlockSpec((tm, LANES), lambda i: (i, 0)),
                  pl.BlockSpec(memory_space=pl.ANY)],
        out_specs=pl.BlockSpec((tm, D), lambda i: (i, 0)),
        out_shape=jax.ShapeDtypeStruct((T, D), F32),
        scratch_shapes=[pltpu.VMEM((2, DISPATCH_TILE * SUBLANES, LANES), F32), pltpu.SemaphoreType.DMA((2,))],
        compiler_params=_cparams(("arbitrary",)),
        name="moe_combine",
    )(dest3, dest3, x, route, yb)


def hier_moe_block(x, layer, gain, w_grp, b_grp, w_exp, b_exp, w_gate, w_up, w_down):
    T, D = x.shape
    hn, route, compact, cnt = moe_router(x, gain, w_grp, b_grp, w_exp, b_exp)
    A = 2 * T
    n_rows = -(-A // MOE_BLOCK) * MOE_BLOCK + N_EXPERTS * MOE_BLOCK
    n_blocks = n_rows // MOE_BLOCK
    counts = cnt[0, R_EXP:R_EXP + N_EXPERTS].astype(I32)
    pcounts = (counts + MOE_BLOCK - 1) // MOE_BLOCK * MOE_BLOCK
    pend = jnp.cumsum(pcounts)
    pstart = pend - pcounts
    experts = jnp.arange(N_EXPERTS, dtype=I32)
    fields = compact[:-1].astype(I32)
    slot_e = jnp.stack([fields[:, 0], fields[:, 1]], axis=-1).reshape(T, 2)
    slot_rank = jnp.stack([fields[:, 4], fields[:, 5]], axis=-1).reshape(T, 2)
    seg_start = jnp.sum(jnp.where(slot_e[..., None] == experts, pstart, 0), axis=-1)
    dest = (slot_rank + seg_start).reshape(A)
    n_pad = n_rows - A
    pad_cnt = jnp.concatenate([pcounts - counts, (n_rows - pend[-1])[None]])
    pad_start = jnp.concatenate([pstart + counts, pend[-1:]])
    pad_end = jnp.cumsum(pad_cnt)
    k = jnp.arange(n_pad, dtype=I32)
    seg = (k[:, None] >= pad_end[None, :]).astype(I32).sum(axis=1)
    seg_hot = seg[:, None] == jnp.arange(N_EXPERTS + 1, dtype=I32)[None, :]
    pad_dest = k + jnp.sum(jnp.where(seg_hot, (pad_start - (pad_end - pad_cnt))[None, :], 0), axis=1)
    dest_all = jnp.concatenate([dest, pad_dest]).astype(I32)
    blk_start = jnp.arange(n_blocks, dtype=I32) * MOE_BLOCK
    blk_expert = jnp.minimum((pend[None, :] <= blk_start[:, None]).astype(I32).sum(axis=1), N_EXPERTS - 1)
    xb = moe_dispatch(hn, dest_all, T)
    yb = moe_experts(xb, blk_expert, layer, w_gate, w_up, w_down)
    return moe_combine(x, yb, dest, route)


def _even_w_in(w_in):
    hgw, qw, kvw, ng = 4 * HG_WIDTH, NSA_HEADS * HD, NSA_KV * HD, 3 * NSA_HPG
    kc, vc, ks, vs, kw, vw = (hgw + qw + i * kvw for i in range(6))
    gate = hgw + qw + 6 * kvw
    pad = jnp.zeros((w_in.shape[0], LANES - ng), w_in.dtype)
    cols = [w_in[:, :hgw], w_in[:, kc:kc + kvw], w_in[:, vc:vc + kvw]]
    for g in range(NSA_KV):
        cols += [w_in[:, gate + g * ng:gate + (g + 1) * ng], pad]
    cols += [w_in[:, hgw:hgw + qw]] + [w_in[:, c:c + kvw] for c in (ks, kw, vs, vw)]
    return jnp.concatenate(cols, axis=1).astype(BF16)


def even_layer(x, B, S, norm, w_in, lb, hg_gain, q_gain, k_gain, cmp_pe, cmp_w1, cmp_w2, w_out):
    T = B * S
    proj, qn, ksn, kwn, vsb, vwb = in_proj(
        x, norm, _even_w_in(w_in), N_PLAIN, [NSA_HEADS, NSA_KV, NSA_KV, NSA_KV, NSA_KV],
        [q_gain, k_gain[1], k_gain[2], None, None], ['pad', 'blk', 'pad', 'v', 'v'],
        [HD ** -0.5, 1.0, 1.0, 1.0, 1.0], B, S)
    proj = proj.reshape(B, S, N_PLAIN)
    a_out = hgrn2(proj, lb, hg_gain)
    kcb, vcb = nsa_compress(proj, C_KC // LANES, C_KC // LANES + 1, cmp_pe, cmp_w1, cmp_w2, k_gain[0])
    b_out = nsa_attention(qn, kcb, vcb, ksn, vsb, kwn, vwb, proj)
    return proj_res(a_out.reshape(T, HG_WIDTH), 0, b_out.reshape(T, NSA_HEADS * HD), 0, w_out.astype(BF16), x)


def odd_layer(x, B, S, norm, w_in, q_gain, k_gain, sinks, w_out):
    T = B * S
    qn, kn, vb = in_proj(x, norm, w_in.astype(BF16), 0, [SWA_HEADS, SWA_KV, SWA_KV], [q_gain, k_gain, None],
                         ['qk', 'qk', 'v'], [HD ** -0.5, 1.0, 1.0], B, S)
    att = swa_attention(qn, kn, vb, sinks).reshape(T, SWA_HEADS * HD)
    return proj_res(att, 0, att, 1, w_out.astype(BF16), x)


def kernel(x, ev_norm, ev_w_in, hg_lb_logits, hg_out_gain, nsa_q_gain, nsa_k_gain, nsa_cmp_pe, nsa_cmp_w1,
           nsa_cmp_w2, ev_w_out, od_norm, od_w_in, swa_q_gain, swa_k_gain, swa_sinks, od_w_out, moe_norm,
           moe_w_grp, moe_b_grp, moe_w_exp, moe_b_exp, moe_w_gate, moe_w_up, moe_w_down):
    B, S, D = x.shape
    lower_bounds = jnp.cumsum(jax.nn.softmax(hg_lb_logits.astype(F32), axis=0), axis=0)
    h = x.reshape(B * S, D)
    h = even_layer(h, B, S, ev_norm[0], ev_w_in[0], lower_bounds[0], hg_out_gain[0], nsa_q_gain[0],
                   nsa_k_gain[0], nsa_cmp_pe[0], nsa_cmp_w1[0], nsa_cmp_w2[0], ev_w_out[0])
    h = hier_moe_block(h, 0, moe_norm[0], moe_w_grp[0], moe_b_grp[0], moe_w_exp[0], moe_b_exp[0],
                       moe_w_gate, moe_w_up, moe_w_down)
    h = odd_layer(h, B, S, od_norm[0], od_w_in[0], swa_q_gain[0], swa_k_gain[0], swa_sinks[0], od_w_out[0])
    h = hier_moe_block(h, 1, moe_norm[1], moe_w_grp[1], moe_b_grp[1], moe_w_exp[1], moe_b_exp[1],
                       moe_w_gate, moe_w_up, moe_w_down)
    return h.reshape(B, S, D)
```

```python
import functools

import numpy as np
import jax
import jax.numpy as jnp
from jax import lax
from jax.experimental import pallas as pl
from jax.experimental.pallas import tpu as pltpu

F32 = jnp.float32
BF16 = jnp.bfloat16
I32 = jnp.int32

EPS = 1e-6
ROPE_THETA = 500000.0
HD = 64
ROT = HD // 4
HALF = ROT // 2

HG_DIM = 128
HG_WIDTH = 512
HG_HEADS = 4
HG_CHUNK = 256

NSA_HEADS = 8
NSA_KV = 2
NSA_HPG = NSA_HEADS // NSA_KV
CMP_LEN = 32
CMP_STRIDE = 16
CMP_HIDDEN = 256
SEL_LEN = 64
SEL_TOPN = 8
NSA_WINDOW = 512
FORCE_BONUS = 1e4

SWA_HEADS = 16
SWA_KV = 2
SWA_HPG = SWA_HEADS // SWA_KV
SWA_WINDOW = 128

N_GROUPS = 4
EXP_PER_GROUP = 8
N_EXPERTS = 32
EXPERT_FF = 512
MOE_BLOCK = 512

LANES = 128
NEG = -1e30
VMEM_V7X = 64 * 1024 * 1024
VMEM_LIMIT = VMEM_V7X * 7 // 8

C_KC = 2048
C_GATE = 2304
N_PLAIN = 2560


def _cparams(sem):
    return pltpu.CompilerParams(dimension_semantics=sem, vmem_limit_bytes=VMEM_LIMIT)


def _dot(a, b):
    return jnp.dot(a, b, preferred_element_type=F32)


def _dot_nt(a, b):
    return lax.dot_general(a, b, (((1,), (1,)), ((), ())), preferred_element_type=F32)


def _proj_res_kernel(a_ref, b_ref, wa_ref, wb_ref, r_ref, o_ref):
    acc = _dot(a_ref[...], wa_ref[...]) + _dot(b_ref[...], wb_ref[...])
    o_ref[...] = r_ref[...] + acc


def proj_res(a, a_blk, b, b_blk, w, res, tm=512):
    T, D = res.shape
    K = w.shape[0]
    kh = K // 2
    return pl.pallas_call(
        _proj_res_kernel,
        grid=(T // tm,),
        in_specs=[pl.BlockSpec((tm, kh), lambda i: (i, a_blk)),
                  pl.BlockSpec((tm, kh), lambda i: (i, b_blk)),
                  pl.BlockSpec((kh, D), lambda i: (0, 0)),
                  pl.BlockSpec((kh, D), lambda i: (1, 0)),
                  pl.BlockSpec((tm, D), lambda i: (i, 0))],
        out_specs=pl.BlockSpec((tm, D), lambda i: (i, 0)),
        out_shape=jax.ShapeDtypeStruct((T, D), F32),
        compiler_params=_cparams(("parallel",)),
        name="proj_res",
    )(a, b, w, w, res)


def _rope_tables(pos, width):
    inv = jnp.power(ROPE_THETA, -jnp.arange(HALF, dtype=F32) * 2.0 / ROT)
    ang = pos.astype(F32)[:, None] * inv[None, :]
    cos, sin = jnp.cos(ang), jnp.sin(ang)
    n = pos.shape[0]
    c = jnp.concatenate([cos, cos, jnp.ones((n, HD - ROT), F32)], axis=1)
    s = jnp.concatenate([-sin, sin, jnp.zeros((n, HD - ROT), F32)], axis=1)
    reps = width // HD
    return jnp.tile(c, (1, reps)), jnp.tile(s, (1, reps))


def _block_ones(width):
    idx = np.arange(width) // HD
    return jnp.asarray((idx[:, None] == idx[None, :]).astype(np.float32), dtype=BF16)


def _split_dot(a, b):
    hi = a.astype(BF16)
    lo = (a - hi.astype(F32)).astype(BF16)
    return _dot(hi, b) + _dot(lo, b)


def _head_norm_rope(x, gain, cos, sin, bones, scale):
    ms = _split_dot(x * x, bones) * (1.0 / HD)
    y = x * lax.rsqrt(ms + EPS) * gain
    lane = lax.broadcasted_iota(I32, y.shape, 1) % HD
    partner = jnp.where(lane < HALF, pltpu.roll(y, LANES - HALF, axis=1), pltpu.roll(y, HALF, axis=1))
    out = y * cos + partner * sin
    return out * scale if scale != 1.0 else out


def _in_proj_kernel(n_plain, tn, n_heads, kinds, scales, s_tiles, x_ref, g_ref, w_ref,
                    gain_ref, cos_ref, sin_ref, bones_ref, *o_refs):
    x = x_ref[...]
    ms = jnp.mean(x * x, axis=-1, keepdims=True)
    xn = (x * lax.rsqrt(ms + EPS) * g_ref[...]).astype(BF16)
    if n_plain:
        plain_ref, o_refs = o_refs[0], o_refs[1:]
        for j in range(n_plain // tn):
            plain_ref[:, j * tn:(j + 1) * tn] = _dot(xn, w_ref[:, j * tn:(j + 1) * tn])
    cos = cos_ref[...]
    sin = sin_ref[...]
    bones = bones_ref[...]
    tm = cos.shape[0]
    lane = lax.broadcasted_iota(I32, cos.shape, 1)
    pos = (pl.program_id(0) % s_tiles) * tm + lax.broadcasted_iota(I32, cos.shape, 0)
    col = n_plain
    for t in range(len(n_heads)):
        for c in range(n_heads[t] // 2):
            if c % (tn // LANES) == 0:
                width = min(tn, (n_heads[t] // 2 - c) * LANES)
                y_wide = _dot(xn, w_ref[:, col:col + width])
                col += width
            k = c % (tn // LANES)
            y2 = y_wide[:, k * LANES:(k + 1) * LANES]
            if kinds[t] != 'v':
                y2 = _head_norm_rope(y2, gain_ref[t:t + 1, :], cos, sin, bones, scales[t])
            for j, yh in enumerate((y2, pltpu.roll(y2, HD, axis=1))):
                if kinds[t] == 'qk':
                    y = yh[:, :HD]
                elif kinds[t] == 'pad':
                    y = jnp.where(lane < HD, yh, 0.0)
                elif kinds[t] == 'blk':
                    y = jnp.where(lane < HD, yh, jnp.where(lane - HD == pos // SEL_LEN, 1.0, 0.0))
                else:
                    y = jnp.where(lane < HD, yh, jnp.where(lane == HD, 1.0, 0.0)).T
                o_refs[t][0, 2 * c + j] = y.astype(BF16)


def in_proj(x, gain, w, n_plain, n_heads, head_gains, kinds, scales, B, S, tm=512, tn=512):
    T, D = x.shape
    N = w.shape[1]
    nt = len(n_heads)
    s_tiles = S // tm
    assert S // SEL_LEN <= LANES - HD and n_plain % tn == 0
    cos, sin = _rope_tables(jnp.arange(S), LANES)
    gain_arr = jnp.stack([jnp.tile((jnp.ones((HD,), F32) if g is None else g).astype(F32), 2) for g in head_gains])
    out_specs, out_shape = [], []
    if n_plain:
        out_specs.append(pl.BlockSpec((tm, n_plain), lambda i: (i, 0)))
        out_shape.append(jax.ShapeDtypeStruct((T, n_plain), F32))
    for t in range(nt):
        if kinds[t] == 'v':
            out_specs.append(pl.BlockSpec((1, n_heads[t], LANES, tm), lambda i: (i // s_tiles, 0, 0, i % s_tiles)))
            out_shape.append(jax.ShapeDtypeStruct((B, n_heads[t], LANES, S), BF16))
        else:
            width = HD if kinds[t] == 'qk' else LANES
            out_specs.append(pl.BlockSpec((1, n_heads[t], tm, width), lambda i: (i // s_tiles, 0, i % s_tiles, 0)))
            out_shape.append(jax.ShapeDtypeStruct((B, n_heads[t], S, width), BF16))
    return pl.pallas_call(
        functools.partial(_in_proj_kernel, n_plain, tn, tuple(n_heads), tuple(kinds), tuple(scales), s_tiles),
        grid=(T // tm,),
        in_specs=[pl.BlockSpec((tm, D), lambda i: (i, 0)),
                  pl.BlockSpec((1, D), lambda i: (0, 0)),
                  pl.BlockSpec((D, N), lambda i: (0, 0)),
                  pl.BlockSpec((nt, LANES), lambda i: (0, 0)),
                  pl.BlockSpec((tm, LANES), lambda i: (i % s_tiles, 0)),
                  pl.BlockSpec((tm, LANES), lambda i: (i % s_tiles, 0)),
                  pl.BlockSpec((LANES, LANES), lambda i: (0, 0))],
        out_specs=out_specs,
        out_shape=out_shape,
        compiler_params=_cparams(("parallel",)),
        name="in_proj",
    )(x, gain.reshape(1, D), w, gain_arr, cos, sin, _block_ones(LANES))


_N_LEVELS = HG_CHUNK.bit_length() - 1


def _hgrn_constants():
    C = HG_CHUNK
    r = np.arange(C)
    mats = [(r[None, :] <= r[:, None]).astype(np.float32),
            (r[None, :] > r[:, None]).astype(np.float32)]
    masks = []
    for lv in range(_N_LEVELS):
        n = 1 << lv
        blk = r // (2 * n)
        right = (r // n) % 2 == 1
        bnd = blk * 2 * n + n - 1
        m = np.zeros((C, C), np.float32)
        for t in range(C):
            if right[t]:
                m[t, bnd[t] + 1:t + 1] = 1.0
            else:
                m[t, t + 1:bnd[t] + 1] = 1.0
        mats.append(m)
        masks.append((blk[:, None] == blk[None, :]).astype(np.float32))
    masks.append(np.eye(C, dtype=np.float32))
    return jnp.asarray(np.concatenate(mats, axis=0), dtype=BF16), jnp.asarray(np.stack(masks))


def _hgrn_kernel(q_ref, f_ref, i_ref, g_ref, lb_ref, gain_ref, mall_ref, masks_ref, o_ref, st_ref):
    C = HG_CHUNK
    ts = q_ref.shape[1]

    @pl.when(pl.program_id(1) == 0)
    def _():
        st_ref[...] = jnp.zeros_like(st_ref)

    mall = mall_ref[...]
    row = lax.broadcasted_iota(I32, (C, HG_DIM), 0)
    gain = gain_ref[...]

    def chunk(c, carry):
        r0 = pl.multiple_of(c * C, C)
        for h in range(HG_HEADS):
            ls = slice(h * HG_DIM, (h + 1) * HG_DIM)
            q = q_ref[0, pl.ds(r0, C), ls]
            z = f_ref[0, pl.ds(r0, C), ls]
            v = i_ref[0, pl.ds(r0, C), ls]
            g = g_ref[0, pl.ds(r0, C), ls]
            lb = lb_ref[:, ls]
            lf = jnp.log(lb + (1.0 - lb) * jax.nn.sigmoid(z))
            k = (1.0 - lb) * jax.nn.sigmoid(-z)
            lf_hi = lf.astype(BF16)
            lf_lo = (lf - lf_hi.astype(F32)).astype(BF16)
            seg = _dot(mall, jnp.concatenate([lf_hi, lf_lo], axis=1))
            ex = jnp.exp(seg[:, :HG_DIM] + seg[:, HG_DIM:])
            qs = q * ex[0:C]
            ks = k * ex[C:2 * C]
            st = st_ref[h]
            o = _dot_nt(qs.astype(BF16), st.astype(BF16))
            a = masks_ref[_N_LEVELS] * _dot_nt(q.astype(BF16), k.astype(BF16))
            for lv in range(_N_LEVELS):
                e = ex[(2 + lv) * C:(3 + lv) * C]
                right = ((row >> lv) & 1) == 1
                qe = jnp.where(right, q * e, 0.0).astype(BF16)
                ke = jnp.where(right, 0.0, k * e).astype(BF16)
                a = a + masks_ref[lv] * _dot_nt(qe, ke)
            vb = v.astype(BF16)
            o = o + _dot(a.astype(BF16), vb)
            decay = ex[C - 1:C]
            st_ref[h] = st * decay + _dot(v.T.astype(BF16), ks.astype(BF16))
            ms = jnp.mean(o * o, axis=-1, keepdims=True)
            y = o * lax.rsqrt(ms + EPS) * gain * (g * jax.nn.sigmoid(g))
            o_ref[0, pl.ds(r0, C), ls] = y.astype(BF16)
        return carry

    lax.fori_loop(0, ts // C, chunk, 0)


def hgrn2(proj, lb, out_gain, ts=512):
    B, S, _ = proj.shape
    mall, masks = _hgrn_constants()
    W = HG_WIDTH
    return pl.pallas_call(
        _hgrn_kernel,
        grid=(B, S // ts),
        in_specs=[pl.BlockSpec((1, ts, W), lambda b, s: (b, s, 0)),
                  pl.BlockSpec((1, ts, W), lambda b, s: (b, s, 1)),
                  pl.BlockSpec((1, ts, W), lambda b, s: (b, s, 2)),
                  pl.BlockSpec((1, ts, W), lambda b, s: (b, s, 3)),
                  pl.BlockSpec((1, W), lambda b, s: (0, 0)),
                  pl.BlockSpec((1, HG_DIM), lambda b, s: (0, 0)),
                  pl.BlockSpec(mall.shape, lambda b, s: (0, 0)),
                  pl.BlockSpec(masks.shape, lambda b, s: (0, 0, 0))],
        out_specs=pl.BlockSpec((1, ts, W), lambda b, s: (b, s, 0)),
        out_shape=jax.ShapeDtypeStruct((B, S, W), BF16),
        scratch_shapes=[pltpu.VMEM((HG_HEADS, HG_DIM, HG_DIM), F32)],
        compiler_params=_cparams(("parallel", "arbitrary")),
        name="hgrn2",
    )(proj, proj, proj, proj, lb.reshape(1, W), out_gain.reshape(1, HG_DIM), mall, masks)


def _compress_kernel(xk_ref, xv_ref, w1k_ref, w1v_ref, pek_ref, pev_ref, w2k_ref, w2v_ref,
                     gain_ref, cos_ref, sin_ref, bones_ref, ok_ref, ov_ref):
    nc = xk_ref.shape[1] // CMP_STRIDE

    def hidden(x_ref, w1_ref, pe_ref):
        width = NSA_KV * CMP_HIDDEN
        u0 = jnp.zeros((nc, width), F32)
        u1 = jnp.zeros((nc, width), F32)
        const = jnp.zeros((8, width), F32)
        for p in range(CMP_STRIDE):
            xp = x_ref[0, pl.ds(p, nc, stride=CMP_STRIDE), :].astype(BF16)
            u0 = u0 + _dot(xp, w1_ref[p])
            u1 = u1 + _dot(xp, w1_ref[CMP_STRIDE + p])
            const = const + _dot(pe_ref[p].astype(BF16), w1_ref[p])
            const = const + _dot(pe_ref[CMP_STRIDE + p].astype(BF16), w1_ref[CMP_STRIDE + p])
        h = u0 + pltpu.roll(u1, nc - 1, axis=0) + const[0:1]
        return jax.nn.gelu(h).astype(BF16)

    yk = _dot(hidden(xk_ref, w1k_ref, pek_ref), w2k_ref[...])
    yk = _head_norm_rope(yk, gain_ref[...], cos_ref[...], sin_ref[...], bones_ref[...], 1.0)
    lane = lax.broadcasted_iota(I32, yk.shape, 1)
    ok_ref[0] = jnp.where(lane < HD, yk, 0.0).astype(BF16)
    ok_ref[1] = jnp.where(lane < HD, pltpu.roll(yk, HD, axis=1), 0.0).astype(BF16)
    yv_t = _dot_nt(w2v_ref[...], hidden(xv_ref, w1v_ref, pev_ref)).astype(BF16)
    ov_ref[0] = yv_t[:HD]
    ov_ref[1] = yv_t[HD:]


def _group_diag(w):
    z = jnp.zeros_like(w)
    return jnp.concatenate([jnp.concatenate([w, z], axis=-1), jnp.concatenate([z, w], axis=-1)], axis=-2)


def nsa_compress(proj, kc_blk, vc_blk, pe, w1, w2, k_gain):
    B, S, _ = proj.shape
    nc = S // CMP_STRIDE
    pos_end = jnp.arange(nc) * CMP_STRIDE + CMP_LEN - 1
    cos, sin = _rope_tables(pos_end, LANES)
    w1d = _group_diag(w1.reshape(2, CMP_LEN, HD, CMP_HIDDEN)).astype(BF16)
    w2d = _group_diag(w2).astype(BF16)
    pe2 = jnp.broadcast_to(jnp.tile(pe, (1, 1, NSA_KV))[:, :, None, :], (2, CMP_LEN, 8, LANES))
    full = lambda shape: pl.BlockSpec(shape, lambda b: (0,) * len(shape))
    w1_shape = (CMP_LEN, LANES, NSA_KV * CMP_HIDDEN)
    return pl.pallas_call(
        _compress_kernel,
        grid=(B,),
        in_specs=[pl.BlockSpec((1, S, LANES), lambda b: (b, 0, kc_blk)),
                  pl.BlockSpec((1, S, LANES), lambda b: (b, 0, vc_blk)),
                  full(w1_shape), full(w1_shape),
                  full((CMP_LEN, 8, LANES)), full((CMP_LEN, 8, LANES)),
                  full((NSA_KV * CMP_HIDDEN, LANES)), full((LANES, NSA_KV * CMP_HIDDEN)),
                  full((1, LANES)), full((nc, LANES)), full((nc, LANES)), full((LANES, LANES))],
        out_specs=[pl.BlockSpec((NSA_KV, nc, LANES), lambda b: (b, 0, 0)),
                   pl.BlockSpec((NSA_KV, HD, nc), lambda b: (b, 0, 0))],
        out_shape=[jax.ShapeDtypeStruct((B * NSA_KV, nc, LANES), BF16),
                   jax.ShapeDtypeStruct((B * NSA_KV, HD, nc), BF16)],
        compiler_params=_cparams(("parallel",)),
        name="nsa_compress",
    )(proj, proj, w1d[0], w1d[1], pe2[0], pe2[1], w2d[0], w2d[1].T,
      jnp.tile(k_gain.astype(F32), 2).reshape(1, LANES), cos, sin, _block_ones(LANES))


NSA_TQ = 256
NSA_TK = 1024


def _nsa_kernel(q_ref, kc_ref, vc_ref, ks_ref, vs_ref, kw_ref, vw_ref, gate_ref, cover_ref, place_ref,
                o_ref, acc_ref):
    TQ, TK, H = NSA_TQ, NSA_TK, NSA_HPG
    R = H * TQ
    t0 = pl.program_id(1) * TQ
    tl = t0 + lax.broadcasted_iota(I32, (1, TQ), 1)
    ncmp = kc_ref.shape[1]
    nsel = cover_ref.shape[0]

    def all_heads(x):
        return jnp.concatenate([x] * H, axis=1)

    def select(g):
        q = q_ref[0, g * H:(g + 1) * H].reshape(R, LANES)
        cmp_end = lax.broadcasted_iota(I32, (ncmp, 1), 0) * CMP_STRIDE + (CMP_LEN - 1)
        valid = all_heads(jnp.where(cmp_end <= tl, 1.0, 0.0))
        s = jnp.where(valid > 0.5, _dot_nt(kc_ref[g], q), NEG)
        p = jnp.exp(s - jnp.max(s, axis=0, keepdims=True)) * valid
        p_c = p / jnp.maximum(jnp.sum(p, axis=0, keepdims=True), 1e-30)
        o_c = _dot(vc_ref[g], p_c.astype(BF16))
        p_tok = p_c[:, 0:TQ]
        for h in range(1, H):
            p_tok = p_tok + p_c[:, h * TQ:(h + 1) * TQ]
        p_hi = p_tok.astype(BF16)
        p_lo = (p_tok - p_hi.astype(F32)).astype(BF16)
        imp = _dot(cover_ref[...], p_hi) + _dot(cover_ref[...], p_lo)
        jj = lax.broadcasted_iota(I32, (nsel, TQ), 0)
        jf = jj.astype(F32)
        cur = tl // SEL_LEN
        allowed = jj * SEL_LEN <= tl
        forced = (jj == 0) | (jj == cur) | (jj == cur - 1)
        score = jnp.where(allowed, imp + jnp.where(forced, FORCE_BONUS, 0.0), NEG)
        sel = jnp.zeros((nsel, TQ), F32)
        for _ in range(min(SEL_TOPN, nsel)):
            mx = jnp.max(score, axis=0, keepdims=True)
            first = jnp.min(jnp.where(score == mx, jf, float(nsel)), axis=0, keepdims=True)
            hit = jf == first
            sel = jnp.where(hit & (mx > 0.5 * NEG), 1.0, sel)
            score = jnp.where(hit, NEG, score)
        block_bias = jnp.where(sel > 0.5, 0.0, NEG).T.astype(BF16)
        placed = _dot(block_bias, place_ref[...]).astype(BF16)
        return o_c, q, q + jnp.concatenate([placed] * H, axis=0)

    def sweep_tile(g, q_sel, k0, size, m_old, causal):
        s = _dot_nt(ks_ref[0, g, pl.ds(k0, size), :], q_sel)
        if causal:
            kpos = k0 + lax.broadcasted_iota(I32, (size, 1), 0)
            s = s + all_heads(jnp.where(kpos <= tl, 0.0, NEG))
        m_new = jnp.maximum(m_old, jnp.max(s, axis=0, keepdims=True))
        p = jnp.exp(s - m_new).astype(BF16)
        acc_ref[g] = jnp.exp(m_old - m_new) * acc_ref[g] + _dot(vs_ref[0, g, :, pl.ds(k0, size)], p)
        return m_new

    def sweep(n, start, size, ms):
        def body(j, ms):
            k0 = pl.multiple_of(start + j * size, size)
            return tuple(sweep_tile(g, chosen[g][2], k0, size, ms[g], False) for g in groups)

        return lax.fori_loop(0, n, body, ms)

    def window(g, q):
        WK = NSA_WINDOW + TQ
        ws = pl.multiple_of(jnp.maximum(t0 - NSA_WINDOW, 0), TQ)
        kp = ws + lax.broadcasted_iota(I32, (WK, 1), 0)
        wbias = jnp.where((kp <= tl) & (kp > tl - NSA_WINDOW), 0.0, NEG)
        s = _dot_nt(kw_ref[0, g, pl.ds(ws, WK), :], q) + all_heads(wbias)
        p = jnp.exp(s - jnp.max(s, axis=0, keepdims=True)).astype(BF16)
        acc_w = _dot(vw_ref[0, g, :, pl.ds(ws, WK)], p)
        return acc_w[:HD] / acc_w[HD:HD + 1]

    groups = range(NSA_KV)
    chosen = [select(g) for g in groups]
    acc_ref[...] = jnp.zeros(acc_ref.shape, F32)
    n_big = t0 // TK
    m_run = sweep(n_big, 0, TK, tuple(jnp.full((1, R), NEG, F32) for _ in groups))
    m_run = sweep((t0 - n_big * TK) // TQ, n_big * TK, TQ, m_run)
    heads_out = []
    for g in groups:
        o_c, q, q_sel = chosen[g]
        sweep_tile(g, q_sel, pl.multiple_of(t0, TQ), TQ, m_run[g], True)
        acc = acc_ref[g]
        o_s = acc[:HD] / jnp.maximum(acc[HD:HD + 1], 1e-30)
        o_w = window(g, q)
        sg = jax.nn.sigmoid(gate_ref[0, :, g * LANES:(g + 1) * LANES]).T
        for h in range(H):
            cols = slice(h * TQ, (h + 1) * TQ)
            heads_out.append(sg[3 * h:3 * h + 1] * o_c[:, cols] + sg[3 * h + 1:3 * h + 2] * o_s[:, cols]
                             + sg[3 * h + 2:3 * h + 3] * o_w[:, cols])
    o_ref[0] = jnp.concatenate(heads_out, axis=0).T.astype(BF16)


def nsa_attention(qn, kcb, vcb, ksn, vsb, kwn, vwb, proj):
    B, _, S, _ = qn.shape
    G, H, TQ = NSA_KV, NSA_HPG, NSA_TQ
    assert S % NSA_TK == 0 and S >= NSA_WINDOW + TQ
    ncmp = kcb.shape[1]
    nsel = S // SEL_LEN
    n = np.arange(ncmp)[:, None] * CMP_STRIDE
    j = np.arange(nsel)[None, :] * SEL_LEN
    cover = jnp.asarray(((n < j + SEL_LEN) & (n + CMP_LEN > j)).astype(np.float32).T, dtype=BF16)
    place = np.zeros((nsel, LANES), np.float32)
    place[np.arange(nsel), HD + np.arange(nsel)] = 1.0
    R = H * TQ
    k_spec = pl.BlockSpec((1, G, S, LANES), lambda b, i: (b, 0, 0, 0))
    v_spec = pl.BlockSpec((1, G, LANES, S), lambda b, i: (b, 0, 0, 0))
    assert C_GATE % (G * LANES) == 0
    return pl.pallas_call(
        _nsa_kernel,
        grid=(B, S // TQ),
        in_specs=[pl.BlockSpec((1, G * H, TQ, LANES), lambda b, i: (b, 0, i, 0)),
                  pl.BlockSpec((G, ncmp, LANES), lambda b, i: (b, 0, 0)),
                  pl.BlockSpec((G, HD, ncmp), lambda b, i: (b, 0, 0)),
                  k_spec, v_spec, k_spec, v_spec,
                  pl.BlockSpec((1, TQ, G * LANES), lambda b, i: (b, i, C_GATE // (G * LANES))),
                  pl.BlockSpec((nsel, ncmp), lambda b, i: (0, 0)),
                  pl.BlockSpec((nsel, LANES), lambda b, i: (0, 0))],
        out_specs=pl.BlockSpec((1, TQ, G * H * HD), lambda b, i: (b, i, 0)),
        out_shape=jax.ShapeDtypeStruct((B, S, NSA_HEADS * HD), BF16),
        scratch_shapes=[pltpu.VMEM((G, LANES, R), F32)],
        compiler_params=_cparams(("parallel", "arbitrary")),
        name="nsa_attention",
    )(qn, kcb, vcb, ksn, vsb, kwn, vwb, proj, cover, jnp.asarray(place, dtype=BF16))


SWA_TQ = 256


def _swa_kernel(sink_ref, q_ref, k_ref, v_ref, o_ref):
    TQ, H = SWA_TQ, SWA_HPG
    R = H * TQ
    g = pl.program_id(1)
    t0 = pl.program_id(2) * TQ
    q = q_ref[0].reshape(R, HD)
    tl = t0 + lax.broadcasted_iota(I32, (1, TQ), 1)
    WK = SWA_WINDOW + TQ
    ws = pl.multiple_of(jnp.maximum(t0 - SWA_WINDOW, 0), LANES)
    kp = ws + lax.broadcasted_iota(I32, (WK, 1), 0)
    bias = jnp.where((kp <= tl) & (kp > tl - SWA_WINDOW), 0.0, NEG)
    s = _dot_nt(k_ref[0, 0, pl.ds(ws, WK), :], q) + jnp.concatenate([bias] * H, axis=1)
    head = lax.broadcasted_iota(I32, (1, R), 1) // TQ
    sink = jnp.zeros((1, R), F32)
    for h in range(H):
        sink = jnp.where(head == h, sink_ref[g * H + h], sink)
    m = jnp.maximum(jnp.max(s, axis=0, keepdims=True), sink)
    p = jnp.exp(s - m).astype(BF16)
    acc = _dot(v_ref[0, 0, :, pl.ds(ws, WK)], p)
    o = acc[:HD] / (acc[HD:HD + 1] + jnp.exp(sink - m))
    o_ref[0] = jnp.concatenate([o[:, h * TQ:(h + 1) * TQ] for h in range(H)], axis=0).T.astype(BF16)


def swa_attention(qn, kn, vb, sinks):
    B, _, S, _ = qn.shape
    G, H, TQ = SWA_KV, SWA_HPG, SWA_TQ
    assert S >= SWA_WINDOW + TQ
    k_spec = pl.BlockSpec((1, 1, S, HD), lambda b, g, i, sk: (b, g, 0, 0))
    v_spec = pl.BlockSpec((1, 1, LANES, S), lambda b, g, i, sk: (b, g, 0, 0))
    return pl.pallas_call(
        _swa_kernel,
        grid_spec=pltpu.PrefetchScalarGridSpec(
            num_scalar_prefetch=1,
            grid=(B, G, S // TQ),
            in_specs=[pl.BlockSpec((1, H, TQ, HD), lambda b, g, i, sk: (b, g, i, 0)), k_spec, v_spec],
            out_specs=pl.BlockSpec((1, TQ, H * HD), lambda b, g, i, sk: (b, i, g))),
        out_shape=jax.ShapeDtypeStruct((B, S, SWA_HEADS * HD), BF16),
        compiler_params=_cparams(("parallel", "parallel", "arbitrary")),
        name="swa_attention",
    )(sinks.astype(F32), qn, kn, vb)


R_GRP = 0
R_EXP = 32


SUBLANES = 8


def _store_row_tiles(ref, val, base=0):
    for s in range(SUBLANES):
        ref[pl.ds(base * SUBLANES + s, val.shape[0], stride=SUBLANES), :] = val[:, s * LANES:(s + 1) * LANES]


def _load_row_tiles(ref, rows, base=0):
    return [ref[pl.ds(base * SUBLANES + s, rows, stride=SUBLANES), :] for s in range(SUBLANES)]


def _router_kernel(x_ref, g_ref, w_ref, b_ref, tri_ref, hn_ref, route_ref, compact_ref, cnt_ref, base_ref):
    i = pl.program_id(0)
    last = pl.num_programs(0) - 1

    @pl.when(i == 0)
    def _():
        base_ref[...] = jnp.zeros_like(base_ref)

    @pl.when(i < last)
    def _():
        _route_tile(x_ref, g_ref, w_ref, b_ref, tri_ref, hn_ref, route_ref, compact_ref, base_ref)

    @pl.when(i == last)
    def _():
        hn_ref[...] = jnp.zeros_like(hn_ref)
        route_ref[...] = jnp.zeros_like(route_ref)
        compact_ref[...] = jnp.zeros_like(compact_ref)

    cnt_ref[...] = base_ref[...]


def _route_tile(x_ref, g_ref, w_ref, b_ref, tri_ref, hn_ref, route_ref, compact_ref, base_ref):
    x = x_ref[...]
    ms = jnp.mean(x * x, axis=-1, keepdims=True)
    h = x * lax.rsqrt(ms + EPS) * g_ref[...]
    _store_row_tiles(hn_ref, h)
    h_hi = h.astype(BF16)
    h_lo = (h - h_hi.astype(F32)).astype(BF16)
    logits = _dot(h_hi, w_ref[0]) + _dot(h_lo, w_ref[0]) + _dot(h_hi, w_ref[1]) + b_ref[...]
    lane = lax.broadcasted_iota(I32, logits.shape, 1)
    is_grp = lane < N_GROUPS
    gl = jnp.where(is_grp, logits, NEG)
    gmax = jnp.max(gl, axis=-1, keepdims=True)
    lanef = lane.astype(F32)
    gidx = jnp.min(jnp.where(gl == gmax, lanef, float(LANES)), axis=-1, keepdims=True)
    gw = 1.0 / jnp.sum(jnp.where(is_grp, jnp.exp(logits - gmax), 0.0), axis=-1, keepdims=True)
    in_grp = ((lane - R_EXP) // EXP_PER_GROUP).astype(F32) == gidx
    w0 = jnp.where(in_grp, logits, NEG)
    m1 = jnp.max(w0, axis=-1, keepdims=True)
    i1 = jnp.min(jnp.where(w0 == m1, lanef, float(LANES)), axis=-1, keepdims=True)
    w1 = jnp.where(lanef == i1, NEG, w0)
    m2 = jnp.max(w1, axis=-1, keepdims=True)
    i2 = jnp.min(jnp.where(w1 == m2, lanef, float(LANES)), axis=-1, keepdims=True)
    e2 = jnp.exp(m2 - m1)
    p1 = gw / (1.0 + e2)
    p2 = gw * e2 / (1.0 + e2)
    oh1 = jnp.where(lanef == i1, 1.0, 0.0)
    oh2 = jnp.where(lanef == i2, 1.0, 0.0)
    oh = oh1 + oh2
    before = _dot(tri_ref[...], oh.astype(BF16)) + base_ref[0:1, :]
    r1 = jnp.sum(before * oh1, axis=-1, keepdims=True)
    r2 = jnp.sum(before * oh2, axis=-1, keepdims=True)
    base_ref[...] = base_ref[...] + jnp.sum(oh, axis=0, keepdims=True)
    route = jnp.zeros_like(logits)
    for ln, val in enumerate((i1 - R_EXP, i2 - R_EXP, p1, p2, r1, r2)):
        route = jnp.where(lane == ln, val, route)
    route_ref[...] = route
    compact_ref[0] = route.T[0:SUBLANES]


def moe_router(x, gain, w_grp, b_grp, w_exp, b_exp, tm=512):
    T, D = x.shape
    nt = T // tm
    w = jnp.zeros((D, LANES), F32).at[:, R_GRP:R_GRP + N_GROUPS].set(w_grp).at[:, R_EXP:R_EXP + N_EXPERTS].set(w_exp)
    b = jnp.zeros((1, LANES), F32).at[0, R_GRP:R_GRP + N_GROUPS].set(b_grp).at[0, R_EXP:R_EXP + N_EXPERTS].set(b_exp)
    w_hi = w.astype(BF16)
    w = jnp.stack([w_hi, (w - w_hi.astype(F32)).astype(BF16)])
    tri = jnp.asarray(np.tril(np.ones((tm, tm), np.float32), -1), dtype=BF16)
    return pl.pallas_call(
        _router_kernel,
        grid=(nt + 1,),
        in_specs=[pl.BlockSpec((tm, D), lambda i: (jnp.minimum(i, nt - 1), 0)),
                  pl.BlockSpec((1, D), lambda i: (0, 0)),
                  pl.BlockSpec((2, D, LANES), lambda i: (0, 0, 0)),
                  pl.BlockSpec((1, LANES), lambda i: (0, 0)),
                  pl.BlockSpec((tm, tm), lambda i: (0, 0))],
        out_specs=[pl.BlockSpec((tm * SUBLANES, LANES), lambda i: (i, 0)),
                   pl.BlockSpec((tm, LANES), lambda i: (i, 0)),
                   pl.BlockSpec((1, SUBLANES, tm), lambda i: (i, 0, 0)),
                   pl.BlockSpec((8, LANES), lambda i: (0, 0))],
        out_shape=[jax.ShapeDtypeStruct(((T + tm) * SUBLANES, LANES), F32),
                   jax.ShapeDtypeStruct((T + tm, LANES), F32),
                   jax.ShapeDtypeStruct((nt + 1, SUBLANES, tm), F32),
                   jax.ShapeDtypeStruct((8, LANES), F32)],
        scratch_shapes=[pltpu.VMEM((8, LANES), F32)],
        compiler_params=_cparams(("arbitrary",)),
        name="moe_router",
    )(x, gain.reshape(1, D), w, b, tri)


def _row_copy(src, src_row, dst, dst_row, sem):
    def tile(ref, row):
        return ref.at[pl.ds(pl.multiple_of(row * SUBLANES, SUBLANES), SUBLANES)]

    return pltpu.make_async_copy(tile(src, src_row), tile(dst, dst_row), sem)


DISPATCH_TILE = 1024
ROW_UNROLL = 4


def _wait_rows(hbm_ref, sem, n):
    view = hbm_ref.at[pl.ds(0, n * SUBLANES)]
    pltpu.make_async_copy(view, view, sem).wait()


def _dispatch_kernel(dest_ref, x_ref, xb_ref, stage, sems):
    i = pl.program_id(0)
    last = pl.num_programs(0) - 1
    slot = i % 2

    @pl.when(i >= 2)
    def _():
        _wait_rows(xb_ref, sems.at[slot], DISPATCH_TILE)

    stage[slot] = x_ref[...]

    def issue(t, c):
        _row_copy(stage.at[slot], t, xb_ref, dest_ref[0, 0, 2 * t], sems.at[slot]).start(priority=0)
        _row_copy(stage.at[slot], t, xb_ref, dest_ref[0, 0, 2 * t + 1], sems.at[slot]).start(priority=1)
        return c

    lax.fori_loop(0, DISPATCH_TILE // 2, issue, 0, unroll=ROW_UNROLL)

    @pl.when(i == last)
    def _():
        _wait_rows(xb_ref, sems.at[slot], DISPATCH_TILE)

    @pl.when((i == last) & (i >= 1))
    def _():
        _wait_rows(xb_ref, sems.at[1 - slot], DISPATCH_TILE)


def moe_dispatch(hn, dest_all, n_tok):
    n_rows = dest_all.shape[0]
    tm = DISPATCH_TILE // 2
    nt = n_rows // DISPATCH_TILE
    n_tok_tiles = n_tok // tm
    return pl.pallas_call(
        _dispatch_kernel,
        grid=(nt,),
        in_specs=[pl.BlockSpec((1, 1, DISPATCH_TILE), lambda i: (i, 0, 0), memory_space=pltpu.SMEM),
                  pl.BlockSpec((tm * SUBLANES, LANES), lambda i: (jnp.minimum(i, n_tok_tiles), 0))],
        out_specs=pl.BlockSpec(memory_space=pl.ANY),
        out_shape=jax.ShapeDtypeStruct((n_rows * SUBLANES, LANES), F32),
        scratch_shapes=[pltpu.VMEM((2, tm * SUBLANES, LANES), F32), pltpu.SemaphoreType.DMA((2,))],
        compiler_params=_cparams(("arbitrary",)),
        name="moe_dispatch",
    )(dest_all.reshape(nt, 1, DISPATCH_TILE), hn)


def _expert_kernel(be_ref, x_ref, wg_ref, wu_ref, wd_ref, y_ref, wg_b, wu_b, wd_b):
    i = pl.program_id(0)

    @pl.when((i == 0) | (be_ref[i] != be_ref[jnp.maximum(i - 1, 0)]))
    def _():
        wg_b[...] = wg_ref[0, 0].astype(BF16)
        wu_b[...] = wu_ref[0, 0].astype(BF16)
        wd_b[...] = wd_ref[0, 0].astype(BF16)

    x = jnp.concatenate(_load_row_tiles(x_ref, MOE_BLOCK), axis=1).astype(BF16)
    hid = _dot(x, wg_b[...])
    hid = hid * jax.nn.sigmoid(hid) * _dot(x, wu_b[...])
    _store_row_tiles(y_ref, _dot(hid.astype(BF16), wd_b[...]))


def moe_experts(xb, blk_expert, layer, w_gate, w_up, w_down):
    n_rows = xb.shape[0] // SUBLANES
    D = w_gate.shape[2]
    blk = (MOE_BLOCK * SUBLANES, LANES)
    wspec = lambda shape: pl.BlockSpec((1, 1) + shape, lambda i, be: (layer, be[i], 0, 0))
    return pl.pallas_call(
        _expert_kernel,
        grid_spec=pltpu.PrefetchScalarGridSpec(
            num_scalar_prefetch=1,
            grid=(n_rows // MOE_BLOCK,),
            in_specs=[pl.BlockSpec(blk, lambda i, be: (i, 0)),
                      wspec((D, EXPERT_FF)), wspec((D, EXPERT_FF)), wspec((EXPERT_FF, D))],
            out_specs=pl.BlockSpec(blk, lambda i, be: (i, 0)),
            scratch_shapes=[pltpu.VMEM((D, EXPERT_FF), BF16), pltpu.VMEM((D, EXPERT_FF), BF16),
                            pltpu.VMEM((EXPERT_FF, D), BF16)]),
        out_shape=jax.ShapeDtypeStruct((n_rows * SUBLANES, LANES), F32),
        compiler_params=_cparams(("arbitrary",)),
        name="moe_experts",
    )(blk_expert, xb, w_gate, w_up, w_down)


def _combine_kernel(dest_ref, dest_next_ref, x_ref, route_ref, yb_ref, o_ref, ybuf, sems):
    tm = x_ref.shape[0]
    i = pl.program_id(0)
    slot = i % 2

    def gather(d_ref, s):
        def body(t, c):
            _row_copy(yb_ref, d_ref[0, 0, 2 * t], ybuf.at[s], t, sems.at[s]).start(priority=0)
            _row_copy(yb_ref, d_ref[0, 0, 2 * t + 1], ybuf.at[s], tm + t, sems.at[s]).start(priority=1)
            return c

        lax.fori_loop(0, tm, body, 0, unroll=ROW_UNROLL)

    @pl.when(i == 0)
    def _():
        gather(dest_ref, 0)

    @pl.when(i + 1 < pl.num_programs(0))
    def _():
        gather(dest_next_ref, 1 - slot)

    _wait_rows(yb_ref, sems.at[slot], 2 * tm)
    r = route_ref[...]
    w0, w1 = r[:, 2:3], r[:, 3:4]
    y0 = _load_row_tiles(ybuf.at[slot], tm)
    y1 = _load_row_tiles(ybuf.at[slot], tm, base=tm)
    for s in range(SUBLANES):
        cols = slice(s * LANES, (s + 1) * LANES)
        o_ref[:, cols] = x_ref[:, cols] + w0 * y0[s] + w1 * y1[s]


def moe_combine(x, yb, dest, route):
    T, D = x.shape
    tm = DISPATCH_TILE // 2
    nt = T // tm
    dest3 = dest.reshape(nt, 1, DISPATCH_TILE)
    return pl.pallas_call(
        _combine_kernel,
        grid=(nt,),
        in_specs=[pl.BlockSpec((1, 1, DISPATCH_TILE), lambda i: (i, 0, 0), memory_space=pltpu.SMEM),
                  pl.BlockSpec((1, 1, DISPATCH_TILE), lambda i: (jnp.minimum(i + 1, nt - 1), 0, 0),
                               memory_space=pltpu.SMEM),
                  pl.BlockSpec((tm, D), lambda i: (i, 0)),
                  pl.BlockSpec((tm, LANES), lambda i: (i, 0)),
                  pl.BlockSpec(memory_space=pl.ANY)],
        out_specs=pl.BlockSpec((tm, D), lambda i: (i, 0)),
        out_shape=jax.ShapeDtypeStruct((T, D), F32),
        scratch_shapes=[pltpu.VMEM((2, DISPATCH_TILE * SUBLANES, LANES), F32), pltpu.SemaphoreType.DMA((2,))],
        compiler_params=_cparams(("arbitrary",)),
        name="moe_combine",
    )(dest3, dest3, x, route, yb)


def hier_moe_block(x, layer, gain, w_grp, b_grp, w_exp, b_exp, w_gate, w_up, w_down):
    T, D = x.shape
    hn, route, compact, cnt = moe_router(x, gain, w_grp, b_grp, w_exp, b_exp)
    A = 2 * T
    n_rows = -(-A // MOE_BLOCK) * MOE_BLOCK + N_EXPERTS * MOE_BLOCK
    n_blocks = n_rows // MOE_BLOCK
    counts = cnt[0, R_EXP:R_EXP + N_EXPERTS].astype(I32)
    pcounts = (counts + MOE_BLOCK - 1) // MOE_BLOCK * MOE_BLOCK
    pend = jnp.cumsum(pcounts)
    pstart = pend - pcounts
    experts = jnp.arange(N_EXPERTS, dtype=I32)
    fields = compact[:-1].astype(I32)
    slot_e = jnp.stack([fields[:, 0], fields[:, 1]], axis=-1).reshape(T, 2)
    slot_rank = jnp.stack([fields[:, 4], fields[:, 5]], axis=-1).reshape(T, 2)
    seg_start = jnp.sum(jnp.where(slot_e[..., None] == experts, pstart, 0), axis=-1)
    dest = (slot_rank + seg_start).reshape(A)
    n_pad = n_rows - A
    pad_cnt = jnp.concatenate([pcounts - counts, (n_rows - pend[-1])[None]])
    pad_start = jnp.concatenate([pstart + counts, pend[-1:]])
    pad_end = jnp.cumsum(pad_cnt)
    k = jnp.arange(n_pad, dtype=I32)
    seg = (k[:, None] >= pad_end[None, :]).astype(I32).sum(axis=1)
    seg_hot = seg[:, None] == jnp.arange(N_EXPERTS + 1, dtype=I32)[None, :]
    pad_dest = k + jnp.sum(jnp.where(seg_hot, (pad_start - (pad_end - pad_cnt))[None, :], 0), axis=1)
    dest_all = jnp.concatenate([dest, pad_dest]).astype(I32)
    blk_start = jnp.arange(n_blocks, dtype=I32) * MOE_BLOCK
    blk_expert = jnp.minimum((pend[None, :] <= blk_start[:, None]).astype(I32).sum(axis=1), N_EXPERTS - 1)
    xb = moe_dispatch(hn, dest_all, T)
    yb = moe_experts(xb, blk_expert, layer, w_gate, w_up, w_down)
    return moe_combine(x, yb, dest, route)


def _even_w_in(w_in):
    hgw, qw, kvw, ng = 4 * HG_WIDTH, NSA_HEADS * HD, NSA_KV * HD, 3 * NSA_HPG
    kc, vc, ks, vs, kw, vw = (hgw + qw + i * kvw for i in range(6))
    gate = hgw + qw + 6 * kvw
    pad = jnp.zeros((w_in.shape[0], LANES - ng), w_in.dtype)
    cols = [w_in[:, :hgw], w_in[:, kc:kc + kvw], w_in[:, vc:vc + kvw]]
    for g in range(NSA_KV):
        cols += [w_in[:, gate + g * ng:gate + (g + 1) * ng], pad]
    cols += [w_in[:, hgw:hgw + qw]] + [w_in[:, c:c + kvw] for c in (ks, kw, vs, vw)]
    return jnp.concatenate(cols, axis=1).astype(BF16)


def even_layer(x, B, S, norm, w_in, lb, hg_gain, q_gain, k_gain, cmp_pe, cmp_w1, cmp_w2, w_out):
    T = B * S
    proj, qn, ksn, kwn, vsb, vwb = in_proj(
        x, norm, _even_w_in(w_in), N_PLAIN, [NSA_HEADS, NSA_KV, NSA_KV, NSA_KV, NSA_KV],
        [q_gain, k_gain[1], k_gain[2], None, None], ['pad', 'blk', 'pad', 'v', 'v'],
        [HD ** -0.5, 1.0, 1.0, 1.0, 1.0], B, S)
    proj = proj.reshape(B, S, N_PLAIN)
    a_out = hgrn2(proj, lb, hg_gain)
    kcb, vcb = nsa_compress(proj, C_KC // LANES, C_KC // LANES + 1, cmp_pe, cmp_w1, cmp_w2, k_gain[0])
    b_out = nsa_attention(qn, kcb, vcb, ksn, vsb, kwn, vwb, proj)
    return proj_res(a_out.reshape(T, HG_WIDTH), 0, b_out.reshape(T, NSA_HEADS * HD), 0, w_out.astype(BF16), x)


def odd_layer(x, B, S, norm, w_in, q_gain, k_gain, sinks, w_out):
    T = B * S
    qn, kn, vb = in_proj(x, norm, w_in.astype(BF16), 0, [SWA_HEADS, SWA_KV, SWA_KV], [q_gain, k_gain, None],
                         ['qk', 'qk', 'v'], [HD ** -0.5, 1.0, 1.0], B, S)
    att = swa_attention(qn, kn, vb, sinks).reshape(T, SWA_HEADS * HD)
    return proj_res(att, 0, att, 1, w_out.astype(BF16), x)


def kernel(x, ev_norm, ev_w_in, hg_lb_logits, hg_out_gain, nsa_q_gain, nsa_k_gain, nsa_cmp_pe, nsa_cmp_w1,
           nsa_cmp_w2, ev_w_out, od_norm, od_w_in, swa_q_gain, swa_k_gain, swa_sinks, od_w_out, moe_norm,
           moe_w_grp, moe_b_grp, moe_w_exp, moe_b_exp, moe_w_gate, moe_w_up, moe_w_down):
    B, S, D = x.shape
    lower_bounds = jnp.cumsum(jax.nn.softmax(hg_lb_logits.astype(F32), axis=0), axis=0)
    h = x.reshape(B * S, D)
    h = even_layer(h, B, S, ev_norm[0], ev_w_in[0], lower_bounds[0], hg_out_gain[0], nsa_q_gain[0],
                   nsa_k_gain[0], nsa_cmp_pe[0], nsa_cmp_w1[0], nsa_cmp_w2[0], ev_w_out[0])
    h = hier_moe_block(h, 0, moe_norm[0], moe_w_grp[0], moe_b_grp[0], moe_w_exp[0], moe_b_exp[0],
                       moe_w_gate, moe_w_up, moe_w_down)
    h = odd_layer(h, B, S, od_norm[0], od_w_in[0], swa_q_gain[0], swa_k_gain[0], swa_sinks[0], od_w_out[0])
    h = hier_moe_block(h, 1, moe_norm[1], moe_w_grp[1], moe_b_grp[1], moe_w_exp[1], moe_b_exp[1],
                       moe_w_gate, moe_w_up, moe_w_down)
    return h.reshape(B, S, D)
```
